```python
import jax, jax.numpy as jnp
from jax import lax
import numpy as np

D_MODEL = 1024
BATCH = 16
SEQ = 2048
DEPTH = 2

HEAD_DIM_A = 64
N_HEADS_A = 6
D_A = N_HEADS_A * HEAD_DIM_A
HEAD_DIM_B = 96
N_HEADS_B = 4
D_B = N_HEADS_B * HEAD_DIM_B
POOL_WINDOWS = (2, 4, 8, 16)
N_GROUPS_C = len(POOL_WINDOWS)
GROUP_DIM_C = 64
D_C = N_GROUPS_C * GROUP_DIM_C
D_MIX = D_A + D_B + D_C
D_IN = 2 * D_A + 2 * D_B + D_C
CONV_WIDTH = 31
CHUNK = 128
D_FF = ((8 * D_MODEL // 3 + 255) // 256) * 256
RMS_EPS = 1e-6
LN_EPS = 1e-5

kernel_name = "hybrid_conv_gmlp_pool_block"


def rms_norm(x, g):
    xf = x.astype(jnp.float32)
    y = xf * lax.rsqrt(jnp.mean(xf * xf, axis=-1, keepdims=True) + RMS_EPS)
    return (y * g.astype(jnp.float32)).astype(x.dtype)


def layer_norm(x, g, b):
    xf = x.astype(jnp.float32)
    mu = jnp.mean(xf, axis=-1, keepdims=True)
    xc = xf - mu
    var = jnp.mean(xc * xc, axis=-1, keepdims=True)
    y = xc * lax.rsqrt(var + LN_EPS) * g.astype(jnp.float32) + b.astype(jnp.float32)
    return y.astype(x.dtype)


def conformer_conv(z, conv_w, conv_b, ln_g, ln_b, w_pw):
    a, gate = jnp.split(z, 2, axis=-1)
    y = a * jax.nn.sigmoid(gate)
    y = lax.conv_general_dilated(
        y, conv_w[:, None, :], window_strides=(1,),
        padding=[(CONV_WIDTH - 1, 0)],
        dimension_numbers=("NWC", "WIO", "NWC"),
        feature_group_count=D_A) + conv_b
    y = jax.nn.silu(layer_norm(y, ln_g, ln_b))
    return y @ w_pw


def spatial_gating(z, ln_g, ln_b, w_s, b_s):
    bsz, seq, _ = z.shape
    z = jax.nn.gelu(z)
    u, v = jnp.split(z, 2, axis=-1)
    v = layer_norm(v, ln_g, ln_b)
    v = v.reshape(bsz, seq // CHUNK, CHUNK, N_HEADS_B, HEAD_DIM_B)
    causal = jnp.tril(jnp.ones((CHUNK, CHUNK), dtype=bool))
    w = jnp.where(causal[None], w_s, jnp.zeros_like(w_s))
    s = jnp.einsum('hts,bnshd->bnthd', w, v) + b_s.T[:, :, None]
    return u * s.reshape(bsz, seq, D_B)


def multiscale_pool(z, w_pool, pool_scale):
    bsz, seq, _ = z.shape
    zf = z.astype(jnp.float32)
    cs0 = jnp.concatenate([jnp.zeros((bsz, 1, D_C), jnp.float32), jnp.cumsum(zf, axis=1)], axis=1)
    t = jnp.arange(seq, dtype=jnp.float32)[:, None]
    outs = []
    for g, w in enumerate(POOL_WINDOWS):
        sl = slice(g * GROUP_DIM_C, (g + 1) * GROUP_DIM_C)
        c = cs0[..., sl]
        upper = c[:, 1:]
        lower = jnp.concatenate([jnp.zeros((bsz, w - 1, GROUP_DIM_C), jnp.float32),
                                 c[:, :seq - w + 1]], axis=1)
        cnt = jnp.minimum(t + 1.0, float(w))
        outs.append((upper - lower) / cnt - zf[..., sl])
    p = jnp.stack(outs, axis=2).astype(z.dtype)
    y = jnp.einsum('bsgi,gio->bsgo', p, w_pool).reshape(bsz, seq, D_C)
    return y * pool_scale


def _fwd_setup_inputs(seed: int = 0) -> dict:
    key = jax.random.key(seed)
    ks = jax.random.split(key, 20)
    n = lambda k, shape, s: jax.random.normal(k, shape, jnp.float32) * s
    L = DEPTH
    return {
        "x": n(ks[0], (BATCH, SEQ, D_MODEL), 1.0),
        "norm1_g": 1.0 + n(ks[1], (L, D_MODEL), 0.02),
        "w_in": n(ks[2], (L, D_MODEL, D_IN), D_MODEL ** -0.5),
        "conv_w": n(ks[3], (L, CONV_WIDTH, D_A), CONV_WIDTH ** -0.5),
        "conv_b": n(ks[4], (L, D_A), 0.01),
        "conv_ln_g": 1.0 + n(ks[5], (L, D_A), 0.02),
        "conv_ln_b": n(ks[6], (L, D_A), 0.01),
        "w_pw": n(ks[7], (L, D_A, D_A), D_A ** -0.5),
        "sg_ln_g": 1.0 + n(ks[8], (L, D_B), 0.02),
        "sg_ln_b": n(ks[9], (L, D_B), 0.01),
        "w_s": n(ks[10], (L, N_HEADS_B, CHUNK, CHUNK), CHUNK ** -0.5),
        "b_s": 1.0 + n(ks[11], (L, N_HEADS_B, CHUNK), 0.02),
        "w_pool": n(ks[12], (L, N_GROUPS_C, GROUP_DIM_C, GROUP_DIM_C), GROUP_DIM_C ** -0.5),
        "pool_scale": 1.0 + n(ks[13], (L, D_C), 0.02),
        "w_out": n(ks[14], (L, D_MIX, D_MODEL), D_MIX ** -0.5),
        "norm2_g": 1.0 + n(ks[15], (L, D_MODEL), 0.02),
        "w_gate_up": n(ks[16], (L, D_MODEL, 2 * D_FF), D_MODEL ** -0.5),
        "w_down": n(ks[17], (L, D_FF, D_MODEL), D_FF ** -0.5),
        "final_g": 1.0 + n(ks[18], (D_MODEL,), 0.02),
    }


def _fwd_reference(x, norm1_g, w_in, conv_w, conv_b, conv_ln_g, conv_ln_b, w_pw,
              sg_ln_g, sg_ln_b, w_s, b_s, w_pool, pool_scale, w_out,
              norm2_g, w_gate_up, w_down, final_g):
    for l in range(DEPTH):
        h = rms_norm(x, norm1_g[l])
        z = h @ w_in[l]
        za = z[..., :2 * D_A]
        zb = z[..., 2 * D_A:2 * D_A + 2 * D_B]
        zc = z[..., 2 * D_A + 2 * D_B:]
        ya = conformer_conv(za, conv_w[l], conv_b[l], conv_ln_g[l], conv_ln_b[l], w_pw[l])
        yb = spatial_gating(zb, sg_ln_g[l], sg_ln_b[l], w_s[l], b_s[l])
        yc = multiscale_pool(zc, w_pool[l], pool_scale[l])
        x = x + jnp.concatenate([ya, yb, yc], axis=-1) @ w_out[l]
        h = rms_norm(x, norm2_g[l])
        gate, up = jnp.split(h @ w_gate_up[l], 2, axis=-1)
        x = x + (jax.nn.silu(gate) * up) @ w_down[l]
    return rms_norm(x, final_g)


import jax as _jax
import jax.numpy as _jnp

TWIN_FORMAT = 'train_step'
FWD_PARAMS = ['x', 'norm1_g', 'w_in', 'conv_w', 'conv_b', 'conv_ln_g', 'conv_ln_b', 'w_pw', 'sg_ln_g', 'sg_ln_b', 'w_s', 'b_s', 'w_pool', 'pool_scale', 'w_out', 'norm2_g', 'w_gate_up', 'w_down', 'final_g']
TWIN_WEIGHTS = ['norm1_g', 'w_in', 'conv_w', 'conv_b', 'conv_ln_g', 'conv_ln_b', 'w_pw', 'sg_ln_g', 'sg_ln_b', 'w_s', 'b_s', 'w_pool', 'pool_scale', 'w_out', 'norm2_g', 'w_gate_up', 'w_down', 'final_g']
TWIN_DIFF_INPUT = 'x'
TWIN_INPUTS = ['x', 'norm1_g', 'w_in', 'conv_w', 'conv_b', 'conv_ln_g', 'conv_ln_b', 'w_pw', 'sg_ln_g', 'sg_ln_b', 'w_s', 'b_s', 'w_pool', 'pool_scale', 'w_out', 'norm2_g', 'w_gate_up', 'w_down', 'final_g', 'loss_target', 'm_norm1_g', 'm_w_in', 'm_conv_w', 'm_conv_b', 'm_conv_ln_g', 'm_conv_ln_b', 'm_w_pw', 'm_sg_ln_g', 'm_sg_ln_b', 'm_w_s', 'm_b_s', 'm_w_pool', 'm_pool_scale', 'm_w_out', 'm_norm2_g', 'm_w_gate_up', 'm_w_down', 'm_final_g', 'v_norm1_g', 'v_w_in', 'v_conv_w', 'v_conv_b', 'v_conv_ln_g', 'v_conv_ln_b', 'v_w_pw', 'v_sg_ln_g', 'v_sg_ln_b', 'v_w_s', 'v_b_s', 'v_w_pool', 'v_pool_scale', 'v_w_out', 'v_norm2_g', 'v_w_gate_up', 'v_w_down', 'v_final_g']
TWIN_OUTPUTS = ['loss', 'grad_x', 'grad_norm1_g', 'grad_w_in', 'grad_conv_w', 'grad_conv_b', 'grad_conv_ln_g', 'grad_conv_ln_b', 'grad_w_pw', 'grad_sg_ln_g', 'grad_sg_ln_b', 'grad_w_s', 'grad_b_s', 'grad_w_pool', 'grad_pool_scale', 'grad_w_out', 'grad_norm2_g', 'grad_w_gate_up', 'grad_w_down', 'grad_final_g', 'delta_norm1_g', 'delta_w_in', 'delta_conv_w', 'delta_conv_b', 'delta_conv_ln_g', 'delta_conv_ln_b', 'delta_w_pw', 'delta_sg_ln_g', 'delta_sg_ln_b', 'delta_w_s', 'delta_b_s', 'delta_w_pool', 'delta_pool_scale', 'delta_w_out', 'delta_norm2_g', 'delta_w_gate_up', 'delta_w_down', 'delta_final_g', 'new_m_norm1_g', 'new_m_w_in', 'new_m_conv_w', 'new_m_conv_b', 'new_m_conv_ln_g', 'new_m_conv_ln_b', 'new_m_w_pw', 'new_m_sg_ln_g', 'new_m_sg_ln_b', 'new_m_w_s', 'new_m_b_s', 'new_m_w_pool', 'new_m_pool_scale', 'new_m_w_out', 'new_m_norm2_g', 'new_m_w_gate_up', 'new_m_w_down', 'new_m_final_g', 'new_v_norm1_g', 'new_v_w_in', 'new_v_conv_w', 'new_v_conv_b', 'new_v_conv_ln_g', 'new_v_conv_ln_b', 'new_v_w_pw', 'new_v_sg_ln_g', 'new_v_sg_ln_b', 'new_v_w_s', 'new_v_b_s', 'new_v_w_pool', 'new_v_pool_scale', 'new_v_w_out', 'new_v_norm2_g', 'new_v_w_gate_up', 'new_v_w_down', 'new_v_final_g']
TWIN_LEAF_KINDS = {'loss': 'loss', 'grad_x': 'grad_x', 'grad_norm1_g': 'grad_w', 'grad_w_in': 'grad_w', 'grad_conv_w': 'grad_w', 'grad_conv_b': 'grad_w', 'grad_conv_ln_g': 'grad_w', 'grad_conv_ln_b': 'grad_w', 'grad_w_pw': 'grad_w', 'grad_sg_ln_g': 'grad_w', 'grad_sg_ln_b': 'grad_w', 'grad_w_s': 'grad_w', 'grad_b_s': 'grad_w', 'grad_w_pool': 'grad_w', 'grad_pool_scale': 'grad_w', 'grad_w_out': 'grad_w', 'grad_norm2_g': 'grad_w', 'grad_w_gate_up': 'grad_w', 'grad_w_down': 'grad_w', 'grad_final_g': 'grad_w', 'delta_norm1_g': 'delta_w', 'delta_w_in': 'delta_w', 'delta_conv_w': 'delta_w', 'delta_conv_b': 'delta_w', 'delta_conv_ln_g': 'delta_w', 'delta_conv_ln_b': 'delta_w', 'delta_w_pw': 'delta_w', 'delta_sg_ln_g': 'delta_w', 'delta_sg_ln_b': 'delta_w', 'delta_w_s': 'delta_w', 'delta_b_s': 'delta_w', 'delta_w_pool': 'delta_w', 'delta_pool_scale': 'delta_w', 'delta_w_out': 'delta_w', 'delta_norm2_g': 'delta_w', 'delta_w_gate_up': 'delta_w', 'delta_w_down': 'delta_w', 'delta_final_g': 'delta_w', 'new_m_norm1_g': 'new_m', 'new_m_w_in': 'new_m', 'new_m_conv_w': 'new_m', 'new_m_conv_b': 'new_m', 'new_m_conv_ln_g': 'new_m', 'new_m_conv_ln_b': 'new_m', 'new_m_w_pw': 'new_m', 'new_m_sg_ln_g': 'new_m', 'new_m_sg_ln_b': 'new_m', 'new_m_w_s': 'new_m', 'new_m_b_s': 'new_m', 'new_m_w_pool': 'new_m', 'new_m_pool_scale': 'new_m', 'new_m_w_out': 'new_m', 'new_m_norm2_g': 'new_m', 'new_m_w_gate_up': 'new_m', 'new_m_w_down': 'new_m', 'new_m_final_g': 'new_m', 'new_v_norm1_g': 'new_v', 'new_v_w_in': 'new_v', 'new_v_conv_w': 'new_v', 'new_v_conv_b': 'new_v', 'new_v_conv_ln_g': 'new_v', 'new_v_conv_ln_b': 'new_v', 'new_v_w_pw': 'new_v', 'new_v_sg_ln_g': 'new_v', 'new_v_sg_ln_b': 'new_v', 'new_v_w_s': 'new_v', 'new_v_b_s': 'new_v', 'new_v_w_pool': 'new_v', 'new_v_pool_scale': 'new_v', 'new_v_w_out': 'new_v', 'new_v_norm2_g': 'new_v', 'new_v_w_gate_up': 'new_v', 'new_v_w_down': 'new_v', 'new_v_final_g': 'new_v'}


def _forward(args):
    return _fwd_reference(*[args[k] for k in FWD_PARAMS])


def _output_shape():
    out = _jax.eval_shape(lambda: _forward(_fwd_setup_inputs(0)))
    return out.shape, out.dtype

N_MICROBATCH = 1
ADAM_LR = 0.001
ADAM_B1 = 0.9
ADAM_B2 = 0.999
ADAM_EPS = 1e-08
ADAM_WD = 0.01
ADAM_STEP = 10
PER_EXAMPLE_BATCH_AXIS = {'x': 0, 'loss_target': 0}
SHARED_INPUTS = []
_WEIGHT_DTYPES = {'norm1_g': _jnp.float32, 'w_in': _jnp.float32, 'conv_w': _jnp.float32, 'conv_b': _jnp.float32, 'conv_ln_g': _jnp.float32, 'conv_ln_b': _jnp.float32, 'w_pw': _jnp.float32, 'sg_ln_g': _jnp.float32, 'sg_ln_b': _jnp.float32, 'w_s': _jnp.float32, 'b_s': _jnp.float32, 'w_pool': _jnp.float32, 'pool_scale': _jnp.float32, 'w_out': _jnp.float32, 'norm2_g': _jnp.float32, 'w_gate_up': _jnp.float32, 'w_down': _jnp.float32, 'final_g': _jnp.float32}
MOMENT_SCALE = {'norm1_g': 1.273104e-01, 'w_in': 9.465173e-02, 'conv_w': 9.167690e-02, 'conv_b': 2.001961e-01, 'conv_ln_g': 1.149539e-01, 'conv_ln_b': 9.610781e-02, 'w_pw': 8.880963e-02, 'sg_ln_g': 6.604354e-02, 'sg_ln_b': 8.047438e-02, 'w_s': 5.780756e-02, 'b_s': 8.519987e-02, 'w_pool': 1.275436e-01, 'pool_scale': 1.294115e-01, 'w_out': 1.117275e-01, 'norm2_g': 1.085271e-01, 'w_gate_up': 4.526149e-02, 'w_down': 7.380149e-02, 'final_g': 3.202188e+01}


def _to_microbatches(a, axis):
    t = _jnp.moveaxis(a, axis, 0)
    t = t.reshape((N_MICROBATCH, t.shape[0] // N_MICROBATCH) + t.shape[1:])
    return _jnp.moveaxis(t, 1, axis + 1)


def setup_inputs(seed: int = 0) -> dict:
    inp = _fwd_setup_inputs(seed)
    key = _jax.random.fold_in(_jax.random.key(seed), 7919)
    shape, _ = _output_shape()
    out = dict(inp)
    out["loss_target"] = _jax.random.normal(_jax.random.fold_in(key, 0), shape, _jnp.float32)
    for i, name in enumerate(TWIN_WEIGHTS):
        w = inp[name].astype(_jnp.float32)
        if MOMENT_SCALE is None:
            s = _jnp.sqrt(_jnp.mean(_jnp.square(w)) + 1e-30)
        else:
            s = MOMENT_SCALE[name]
        km, kv = _jax.random.split(_jax.random.fold_in(key, i + 1))
        out[name] = w
        out["m_" + name] = s * _jax.random.normal(km, w.shape, _jnp.float32)
        out["v_" + name] = (s * s) * _jax.random.uniform(kv, w.shape, _jnp.float32, 0.5, 1.5)
    if N_MICROBATCH > 1:
        for name, axis in PER_EXAMPLE_BATCH_AXIS.items():
            out[name] = _to_microbatches(out[name], axis)
    return {'x': out['x'], 'norm1_g': out['norm1_g'], 'w_in': out['w_in'], 'conv_w': out['conv_w'], 'conv_b': out['conv_b'], 'conv_ln_g': out['conv_ln_g'], 'conv_ln_b': out['conv_ln_b'], 'w_pw': out['w_pw'], 'sg_ln_g': out['sg_ln_g'], 'sg_ln_b': out['sg_ln_b'], 'w_s': out['w_s'], 'b_s': out['b_s'], 'w_pool': out['w_pool'], 'pool_scale': out['pool_scale'], 'w_out': out['w_out'], 'norm2_g': out['norm2_g'], 'w_gate_up': out['w_gate_up'], 'w_down': out['w_down'], 'final_g': out['final_g'], 'loss_target': out['loss_target'], 'm_norm1_g': out['m_norm1_g'], 'm_w_in': out['m_w_in'], 'm_conv_w': out['m_conv_w'], 'm_conv_b': out['m_conv_b'], 'm_conv_ln_g': out['m_conv_ln_g'], 'm_conv_ln_b': out['m_conv_ln_b'], 'm_w_pw': out['m_w_pw'], 'm_sg_ln_g': out['m_sg_ln_g'], 'm_sg_ln_b': out['m_sg_ln_b'], 'm_w_s': out['m_w_s'], 'm_b_s': out['m_b_s'], 'm_w_pool': out['m_w_pool'], 'm_pool_scale': out['m_pool_scale'], 'm_w_out': out['m_w_out'], 'm_norm2_g': out['m_norm2_g'], 'm_w_gate_up': out['m_w_gate_up'], 'm_w_down': out['m_w_down'], 'm_final_g': out['m_final_g'], 'v_norm1_g': out['v_norm1_g'], 'v_w_in': out['v_w_in'], 'v_conv_w': out['v_conv_w'], 'v_conv_b': out['v_conv_b'], 'v_conv_ln_g': out['v_conv_ln_g'], 'v_conv_ln_b': out['v_conv_ln_b'], 'v_w_pw': out['v_w_pw'], 'v_sg_ln_g': out['v_sg_ln_g'], 'v_sg_ln_b': out['v_sg_ln_b'], 'v_w_s': out['v_w_s'], 'v_b_s': out['v_b_s'], 'v_w_pool': out['v_w_pool'], 'v_pool_scale': out['v_pool_scale'], 'v_w_out': out['v_w_out'], 'v_norm2_g': out['v_norm2_g'], 'v_w_gate_up': out['v_w_gate_up'], 'v_w_down': out['v_w_down'], 'v_final_g': out['v_final_g']}


def _loss(weights, diff, rest, loss_target):
    with _jax.named_scope("forward"):
        args = {**rest, TWIN_DIFF_INPUT: diff, **{k: w.astype(_WEIGHT_DTYPES[k]) for k, w in weights.items()}}
        y = _forward(args)
    with _jax.named_scope("loss_head"):
        err = _jnp.square(y.astype(_jnp.float32) - loss_target)
        return 0.5 * _jnp.sum(_jnp.mean(err, axis=-1)) if err.ndim else 0.5 * err


def _adamw(w, g, m, v):
    m = ADAM_B1 * m + (1.0 - ADAM_B1) * g
    v = ADAM_B2 * v + (1.0 - ADAM_B2) * _jnp.square(g)
    m_hat = m / (1.0 - ADAM_B1 ** ADAM_STEP)
    v_hat = v / (1.0 - ADAM_B2 ** ADAM_STEP)
    delta = -ADAM_LR * (m_hat / (_jnp.sqrt(v_hat) + ADAM_EPS) + ADAM_WD * w)
    return delta, m, v


def reference(x, norm1_g, w_in, conv_w, conv_b, conv_ln_g, conv_ln_b, w_pw, sg_ln_g, sg_ln_b, w_s, b_s, w_pool, pool_scale, w_out, norm2_g, w_gate_up, w_down, final_g, loss_target, m_norm1_g, m_w_in, m_conv_w, m_conv_b, m_conv_ln_g, m_conv_ln_b, m_w_pw, m_sg_ln_g, m_sg_ln_b, m_w_s, m_b_s, m_w_pool, m_pool_scale, m_w_out, m_norm2_g, m_w_gate_up, m_w_down, m_final_g, v_norm1_g, v_w_in, v_conv_w, v_conv_b, v_conv_ln_g, v_conv_ln_b, v_w_pw, v_sg_ln_g, v_sg_ln_b, v_w_s, v_b_s, v_w_pool, v_pool_scale, v_w_out, v_norm2_g, v_w_gate_up, v_w_down, v_final_g):
    given = dict(x=x, norm1_g=norm1_g, w_in=w_in, conv_w=conv_w, conv_b=conv_b, conv_ln_g=conv_ln_g, conv_ln_b=conv_ln_b, w_pw=w_pw, sg_ln_g=sg_ln_g, sg_ln_b=sg_ln_b, w_s=w_s, b_s=b_s, w_pool=w_pool, pool_scale=pool_scale, w_out=w_out, norm2_g=norm2_g, w_gate_up=w_gate_up, w_down=w_down, final_g=final_g, loss_target=loss_target, m_norm1_g=m_norm1_g, m_w_in=m_w_in, m_conv_w=m_conv_w, m_conv_b=m_conv_b, m_conv_ln_g=m_conv_ln_g, m_conv_ln_b=m_conv_ln_b, m_w_pw=m_w_pw, m_sg_ln_g=m_sg_ln_g, m_sg_ln_b=m_sg_ln_b, m_w_s=m_w_s, m_b_s=m_b_s, m_w_pool=m_w_pool, m_pool_scale=m_pool_scale, m_w_out=m_w_out, m_norm2_g=m_norm2_g, m_w_gate_up=m_w_gate_up, m_w_down=m_w_down, m_final_g=m_final_g, v_norm1_g=v_norm1_g, v_w_in=v_w_in, v_conv_w=v_conv_w, v_conv_b=v_conv_b, v_conv_ln_g=v_conv_ln_g, v_conv_ln_b=v_conv_ln_b, v_w_pw=v_w_pw, v_sg_ln_g=v_sg_ln_g, v_sg_ln_b=v_sg_ln_b, v_w_s=v_w_s, v_b_s=v_b_s, v_w_pool=v_w_pool, v_pool_scale=v_pool_scale, v_w_out=v_w_out, v_norm2_g=v_norm2_g, v_w_gate_up=v_w_gate_up, v_w_down=v_w_down, v_final_g=v_final_g)
    weights = {n: given[n] for n in TWIN_WEIGHTS}
    shared = {n: given[n] for n in SHARED_INPUTS}
    per_example = {n: given[n] for n in ['x']}
    grad_fn = _jax.value_and_grad(_loss, argnums=(0, 1))

    def one_microbatch(ex, loss_target):
        ex = dict(ex)
        diff = ex.pop(TWIN_DIFF_INPUT)
        return grad_fn(weights, diff, {**shared, **ex}, loss_target)

    if N_MICROBATCH == 1:
        loss, (grad_w, grad_x) = one_microbatch(per_example, given["loss_target"])
    else:
        def body(carry, xs):
            loss_sum, grad_sum = carry
            l_k, (gw_k, gx_k) = one_microbatch(xs[0], xs[1])
            with _jax.named_scope("update"):
                return (loss_sum + l_k, _jax.tree.map(_jnp.add, grad_sum, gw_k)), gx_k

        init = (_jnp.zeros((), _jnp.float32), _jax.tree.map(_jnp.zeros_like, weights))
        (loss, grad_w), grad_x = _jax.lax.scan(body, init, (per_example, given["loss_target"]))
    with _jax.named_scope("update"):
        delta_w, new_m, new_v = {}, {}, {}
        for n in TWIN_WEIGHTS:
            delta_w[n], new_m[n], new_v[n] = _adamw(weights[n], grad_w[n], given["m_" + n], given["v_" + n])
    return (loss, grad_x, *[grad_w[n] for n in TWIN_WEIGHTS], *[delta_w[n] for n in TWIN_WEIGHTS],
            *[new_m[n] for n in TWIN_WEIGHTS], *[new_v[n] for n in TWIN_WEIGHTS])
```

```python
import functools
import math

import jax
import jax.numpy as jnp
from jax import lax
from jax.experimental import pallas as pl
from jax.experimental.pallas import tpu as pltpu

F32 = jnp.float32
BF16 = jnp.bfloat16

D_MODEL = 1024
D_A = 384
D_B = 384
D_C = 256
D_IN = 2 * D_A + 2 * D_B + D_C
N_HEADS_B = 4
HEAD_DIM_B = 96
POOL_WINDOWS = (2, 4, 8, 16)
GROUP_DIM_C = 64
CONV_WIDTH = 31
CHUNK = 128
D_FF = 2816
RMS_EPS = 1e-6
LN_EPS = 1e-5
DEPTH = 2
N_DEV = 8

ADAM_LR = 0.001
ADAM_B1 = 0.9
ADAM_B2 = 0.999
ADAM_EPS = 1e-08
ADAM_WD = 0.01
ADAM_STEP = 10

CONV_HALO = 32
POOL_HALO = 16

VMEM_LIMIT = 56 * 1024 * 1024

MESH_ID = pl.DeviceIdType.MESH


def _dot(a, b):
    return jnp.dot(a, b, preferred_element_type=F32)


def _dot_nt(a, b):
    return lax.dot_general(a, b, (((1,), (1,)), ((), ())), preferred_element_type=F32)


def _dot_tn(a, b):
    return lax.dot_general(a, b, (((0,), (0,)), ((), ())), preferred_element_type=F32)


def _sigmoid(x):
    return 1.0 / (1.0 + jnp.exp(-x))


_GELU_C = math.sqrt(2.0 / math.pi)


def _gelu_and_grad(x):
    x2 = x * x
    inner = _GELU_C * (x + 0.044715 * x2 * x)
    t = jnp.tanh(inner)
    g = 0.5 * x * (1.0 + t)
    dg = 0.5 * (1.0 + t) + 0.5 * x * (1.0 - t * t) * _GELU_C * (1.0 + 3.0 * 0.044715 * x2)
    return g, dg


def _ln_stats(x):
    mu = jnp.mean(x, axis=-1, keepdims=True)
    xc = x - mu
    var = jnp.mean(xc * xc, axis=-1, keepdims=True)
    rstd = lax.rsqrt(var + LN_EPS)
    return xc * rstd, rstd


def _ln_bwd(dy, xhat, rstd, g):
    dxhat = dy * g
    return rstd * (dxhat - jnp.mean(dxhat, axis=-1, keepdims=True)
                   - xhat * jnp.mean(dxhat * xhat, axis=-1, keepdims=True))


def _rms_bwd(dh, xn, r, g):
    dxn = dh * g
    return r * (dxn - xn * jnp.mean(dxn * xn, axis=-1, keepdims=True))


def _head_masks(width):
    lane = lax.broadcasted_iota(jnp.int32, (1, width), 1)
    return [(lane >= h * HEAD_DIM_B) & (lane < (h + 1) * HEAD_DIM_B) for h in range(N_HEADS_B)]


def _pool_select(vals, width):
    lane = lax.broadcasted_iota(jnp.int32, (1, width), 1)
    out = vals[-1]
    for g in range(len(vals) - 2, -1, -1):
        out = jnp.where(lane < (g + 1) * GROUP_DIM_C, vals[g], out)
    return out


def _pool_counts(pos):
    return _pool_select([jnp.minimum(pos + 1.0, float(w)) for w in POOL_WINDOWS], D_C)


def _full(shape):
    n = len(shape)
    return pl.BlockSpec(shape, lambda *_: (0,) * n)


def _params(sem):
    return pltpu.CompilerParams(dimension_semantics=sem, vmem_limit_bytes=VMEM_LIMIT)


def mixer_fwd(x, g1, winT, convw, convb, clng, clnb, wpw, slng, slnb, wm, bias, wbd, pscale, wout, *, seq, tm):
    T = x.shape[0]
    tiles_per_seq = seq // tm
    n_chunks = tm // CHUNK

    def body(x_ref, g1_ref, winT_ref, convw_ref, convb_ref, clng_ref, clnb_ref, wpw_ref, slng_ref, slnb_ref,
             wm_ref, bias_ref, wbd_ref, pscale_ref, wout_ref,
             z_ref, ycv_ref, p_ref, mix_ref, x1_ref, ybuf, zcbuf):
        i = pl.program_id(0)
        tile_in_seq = i % tiles_per_seq

        @pl.when(tile_in_seq == 0)
        def _():
            ybuf[0:CONV_HALO, :] = jnp.zeros((CONV_HALO, D_A), F32)
            zcbuf[0:POOL_HALO, :] = jnp.zeros((POOL_HALO, D_C), F32)

        x = x_ref[...]
        r = lax.rsqrt(jnp.mean(x * x, axis=-1, keepdims=True) + RMS_EPS)
        h = (x * r * g1_ref[...]).astype(BF16)
        z = _dot_nt(h, winT_ref[...])
        z_ref[...] = z

        y = z[:, 0:D_A] * _sigmoid(z[:, D_A:2 * D_A])
        ybuf[CONV_HALO:CONV_HALO + tm, :] = y
        acc = jnp.zeros((tm, D_A), F32) + convb_ref[...]
        for k in range(CONV_WIDTH):
            off = CONV_HALO - (CONV_WIDTH - 1) + k
            acc = acc + convw_ref[k:k + 1, :] * ybuf[off:off + tm, :]
        ybuf[0:CONV_HALO, :] = ybuf[tm:tm + CONV_HALO, :]
        ycv_ref[...] = acc
        xhat, _ = _ln_stats(acc)
        ln = xhat * clng_ref[...] + clnb_ref[...]
        s = ln * _sigmoid(ln)
        ya = _dot(s.astype(BF16), wpw_ref[...])

        gb, _ = _gelu_and_grad(z[:, 2 * D_A:2 * D_A + 2 * D_B])
        u = gb[:, 0:D_B]
        vhat, _ = _ln_stats(gb[:, D_B:2 * D_B])
        vn = vhat * slng_ref[...] + slnb_ref[...]
        masks = _head_masks(D_B)
        yb_parts = []
        for c in range(n_chunks):
            vn_c = vn[c * CHUNK:(c + 1) * CHUNK, :]
            sg = bias_ref[...]
            for hh in range(N_HEADS_B):
                sg = sg + _dot(wm_ref[hh], jnp.where(masks[hh], vn_c, 0.0).astype(BF16))
            yb_parts.append(u[c * CHUNK:(c + 1) * CHUNK, :] * sg)
        yb = jnp.concatenate(yb_parts, axis=0) if n_chunks > 1 else yb_parts[0]

        zc = z[:, 2 * D_A + 2 * D_B:D_IN]
        zcbuf[POOL_HALO:POOL_HALO + tm, :] = zc
        run = zc
        sums = []
        for j in range(1, POOL_WINDOWS[-1]):
            run = run + zcbuf[POOL_HALO - j:POOL_HALO - j + tm, :]
            if (j + 1) in POOL_WINDOWS:
                sums.append(run)
        zcbuf[0:POOL_HALO, :] = zcbuf[tm:tm + POOL_HALO, :]
        pos = (tile_in_seq * tm + lax.broadcasted_iota(jnp.int32, (tm, 1), 0)).astype(F32)
        p = _pool_select(sums, D_C) / _pool_counts(pos) - zc
        p_ref[...] = p
        yc = _dot(p.astype(BF16), wbd_ref[...]) * pscale_ref[...]

        mix = jnp.concatenate([ya, yb, yc], axis=1).astype(BF16)
        mix_ref[...] = mix
        x1_ref[...] = x + _dot(mix, wout_ref[...])

    row = lambda w: pl.BlockSpec((tm, w), lambda i: (i, 0))
    return pl.pallas_call(
        body, name="mixer_fwd", grid=(T // tm,),
        in_specs=[row(D_MODEL), _full((1, D_MODEL)), _full((D_IN, D_MODEL)), _full((CONV_WIDTH, D_A)),
                  _full((1, D_A)), _full((1, D_A)), _full((1, D_A)), _full((D_A, D_A)), _full((1, D_B)), _full((1, D_B)),
                  _full((N_HEADS_B, CHUNK, CHUNK)), _full((CHUNK, D_B)), _full((D_C, D_C)), _full((1, D_C)),
                  _full((D_MODEL, D_MODEL))],
        out_specs=[row(D_IN), row(D_A), row(D_C), row(D_MODEL), row(D_MODEL)],
        out_shape=[jax.ShapeDtypeStruct((T, D_IN), F32), jax.ShapeDtypeStruct((T, D_A), F32),
                   jax.ShapeDtypeStruct((T, D_C), F32), jax.ShapeDtypeStruct((T, D_MODEL), BF16),
                   jax.ShapeDtypeStruct((T, D_MODEL), F32)],
        scratch_shapes=[pltpu.VMEM((CONV_HALO + tm, D_A), F32), pltpu.VMEM((POOL_HALO + tm, D_C), F32)],
        compiler_params=_params(("arbitrary",)),
    )(x, g1, winT, convw, convb, clng, clnb, wpw, slng, slnb, wm, bias, wbd, pscale, wout)


def mixer_bwd(dx1, x, z, ycv, p, g1, winT, convw, clng, clnb, wpw, slng, slnb, wm, wmT, bias, wbd, pscale, wout,
              *, seq, tm):
    T = x.shape[0]
    tiles_per_seq = seq // tm
    n_tiles = T // tm
    n_chunks = tm // CHUNK

    def body(dx1_ref, x_ref, z_ref, ycv_ref, p_ref, g1_ref, winT_ref, convw_ref, clng_ref, clnb_ref, wpw_ref,
             slng_ref, slnb_ref, wm_ref, wmT_ref, bias_ref, wbd_ref, pscale_ref, wout_ref,
             dx_ref, dz_ref, h_ref, s_ref, dya_ref,
             dg1_ref, dconvw_ref, dconvb_ref, dclng_ref, dclnb_ref, dslng_ref, dslnb_ref, dwm_ref, dbs_ref,
             dwbd_ref, dpscale_ref, dycbuf, dpcbuf):
        i = pl.program_id(0)
        tile_in_seq = (n_tiles - 1 - i) % tiles_per_seq

        @pl.when(i == 0)
        def _():
            for ref in (dg1_ref, dconvw_ref, dconvb_ref, dclng_ref, dclnb_ref, dslng_ref, dslnb_ref, dwm_ref,
                        dbs_ref, dwbd_ref, dpscale_ref):
                ref[...] = jnp.zeros(ref.shape, F32)

        @pl.when(tile_in_seq == tiles_per_seq - 1)
        def _():
            dycbuf[tm:tm + CONV_HALO, :] = jnp.zeros((CONV_HALO, D_A), F32)
            dpcbuf[tm:tm + POOL_HALO, :] = jnp.zeros((POOL_HALO, D_C), F32)

        dx1 = dx1_ref[...]
        z = z_ref[...]
        dmix = _dot_nt(dx1.astype(BF16), wout_ref[...])
        dya = dmix[:, 0:D_A]
        dyb = dmix[:, D_A:D_A + D_B]
        dyc = dmix[:, D_A + D_B:D_MODEL]

        p = p_ref[...]
        pb = p.astype(BF16)
        q = _dot(pb, wbd_ref[...])
        dpscale_ref[...] += jnp.sum(dyc * q, axis=0, keepdims=True)
        dq = (dyc * pscale_ref[...]).astype(BF16)
        dwbd_ref[...] += _dot_tn(pb, dq)
        dp = _dot_nt(dq, wbd_ref[...])
        pos = (tile_in_seq * tm + lax.broadcasted_iota(jnp.int32, (tm, 1), 0)).astype(F32)
        dpc = dp / _pool_counts(pos)
        dpcbuf[0:tm, :] = dpc
        run = dpc
        sums = []
        for j in range(1, POOL_WINDOWS[-1]):
            run = run + dpcbuf[j:j + tm, :]
            if (j + 1) in POOL_WINDOWS:
                sums.append(run)
        dpcbuf[tm:tm + POOL_HALO, :] = dpcbuf[0:POOL_HALO, :]
        dzc = _pool_select(sums, D_C) - dp

        dya_ref[...] = dya.astype(BF16)
        ds = _dot_nt(dya.astype(BF16), wpw_ref[...])
        xhat, rstd = _ln_stats(ycv_ref[...])
        ln = xhat * clng_ref[...] + clnb_ref[...]
        sg = _sigmoid(ln)
        s_ref[...] = (ln * sg).astype(BF16)
        dln = ds * (sg * (1.0 + ln * (1.0 - sg)))
        dclng_ref[...] += jnp.sum(dln * xhat, axis=0, keepdims=True)
        dclnb_ref[...] += jnp.sum(dln, axis=0, keepdims=True)
        dycv = _ln_bwd(dln, xhat, rstd, clng_ref[...])
        dconvb_ref[...] += jnp.sum(dycv, axis=0, keepdims=True)
        a = z[:, 0:D_A]
        sgate = _sigmoid(z[:, D_A:2 * D_A])
        y = a * sgate
        dycbuf[0:tm, :] = dycv
        dy = jnp.zeros((tm, D_A), F32)
        for d in range(CONV_WIDTH):
            k = CONV_WIDTH - 1 - d
            sh = dycbuf[d:d + tm, :]
            dy = dy + convw_ref[k:k + 1, :] * sh
            dconvw_ref[k:k + 1, :] += jnp.sum(y * sh, axis=0, keepdims=True)
        dycbuf[tm:tm + CONV_HALO, :] = dycbuf[0:CONV_HALO, :]
        da = dy * sgate
        dgate = dy * a * sgate * (1.0 - sgate)

        gb, dgb = _gelu_and_grad(z[:, 2 * D_A:2 * D_A + 2 * D_B])
        u = gb[:, 0:D_B]
        vhat, vrstd = _ln_stats(gb[:, D_B:2 * D_B])
        vn = vhat * slng_ref[...] + slnb_ref[...]
        masks = _head_masks(D_B)
        tril = (lax.broadcasted_iota(jnp.int32, (CHUNK, CHUNK), 0)
                >= lax.broadcasted_iota(jnp.int32, (CHUNK, CHUNK), 1))
        lane128 = lax.broadcasted_iota(jnp.int32, (1, CHUNK), 1)
        du_parts, dvn_parts = [], []
        for c in range(n_chunks):
            rows = slice(c * CHUNK, (c + 1) * CHUNK)
            vn_c = vn[rows, :]
            vh = [jnp.where(masks[hh], vn_c, 0.0).astype(BF16) for hh in range(N_HEADS_B)]
            sgc = bias_ref[...]
            for hh in range(N_HEADS_B):
                sgc = sgc + _dot(wm_ref[hh], vh[hh])
            dyb_c = dyb[rows, :]
            du_parts.append(dyb_c * sgc)
            dsg = dyb_c * u[rows, :]
            dvn_c = jnp.zeros((CHUNK, D_B), F32)
            dbs = jnp.zeros((CHUNK, CHUNK), F32)
            for hh in range(N_HEADS_B):
                dsg_h = jnp.where(masks[hh], dsg, 0.0)
                dsg_hb = dsg_h.astype(BF16)
                dwm_ref[hh] += jnp.where(tril, _dot_nt(dsg_hb, vh[hh]), 0.0)
                dvn_c = dvn_c + _dot(wmT_ref[hh], dsg_hb)
                dbs = dbs + jnp.where(lane128 == hh, jnp.sum(dsg_h, axis=1, keepdims=True), 0.0)
            dbs_ref[...] += dbs
            dvn_parts.append(dvn_c)
        du = jnp.concatenate(du_parts, axis=0) if n_chunks > 1 else du_parts[0]
        dvn = jnp.concatenate(dvn_parts, axis=0) if n_chunks > 1 else dvn_parts[0]
        dslng_ref[...] += jnp.sum(dvn * vhat, axis=0, keepdims=True)
        dslnb_ref[...] += jnp.sum(dvn, axis=0, keepdims=True)
        dv = _ln_bwd(dvn, vhat, vrstd, slng_ref[...])
        dzb = jnp.concatenate([du, dv], axis=1) * dgb

        dz = jnp.concatenate([da, dgate, dzb, dzc], axis=1).astype(BF16)
        dz_ref[...] = dz
        dh = _dot(dz, winT_ref[...])
        x = x_ref[...]
        r = lax.rsqrt(jnp.mean(x * x, axis=-1, keepdims=True) + RMS_EPS)
        xn = x * r
        h_ref[...] = (xn * g1_ref[...]).astype(BF16)
        dg1_ref[...] += jnp.sum(dh * xn, axis=0, keepdims=True)
        dx_ref[...] = dx1 + _rms_bwd(dh, xn, r, g1_ref[...])

    row = lambda w: pl.BlockSpec((tm, w), lambda i: (n_tiles - 1 - i, 0))
    acc_shapes = [(1, D_MODEL), (CONV_WIDTH, D_A), (1, D_A), (1, D_A), (1, D_A), (1, D_B), (1, D_B),
                  (N_HEADS_B, CHUNK, CHUNK), (CHUNK, CHUNK), (D_C, D_C), (1, D_C)]
    return pl.pallas_call(
        body, name="mixer_bwd", grid=(n_tiles,),
        in_specs=[row(D_MODEL), row(D_MODEL), row(D_IN), row(D_A), row(D_C),
                  _full((1, D_MODEL)), _full((D_IN, D_MODEL)), _full((CONV_WIDTH, D_A)), _full((1, D_A)), _full((1, D_A)),
                  _full((D_A, D_A)), _full((1, D_B)), _full((1, D_B)), _full((N_HEADS_B, CHUNK, CHUNK)),
                  _full((N_HEADS_B, CHUNK, CHUNK)), _full((CHUNK, D_B)), _full((D_C, D_C)), _full((1, D_C)),
                  _full((D_MODEL, D_MODEL))],
        out_specs=[row(D_MODEL), row(D_IN), row(D_MODEL), row(D_A), row(D_A)] + [_full(s) for s in acc_shapes],
        out_shape=[jax.ShapeDtypeStruct((T, D_MODEL), F32), jax.ShapeDtypeStruct((T, D_IN), BF16),
                   jax.ShapeDtypeStruct((T, D_MODEL), BF16), jax.ShapeDtypeStruct((T, D_A), BF16),
                   jax.ShapeDtypeStruct((T, D_A), BF16)] + [jax.ShapeDtypeStruct(s, F32) for s in acc_shapes],
        scratch_shapes=[pltpu.VMEM((tm + CONV_HALO, D_A), F32), pltpu.VMEM((tm + POOL_HALO, D_C), F32)],
        compiler_params=_params(("arbitrary",)),
    )(dx1, x, z, ycv, p, g1, winT, convw, clng, clnb, wpw, slng, slnb, wm, wmT, bias, wbd, pscale, wout)


def ffn_fwd(x1, g2, wguT, wd, *, tm, th):
    T = x1.shape[0]
    n_h = D_FF // th

    def body(x1_ref, g2_ref, wgu_ref, wd_ref, x2_ref, gu_ref, h2_buf, acc):
        j = pl.program_id(1)

        @pl.when(j == 0)
        def _():
            x = x1_ref[...]
            r = lax.rsqrt(jnp.mean(x * x, axis=-1, keepdims=True) + RMS_EPS)
            h2_buf[...] = (x * r * g2_ref[...]).astype(BF16)
            acc[...] = x

        h2 = h2_buf[...]
        g = _dot_nt(h2, wgu_ref[0])
        u = _dot_nt(h2, wgu_ref[1])
        gu_ref[0] = g.astype(BF16)
        gu_ref[1] = u.astype(BF16)
        f = (g * _sigmoid(g) * u).astype(BF16)
        acc[...] += _dot(f, wd_ref[...])

        @pl.when(j == n_h - 1)
        def _():
            x2_ref[...] = acc[...]

    return pl.pallas_call(
        body, name="ffn_fwd", grid=(T // tm, n_h),
        in_specs=[pl.BlockSpec((tm, D_MODEL), lambda i, j: (i, 0)), _full((1, D_MODEL)),
                  pl.BlockSpec((2, th, D_MODEL), lambda i, j: (0, j, 0)),
                  pl.BlockSpec((th, D_MODEL), lambda i, j: (j, 0))],
        out_specs=[pl.BlockSpec((tm, D_MODEL), lambda i, j: (i, 0)),
                   pl.BlockSpec((2, tm, th), lambda i, j: (0, i, j))],
        out_shape=[jax.ShapeDtypeStruct((T, D_MODEL), F32), jax.ShapeDtypeStruct((2, T, D_FF), BF16)],
        scratch_shapes=[pltpu.VMEM((tm, D_MODEL), BF16), pltpu.VMEM((tm, D_MODEL), F32)],
        compiler_params=_params(("arbitrary", "arbitrary")),
    )(x1, g2, wguT, wd)


def ffn_bwd(dx2, x1, gu, g2, wguT, wd, *, tm, th):
    T = x1.shape[0]
    n_h = D_FF // th

    def body(dx2_ref, x1_ref, gu_ref, g2_ref, wgu_ref, wd_ref, dx1_ref, h2_ref, f_ref, dgu_ref, dg2_ref, acc):
        i = pl.program_id(0)
        j = pl.program_id(1)

        @pl.when((i == 0) & (j == 0))
        def _():
            dg2_ref[...] = jnp.zeros(dg2_ref.shape, F32)

        dx2 = dx2_ref[...]
        df = _dot_nt(dx2.astype(BF16), wd_ref[...])
        g = gu_ref[0].astype(F32)
        u = gu_ref[1].astype(F32)
        sg = _sigmoid(g)
        silu = g * sg
        f_ref[...] = (silu * u).astype(BF16)
        dgate = (df * u * (sg * (1.0 + g * (1.0 - sg)))).astype(BF16)
        dup = (df * silu).astype(BF16)
        dgu_ref[0] = dgate
        dgu_ref[1] = dup
        part = _dot(dgate, wgu_ref[0]) + _dot(dup, wgu_ref[1])

        @pl.when(j == 0)
        def _():
            acc[...] = part

        @pl.when(j > 0)
        def _():
            acc[...] += part

        @pl.when(j == n_h - 1)
        def _():
            x = x1_ref[...]
            r = lax.rsqrt(jnp.mean(x * x, axis=-1, keepdims=True) + RMS_EPS)
            xn = x * r
            dh = acc[...]
            h2_ref[...] = (xn * g2_ref[...]).astype(BF16)
            dg2_ref[...] += jnp.sum(dh * xn, axis=0, keepdims=True)
            dx1_ref[...] = dx2 + _rms_bwd(dh, xn, r, g2_ref[...])

    return pl.pallas_call(
        body, name="ffn_bwd", grid=(T // tm, n_h),
        in_specs=[pl.BlockSpec((tm, D_MODEL), lambda i, j: (i, 0)), pl.BlockSpec((tm, D_MODEL), lambda i, j: (i, 0)),
                  pl.BlockSpec((2, tm, th), lambda i, j: (0, i, j)), _full((1, D_MODEL)),
                  pl.BlockSpec((2, th, D_MODEL), lambda i, j: (0, j, 0)),
                  pl.BlockSpec((th, D_MODEL), lambda i, j: (j, 0))],
        out_specs=[pl.BlockSpec((tm, D_MODEL), lambda i, j: (i, 0)), pl.BlockSpec((tm, D_MODEL), lambda i, j: (i, 0)),
                   pl.BlockSpec((tm, th), lambda i, j: (i, j)), pl.BlockSpec((2, tm, th), lambda i, j: (0, i, j)),
                   _full((1, D_MODEL))],
        out_shape=[jax.ShapeDtypeStruct((T, D_MODEL), F32), jax.ShapeDtypeStruct((T, D_MODEL), BF16),
                   jax.ShapeDtypeStruct((T, D_FF), BF16), jax.ShapeDtypeStruct((2, T, D_FF), BF16),
                   jax.ShapeDtypeStruct((1, D_MODEL), F32)],
        scratch_shapes=[pltpu.VMEM((tm, D_MODEL), F32)],
        compiler_params=_params(("arbitrary", "arbitrary")),
    )(dx2, x1, gu, g2, wguT, wd)


def head_fwd_bwd(x, target, fg, *, tm):
    T = x.shape[0]
    n_tiles = T // tm

    def body(x_ref, t_ref, fg_ref, loss_ref, dx_ref, dfg_ref, lacc):
        i = pl.program_id(0)

        @pl.when(i == 0)
        def _():
            lacc[...] = jnp.zeros(lacc.shape, F32)
            dfg_ref[...] = jnp.zeros(dfg_ref.shape, F32)

        x = x_ref[...]
        r = lax.rsqrt(jnp.mean(x * x, axis=-1, keepdims=True) + RMS_EPS)
        xn = x * r
        e = xn * fg_ref[...] - t_ref[...]
        lacc[...] += jnp.sum(e * e, axis=0, keepdims=True)
        dy = e * (1.0 / D_MODEL)
        dfg_ref[...] += jnp.sum(dy * xn, axis=0, keepdims=True)
        dx_ref[...] = _rms_bwd(dy, xn, r, fg_ref[...])

        @pl.when(i == n_tiles - 1)
        def _():
            loss_ref[...] = jnp.sum(lacc[...], axis=1, keepdims=True) * (0.5 / D_MODEL)

    row = pl.BlockSpec((tm, D_MODEL), lambda i: (i, 0))
    return pl.pallas_call(
        body, name="head_fwd_bwd", grid=(n_tiles,),
        in_specs=[row, row, _full((1, D_MODEL))],
        out_specs=[_full((1, 1)), row, _full((1, D_MODEL))],
        out_shape=[jax.ShapeDtypeStruct((1, 1), F32), jax.ShapeDtypeStruct((T, D_MODEL), F32),
                   jax.ShapeDtypeStruct((1, D_MODEL), F32)],
        scratch_shapes=[pltpu.VMEM((1, D_MODEL), F32)],
        compiler_params=_params(("arbitrary",)),
    )(x, target, fg)


def wgrad(a, b, *, tmo, tk, name):
    G, T, M = a.shape
    N = b.shape[1]
    n_k = T // tk

    def body(a_ref, b_ref, o_ref, acc):
        k = pl.program_id(2)
        part = _dot_tn(a_ref[0].astype(BF16), b_ref[...].astype(BF16))

        @pl.when(k == 0)
        def _():
            acc[...] = part

        @pl.when(k > 0)
        def _():
            acc[...] += part

        @pl.when(k == n_k - 1)
        def _():
            o_ref[0] = acc[...].astype(BF16)

    return pl.pallas_call(
        body, name=name, grid=(G, M // tmo, n_k),
        in_specs=[pl.BlockSpec((1, tk, tmo), lambda g, m, k: (g, k, m)),
                  pl.BlockSpec((tk, N), lambda g, m, k: (k, 0))],
        out_specs=pl.BlockSpec((1, tmo, N), lambda g, m, k: (g, m, 0)),
        out_shape=jax.ShapeDtypeStruct((G, M, N), BF16),
        scratch_shapes=[pltpu.VMEM((tmo, N), F32)],
        compiler_params=_params(("arbitrary", "arbitrary", "arbitrary")),
    )(a, b)


def _my_coords():
    return lax.axis_index("x"), lax.axis_index("y"), lax.axis_index("c")


def _peer(me, rel):
    x, y, c = me
    bx, by, bc = (rel >> 2) & 1, (rel >> 1) & 1, rel & 1
    return (x ^ bx if bx else x, y ^ by if by else y, c ^ bc if bc else c)


def _index_of(dev):
    return 4 * dev[0] + 2 * dev[1] + dev[2]


def all_gather_rows(shards, *, name):
    n = len(shards)

    def body(*refs):
        ins, outs = refs[:n], refs[n:2 * n]
        send_sems, recv_sems, local_sems = refs[2 * n:]
        me = _my_coords()
        mine = _index_of(me)
        local = [pltpu.make_async_copy(ins[k], outs[k].at[mine], local_sems.at[k]) for k in range(n)]
        for cp in local:
            cp.start()
        sends = []
        for rel in range(1, N_DEV):
            to = _peer(me, rel)
            for k in range(n):
                cp = pltpu.make_async_remote_copy(
                    src_ref=ins[k], dst_ref=outs[k].at[mine], send_sem=send_sems.at[rel - 1, k],
                    recv_sem=recv_sems.at[rel - 1, k], device_id=to, device_id_type=MESH_ID)
                cp.start()
                sends.append(cp)
        for rel in range(1, N_DEV):
            frm = _peer(me, rel)
            for k in range(n):
                pltpu.make_async_remote_copy(
                    src_ref=ins[k], dst_ref=outs[k].at[_index_of(frm)], send_sem=send_sems.at[rel - 1, k],
                    recv_sem=recv_sems.at[rel - 1, k], device_id=frm, device_id_type=MESH_ID).wait_recv()
        for cp in sends:
            cp.wait_send()
        for cp in local:
            cp.wait()

    any_spec = pl.BlockSpec(memory_space=pl.ANY)
    return pl.pallas_call(
        body, name=name,
        in_specs=[any_spec] * n, out_specs=[any_spec] * n,
        out_shape=[jax.ShapeDtypeStruct((N_DEV,) + s.shape, s.dtype) for s in shards],
        scratch_shapes=[pltpu.SemaphoreType.DMA((N_DEV - 1, n)), pltpu.SemaphoreType.DMA((N_DEV - 1, n)),
                        pltpu.SemaphoreType.DMA((n,))],
        compiler_params=pltpu.CompilerParams(has_side_effects=True),
    )(*shards)


def exchange_blocks(groups, *, name):
    flat = [(gi, li) for gi, grp in enumerate(groups) for li in range(len(grp))]
    n, n_out = len(flat), len(groups)

    def body(*refs):
        ins, outs = refs[:n], refs[n:n + n_out]
        send_sems, recv_sems, local_sems = refs[n + n_out:]
        me = _my_coords()
        mine = _index_of(me)
        dst = lambda k, slot: outs[flat[k][0]].at[flat[k][1], slot]
        local = [pltpu.make_async_copy(ins[k].at[mine], dst(k, mine), local_sems.at[k]) for k in range(n)]
        for cp in local:
            cp.start()
        sends = []
        for rel in range(1, N_DEV):
            to = _peer(me, rel)
            for k in range(n):
                cp = pltpu.make_async_remote_copy(
                    src_ref=ins[k].at[_index_of(to)], dst_ref=dst(k, mine), send_sem=send_sems.at[rel - 1, k],
                    recv_sem=recv_sems.at[rel - 1, k], device_id=to, device_id_type=MESH_ID)
                cp.start()
                sends.append(cp)
        for rel in range(1, N_DEV):
            frm = _peer(me, rel)
            for k in range(n):
                pltpu.make_async_remote_copy(
                    src_ref=ins[k].at[mine], dst_ref=dst(k, _index_of(frm)), send_sem=send_sems.at[rel - 1, k],
                    recv_sem=recv_sems.at[rel - 1, k], device_id=frm, device_id_type=MESH_ID).wait_recv()
        for cp in sends:
            cp.wait_send()
        for cp in local:
            cp.wait()

    any_spec = pl.BlockSpec(memory_space=pl.ANY)
    return pl.pallas_call(
        body, name=name,
        in_specs=[any_spec] * n, out_specs=[any_spec] * n_out,
        out_shape=[jax.ShapeDtypeStruct((len(grp),) + grp[0].shape, grp[0].dtype) for grp in groups],
        scratch_shapes=[pltpu.SemaphoreType.DMA((N_DEV - 1, n)), pltpu.SemaphoreType.DMA((N_DEV - 1, n)),
                        pltpu.SemaphoreType.DMA((n,))],
        compiler_params=pltpu.CompilerParams(has_side_effects=True),
    )(*[a for grp in groups for a in grp])


def all_reduce_small(buf):
    _, R, W = buf.shape

    def body(in_ref, out_ref, land, send1, recv1, send2, recv2):
        me = _my_coords()
        mine = _index_of(me)
        sends = []
        for rel in range(1, N_DEV):
            to = _peer(me, rel)
            cp = pltpu.make_async_remote_copy(
                src_ref=in_ref.at[_index_of(to)], dst_ref=land.at[mine], send_sem=send1.at[rel - 1],
                recv_sem=recv1.at[rel - 1], device_id=to, device_id_type=MESH_ID)
            cp.start()
            sends.append(cp)
        land[mine] = in_ref[mine]
        for rel in range(1, N_DEV):
            frm = _peer(me, rel)
            pltpu.make_async_remote_copy(
                src_ref=in_ref.at[mine], dst_ref=land.at[_index_of(frm)], send_sem=send1.at[rel - 1],
                recv_sem=recv1.at[rel - 1], device_id=frm, device_id_type=MESH_ID).wait_recv()
        total = land[0]
        for d in range(1, N_DEV):
            total = total + land[d]
        out_ref[mine] = total
        for rel in range(1, N_DEV):
            to = _peer(me, rel)
            cp = pltpu.make_async_remote_copy(
                src_ref=out_ref.at[mine], dst_ref=out_ref.at[mine], send_sem=send2.at[rel - 1],
                recv_sem=recv2.at[rel - 1], device_id=to, device_id_type=MESH_ID)
            cp.start()
            sends.append(cp)
        for rel in range(1, N_DEV):
            frm = _peer(me, rel)
            pltpu.make_async_remote_copy(
                src_ref=out_ref.at[mine], dst_ref=out_ref.at[_index_of(frm)], send_sem=send2.at[rel - 1],
                recv_sem=recv2.at[rel - 1], device_id=frm, device_id_type=MESH_ID).wait_recv()
        for cp in sends:
            cp.wait_send()

    vmem = pl.BlockSpec(memory_space=pltpu.VMEM)
    return pl.pallas_call(
        body, name="all_reduce_small",
        in_specs=[vmem], out_specs=vmem,
        out_shape=jax.ShapeDtypeStruct(buf.shape, F32),
        scratch_shapes=[pltpu.VMEM(buf.shape, F32)] + [pltpu.SemaphoreType.DMA((N_DEV - 1,))] * 4,
        compiler_params=pltpu.CompilerParams(has_side_effects=True, vmem_limit_bytes=VMEM_LIMIT),
    )(buf)


_ADAM_C1 = 1.0 - ADAM_B1 ** ADAM_STEP
_ADAM_C2 = 1.0 - ADAM_B2 ** ADAM_STEP


def _adamw_math(w, g, m, v):
    m = ADAM_B1 * m + (1.0 - ADAM_B1) * g
    v = ADAM_B2 * v + (1.0 - ADAM_B2) * (g * g)
    m_hat = m / _ADAM_C1
    v_hat = v / _ADAM_C2
    delta = -ADAM_LR * (m_hat / (jnp.sqrt(v_hat) + ADAM_EPS) + ADAM_WD * w)
    return delta, m, v


def adamw_sharded(parts, w, m, v, *, transposed, tr, name):
    L, _, R, C = parts.shape

    def body(p_ref, w_ref, m_ref, v_ref, g_ref, d_ref, nm_ref, nv_ref):
        g = p_ref[0, 0].astype(F32)
        for d in range(1, N_DEV):
            g = g + p_ref[0, d].astype(F32)
        if transposed:
            g = g.T
        delta, nm, nv = _adamw_math(w_ref[0], g, m_ref[0], v_ref[0])
        g_ref[0] = g
        d_ref[0] = delta
        nm_ref[0] = nm
        nv_ref[0] = nv

    if transposed:
        p_spec = pl.BlockSpec((1, N_DEV, R, tr), lambda l, i: (l, 0, 0, i))
        o_spec = pl.BlockSpec((1, tr, R), lambda l, i: (l, i, 0))
        grid = (L, C // tr)
    else:
        p_spec = pl.BlockSpec((1, N_DEV, tr, C), lambda l, i: (l, 0, i, 0))
        o_spec = pl.BlockSpec((1, tr, C), lambda l, i: (l, i, 0))
        grid = (L, R // tr)
    return pl.pallas_call(
        body, name=name, grid=grid,
        in_specs=[p_spec, o_spec, o_spec, o_spec], out_specs=[o_spec] * 4,
        out_shape=[jax.ShapeDtypeStruct(w.shape, F32)] * 4,
        compiler_params=_params(("arbitrary", "arbitrary")),
    )(parts, w, m, v)


def adamw_small(g, w, m, v):
    def body(g_ref, w_ref, m_ref, v_ref, d_ref, nm_ref, nv_ref):
        delta, nm, nv = _adamw_math(w_ref[...], g_ref[...], m_ref[...], v_ref[...])
        d_ref[...] = delta
        nm_ref[...] = nm
        nv_ref[...] = nv

    vmem = pl.BlockSpec(memory_space=pltpu.VMEM)
    return pl.pallas_call(
        body, name="adamw_small", in_specs=[vmem] * 4, out_specs=[vmem] * 3,
        out_shape=[jax.ShapeDtypeStruct(g.shape, F32)] * 3,
        compiler_params=pltpu.CompilerParams(vmem_limit_bytes=VMEM_LIMIT),
    )(g, w, m, v)


LANES = 128
SUBLANES = 8


def _pack(arrays, row_multiple):
    flat = jnp.concatenate([a.reshape(-1) for a in arrays])
    rows = -(-flat.shape[0] // (LANES * row_multiple)) * row_multiple
    return jnp.pad(flat, (0, rows * LANES - flat.shape[0])).reshape(rows, LANES)


def _unpack(buf, shapes):
    flat = buf.reshape(-1)
    out, off = [], 0
    for s in shapes:
        n = math.prod(s)
        out.append(flat[off:off + n].reshape(s))
        off += n
    return out


def _block_diag(w_pool):
    G, d, _ = w_pool.shape
    eye = jnp.eye(G, dtype=w_pool.dtype)
    return (eye[:, None, :, None] * w_pool[:, :, None, :]).reshape(G * d, G * d)


def kernel(x, norm1_g, w_in, conv_w, conv_b, conv_ln_g, conv_ln_b, w_pw, sg_ln_g, sg_ln_b, w_s, b_s, w_pool, pool_scale, w_out, norm2_g, w_gate_up, w_down, final_g, loss_target, m_norm1_g, m_w_in, m_conv_w, m_conv_b, m_conv_ln_g, m_conv_ln_b, m_w_pw, m_sg_ln_g, m_sg_ln_b, m_w_s, m_b_s, m_w_pool, m_pool_scale, m_w_out, m_norm2_g, m_w_gate_up, m_w_down, m_final_g, v_norm1_g, v_w_in, v_conv_w, v_conv_b, v_conv_ln_g, v_conv_ln_b, v_w_pw, v_sg_ln_g, v_sg_ln_b, v_w_s, v_b_s, v_w_pool, v_pool_scale, v_w_out, v_norm2_g, v_w_gate_up, v_w_down, v_final_g):
    b_loc, seq, _ = x.shape
    T = b_loc * seq
    tm_mix = min(256, seq)
    tm_ffn = min(256, T)
    tm_head = min(512, T)
    tk = min(512, T)
    th = D_FF // 2
    cw = conv_w.shape[2]
    my_index = _index_of(_my_coords())

    xf = x.reshape(T, D_MODEL)
    tgt = loss_target.reshape(T, D_MODEL)

    shards = []
    for l in range(DEPTH):
        shards += [w_in[l].T.astype(BF16), w_gate_up[l].T.astype(BF16), w_down[l].astype(BF16),
                   w_out[l].astype(BF16), w_pw[l].astype(BF16)]
    shards.append(conv_w.reshape(DEPTH * CONV_WIDTH, cw).T)
    gathered = all_gather_rows(shards, name="gather_weights")
    convw_full = gathered[-1].reshape(D_A, DEPTH * CONV_WIDTH).T.reshape(DEPTH, CONV_WIDTH, D_A)

    tril = jnp.tril(jnp.ones((CHUNK, CHUNK), dtype=bool))
    layers = []
    for l in range(DEPTH):
        g_in, g_gu, g_d, g_out, g_pw = gathered[5 * l:5 * l + 5]
        wm = jnp.where(tril[None], w_s[l], 0.0).astype(BF16)
        layers.append(dict(
            winT=g_in.reshape(D_IN, D_MODEL), wguT=g_gu.reshape(2, D_FF, D_MODEL), wd=g_d.reshape(D_FF, D_MODEL),
            wout=g_out.reshape(D_MODEL, D_MODEL), wpw=g_pw.reshape(D_A, D_A), convw=convw_full[l],
            g1=norm1_g[l][None], convb=conv_b[l][None], clng=conv_ln_g[l][None], clnb=conv_ln_b[l][None],
            slng=sg_ln_g[l][None], slnb=sg_ln_b[l][None], wm=wm, wmT=jnp.swapaxes(wm, 1, 2),
            bias=jnp.repeat(b_s[l].T, HEAD_DIM_B, axis=1), wbd=_block_diag(w_pool[l]).astype(BF16),
            pscale=pool_scale[l][None], g2=norm2_g[l][None]))

    saved = []
    cur = xf
    for l in range(DEPTH):
        w = layers[l]
        z, ycv, p, mix, x1 = mixer_fwd(cur, w["g1"], w["winT"], w["convw"], w["convb"], w["clng"], w["clnb"], w["wpw"],
                                       w["slng"], w["slnb"], w["wm"], w["bias"], w["wbd"], w["pscale"], w["wout"],
                                       seq=seq, tm=tm_mix)
        x2, gu = ffn_fwd(x1, w["g2"], w["wguT"], w["wd"], tm=tm_ffn, th=th)
        saved.append((cur, z, ycv, p, mix, x1, gu))
        cur = x2
    loss_part, dx, dfg = head_fwd_bwd(cur, tgt, final_g[None], tm=tm_head)
    loss = lax.psum(loss_part[0, 0], ("x", "y", "c"))

    big = [None] * DEPTH
    small = [None] * DEPTH
    for l in reversed(range(DEPTH)):
        w = layers[l]
        x0, z, ycv, p, mix, x1, gu = saved[l]
        dx1, h2, f, dgu, dg2 = ffn_bwd(dx, x1, gu, w["g2"], w["wguT"], w["wd"], tm=tm_ffn, th=th)
        gw_gu = wgrad(dgu, h2, tmo=th, tk=tk, name="wgrad_gate_up")
        gw_d = wgrad(f[None], dx, tmo=th, tk=tk, name="wgrad_down")
        (dx, dz, h, s, dya, dg1, dconvw, dconvb, dclng, dclnb, dslng, dslnb, dwm, dbs, dwbd, dpscale) = mixer_bwd(
            dx1, x0, z, ycv, p, w["g1"], w["winT"], w["convw"], w["clng"], w["clnb"], w["wpw"], w["slng"], w["slnb"],
            w["wm"], w["wmT"], w["bias"], w["wbd"], w["pscale"], w["wout"], seq=seq, tm=tm_mix)
        gw_out = wgrad(mix[None], dx1, tmo=D_MODEL, tk=tk, name="wgrad_out")
        gw_in = wgrad(dz[None], h, tmo=D_IN // 2, tk=tk, name="wgrad_in")
        gw_pw = wgrad(s[None], dya, tmo=D_A, tk=tk, name="wgrad_pw")
        big[l] = (gw_in.reshape(N_DEV, D_IN // N_DEV, D_MODEL), gw_gu.reshape(N_DEV, 2 * D_FF // N_DEV, D_MODEL),
                  gw_d.reshape(N_DEV, D_FF // N_DEV, D_MODEL), gw_out.reshape(N_DEV, D_MODEL // N_DEV, D_MODEL),
                  gw_pw.reshape(N_DEV, D_A // N_DEV, D_A))
        small[l] = (dg1, dconvw, dconvb, dclng, dclnb, dslng, dslnb, dwm, dbs, dwbd, dpscale, dg2)
    grad_x = dx.reshape(x.shape)

    p_in, p_gu, p_d, p_out, p_pw = exchange_blocks([[big[l][k] for l in range(DEPTH)] for k in range(5)],
                                                   name="exchange_grads")
    small_arrays = [a for l in range(DEPTH) for a in small[l]] + [dfg]
    small_shapes = [a.shape for a in small_arrays]
    packed = _pack(small_arrays, N_DEV * SUBLANES)
    summed = all_reduce_small(packed.reshape(N_DEV, packed.shape[0] // N_DEV, LANES))
    sums = _unpack(summed, small_shapes)
    per_layer = len(small[0])
    g_small = {k: [] for k in ("norm1_g", "conv_w", "conv_b", "conv_ln_g", "conv_ln_b", "sg_ln_g", "sg_ln_b", "w_s", "b_s",
                               "w_pool", "pool_scale", "norm2_g")}
    for l in range(DEPTH):
        dg1, dconvw, dconvb, dclng, dclnb, dslng, dslnb, dwm, dbs, dwbd, dpscale, dg2 = sums[per_layer * l:per_layer * (l + 1)]
        g_small["norm1_g"].append(dg1[0])
        g_small["conv_w"].append(lax.dynamic_slice_in_dim(dconvw, my_index * cw, cw, axis=1))
        g_small["conv_b"].append(dconvb[0])
        g_small["conv_ln_g"].append(dclng[0])
        g_small["conv_ln_b"].append(dclnb[0])
        g_small["sg_ln_g"].append(dslng[0])
        g_small["sg_ln_b"].append(dslnb[0])
        g_small["w_s"].append(dwm)
        g_small["b_s"].append(dbs[:, :N_HEADS_B].T)
        g_small["w_pool"].append(jnp.stack([dwbd[g * GROUP_DIM_C:(g + 1) * GROUP_DIM_C, g * GROUP_DIM_C:(g + 1) * GROUP_DIM_C]
                                            for g in range(len(POOL_WINDOWS))]))
        g_small["pool_scale"].append(dpscale[0])
        g_small["norm2_g"].append(dg2[0])
    g_small = {k: jnp.stack(v) for k, v in g_small.items()}
    g_small["final_g"] = sums[-1][0]

    g_w_in, d_w_in, nm_w_in, nv_w_in = adamw_sharded(p_in, w_in, m_w_in, v_w_in, transposed=True, tr=256, name="adamw_w_in")
    g_w_gu, d_w_gu, nm_w_gu, nv_w_gu = adamw_sharded(p_gu, w_gate_up, m_w_gate_up, v_w_gate_up, transposed=True, tr=256,
                                                     name="adamw_w_gate_up")
    g_w_d, d_w_d, nm_w_d, nv_w_d = adamw_sharded(p_d, w_down, m_w_down, v_w_down, transposed=False, tr=D_FF // N_DEV // 2,
                                                 name="adamw_w_down")
    g_w_out, d_w_out, nm_w_out, nv_w_out = adamw_sharded(p_out, w_out, m_w_out, v_w_out, transposed=False,
                                                         tr=D_MODEL // N_DEV, name="adamw_w_out")
    g_w_pw, d_w_pw, nm_w_pw, nv_w_pw = adamw_sharded(p_pw, w_pw, m_w_pw, v_w_pw, transposed=False, tr=D_A // N_DEV,
                                                     name="adamw_w_pw")

    small_names = ["norm1_g", "conv_w", "conv_b", "conv_ln_g", "conv_ln_b", "sg_ln_g", "sg_ln_b", "w_s", "b_s", "w_pool",
                   "pool_scale", "norm2_g", "final_g"]
    small_w = dict(norm1_g=norm1_g, conv_w=conv_w, conv_b=conv_b, conv_ln_g=conv_ln_g, conv_ln_b=conv_ln_b, sg_ln_g=sg_ln_g,
                   sg_ln_b=sg_ln_b, w_s=w_s, b_s=b_s, w_pool=w_pool, pool_scale=pool_scale, norm2_g=norm2_g, final_g=final_g)
    small_m = dict(norm1_g=m_norm1_g, conv_w=m_conv_w, conv_b=m_conv_b, conv_ln_g=m_conv_ln_g, conv_ln_b=m_conv_ln_b,
                   sg_ln_g=m_sg_ln_g, sg_ln_b=m_sg_ln_b, w_s=m_w_s, b_s=m_b_s, w_pool=m_w_pool, pool_scale=m_pool_scale,
                   norm2_g=m_norm2_g, final_g=m_final_g)
    small_v = dict(norm1_g=v_norm1_g, conv_w=v_conv_w, conv_b=v_conv_b, conv_ln_g=v_conv_ln_g, conv_ln_b=v_conv_ln_b,
                   sg_ln_g=v_sg_ln_g, sg_ln_b=v_sg_ln_b, w_s=v_w_s, b_s=v_b_s, w_pool=v_w_pool, pool_scale=v_pool_scale,
                   norm2_g=v_norm2_g, final_g=v_final_g)
    shapes = [small_w[k].shape for k in small_names]
    d_s, nm_s, nv_s = adamw_small(_pack([g_small[k] for k in small_names], SUBLANES),
                                  _pack([small_w[k] for k in small_names], SUBLANES),
                                  _pack([small_m[k] for k in small_names], SUBLANES),
                                  _pack([small_v[k] for k in small_names], SUBLANES))
    d_small = dict(zip(small_names, _unpack(d_s, shapes)))
    nm_small = dict(zip(small_names, _unpack(nm_s, shapes)))
    nv_small = dict(zip(small_names, _unpack(nv_s, shapes)))

    order = ["norm1_g", "w_in", "conv_w", "conv_b", "conv_ln_g", "conv_ln_b", "w_pw", "sg_ln_g", "sg_ln_b", "w_s", "b_s",
             "w_pool", "pool_scale", "w_out", "norm2_g", "w_gate_up", "w_down", "final_g"]
    grads = dict(g_small, w_in=g_w_in, w_pw=g_w_pw, w_out=g_w_out, w_gate_up=g_w_gu, w_down=g_w_d)
    deltas = dict(d_small, w_in=d_w_in, w_pw=d_w_pw, w_out=d_w_out, w_gate_up=d_w_gu, w_down=d_w_d)
    new_m = dict(nm_small, w_in=nm_w_in, w_pw=nm_w_pw, w_out=nm_w_out, w_gate_up=nm_w_gu, w_down=nm_w_d)
    new_v = dict(nv_small, w_in=nv_w_in, w_pw=nv_w_pw, w_out=nv_w_out, w_gate_up=nv_w_gu, w_down=nv_w_d)
    return (loss, grad_x, *[grads[k] for k in order], *[deltas[k] for k in order], *[new_m[k] for k in order],
            *[new_v[k] for k in order])
```

```python
import functools
import math

import jax
import jax.numpy as jnp
from jax import lax
from jax.experimental import pallas as pl
from jax.experimental.pallas import tpu as pltpu

F32 = jnp.float32
BF16 = jnp.bfloat16

D_MODEL = 1024
D_A = 384
D_B = 384
D_C = 256
D_IN = 2 * D_A + 2 * D_B + D_C
N_HEADS_B = 4
HEAD_DIM_B = 96
POOL_WINDOWS = (2, 4, 8, 16)
GROUP_DIM_C = 64
CONV_WIDTH = 31
CHUNK = 128
D_FF = 2816
RMS_EPS = 1e-6
LN_EPS = 1e-5
DEPTH = 2
N_DEV = 8

ADAM_LR = 0.001
ADAM_B1 = 0.9
ADAM_B2 = 0.999
ADAM_EPS = 1e-08
ADAM_WD = 0.01
ADAM_STEP = 10

CONV_HALO = 32
POOL_HALO = 16

VMEM_LIMIT = 56 * 1024 * 1024

MESH_ID = pl.DeviceIdType.MESH


def _dot(a, b):
    return jnp.dot(a, b, preferred_element_type=F32)


def _dot_nt(a, b):
    return lax.dot_general(a, b, (((1,), (1,)), ((), ())), preferred_element_type=F32)


def _dot_tn(a, b):
    return lax.dot_general(a, b, (((0,), (0,)), ((), ())), preferred_element_type=F32)


def _sigmoid(x):
    return 1.0 / (1.0 + jnp.exp(-x))


_GELU_C = math.sqrt(2.0 / math.pi)


def _gelu_and_grad(x):
    x2 = x * x
    inner = _GELU_C * (x + 0.044715 * x2 * x)
    t = jnp.tanh(inner)
    g = 0.5 * x * (1.0 + t)
    dg = 0.5 * (1.0 + t) + 0.5 * x * (1.0 - t * t) * _GELU_C * (1.0 + 3.0 * 0.044715 * x2)
    return g, dg


def _ln_stats(x):
    mu = jnp.mean(x, axis=-1, keepdims=True)
    xc = x - mu
    var = jnp.mean(xc * xc, axis=-1, keepdims=True)
    rstd = lax.rsqrt(var + LN_EPS)
    return xc * rstd, rstd


def _ln_bwd(dy, xhat, rstd, g):
    dxhat = dy * g
    return rstd * (dxhat - jnp.mean(dxhat, axis=-1, keepdims=True)
                   - xhat * jnp.mean(dxhat * xhat, axis=-1, keepdims=True))


def _rms_bwd(dh, xn, r, g):
    dxn = dh * g
    return r * (dxn - xn * jnp.mean(dxn * xn, axis=-1, keepdims=True))


def _head_masks(width):
    lane = lax.broadcasted_iota(jnp.int32, (1, width), 1)
    return [(lane >= h * HEAD_DIM_B) & (lane < (h + 1) * HEAD_DIM_B) for h in range(N_HEADS_B)]


def _pool_select(vals, width):
    lane = lax.broadcasted_iota(jnp.int32, (1, width), 1)
    out = vals[-1]
    for g in range(len(vals) - 2, -1, -1):
        out = jnp.where(lane < (g + 1) * GROUP_DIM_C, vals[g], out)
    return out


def _pool_counts(pos):
    return _pool_select([jnp.minimum(pos + 1.0, float(w)) for w in POOL_WINDOWS], D_C)


def _full(shape):
    n = len(shape)
    return pl.BlockSpec(shape, lambda *_: (0,) * n)


def _params(sem):
    return pltpu.CompilerParams(dimension_semantics=sem, vmem_limit_bytes=VMEM_LIMIT)


def _my_coords():
    return lax.axis_index("x"), lax.axis_index("y"), lax.axis_index("c")


def _peer(me, rel):
    x, y, c = me
    bx, by, bc = (rel >> 2) & 1, (rel >> 1) & 1, rel & 1
    return (1 - x if bx else x, 1 - y if by else y, 1 - c if bc else c)


def _index_of(dev):
    return 4 * dev[0] + 2 * dev[1] + dev[2]


SIBLING = 1
OTHER_CHIPS = (2, 4, 6)


class Gather:
    def __init__(self, shards):
        n = len(shards)
        self.inputs = list(shards)
        self.out_shape = [jax.ShapeDtypeStruct((N_DEV,) + s.shape, s.dtype) for s in shards]
        self.scratch = [pltpu.SemaphoreType.DMA((N_DEV - 1, n)), pltpu.SemaphoreType.DMA((N_DEV - 1, n)),
                        pltpu.SemaphoreType.DMA((n,))]

    @staticmethod
    def _copy(src, dst, sems, rel, k, to):
        return pltpu.make_async_remote_copy(src_ref=src, dst_ref=dst, send_sem=sems[0].at[rel - 1, k],
                                            recv_sem=sems[1].at[rel - 1, k], device_id=to, device_id_type=MESH_ID)

    def start(self, ins, outs, sems):
        me = _my_coords()
        mine = _index_of(me)
        for k, src in enumerate(ins):
            pltpu.make_async_copy(src, outs[k].at[mine], sems[2].at[k]).start()
            for rel in (SIBLING,) + OTHER_CHIPS:
                self._copy(src, outs[k].at[mine], sems, rel, k, _peer(me, rel)).start()

    def finish(self, ins, outs, sems):
        me = _my_coords()
        mine = _index_of(me)
        sibling = _peer(me, SIBLING)
        for rel in OTHER_CHIPS:
            slot = _index_of(_peer(me, rel))
            for k in range(len(ins)):
                self._copy(ins[k], outs[k].at[slot], sems, rel, k, sibling).wait_recv()
                self._copy(outs[k].at[slot], outs[k].at[slot], sems, rel + 1, k, sibling).start()
        for rel in (SIBLING,) + tuple(r + 1 for r in OTHER_CHIPS):
            slot = _index_of(_peer(me, rel))
            for k in range(len(ins)):
                self._copy(ins[k], outs[k].at[slot], sems, rel, k, sibling).wait_recv()
        for rel in range(1, N_DEV):
            for k in range(len(ins)):
                self._copy(ins[k], outs[k].at[mine], sems, rel, k, sibling).wait_send()
        for k, src in enumerate(ins):
            pltpu.make_async_copy(src, outs[k].at[mine], sems[2].at[k]).wait()


class Exchange:
    def __init__(self, fulls):
        n = len(fulls)
        self.inputs = list(fulls)
        self.out_shape = [jax.ShapeDtypeStruct(f.shape, f.dtype) for f in fulls]
        self.scratch = [pltpu.SemaphoreType.DMA((N_DEV - 1, n)), pltpu.SemaphoreType.DMA((N_DEV - 1, n)),
                        pltpu.SemaphoreType.DMA((n,))]

    def start(self, ins, outs, sems):
        me = _my_coords()
        mine = _index_of(me)
        for k, src in enumerate(ins):
            pltpu.make_async_copy(src.at[mine], outs[k].at[mine], sems[2].at[k]).start()
            for rel in range(1, N_DEV):
                to = _peer(me, rel)
                Gather._copy(src.at[_index_of(to)], outs[k].at[mine], sems, rel, k, to).start()

    def finish(self, ins, outs, sems):
        me = _my_coords()
        mine = _index_of(me)
        for rel in range(1, N_DEV):
            frm = _peer(me, rel)
            for k, src in enumerate(ins):
                Gather._copy(src.at[mine], outs[k].at[_index_of(frm)], sems, rel, k, frm).wait_recv()
        for rel in range(1, N_DEV):
            for k, src in enumerate(ins):
                Gather._copy(src.at[mine], outs[k].at[mine], sems, rel, k, _peer(me, rel)).wait_send()
        for k, src in enumerate(ins):
            pltpu.make_async_copy(src.at[mine], outs[k].at[mine], sems[2].at[k]).wait()


def _hosted_call(body, *, name, grid, in_specs, out_specs, out_shape, scratch_shapes, args, comm=None):
    sem = ("arbitrary",) * len(grid)
    if comm is None:
        res = pl.pallas_call(body, name=name, grid=grid, in_specs=in_specs, out_specs=out_specs, out_shape=out_shape,
                             scratch_shapes=scratch_shapes, compiler_params=_params(sem))(*args)
        return list(res), []
    n_in, n_out, n_scr = len(in_specs), len(out_specs), len(scratch_shapes)
    n_cin, n_cout = len(comm.inputs), len(comm.out_shape)

    def hosted(*refs):
        ins, refs = refs[:n_in], refs[n_in:]
        cins, refs = refs[:n_cin], refs[n_cin:]
        outs, refs = refs[:n_out], refs[n_out:]
        couts, refs = refs[:n_cout], refs[n_cout:]
        scr, csems = refs[:n_scr], refs[n_scr:]
        ids = [pl.program_id(a) for a in range(len(grid))]
        first = functools.reduce(lambda a, b: a & b, [i == 0 for i in ids])
        last = functools.reduce(lambda a, b: a & b, [i == g - 1 for i, g in zip(ids, grid)])

        @pl.when(first)
        def _():
            comm.start(cins, couts, csems)

        body(*ins, *outs, *scr)

        @pl.when(last)
        def _():
            comm.finish(cins, couts, csems)

    any_spec = pl.BlockSpec(memory_space=pl.ANY)
    res = pl.pallas_call(
        hosted, name=name, grid=grid, in_specs=list(in_specs) + [any_spec] * n_cin,
        out_specs=list(out_specs) + [any_spec] * n_cout, out_shape=list(out_shape) + comm.out_shape,
        scratch_shapes=list(scratch_shapes) + comm.scratch,
        compiler_params=pltpu.CompilerParams(dimension_semantics=sem, vmem_limit_bytes=VMEM_LIMIT, has_side_effects=True),
    )(*args, *comm.inputs)
    return list(res[:n_out]), list(res[n_out:])


def run_comm(comm, *, name):
    n_cin, n_cout = len(comm.inputs), len(comm.out_shape)

    def body(*refs):
        cins, couts, csems = refs[:n_cin], refs[n_cin:n_cin + n_cout], refs[n_cin + n_cout:]
        comm.start(cins, couts, csems)
        comm.finish(cins, couts, csems)

    any_spec = pl.BlockSpec(memory_space=pl.ANY)
    return pl.pallas_call(
        body, name=name, in_specs=[any_spec] * n_cin, out_specs=[any_spec] * n_cout, out_shape=comm.out_shape,
        scratch_shapes=comm.scratch, compiler_params=pltpu.CompilerParams(has_side_effects=True),
    )(*comm.inputs)


def mixer_fwd(x, g1, winT, convw, convb, clng, clnb, wpw, slng, slnb, wm, bias, wbd, pscale, wout, *, seq, tm,
              comm=None):
    T = x.shape[0]
    tiles_per_seq = seq // tm
    n_chunks = tm // CHUNK

    def body(x_ref, g1_ref, winT_ref, convw_ref, convb_ref, clng_ref, clnb_ref, wpw_ref, slng_ref, slnb_ref,
             wm_ref, bias_ref, wbd_ref, pscale_ref, wout_ref,
             z_ref, ycv_ref, p_ref, mix_ref, x1_ref, ybuf, zcbuf):
        i = pl.program_id(0)
        tile_in_seq = i % tiles_per_seq

        @pl.when(tile_in_seq == 0)
        def _():
            ybuf[0:CONV_HALO, :] = jnp.zeros((CONV_HALO, D_A), F32)
            zcbuf[0:POOL_HALO, :] = jnp.zeros((POOL_HALO, D_C), F32)

        x = x_ref[...]
        r = lax.rsqrt(jnp.mean(x * x, axis=-1, keepdims=True) + RMS_EPS)
        h = (x * r * g1_ref[...]).astype(BF16)
        z = _dot_nt(h, winT_ref[...])
        z_ref[...] = z

        y = z[:, 0:D_A] * _sigmoid(z[:, D_A:2 * D_A])
        ybuf[CONV_HALO:CONV_HALO + tm, :] = y
        acc = jnp.zeros((tm, D_A), F32) + convb_ref[...]
        for k in range(CONV_WIDTH):
            off = CONV_HALO - (CONV_WIDTH - 1) + k
            acc = acc + convw_ref[k:k + 1, :] * ybuf[off:off + tm, :]
        ybuf[0:CONV_HALO, :] = ybuf[tm:tm + CONV_HALO, :]
        ycv_ref[...] = acc
        xhat, _ = _ln_stats(acc)
        ln = xhat * clng_ref[...] + clnb_ref[...]
        s = ln * _sigmoid(ln)
        ya = _dot(s.astype(BF16), wpw_ref[...])

        gb, _ = _gelu_and_grad(z[:, 2 * D_A:2 * D_A + 2 * D_B])
        u = gb[:, 0:D_B]
        vhat, _ = _ln_stats(gb[:, D_B:2 * D_B])
        vn = vhat * slng_ref[...] + slnb_ref[...]
        masks = _head_masks(D_B)
        yb_parts = []
        for c in range(n_chunks):
            vn_c = vn[c * CHUNK:(c + 1) * CHUNK, :]
            sg = bias_ref[...]
            for hh in range(N_HEADS_B):
                sg = sg + _dot(wm_ref[hh], jnp.where(masks[hh], vn_c, 0.0).astype(BF16))
            yb_parts.append(u[c * CHUNK:(c + 1) * CHUNK, :] * sg)
        yb = jnp.concatenate(yb_parts, axis=0) if n_chunks > 1 else yb_parts[0]

        zc = z[:, 2 * D_A + 2 * D_B:D_IN]
        zcbuf[POOL_HALO:POOL_HALO + tm, :] = zc
        run = zc
        sums = []
        for j in range(1, POOL_WINDOWS[-1]):
            run = run + zcbuf[POOL_HALO - j:POOL_HALO - j + tm, :]
            if (j + 1) in POOL_WINDOWS:
                sums.append(run)
        zcbuf[0:POOL_HALO, :] = zcbuf[tm:tm + POOL_HALO, :]
        pos = (tile_in_seq * tm + lax.broadcasted_iota(jnp.int32, (tm, 1), 0)).astype(F32)
        p = _pool_select(sums, D_C) / _pool_counts(pos) - zc
        p_ref[...] = p
        yc = _dot(p.astype(BF16), wbd_ref[...]) * pscale_ref[...]

        mix = jnp.concatenate([ya, yb, yc], axis=1).astype(BF16)
        mix_ref[...] = mix
        x1_ref[...] = x + _dot(mix, wout_ref[...])

    row = lambda w: pl.BlockSpec((tm, w), lambda i: (i, 0))
    return _hosted_call(
        body, name="mixer_fwd", grid=(T // tm,),
        in_specs=[row(D_MODEL), _full((1, D_MODEL)), _full((D_IN, D_MODEL)), _full((CONV_WIDTH, D_A)),
                  _full((1, D_A)), _full((1, D_A)), _full((1, D_A)), _full((D_A, D_A)), _full((1, D_B)), _full((1, D_B)),
                  _full((N_HEADS_B, CHUNK, CHUNK)), _full((CHUNK, D_B)), _full((D_C, D_C)), _full((1, D_C)),
                  _full((D_MODEL, D_MODEL))],
        out_specs=[row(D_IN), row(D_A), row(D_C), row(D_MODEL), row(D_MODEL)],
        out_shape=[jax.ShapeDtypeStruct((T, D_IN), F32), jax.ShapeDtypeStruct((T, D_A), F32),
                   jax.ShapeDtypeStruct((T, D_C), F32), jax.ShapeDtypeStruct((T, D_MODEL), BF16),
                   jax.ShapeDtypeStruct((T, D_MODEL), F32)],
        scratch_shapes=[pltpu.VMEM((CONV_HALO + tm, D_A), F32), pltpu.VMEM((POOL_HALO + tm, D_C), F32)],
        args=(x, g1, winT, convw, convb, clng, clnb, wpw, slng, slnb, wm, bias, wbd, pscale, wout), comm=comm)


def mixer_bwd(dx1, x, z, ycv, p, g1, winT, convw, clng, clnb, wpw, slng, slnb, wm, wmT, bias, wbd, pscale, wout,
              *, seq, tm, comm=None):
    T = x.shape[0]
    tiles_per_seq = seq // tm
    n_tiles = T // tm
    n_chunks = tm // CHUNK

    def body(dx1_ref, x_ref, z_ref, ycv_ref, p_ref, g1_ref, winT_ref, convw_ref, clng_ref, clnb_ref, wpw_ref,
             slng_ref, slnb_ref, wm_ref, wmT_ref, bias_ref, wbd_ref, pscale_ref, wout_ref,
             dx_ref, dz_ref, h_ref, s_ref, dya_ref,
             dg1_ref, dconvw_ref, dconvb_ref, dclng_ref, dclnb_ref, dslng_ref, dslnb_ref, dwm_ref, dbs_ref,
             dwbd_ref, dpscale_ref, dycbuf, dpcbuf):
        i = pl.program_id(0)
        tile_in_seq = (n_tiles - 1 - i) % tiles_per_seq

        @pl.when(i == 0)
        def _():
            for ref in (dg1_ref, dconvw_ref, dconvb_ref, dclng_ref, dclnb_ref, dslng_ref, dslnb_ref, dwm_ref,
                        dbs_ref, dwbd_ref, dpscale_ref):
                ref[...] = jnp.zeros(ref.shape, F32)

        @pl.when(tile_in_seq == tiles_per_seq - 1)
        def _():
            dycbuf[tm:tm + CONV_HALO, :] = jnp.zeros((CONV_HALO, D_A), F32)
            dpcbuf[tm:tm + POOL_HALO, :] = jnp.zeros((POOL_HALO, D_C), F32)

        dx1 = dx1_ref[...]
        z = z_ref[...]
        dmix = _dot_nt(dx1.astype(BF16), wout_ref[...])
        dya = dmix[:, 0:D_A]
        dyb = dmix[:, D_A:D_A + D_B]
        dyc = dmix[:, D_A + D_B:D_MODEL]

        p = p_ref[...]
        pb = p.astype(BF16)
        q = _dot(pb, wbd_ref[...])
        dpscale_ref[...] += jnp.sum(dyc * q, axis=0, keepdims=True)
        dq = (dyc * pscale_ref[...]).astype(BF16)
        dwbd_ref[...] += _dot_tn(pb, dq)
        dp = _dot_nt(dq, wbd_ref[...])
        pos = (tile_in_seq * tm + lax.broadcasted_iota(jnp.int32, (tm, 1), 0)).astype(F32)
        dpc = dp / _pool_counts(pos)
        dpcbuf[0:tm, :] = dpc
        run = dpc
        sums = []
        for j in range(1, POOL_WINDOWS[-1]):
            run = run + dpcbuf[j:j + tm, :]
            if (j + 1) in POOL_WINDOWS:
                sums.append(run)
        dpcbuf[tm:tm + POOL_HALO, :] = dpcbuf[0:POOL_HALO, :]
        dzc = _pool_select(sums, D_C) - dp

        dya_ref[...] = dya.astype(BF16)
        ds = _dot_nt(dya.astype(BF16), wpw_ref[...])
        xhat, rstd = _ln_stats(ycv_ref[...])
        ln = xhat * clng_ref[...] + clnb_ref[...]
        sg = _sigmoid(ln)
        s_ref[...] = (ln * sg).astype(BF16)
        dln = ds * (sg * (1.0 + ln * (1.0 - sg)))
        dclng_ref[...] += jnp.sum(dln * xhat, axis=0, keepdims=True)
        dclnb_ref[...] += jnp.sum(dln, axis=0, keepdims=True)
        dycv = _ln_bwd(dln, xhat, rstd, clng_ref[...])
        dconvb_ref[...] += jnp.sum(dycv, axis=0, keepdims=True)
        a = z[:, 0:D_A]
        sgate = _sigmoid(z[:, D_A:2 * D_A])
        y = a * sgate
        dycbuf[0:tm, :] = dycv
        dy = jnp.zeros((tm, D_A), F32)
        for d in range(CONV_WIDTH):
            k = CONV_WIDTH - 1 - d
            sh = dycbuf[d:d + tm, :]
            dy = dy + convw_ref[k:k + 1, :] * sh
            dconvw_ref[k:k + 1, :] += jnp.sum(y * sh, axis=0, keepdims=True)
        dycbuf[tm:tm + CONV_HALO, :] = dycbuf[0:CONV_HALO, :]
        da = dy * sgate
        dgate = dy * a * sgate * (1.0 - sgate)

        gb, dgb = _gelu_and_grad(z[:, 2 * D_A:2 * D_A + 2 * D_B])
        u = gb[:, 0:D_B]
        vhat, vrstd = _ln_stats(gb[:, D_B:2 * D_B])
        vn = vhat * slng_ref[...] + slnb_ref[...]
        masks = _head_masks(D_B)
        tril = (lax.broadcasted_iota(jnp.int32, (CHUNK, CHUNK), 0)
                >= lax.broadcasted_iota(jnp.int32, (CHUNK, CHUNK), 1))
        lane128 = lax.broadcasted_iota(jnp.int32, (1, CHUNK), 1)
        du_parts, dvn_parts = [], []
        for c in range(n_chunks):
            rows = slice(c * CHUNK, (c + 1) * CHUNK)
            vn_c = vn[rows, :]
            vh = [jnp.where(masks[hh], vn_c, 0.0).astype(BF16) for hh in range(N_HEADS_B)]
            sgc = bias_ref[...]
            for hh in range(N_HEADS_B):
                sgc = sgc + _dot(wm_ref[hh], vh[hh])
            dyb_c = dyb[rows, :]
            du_parts.append(dyb_c * sgc)
            dsg = dyb_c * u[rows, :]
            dvn_c = jnp.zeros((CHUNK, D_B), F32)
            dbs = jnp.zeros((CHUNK, CHUNK), F32)
            for hh in range(N_HEADS_B):
                dsg_h = jnp.where(masks[hh], dsg, 0.0)
                dsg_hb = dsg_h.astype(BF16)
                dwm_ref[hh] += jnp.where(tril, _dot_nt(dsg_hb, vh[hh]), 0.0)
                dvn_c = dvn_c + _dot(wmT_ref[hh], dsg_hb)
                dbs = dbs + jnp.where(lane128 == hh, jnp.sum(dsg_h, axis=1, keepdims=True), 0.0)
            dbs_ref[...] += dbs
            dvn_parts.append(dvn_c)
        du = jnp.concatenate(du_parts, axis=0) if n_chunks > 1 else du_parts[0]
        dvn = jnp.concatenate(dvn_parts, axis=0) if n_chunks > 1 else dvn_parts[0]
        dslng_ref[...] += jnp.sum(dvn * vhat, axis=0, keepdims=True)
        dslnb_ref[...] += jnp.sum(dvn, axis=0, keepdims=True)
        dv = _ln_bwd(dvn, vhat, vrstd, slng_ref[...])
        dzb = jnp.concatenate([du, dv], axis=1) * dgb

        dz = jnp.concatenate([da, dgate, dzb, dzc], axis=1).astype(BF16)
        dz_ref[...] = dz
        dh = _dot(dz, winT_ref[...])
        x = x_ref[...]
        r = lax.rsqrt(jnp.mean(x * x, axis=-1, keepdims=True) + RMS_EPS)
        xn = x * r
        h_ref[...] = (xn * g1_ref[...]).astype(BF16)
        dg1_ref[...] += jnp.sum(dh * xn, axis=0, keepdims=True)
        dx_ref[...] = dx1 + _rms_bwd(dh, xn, r, g1_ref[...])

    row = lambda w: pl.BlockSpec((tm, w), lambda i: (n_tiles - 1 - i, 0))
    acc_shapes = [(1, D_MODEL), (CONV_WIDTH, D_A), (1, D_A), (1, D_A), (1, D_A), (1, D_B), (1, D_B),
                  (N_HEADS_B, CHUNK, CHUNK), (CHUNK, CHUNK), (D_C, D_C), (1, D_C)]
    return _hosted_call(
        body, name="mixer_bwd", grid=(n_tiles,),
        in_specs=[row(D_MODEL), row(D_MODEL), row(D_IN), row(D_A), row(D_C),
                  _full((1, D_MODEL)), _full((D_IN, D_MODEL)), _full((CONV_WIDTH, D_A)), _full((1, D_A)), _full((1, D_A)),
                  _full((D_A, D_A)), _full((1, D_B)), _full((1, D_B)), _full((N_HEADS_B, CHUNK, CHUNK)),
                  _full((N_HEADS_B, CHUNK, CHUNK)), _full((CHUNK, D_B)), _full((D_C, D_C)), _full((1, D_C)),
                  _full((D_MODEL, D_MODEL))],
        out_specs=[row(D_MODEL), row(D_IN), row(D_MODEL), row(D_A), row(D_A)] + [_full(s) for s in acc_shapes],
        out_shape=[jax.ShapeDtypeStruct((T, D_MODEL), F32), jax.ShapeDtypeStruct((T, D_IN), BF16),
                   jax.ShapeDtypeStruct((T, D_MODEL), BF16), jax.ShapeDtypeStruct((T, D_A), BF16),
                   jax.ShapeDtypeStruct((T, D_A), BF16)] + [jax.ShapeDtypeStruct(s, F32) for s in acc_shapes],
        scratch_shapes=[pltpu.VMEM((tm + CONV_HALO, D_A), F32), pltpu.VMEM((tm + POOL_HALO, D_C), F32)],
        args=(dx1, x, z, ycv, p, g1, winT, convw, clng, clnb, wpw, slng, slnb, wm, wmT, bias, wbd, pscale, wout),
        comm=comm)


def ffn_fwd(x1, g2, wguT, wd, *, tm, th, comm=None):
    T = x1.shape[0]
    n_h = D_FF // th

    def body(x1_ref, g2_ref, wgu_ref, wd_ref, x2_ref, gu_ref, h2_buf, acc):
        j = pl.program_id(1)

        @pl.when(j == 0)
        def _():
            x = x1_ref[...]
            r = lax.rsqrt(jnp.mean(x * x, axis=-1, keepdims=True) + RMS_EPS)
            h2_buf[...] = (x * r * g2_ref[...]).astype(BF16)
            acc[...] = x

        h2 = h2_buf[...]
        g = _dot_nt(h2, wgu_ref[0])
        u = _dot_nt(h2, wgu_ref[1])
        gu_ref[0] = g.astype(BF16)
        gu_ref[1] = u.astype(BF16)
        f = (g * _sigmoid(g) * u).astype(BF16)
        acc[...] += _dot(f, wd_ref[...])

        @pl.when(j == n_h - 1)
        def _():
            x2_ref[...] = acc[...]

    return _hosted_call(
        body, name="ffn_fwd", grid=(T // tm, n_h),
        in_specs=[pl.BlockSpec((tm, D_MODEL), lambda i, j: (i, 0)), _full((1, D_MODEL)),
                  pl.BlockSpec((2, th, D_MODEL), lambda i, j: (0, j, 0)),
                  pl.BlockSpec((th, D_MODEL), lambda i, j: (j, 0))],
        out_specs=[pl.BlockSpec((tm, D_MODEL), lambda i, j: (i, 0)),
                   pl.BlockSpec((2, tm, th), lambda i, j: (0, i, j))],
        out_shape=[jax.ShapeDtypeStruct((T, D_MODEL), F32), jax.ShapeDtypeStruct((2, T, D_FF), BF16)],
        scratch_shapes=[pltpu.VMEM((tm, D_MODEL), BF16), pltpu.VMEM((tm, D_MODEL), F32)],
        args=(x1, g2, wguT, wd), comm=comm)


def ffn_bwd(dx2, x1, gu, g2, wguT, wd, *, tm, th, comm=None):
    T = x1.shape[0]
    n_h = D_FF // th

    def body(dx2_ref, x1_ref, gu_ref, g2_ref, wgu_ref, wd_ref, dx1_ref, h2_ref, f_ref, dgu_ref, dg2_ref, acc):
        i = pl.program_id(0)
        j = pl.program_id(1)

        @pl.when((i == 0) & (j == 0))
        def _():
            dg2_ref[...] = jnp.zeros(dg2_ref.shape, F32)

        dx2 = dx2_ref[...]
        df = _dot_nt(dx2.astype(BF16), wd_ref[...])
        g = gu_ref[0].astype(F32)
        u = gu_ref[1].astype(F32)
        sg = _sigmoid(g)
        silu = g * sg
        f_ref[...] = (silu * u).astype(BF16)
        dgate = (df * u * (sg * (1.0 + g * (1.0 - sg)))).astype(BF16)
        dup = (df * silu).astype(BF16)
        dgu_ref[0] = dgate
        dgu_ref[1] = dup
        part = _dot(dgate, wgu_ref[0]) + _dot(dup, wgu_ref[1])

        @pl.when(j == 0)
        def _():
            acc[...] = part

        @pl.when(j > 0)
        def _():
            acc[...] += part

        @pl.when(j == n_h - 1)
        def _():
            x = x1_ref[...]
            r = lax.rsqrt(jnp.mean(x * x, axis=-1, keepdims=True) + RMS_EPS)
            xn = x * r
            dh = acc[...]
            h2_ref[...] = (xn * g2_ref[...]).astype(BF16)
            dg2_ref[...] += jnp.sum(dh * xn, axis=0, keepdims=True)
            dx1_ref[...] = dx2 + _rms_bwd(dh, xn, r, g2_ref[...])

    return _hosted_call(
        body, name="ffn_bwd", grid=(T // tm, n_h),
        in_specs=[pl.BlockSpec((tm, D_MODEL), lambda i, j: (i, 0)), pl.BlockSpec((tm, D_MODEL), lambda i, j: (i, 0)),
                  pl.BlockSpec((2, tm, th), lambda i, j: (0, i, j)), _full((1, D_MODEL)),
                  pl.BlockSpec((2, th, D_MODEL), lambda i, j: (0, j, 0)),
                  pl.BlockSpec((th, D_MODEL), lambda i, j: (j, 0))],
        out_specs=[pl.BlockSpec((tm, D_MODEL), lambda i, j: (i, 0)), pl.BlockSpec((tm, D_MODEL), lambda i, j: (i, 0)),
                   pl.BlockSpec((tm, th), lambda i, j: (i, j)), pl.BlockSpec((2, tm, th), lambda i, j: (0, i, j)),
                   _full((1, D_MODEL))],
        out_shape=[jax.ShapeDtypeStruct((T, D_MODEL), F32), jax.ShapeDtypeStruct((T, D_MODEL), BF16),
                   jax.ShapeDtypeStruct((T, D_FF), BF16), jax.ShapeDtypeStruct((2, T, D_FF), BF16),
                   jax.ShapeDtypeStruct((1, D_MODEL), F32)],
        scratch_shapes=[pltpu.VMEM((tm, D_MODEL), F32)],
        args=(dx2, x1, gu, g2, wguT, wd), comm=comm)


def head_fwd_bwd(x, target, fg, *, tm):
    T = x.shape[0]
    n_tiles = T // tm

    def body(x_ref, t_ref, fg_ref, loss_ref, dx_ref, dfg_ref, lacc):
        i = pl.program_id(0)

        @pl.when(i == 0)
        def _():
            lacc[...] = jnp.zeros(lacc.shape, F32)
            dfg_ref[...] = jnp.zeros(dfg_ref.shape, F32)

        x = x_ref[...]
        r = lax.rsqrt(jnp.mean(x * x, axis=-1, keepdims=True) + RMS_EPS)
        xn = x * r
        e = xn * fg_ref[...] - t_ref[...]
        lacc[...] += jnp.sum(e * e, axis=0, keepdims=True)
        dy = e * (1.0 / D_MODEL)
        dfg_ref[...] += jnp.sum(dy * xn, axis=0, keepdims=True)
        dx_ref[...] = _rms_bwd(dy, xn, r, fg_ref[...])

        @pl.when(i == n_tiles - 1)
        def _():
            loss_ref[...] = jnp.sum(lacc[...], axis=1, keepdims=True) * (0.5 / D_MODEL)

    row = pl.BlockSpec((tm, D_MODEL), lambda i: (i, 0))
    return pl.pallas_call(
        body, name="head_fwd_bwd", grid=(n_tiles,),
        in_specs=[row, row, _full((1, D_MODEL))],
        out_specs=[_full((1, 1)), row, _full((1, D_MODEL))],
        out_shape=[jax.ShapeDtypeStruct((1, 1), F32), jax.ShapeDtypeStruct((T, D_MODEL), F32),
                   jax.ShapeDtypeStruct((1, D_MODEL), F32)],
        scratch_shapes=[pltpu.VMEM((1, D_MODEL), F32)],
        compiler_params=_params(("arbitrary",)),
    )(x, target, fg)


def wgrad(a, b, *, tmo, tk, name):
    G, T, M = a.shape
    N = b.shape[1]
    n_k = T // tk

    def body(a_ref, b_ref, o_ref, acc):
        k = pl.program_id(2)
        part = _dot_tn(a_ref[0].astype(BF16), b_ref[...].astype(BF16))
        if n_k == 1:
            o_ref[0] = part.astype(BF16)
            return

        @pl.when(k == 0)
        def _():
            acc[...] = part

        @pl.when((k > 0) & (k < n_k - 1))
        def _():
            acc[...] += part

        @pl.when(k == n_k - 1)
        def _():
            o_ref[0] = (acc[...] + part).astype(BF16)

    return pl.pallas_call(
        body, name=name, grid=(G, M // tmo, n_k),
        in_specs=[pl.BlockSpec((1, tk, tmo), lambda g, m, k: (g, k, m)),
                  pl.BlockSpec((tk, N), lambda g, m, k: (k, 0))],
        out_specs=pl.BlockSpec((1, tmo, N), lambda g, m, k: (g, m, 0)),
        out_shape=jax.ShapeDtypeStruct((G, M, N), BF16),
        scratch_shapes=[pltpu.VMEM((tmo, N), F32)],
        compiler_params=_params(("arbitrary", "arbitrary", "arbitrary")),
    )(a, b)


def all_reduce_small(buf):
    _, R, W = buf.shape

    def body(in_ref, out_ref, land, send1, recv1, send2, recv2):
        me = _my_coords()
        mine = _index_of(me)
        sends = []
        for rel in range(1, N_DEV):
            to = _peer(me, rel)
            cp = pltpu.make_async_remote_copy(
                src_ref=in_ref.at[_index_of(to)], dst_ref=land.at[mine], send_sem=send1.at[rel - 1],
                recv_sem=recv1.at[rel - 1], device_id=to, device_id_type=MESH_ID)
            cp.start()
            sends.append(cp)
        land[mine] = in_ref[mine]
        for rel in range(1, N_DEV):
            frm = _peer(me, rel)
            pltpu.make_async_remote_copy(
                src_ref=in_ref.at[mine], dst_ref=land.at[_index_of(frm)], send_sem=send1.at[rel - 1],
                recv_sem=recv1.at[rel - 1], device_id=frm, device_id_type=MESH_ID).wait_recv()
        total = land[0]
        for d in range(1, N_DEV):
            total = total + land[d]
        out_ref[mine] = total
        for rel in range(1, N_DEV):
            to = _peer(me, rel)
            cp = pltpu.make_async_remote_copy(
                src_ref=out_ref.at[mine], dst_ref=out_ref.at[mine], send_sem=send2.at[rel - 1],
                recv_sem=recv2.at[rel - 1], device_id=to, device_id_type=MESH_ID)
            cp.start()
            sends.append(cp)
        for rel in range(1, N_DEV):
            frm = _peer(me, rel)
            pltpu.make_async_remote_copy(
                src_ref=out_ref.at[mine], dst_ref=out_ref.at[_index_of(frm)], send_sem=send2.at[rel - 1],
                recv_sem=recv2.at[rel - 1], device_id=frm, device_id_type=MESH_ID).wait_recv()
        for cp in sends:
            cp.wait_send()

    vmem = pl.BlockSpec(memory_space=pltpu.VMEM)
    return pl.pallas_call(
        body, name="all_reduce_small",
        in_specs=[vmem], out_specs=vmem,
        out_shape=jax.ShapeDtypeStruct(buf.shape, F32),
        scratch_shapes=[pltpu.VMEM(buf.shape, F32)] + [pltpu.SemaphoreType.DMA((N_DEV - 1,))] * 4,
        compiler_params=pltpu.CompilerParams(has_side_effects=True, vmem_limit_bytes=VMEM_LIMIT),
    )(buf)


_ADAM_C1 = 1.0 - ADAM_B1 ** ADAM_STEP
_ADAM_C2 = 1.0 - ADAM_B2 ** ADAM_STEP


def _adamw_math(w, g, m, v):
    m = ADAM_B1 * m + (1.0 - ADAM_B1) * g
    v = ADAM_B2 * v + (1.0 - ADAM_B2) * (g * g)
    m_hat = m / _ADAM_C1
    v_hat = v / _ADAM_C2
    delta = -ADAM_LR * (m_hat / (jnp.sqrt(v_hat) + ADAM_EPS) + ADAM_WD * w)
    return delta, m, v


def adamw_sharded(parts, w, m, v, *, transposed, tr, name):
    _, R, C = parts[0].shape

    def body(p0_ref, p1_ref, w_ref, m_ref, v_ref, g_ref, d_ref, nm_ref, nv_ref):
        def update(p_ref):
            g = p_ref[0].astype(F32)
            for d in range(1, N_DEV):
                g = g + p_ref[d].astype(F32)
            if transposed:
                g = g.T
            delta, nm, nv = _adamw_math(w_ref[0], g, m_ref[0], v_ref[0])
            g_ref[0] = g
            d_ref[0] = delta
            nm_ref[0] = nm
            nv_ref[0] = nv

        @pl.when(pl.program_id(0) == 0)
        def _():
            update(p0_ref)

        @pl.when(pl.program_id(0) == 1)
        def _():
            update(p1_ref)

    n_i = (C if transposed else R) // tr
    walk = [lambda l, i: jnp.where(l == 0, i, n_i - 1), lambda l, i: jnp.where(l == 1, i, 0)]
    if transposed:
        p_specs = [pl.BlockSpec((N_DEV, R, tr), functools.partial(lambda l, i, f: (0, 0, f(l, i)), f=f)) for f in walk]
        o_spec = pl.BlockSpec((1, tr, R), lambda l, i: (l, i, 0))
    else:
        p_specs = [pl.BlockSpec((N_DEV, tr, C), functools.partial(lambda l, i, f: (0, f(l, i), 0), f=f)) for f in walk]
        o_spec = pl.BlockSpec((1, tr, C), lambda l, i: (l, i, 0))
    return pl.pallas_call(
        body, name=name, grid=(DEPTH, n_i),
        in_specs=p_specs + [o_spec, o_spec, o_spec], out_specs=[o_spec] * 4,
        out_shape=[jax.ShapeDtypeStruct(w.shape, F32)] * 4,
        compiler_params=_params(("arbitrary", "arbitrary")),
    )(parts[0], parts[1], w, m, v)


def adamw_small(g, w, m, v):
    def body(g_ref, w_ref, m_ref, v_ref, d_ref, nm_ref, nv_ref):
        delta, nm, nv = _adamw_math(w_ref[...], g_ref[...], m_ref[...], v_ref[...])
        d_ref[...] = delta
        nm_ref[...] = nm
        nv_ref[...] = nv

    vmem = pl.BlockSpec(memory_space=pltpu.VMEM)
    return pl.pallas_call(
        body, name="adamw_small", in_specs=[vmem] * 4, out_specs=[vmem] * 3,
        out_shape=[jax.ShapeDtypeStruct(g.shape, F32)] * 3,
        compiler_params=pltpu.CompilerParams(vmem_limit_bytes=VMEM_LIMIT),
    )(g, w, m, v)


LANES = 128
SUBLANES = 8


def _pack(arrays, row_multiple):
    flat = jnp.concatenate([a.reshape(-1) for a in arrays])
    rows = -(-flat.shape[0] // (LANES * row_multiple)) * row_multiple
    return jnp.pad(flat, (0, rows * LANES - flat.shape[0])).reshape(rows, LANES)


def _unpack(buf, shapes):
    flat = buf.reshape(-1)
    out, off = [], 0
    for s in shapes:
        n = math.prod(s)
        out.append(flat[off:off + n].reshape(s))
        off += n
    return out


def _block_diag(w_pool):
    G, d, _ = w_pool.shape
    eye = jnp.eye(G, dtype=w_pool.dtype)
    return (eye[:, None, :, None] * w_pool[:, :, None, :]).reshape(G * d, G * d)


def kernel(x, norm1_g, w_in, conv_w, conv_b, conv_ln_g, conv_ln_b, w_pw, sg_ln_g, sg_ln_b, w_s, b_s, w_pool, pool_scale, w_out, norm2_g, w_gate_up, w_down, final_g, loss_target, m_norm1_g, m_w_in, m_conv_w, m_conv_b, m_conv_ln_g, m_conv_ln_b, m_w_pw, m_sg_ln_g, m_sg_ln_b, m_w_s, m_b_s, m_w_pool, m_pool_scale, m_w_out, m_norm2_g, m_w_gate_up, m_w_down, m_final_g, v_norm1_g, v_w_in, v_conv_w, v_conv_b, v_conv_ln_g, v_conv_ln_b, v_w_pw, v_sg_ln_g, v_sg_ln_b, v_w_s, v_b_s, v_w_pool, v_pool_scale, v_w_out, v_norm2_g, v_w_gate_up, v_w_down, v_final_g):
    b_loc, seq, _ = x.shape
    T = b_loc * seq
    tm_mix = min(256, seq)
    tm_ffn_fwd = min(512, T)
    tm_ffn_bwd = min(256, T)
    tm_head = min(512, T)
    tk = min(2048, T)
    tk_f32 = min(1024, T)
    th = D_FF // 2
    cw = conv_w.shape[2]
    my_index = _index_of(_my_coords())

    xf = x.reshape(T, D_MODEL)
    tgt = loss_target.reshape(T, D_MODEL)

    mixer_shards = [[w_in[l].T.astype(BF16), w_out[l].astype(BF16), w_pw[l].astype(BF16)] for l in range(DEPTH)]
    ffn_shards = [[w_gate_up[l].T.astype(BF16), w_down[l].astype(BF16)] for l in range(DEPTH)]

    tril = jnp.tril(jnp.ones((CHUNK, CHUNK), dtype=bool))
    layers = []
    for l in range(DEPTH):
        wm = jnp.where(tril[None], w_s[l], 0.0).astype(BF16)
        layers.append(dict(
            g1=norm1_g[l][None], convb=conv_b[l][None], clng=conv_ln_g[l][None], clnb=conv_ln_b[l][None],
            slng=sg_ln_g[l][None], slnb=sg_ln_b[l][None], wm=wm, wmT=jnp.swapaxes(wm, 1, 2),
            bias=jnp.repeat(b_s[l].T, HEAD_DIM_B, axis=1), wbd=_block_diag(w_pool[l]).astype(BF16),
            pscale=pool_scale[l][None], g2=norm2_g[l][None]))

    def set_mixer_weights(l, g_in, g_out, g_pw):
        layers[l].update(winT=g_in.reshape(D_IN, D_MODEL), wout=g_out.reshape(D_MODEL, D_MODEL), wpw=g_pw.reshape(D_A, D_A))

    def set_ffn_weights(l, g_gu, g_d):
        layers[l].update(wguT=g_gu.reshape(2, D_FF, D_MODEL), wd=g_d.reshape(D_FF, D_MODEL))

    first = run_comm(Gather(mixer_shards[0] + [conv_w.reshape(DEPTH * CONV_WIDTH, cw).T]), name="gather_first")
    set_mixer_weights(0, *first[:3])
    convw_full = first[3].reshape(D_A, DEPTH * CONV_WIDTH).T.reshape(DEPTH, CONV_WIDTH, D_A)
    for l in range(DEPTH):
        layers[l]["convw"] = convw_full[l]

    saved = []
    cur = xf
    for l in range(DEPTH):
        w = layers[l]
        (z, ycv, p, mix, x1), got = mixer_fwd(
            cur, w["g1"], w["winT"], w["convw"], w["convb"], w["clng"], w["clnb"], w["wpw"], w["slng"], w["slnb"], w["wm"],
            w["bias"], w["wbd"], w["pscale"], w["wout"], seq=seq, tm=tm_mix, comm=Gather(ffn_shards[0]) if l == 0 else None)
        if l == 0:
            set_ffn_weights(0, *got)
        (x2, gu), got = ffn_fwd(x1, w["g2"], w["wguT"], w["wd"], tm=tm_ffn_fwd, th=th,
                                comm=Gather(mixer_shards[1] + ffn_shards[1]) if l == 0 else None)
        if l == 0:
            set_mixer_weights(1, *got[:3])
            set_ffn_weights(1, *got[3:])
        saved.append((cur, z, ycv, p, mix, x1, gu))
        cur = x2
    loss_part, dx, dfg = head_fwd_bwd(cur, tgt, final_g[None], tm=tm_head)
    loss = lax.psum(loss_part[0, 0], ("x", "y", "c"))

    small = [None] * DEPTH
    parts = {}
    pending = None
    for l in reversed(range(DEPTH)):
        w = layers[l]
        x0, z, ycv, p, mix, x1, gu = saved[l]
        (dx1, h2, f, dgu, dg2), got = ffn_bwd(dx, x1, gu, w["g2"], w["wguT"], w["wd"], tm=tm_ffn_bwd, th=th,
                                              comm=Exchange(pending[1]) if pending else None)
        if pending:
            parts.update(zip(pending[0], got))
        gw_gu = wgrad(dgu, h2, tmo=th, tk=tk, name="wgrad_gate_up").reshape(N_DEV, 2 * D_FF // N_DEV, D_MODEL)
        gw_d = wgrad(f[None], dx, tmo=th, tk=tk_f32, name="wgrad_down").reshape(N_DEV, D_FF // N_DEV, D_MODEL)
        outs, got = mixer_bwd(
            dx1, x0, z, ycv, p, w["g1"], w["winT"], w["convw"], w["clng"], w["clnb"], w["wpw"], w["slng"], w["slnb"],
            w["wm"], w["wmT"], w["bias"], w["wbd"], w["pscale"], w["wout"], seq=seq, tm=tm_mix, comm=Exchange([gw_gu, gw_d]))
        parts[("gu", l)], parts[("d", l)] = got
        (dx, dz, h, s, dya, dg1, dconvw, dconvb, dclng, dclnb, dslng, dslnb, dwm, dbs, dwbd, dpscale) = outs
        gw_out = wgrad(mix[None], dx1, tmo=D_MODEL, tk=tk_f32, name="wgrad_out").reshape(N_DEV, D_MODEL // N_DEV, D_MODEL)
        gw_in = wgrad(dz[None], h, tmo=D_IN // 2, tk=tk, name="wgrad_in").reshape(N_DEV, D_IN // N_DEV, D_MODEL)
        gw_pw = wgrad(s[None], dya, tmo=D_A, tk=tk, name="wgrad_pw").reshape(N_DEV, D_A // N_DEV, D_A)
        pending = ([("in", l), ("out", l), ("pw", l)], [gw_in, gw_out, gw_pw])
        small[l] = (dg1, dconvw, dconvb, dclng, dclnb, dslng, dslnb, dwm, dbs, dwbd, dpscale, dg2)
    grad_x = dx.reshape(x.shape)
    parts.update(zip(pending[0], run_comm(Exchange(pending[1]), name="exchange_last")))
    p_in, p_gu, p_d, p_out, p_pw = [[parts[(k, l)] for l in range(DEPTH)] for k in ("in", "gu", "d", "out", "pw")]

    small_arrays = [a for l in range(DEPTH) for a in small[l]] + [dfg]
    small_shapes = [a.shape for a in small_arrays]
    packed = _pack(small_arrays, N_DEV * SUBLANES)
    summed = all_reduce_small(packed.reshape(N_DEV, packed.shape[0] // N_DEV, LANES))
    sums = _unpack(summed, small_shapes)
    per_layer = len(small[0])
    g_small = {k: [] for k in ("norm1_g", "conv_w", "conv_b", "conv_ln_g", "conv_ln_b", "sg_ln_g", "sg_ln_b", "w_s", "b_s",
                               "w_pool", "pool_scale", "norm2_g")}
    for l in range(DEPTH):
        dg1, dconvw, dconvb, dclng, dclnb, dslng, dslnb, dwm, dbs, dwbd, dpscale, dg2 = sums[per_layer * l:per_layer * (l + 1)]
        g_small["norm1_g"].append(dg1[0])
        g_small["conv_w"].append(lax.dynamic_slice_in_dim(dconvw, my_index * cw, cw, axis=1))
        g_small["conv_b"].append(dconvb[0])
        g_small["conv_ln_g"].append(dclng[0])
        g_small["conv_ln_b"].append(dclnb[0])
        g_small["sg_ln_g"].append(dslng[0])
        g_small["sg_ln_b"].append(dslnb[0])
        g_small["w_s"].append(dwm)
        g_small["b_s"].append(dbs[:, :N_HEADS_B].T)
        g_small["w_pool"].append(jnp.stack([dwbd[g * GROUP_DIM_C:(g + 1) * GROUP_DIM_C, g * GROUP_DIM_C:(g + 1) * GROUP_DIM_C]
                                            for g in range(len(POOL_WINDOWS))]))
        g_small["pool_scale"].append(dpscale[0])
        g_small["norm2_g"].append(dg2[0])
    g_small = {k: jnp.stack(v) for k, v in g_small.items()}
    g_small["final_g"] = sums[-1][0]

    g_w_in, d_w_in, nm_w_in, nv_w_in = adamw_sharded(p_in, w_in, m_w_in, v_w_in, transposed=True, tr=256, name="adamw_w_in")
    g_w_gu, d_w_gu, nm_w_gu, nv_w_gu = adamw_sharded(p_gu, w_gate_up, m_w_gate_up, v_w_gate_up, transposed=True, tr=256,
                                                     name="adamw_w_gate_up")
    g_w_d, d_w_d, nm_w_d, nv_w_d = adamw_sharded(p_d, w_down, m_w_down, v_w_down, transposed=False, tr=D_FF // N_DEV // 2,
                                                 name="adamw_w_down")
    g_w_out, d_w_out, nm_w_out, nv_w_out = adamw_sharded(p_out, w_out, m_w_out, v_w_out, transposed=False,
                                                         tr=D_MODEL // N_DEV, name="adamw_w_out")
    g_w_pw, d_w_pw, nm_w_pw, nv_w_pw = adamw_sharded(p_pw, w_pw, m_w_pw, v_w_pw, transposed=False, tr=D_A // N_DEV,
                                                     name="adamw_w_pw")

    small_names = ["norm1_g", "conv_w", "conv_b", "conv_ln_g", "conv_ln_b", "sg_ln_g", "sg_ln_b", "w_s", "b_s", "w_pool",
                   "pool_scale", "norm2_g", "final_g"]
    small_w = dict(norm1_g=norm1_g, conv_w=conv_w, conv_b=conv_b, conv_ln_g=conv_ln_g, conv_ln_b=conv_ln_b, sg_ln_g=sg_ln_g,
                   sg_ln_b=sg_ln_b, w_s=w_s, b_s=b_s, w_pool=w_pool, pool_scale=pool_scale, norm2_g=norm2_g, final_g=final_g)
    small_m = dict(norm1_g=m_norm1_g, conv_w=m_conv_w, conv_b=m_conv_b, conv_ln_g=m_conv_ln_g, conv_ln_b=m_conv_ln_b,
                   sg_ln_g=m_sg_ln_g, sg_ln_b=m_sg_ln_b, w_s=m_w_s, b_s=m_b_s, w_pool=m_w_pool, pool_scale=m_pool_scale,
                   norm2_g=m_norm2_g, final_g=m_final_g)
    small_v = dict(norm1_g=v_norm1_g, conv_w=v_conv_w, conv_b=v_conv_b, conv_ln_g=v_conv_ln_g, conv_ln_b=v_conv_ln_b,
                   sg_ln_g=v_sg_ln_g, sg_ln_b=v_sg_ln_b, w_s=v_w_s, b_s=v_b_s, w_pool=v_w_pool, pool_scale=v_pool_scale,
                   norm2_g=v_norm2_g, final_g=v_final_g)
    shapes = [small_w[k].shape for k in small_names]
    d_s, nm_s, nv_s = adamw_small(_pack([g_small[k] for k in small_names], SUBLANES),
                                  _pack([small_w[k] for k in small_names], SUBLANES),
                                  _pack([small_m[k] for k in small_names], SUBLANES),
                                  _pack([small_v[k] for k in small_names], SUBLANES))
    d_small = dict(zip(small_names, _unpack(d_s, shapes)))
    nm_small = dict(zip(small_names, _unpack(nm_s, shapes)))
    nv_small = dict(zip(small_names, _unpack(nv_s, shapes)))

    order = ["norm1_g", "w_in", "conv_w", "conv_b", "conv_ln_g", "conv_ln_b", "w_pw", "sg_ln_g", "sg_ln_b", "w_s", "b_s",
             "w_pool", "pool_scale", "w_out", "norm2_g", "w_gate_up", "w_down", "final_g"]
    grads = dict(g_small, w_in=g_w_in, w_pw=g_w_pw, w_out=g_w_out, w_gate_up=g_w_gu, w_down=g_w_d)
    deltas = dict(d_small, w_in=d_w_in, w_pw=d_w_pw, w_out=d_w_out, w_gate_up=d_w_gu, w_down=d_w_d)
    new_m = dict(nm_small, w_in=nm_w_in, w_pw=nm_w_pw, w_out=nm_w_out, w_gate_up=nm_w_gu, w_down=nm_w_d)
    new_v = dict(nv_small, w_in=nv_w_in, w_pw=nv_w_pw, w_out=nv_w_out, w_gate_up=nv_w_gu, w_down=nv_w_d)
    return (loss, grad_x, *[grads[k] for k in order], *[deltas[k] for k in order], *[new_m[k] for k in order],
            *[new_v[k] for k in order])
```

```python
import functools
import math

import jax
import jax.numpy as jnp
from jax import lax
from jax.experimental import pallas as pl
from jax.experimental.pallas import tpu as pltpu

F32 = jnp.float32
BF16 = jnp.bfloat16

D_MODEL = 1024
D_A = 384
D_B = 384
D_C = 256
D_IN = 2 * D_A + 2 * D_B + D_C
N_HEADS_B = 4
HEAD_DIM_B = 96
POOL_WINDOWS = (2, 4, 8, 16)
GROUP_DIM_C = 64
CONV_WIDTH = 31
CHUNK = 128
D_FF = 2816
RMS_EPS = 1e-6
LN_EPS = 1e-5
DEPTH = 2
N_DEV = 8

ADAM_LR = 0.001
ADAM_B1 = 0.9
ADAM_B2 = 0.999
ADAM_EPS = 1e-08
ADAM_WD = 0.01
ADAM_STEP = 10

CONV_HALO = 32
POOL_HALO = 16

VMEM_LIMIT = 56 * 1024 * 1024

MESH_ID = pl.DeviceIdType.MESH


def _dot(a, b):
    return jnp.dot(a, b, preferred_element_type=F32)


def _dot_nt(a, b):
    return lax.dot_general(a, b, (((1,), (1,)), ((), ())), preferred_element_type=F32)


def _dot_tn(a, b):
    return lax.dot_general(a, b, (((0,), (0,)), ((), ())), preferred_element_type=F32)


def _sigmoid(x):
    return 1.0 / (1.0 + jnp.exp(-x))


_GELU_C = math.sqrt(2.0 / math.pi)


def _gelu_and_grad(x):
    x2 = x * x
    inner = _GELU_C * (x + 0.044715 * x2 * x)
    t = jnp.tanh(inner)
    g = 0.5 * x * (1.0 + t)
    dg = 0.5 * (1.0 + t) + 0.5 * x * (1.0 - t * t) * _GELU_C * (1.0 + 3.0 * 0.044715 * x2)
    return g, dg


def _ln_stats(x):
    mu = jnp.mean(x, axis=-1, keepdims=True)
    xc = x - mu
    var = jnp.mean(xc * xc, axis=-1, keepdims=True)
    rstd = lax.rsqrt(var + LN_EPS)
    return xc * rstd, rstd


def _ln_bwd(dy, xhat, rstd, g):
    dxhat = dy * g
    return rstd * (dxhat - jnp.mean(dxhat, axis=-1, keepdims=True)
                   - xhat * jnp.mean(dxhat * xhat, axis=-1, keepdims=True))


def _rms_bwd(dh, xn, r, g):
    dxn = dh * g
    return r * (dxn - xn * jnp.mean(dxn * xn, axis=-1, keepdims=True))


def _head_masks(width):
    lane = lax.broadcasted_iota(jnp.int32, (1, width), 1)
    return [(lane >= h * HEAD_DIM_B) & (lane < (h + 1) * HEAD_DIM_B) for h in range(N_HEADS_B)]


def _pool_select(vals, width):
    lane = lax.broadcasted_iota(jnp.int32, (1, width), 1)
    out = vals[-1]
    for g in range(len(vals) - 2, -1, -1):
        out = jnp.where(lane < (g + 1) * GROUP_DIM_C, vals[g], out)
    return out


def _pool_counts(pos):
    return _pool_select([jnp.minimum(pos + 1.0, float(w)) for w in POOL_WINDOWS], D_C)


def _full(shape):
    n = len(shape)
    return pl.BlockSpec(shape, lambda *_: (0,) * n)


def _params(sem):
    return pltpu.CompilerParams(dimension_semantics=sem, vmem_limit_bytes=VMEM_LIMIT)


def _my_coords():
    return lax.axis_index("x"), lax.axis_index("y"), lax.axis_index("c")


def _peer(me, rel):
    x, y, c = me
    bx, by, bc = (rel >> 2) & 1, (rel >> 1) & 1, rel & 1
    return (1 - x if bx else x, 1 - y if by else y, 1 - c if bc else c)


def _index_of(dev):
    return 4 * dev[0] + 2 * dev[1] + dev[2]


SIBLING = 1
OTHER_CHIPS = (2, 4, 6)


class Gather:
    def __init__(self, shards):
        n = len(shards)
        self.inputs = list(shards)
        self.out_shape = [jax.ShapeDtypeStruct((N_DEV,) + s.shape, s.dtype) for s in shards]
        self.scratch = [pltpu.SemaphoreType.DMA((N_DEV - 1, n)), pltpu.SemaphoreType.DMA((N_DEV - 1, n)),
                        pltpu.SemaphoreType.DMA((n,))]

    @staticmethod
    def _copy(src, dst, sems, rel, k, to):
        return pltpu.make_async_remote_copy(src_ref=src, dst_ref=dst, send_sem=sems[0].at[rel - 1, k],
                                            recv_sem=sems[1].at[rel - 1, k], device_id=to, device_id_type=MESH_ID)

    def start(self, ins, outs, sems):
        me = _my_coords()
        mine = _index_of(me)
        for k, src in enumerate(ins):
            pltpu.make_async_copy(src, outs[k].at[mine], sems[2].at[k]).start()
            for rel in (SIBLING,) + OTHER_CHIPS:
                self._copy(src, outs[k].at[mine], sems, rel, k, _peer(me, rel)).start()

    def finish(self, ins, outs, sems):
        me = _my_coords()
        mine = _index_of(me)
        sibling = _peer(me, SIBLING)
        for rel in OTHER_CHIPS:
            slot = _index_of(_peer(me, rel))
            for k in range(len(ins)):
                self._copy(ins[k], outs[k].at[slot], sems, rel, k, sibling).wait_recv()
                self._copy(outs[k].at[slot], outs[k].at[slot], sems, rel + 1, k, sibling).start()
        for rel in (SIBLING,) + tuple(r + 1 for r in OTHER_CHIPS):
            slot = _index_of(_peer(me, rel))
            for k in range(len(ins)):
                self._copy(ins[k], outs[k].at[slot], sems, rel, k, sibling).wait_recv()
        for rel in range(1, N_DEV):
            for k in range(len(ins)):
                self._copy(ins[k], outs[k].at[mine], sems, rel, k, sibling).wait_send()
        for k, src in enumerate(ins):
            pltpu.make_async_copy(src, outs[k].at[mine], sems[2].at[k]).wait()


class Exchange:
    def __init__(self, fulls):
        n = len(fulls)
        self.inputs = list(fulls)
        self.out_shape = [jax.ShapeDtypeStruct(f.shape, f.dtype) for f in fulls]
        self.scratch = [pltpu.SemaphoreType.DMA((N_DEV - 1, n)), pltpu.SemaphoreType.DMA((N_DEV - 1, n)),
                        pltpu.SemaphoreType.DMA((n,))]

    def start(self, ins, outs, sems):
        me = _my_coords()
        mine = _index_of(me)
        for k, src in enumerate(ins):
            pltpu.make_async_copy(src.at[mine], outs[k].at[mine], sems[2].at[k]).start()
            for rel in range(1, N_DEV):
                to = _peer(me, rel)
                Gather._copy(src.at[_index_of(to)], outs[k].at[mine], sems, rel, k, to).start()

    def finish(self, ins, outs, sems):
        me = _my_coords()
        mine = _index_of(me)
        for rel in range(1, N_DEV):
            frm = _peer(me, rel)
            for k, src in enumerate(ins):
                Gather._copy(src.at[mine], outs[k].at[_index_of(frm)], sems, rel, k, frm).wait_recv()
        for rel in range(1, N_DEV):
            for k, src in enumerate(ins):
                Gather._copy(src.at[mine], outs[k].at[mine], sems, rel, k, _peer(me, rel)).wait_send()
        for k, src in enumerate(ins):
            pltpu.make_async_copy(src.at[mine], outs[k].at[mine], sems[2].at[k]).wait()


def _hosted_call(body, *, name, grid, in_specs, out_specs, out_shape, scratch_shapes, args, comm=None):
    sem = ("arbitrary",) * len(grid)
    if comm is None:
        res = pl.pallas_call(body, name=name, grid=grid, in_specs=in_specs, out_specs=out_specs, out_shape=out_shape,
                             scratch_shapes=scratch_shapes, compiler_params=_params(sem))(*args)
        return list(res), []
    n_in, n_out, n_scr = len(in_specs), len(out_specs), len(scratch_shapes)
    n_cin, n_cout = len(comm.inputs), len(comm.out_shape)

    def hosted(*refs):
        ins, refs = refs[:n_in], refs[n_in:]
        cins, refs = refs[:n_cin], refs[n_cin:]
        outs, refs = refs[:n_out], refs[n_out:]
        couts, refs = refs[:n_cout], refs[n_cout:]
        scr, csems = refs[:n_scr], refs[n_scr:]
        ids = [pl.program_id(a) for a in range(len(grid))]
        first = functools.reduce(lambda a, b: a & b, [i == 0 for i in ids])
        last = functools.reduce(lambda a, b: a & b, [i == g - 1 for i, g in zip(ids, grid)])

        @pl.when(first)
        def _():
            comm.start(cins, couts, csems)

        body(*ins, *outs, *scr)

        @pl.when(last)
        def _():
            comm.finish(cins, couts, csems)

    any_spec = pl.BlockSpec(memory_space=pl.ANY)
    res = pl.pallas_call(
        hosted, name=name, grid=grid, in_specs=list(in_specs) + [any_spec] * n_cin,
        out_specs=list(out_specs) + [any_spec] * n_cout, out_shape=list(out_shape) + comm.out_shape,
        scratch_shapes=list(scratch_shapes) + comm.scratch,
        compiler_params=pltpu.CompilerParams(dimension_semantics=sem, vmem_limit_bytes=VMEM_LIMIT, has_side_effects=True),
    )(*args, *comm.inputs)
    return list(res[:n_out]), list(res[n_out:])


def run_comm(comm, *, name):
    n_cin, n_cout = len(comm.inputs), len(comm.out_shape)

    def body(*refs):
        cins, couts, csems = refs[:n_cin], refs[n_cin:n_cin + n_cout], refs[n_cin + n_cout:]
        comm.start(cins, couts, csems)
        comm.finish(cins, couts, csems)

    any_spec = pl.BlockSpec(memory_space=pl.ANY)
    return pl.pallas_call(
        body, name=name, in_specs=[any_spec] * n_cin, out_specs=[any_spec] * n_cout, out_shape=comm.out_shape,
        scratch_shapes=comm.scratch, compiler_params=pltpu.CompilerParams(has_side_effects=True),
    )(*comm.inputs)


def mixer_fwd(x, g1, winT, convw, convb, clng, clnb, wpw, slng, slnb, wm, bias, wbd, pscale, wout, *, seq, tm,
              comm=None):
    T = x.shape[0]
    tiles_per_seq = seq // tm
    n_chunks = tm // CHUNK

    def body(x_ref, g1_ref, winT_ref, convw_ref, convb_ref, clng_ref, clnb_ref, wpw_ref, slng_ref, slnb_ref,
             wm_ref, bias_ref, wbd_ref, pscale_ref, wout_ref,
             z_ref, ycv_ref, p_ref, mix_ref, x1_ref, ybuf, zcbuf):
        i = pl.program_id(0)
        tile_in_seq = i % tiles_per_seq

        @pl.when(tile_in_seq == 0)
        def _():
            ybuf[0:CONV_HALO, :] = jnp.zeros((CONV_HALO, D_A), F32)
            zcbuf[0:POOL_HALO, :] = jnp.zeros((POOL_HALO, D_C), F32)

        x = x_ref[...]
        r = lax.rsqrt(jnp.mean(x * x, axis=-1, keepdims=True) + RMS_EPS)
        h = (x * r * g1_ref[...]).astype(BF16)
        z = _dot_nt(h, winT_ref[...])
        z_ref[...] = z

        y = z[:, 0:D_A] * _sigmoid(z[:, D_A:2 * D_A])
        ybuf[CONV_HALO:CONV_HALO + tm, :] = y
        acc = jnp.zeros((tm, D_A), F32) + convb_ref[...]
        for k in range(CONV_WIDTH):
            off = CONV_HALO - (CONV_WIDTH - 1) + k
            acc = acc + convw_ref[k:k + 1, :] * ybuf[off:off + tm, :]
        ybuf[0:CONV_HALO, :] = ybuf[tm:tm + CONV_HALO, :]
        ycv_ref[...] = acc
        xhat, _ = _ln_stats(acc)
        ln = xhat * clng_ref[...] + clnb_ref[...]
        s = ln * _sigmoid(ln)
        ya = _dot(s.astype(BF16), wpw_ref[...])

        gb, _ = _gelu_and_grad(z[:, 2 * D_A:2 * D_A + 2 * D_B])
        u = gb[:, 0:D_B]
        vhat, _ = _ln_stats(gb[:, D_B:2 * D_B])
        vn = vhat * slng_ref[...] + slnb_ref[...]
        masks = _head_masks(D_B)
        yb_parts = []
        for c in range(n_chunks):
            vn_c = vn[c * CHUNK:(c + 1) * CHUNK, :]
            sg = bias_ref[...]
            for hh in range(N_HEADS_B):
                sg = sg + _dot(wm_ref[hh], jnp.where(masks[hh], vn_c, 0.0).astype(BF16))
            yb_parts.append(u[c * CHUNK:(c + 1) * CHUNK, :] * sg)
        yb = jnp.concatenate(yb_parts, axis=0) if n_chunks > 1 else yb_parts[0]

        zc = z[:, 2 * D_A + 2 * D_B:D_IN]
        zcbuf[POOL_HALO:POOL_HALO + tm, :] = zc
        run = zc
        sums = []
        for j in range(1, POOL_WINDOWS[-1]):
            run = run + zcbuf[POOL_HALO - j:POOL_HALO - j + tm, :]
            if (j + 1) in POOL_WINDOWS:
                sums.append(run)
        zcbuf[0:POOL_HALO, :] = zcbuf[tm:tm + POOL_HALO, :]
        pos = (tile_in_seq * tm + lax.broadcasted_iota(jnp.int32, (tm, 1), 0)).astype(F32)
        p = _pool_select(sums, D_C) / _pool_counts(pos) - zc
        p_ref[...] = p
        yc = _dot(p.astype(BF16), wbd_ref[...]) * pscale_ref[...]

        mix = jnp.concatenate([ya, yb, yc], axis=1).astype(BF16)
        mix_ref[...] = mix
        x1_ref[...] = x + _dot(mix, wout_ref[...])

    row = lambda w: pl.BlockSpec((tm, w), lambda i: (i, 0))
    return _hosted_call(
        body, name="mixer_fwd", grid=(T // tm,),
        in_specs=[row(D_MODEL), _full((1, D_MODEL)), _full((D_IN, D_MODEL)), _full((CONV_WIDTH, D_A)),
                  _full((1, D_A)), _full((1, D_A)), _full((1, D_A)), _full((D_A, D_A)), _full((1, D_B)), _full((1, D_B)),
                  _full((N_HEADS_B, CHUNK, CHUNK)), _full((CHUNK, D_B)), _full((D_C, D_C)), _full((1, D_C)),
                  _full((D_MODEL, D_MODEL))],
        out_specs=[row(D_IN), row(D_A), row(D_C), row(D_MODEL), row(D_MODEL)],
        out_shape=[jax.ShapeDtypeStruct((T, D_IN), F32), jax.ShapeDtypeStruct((T, D_A), F32),
                   jax.ShapeDtypeStruct((T, D_C), F32), jax.ShapeDtypeStruct((T, D_MODEL), BF16),
                   jax.ShapeDtypeStruct((T, D_MODEL), F32)],
        scratch_shapes=[pltpu.VMEM((CONV_HALO + tm, D_A), F32), pltpu.VMEM((POOL_HALO + tm, D_C), F32)],
        args=(x, g1, winT, convw, convb, clng, clnb, wpw, slng, slnb, wm, bias, wbd, pscale, wout), comm=comm)


def mixer_bwd(dx1, x, z, ycv, p, g1, winT, convw, clng, clnb, wpw, slng, slnb, wm, wmT, bias, wbd, pscale, wout,
              *, seq, tm, comm=None):
    T = x.shape[0]
    tiles_per_seq = seq // tm
    n_tiles = T // tm
    n_chunks = tm // CHUNK

    def body(dx1_ref, x_ref, z_ref, ycv_ref, p_ref, g1_ref, winT_ref, convw_ref, clng_ref, clnb_ref, wpw_ref,
             slng_ref, slnb_ref, wm_ref, wmT_ref, bias_ref, wbd_ref, pscale_ref, wout_ref,
             dx_ref, dz_ref, h_ref, s_ref, dya_ref,
             dg1_ref, dconvw_ref, dconvb_ref, dclng_ref, dclnb_ref, dslng_ref, dslnb_ref, dwm_ref, dbs_ref,
             dwbd_ref, dpscale_ref, dycbuf, dpcbuf):
        i = pl.program_id(0)
        tile_in_seq = (n_tiles - 1 - i) % tiles_per_seq

        @pl.when(i == 0)
        def _():
            for ref in (dg1_ref, dconvw_ref, dconvb_ref, dclng_ref, dclnb_ref, dslng_ref, dslnb_ref, dwm_ref,
                        dbs_ref, dwbd_ref, dpscale_ref):
                ref[...] = jnp.zeros(ref.shape, F32)

        @pl.when(tile_in_seq == tiles_per_seq - 1)
        def _():
            dycbuf[tm:tm + CONV_HALO, :] = jnp.zeros((CONV_HALO, D_A), F32)
            dpcbuf[tm:tm + POOL_HALO, :] = jnp.zeros((POOL_HALO, D_C), F32)

        dx1 = dx1_ref[...]
        z = z_ref[...]
        dmix = _dot_nt(dx1.astype(BF16), wout_ref[...])
        dya = dmix[:, 0:D_A]
        dyb = dmix[:, D_A:D_A + D_B]
        dyc = dmix[:, D_A + D_B:D_MODEL]

        p = p_ref[...]
        pb = p.astype(BF16)
        q = _dot(pb, wbd_ref[...])
        dpscale_ref[...] += jnp.sum(dyc * q, axis=0, keepdims=True)
        dq = (dyc * pscale_ref[...]).astype(BF16)
        dwbd_ref[...] += _dot_tn(pb, dq)
        dp = _dot_nt(dq, wbd_ref[...])
        pos = (tile_in_seq * tm + lax.broadcasted_iota(jnp.int32, (tm, 1), 0)).astype(F32)
        dpc = dp / _pool_counts(pos)
        dpcbuf[0:tm, :] = dpc
        run = dpc
        sums = []
        for j in range(1, POOL_WINDOWS[-1]):
            run = run + dpcbuf[j:j + tm, :]
            if (j + 1) in POOL_WINDOWS:
                sums.append(run)
        dpcbuf[tm:tm + POOL_HALO, :] = dpcbuf[0:POOL_HALO, :]
        dzc = _pool_select(sums, D_C) - dp

        dya_ref[...] = dya.astype(BF16)
        ds = _dot_nt(dya.astype(BF16), wpw_ref[...])
        xhat, rstd = _ln_stats(ycv_ref[...])
        ln = xhat * clng_ref[...] + clnb_ref[...]
        sg = _sigmoid(ln)
        s_ref[...] = (ln * sg).astype(BF16)
        dln = ds * (sg * (1.0 + ln * (1.0 - sg)))
        dclng_ref[...] += jnp.sum(dln * xhat, axis=0, keepdims=True)
        dclnb_ref[...] += jnp.sum(dln, axis=0, keepdims=True)
        dycv = _ln_bwd(dln, xhat, rstd, clng_ref[...])
        dconvb_ref[...] += jnp.sum(dycv, axis=0, keepdims=True)
        a = z[:, 0:D_A]
        sgate = _sigmoid(z[:, D_A:2 * D_A])
        y = a * sgate
        dycbuf[0:tm, :] = dycv
        dy = jnp.zeros((tm, D_A), F32)
        for d in range(CONV_WIDTH):
            k = CONV_WIDTH - 1 - d
            sh = dycbuf[d:d + tm, :]
            dy = dy + convw_ref[k:k + 1, :] * sh
            dconvw_ref[k:k + 1, :] += jnp.sum(y * sh, axis=0, keepdims=True)
        dycbuf[tm:tm + CONV_HALO, :] = dycbuf[0:CONV_HALO, :]
        da = dy * sgate
        dgate = dy * a * sgate * (1.0 - sgate)

        gb, dgb = _gelu_and_grad(z[:, 2 * D_A:2 * D_A + 2 * D_B])
        u = gb[:, 0:D_B]
        vhat, vrstd = _ln_stats(gb[:, D_B:2 * D_B])
        vn = vhat * slng_ref[...] + slnb_ref[...]
        masks = _head_masks(D_B)
        tril = (lax.broadcasted_iota(jnp.int32, (CHUNK, CHUNK), 0)
                >= lax.broadcasted_iota(jnp.int32, (CHUNK, CHUNK), 1))
        lane128 = lax.broadcasted_iota(jnp.int32, (1, CHUNK), 1)
        du_parts, dvn_parts = [], []
        for c in range(n_chunks):
            rows = slice(c * CHUNK, (c + 1) * CHUNK)
            vn_c = vn[rows, :]
            vh = [jnp.where(masks[hh], vn_c, 0.0).astype(BF16) for hh in range(N_HEADS_B)]
            sgc = bias_ref[...]
            for hh in range(N_HEADS_B):
                sgc = sgc + _dot(wm_ref[hh], vh[hh])
            dyb_c = dyb[rows, :]
            du_parts.append(dyb_c * sgc)
            dsg = dyb_c * u[rows, :]
            dvn_c = jnp.zeros((CHUNK, D_B), F32)
            dbs = jnp.zeros((CHUNK, CHUNK), F32)
            for hh in range(N_HEADS_B):
                dsg_h = jnp.where(masks[hh], dsg, 0.0)
                dsg_hb = dsg_h.astype(BF16)
                dwm_ref[hh] += jnp.where(tril, _dot_nt(dsg_hb, vh[hh]), 0.0)
                dvn_c = dvn_c + _dot(wmT_ref[hh], dsg_hb)
                dbs = dbs + jnp.where(lane128 == hh, jnp.sum(dsg_h, axis=1, keepdims=True), 0.0)
            dbs_ref[...] += dbs
            dvn_parts.append(dvn_c)
        du = jnp.concatenate(du_parts, axis=0) if n_chunks > 1 else du_parts[0]
        dvn = jnp.concatenate(dvn_parts, axis=0) if n_chunks > 1 else dvn_parts[0]
        dslng_ref[...] += jnp.sum(dvn * vhat, axis=0, keepdims=True)
        dslnb_ref[...] += jnp.sum(dvn, axis=0, keepdims=True)
        dv = _ln_bwd(dvn, vhat, vrstd, slng_ref[...])
        dzb = jnp.concatenate([du, dv], axis=1) * dgb

        dz = jnp.concatenate([da, dgate, dzb, dzc], axis=1).astype(BF16)
        dz_ref[...] = dz
        dh = _dot(dz, winT_ref[...])
        x = x_ref[...]
        r = lax.rsqrt(jnp.mean(x * x, axis=-1, keepdims=True) + RMS_EPS)
        xn = x * r
        h_ref[...] = (xn * g1_ref[...]).astype(BF16)
        dg1_ref[...] += jnp.sum(dh * xn, axis=0, keepdims=True)
        dx_ref[...] = dx1 + _rms_bwd(dh, xn, r, g1_ref[...])

    row = lambda w: pl.BlockSpec((tm, w), lambda i: (n_tiles - 1 - i, 0))
    acc_shapes = [(1, D_MODEL), (CONV_WIDTH, D_A), (1, D_A), (1, D_A), (1, D_A), (1, D_B), (1, D_B),
                  (N_HEADS_B, CHUNK, CHUNK), (CHUNK, CHUNK), (D_C, D_C), (1, D_C)]
    return _hosted_call(
        body, name="mixer_bwd", grid=(n_tiles,),
        in_specs=[row(D_MODEL), row(D_MODEL), row(D_IN), row(D_A), row(D_C),
                  _full((1, D_MODEL)), _full((D_IN, D_MODEL)), _full((CONV_WIDTH, D_A)), _full((1, D_A)), _full((1, D_A)),
                  _full((D_A, D_A)), _full((1, D_B)), _full((1, D_B)), _full((N_HEADS_B, CHUNK, CHUNK)),
                  _full((N_HEADS_B, CHUNK, CHUNK)), _full((CHUNK, D_B)), _full((D_C, D_C)), _full((1, D_C)),
                  _full((D_MODEL, D_MODEL))],
        out_specs=[row(D_MODEL), row(D_IN), row(D_MODEL), row(D_A), row(D_A)] + [_full(s) for s in acc_shapes],
        out_shape=[jax.ShapeDtypeStruct((T, D_MODEL), F32), jax.ShapeDtypeStruct((T, D_IN), BF16),
                   jax.ShapeDtypeStruct((T, D_MODEL), BF16), jax.ShapeDtypeStruct((T, D_A), BF16),
                   jax.ShapeDtypeStruct((T, D_A), BF16)] + [jax.ShapeDtypeStruct(s, F32) for s in acc_shapes],
        scratch_shapes=[pltpu.VMEM((tm + CONV_HALO, D_A), F32), pltpu.VMEM((tm + POOL_HALO, D_C), F32)],
        args=(dx1, x, z, ycv, p, g1, winT, convw, clng, clnb, wpw, slng, slnb, wm, wmT, bias, wbd, pscale, wout),
        comm=comm)


def ffn_fwd(x1, g2, wguT, wd, *, tm, th, comm=None):
    T = x1.shape[0]
    n_h = D_FF // th

    def body(x1_ref, g2_ref, wgu_ref, wd_ref, x2_ref, gu_ref, h2_buf, acc):
        j = pl.program_id(1)

        @pl.when(j == 0)
        def _():
            x = x1_ref[...]
            r = lax.rsqrt(jnp.mean(x * x, axis=-1, keepdims=True) + RMS_EPS)
            h2_buf[...] = (x * r * g2_ref[...]).astype(BF16)
            acc[...] = x

        h2 = h2_buf[...]
        g = _dot_nt(h2, wgu_ref[0])
        u = _dot_nt(h2, wgu_ref[1])
        gu_ref[0] = g.astype(BF16)
        gu_ref[1] = u.astype(BF16)
        f = (g * _sigmoid(g) * u).astype(BF16)
        acc[...] += _dot(f, wd_ref[...])

        @pl.when(j == n_h - 1)
        def _():
            x2_ref[...] = acc[...]

    return _hosted_call(
        body, name="ffn_fwd", grid=(T // tm, n_h),
        in_specs=[pl.BlockSpec((tm, D_MODEL), lambda i, j: (i, 0)), _full((1, D_MODEL)),
                  pl.BlockSpec((2, th, D_MODEL), lambda i, j: (0, j, 0)),
                  pl.BlockSpec((th, D_MODEL), lambda i, j: (j, 0))],
        out_specs=[pl.BlockSpec((tm, D_MODEL), lambda i, j: (i, 0)),
                   pl.BlockSpec((2, tm, th), lambda i, j: (0, i, j))],
        out_shape=[jax.ShapeDtypeStruct((T, D_MODEL), F32), jax.ShapeDtypeStruct((2, T, D_FF), BF16)],
        scratch_shapes=[pltpu.VMEM((tm, D_MODEL), BF16), pltpu.VMEM((tm, D_MODEL), F32)],
        args=(x1, g2, wguT, wd), comm=comm)


def ffn_bwd(dx2, x1, gu, g2, wguT, wd, *, tm, th, comm=None):
    T = x1.shape[0]
    n_h = D_FF // th

    def body(dx2_ref, x1_ref, gu_ref, g2_ref, wgu_ref, wd_ref, dx1_ref, h2_ref, f_ref, dgu_ref, dg2_ref, acc):
        i = pl.program_id(0)
        j = pl.program_id(1)

        @pl.when((i == 0) & (j == 0))
        def _():
            dg2_ref[...] = jnp.zeros(dg2_ref.shape, F32)

        dx2 = dx2_ref[...]
        df = _dot_nt(dx2.astype(BF16), wd_ref[...])
        g = gu_ref[0].astype(F32)
        u = gu_ref[1].astype(F32)
        sg = _sigmoid(g)
        silu = g * sg
        f_ref[...] = (silu * u).astype(BF16)
        dgate = (df * u * (sg * (1.0 + g * (1.0 - sg)))).astype(BF16)
        dup = (df * silu).astype(BF16)
        dgu_ref[0] = dgate
        dgu_ref[1] = dup
        part = _dot(dgate, wgu_ref[0]) + _dot(dup, wgu_ref[1])

        @pl.when(j == 0)
        def _():
            acc[...] = part

        @pl.when(j > 0)
        def _():
            acc[...] += part

        @pl.when(j == n_h - 1)
        def _():
            x = x1_ref[...]
            r = lax.rsqrt(jnp.mean(x * x, axis=-1, keepdims=True) + RMS_EPS)
            xn = x * r
            dh = acc[...]
            h2_ref[...] = (xn * g2_ref[...]).astype(BF16)
            dg2_ref[...] += jnp.sum(dh * xn, axis=0, keepdims=True)
            dx1_ref[...] = dx2 + _rms_bwd(dh, xn, r, g2_ref[...])

    return _hosted_call(
        body, name="ffn_bwd", grid=(T // tm, n_h),
        in_specs=[pl.BlockSpec((tm, D_MODEL), lambda i, j: (i, 0)), pl.BlockSpec((tm, D_MODEL), lambda i, j: (i, 0)),
                  pl.BlockSpec((2, tm, th), lambda i, j: (0, i, j)), _full((1, D_MODEL)),
                  pl.BlockSpec((2, th, D_MODEL), lambda i, j: (0, j, 0)),
                  pl.BlockSpec((th, D_MODEL), lambda i, j: (j, 0))],
        out_specs=[pl.BlockSpec((tm, D_MODEL), lambda i, j: (i, 0)), pl.BlockSpec((tm, D_MODEL), lambda i, j: (i, 0)),
                   pl.BlockSpec((tm, th), lambda i, j: (i, j)), pl.BlockSpec((2, tm, th), lambda i, j: (0, i, j)),
                   _full((1, D_MODEL))],
        out_shape=[jax.ShapeDtypeStruct((T, D_MODEL), F32), jax.ShapeDtypeStruct((T, D_MODEL), BF16),
                   jax.ShapeDtypeStruct((T, D_FF), BF16), jax.ShapeDtypeStruct((2, T, D_FF), BF16),
                   jax.ShapeDtypeStruct((1, D_MODEL), F32)],
        scratch_shapes=[pltpu.VMEM((tm, D_MODEL), F32)],
        args=(dx2, x1, gu, g2, wguT, wd), comm=comm)


def head_fwd_bwd(x, target, fg, *, tm):
    T = x.shape[0]
    n_tiles = T // tm

    def body(x_ref, t_ref, fg_ref, loss_ref, dx_ref, dfg_ref, lacc):
        i = pl.program_id(0)

        @pl.when(i == 0)
        def _():
            lacc[...] = jnp.zeros(lacc.shape, F32)
            dfg_ref[...] = jnp.zeros(dfg_ref.shape, F32)

        x = x_ref[...]
        r = lax.rsqrt(jnp.mean(x * x, axis=-1, keepdims=True) + RMS_EPS)
        xn = x * r
        e = xn * fg_ref[...] - t_ref[...]
        lacc[...] += jnp.sum(e * e, axis=0, keepdims=True)
        dy = e * (1.0 / D_MODEL)
        dfg_ref[...] += jnp.sum(dy * xn, axis=0, keepdims=True)
        dx_ref[...] = _rms_bwd(dy, xn, r, fg_ref[...])

        @pl.when(i == n_tiles - 1)
        def _():
            loss_ref[...] = jnp.sum(lacc[...], axis=1, keepdims=True) * (0.5 / D_MODEL)

    row = pl.BlockSpec((tm, D_MODEL), lambda i: (i, 0))
    return pl.pallas_call(
        body, name="head_fwd_bwd", grid=(n_tiles,),
        in_specs=[row, row, _full((1, D_MODEL))],
        out_specs=[_full((1, 1)), row, _full((1, D_MODEL))],
        out_shape=[jax.ShapeDtypeStruct((1, 1), F32), jax.ShapeDtypeStruct((T, D_MODEL), F32),
                   jax.ShapeDtypeStruct((1, D_MODEL), F32)],
        scratch_shapes=[pltpu.VMEM((1, D_MODEL), F32)],
        compiler_params=_params(("arbitrary",)),
    )(x, target, fg)


def wgrad(a, b, *, tmo, tk, name):
    G, T, M = a.shape
    N = b.shape[1]
    n_k = T // tk

    def body(a_ref, b_ref, o_ref, acc):
        k = pl.program_id(2)
        part = _dot_tn(a_ref[0].astype(BF16), b_ref[...].astype(BF16))
        if n_k == 1:
            o_ref[0] = part.astype(BF16)
            return

        @pl.when(k == 0)
        def _():
            acc[...] = part

        @pl.when((k > 0) & (k < n_k - 1))
        def _():
            acc[...] += part

        @pl.when(k == n_k - 1)
        def _():
            o_ref[0] = (acc[...] + part).astype(BF16)

    return pl.pallas_call(
        body, name=name, grid=(G, M // tmo, n_k),
        in_specs=[pl.BlockSpec((1, tk, tmo), lambda g, m, k: (g, k, m)),
                  pl.BlockSpec((tk, N), lambda g, m, k: (k, 0))],
        out_specs=pl.BlockSpec((1, tmo, N), lambda g, m, k: (g, m, 0)),
        out_shape=jax.ShapeDtypeStruct((G, M, N), BF16),
        scratch_shapes=[pltpu.VMEM((tmo, N), F32)],
        compiler_params=_params(("arbitrary", "arbitrary", "arbitrary")),
    )(a, b)


def all_reduce_small(buf):
    _, R, W = buf.shape

    def body(in_ref, out_ref, land, send1, recv1, send2, recv2):
        me = _my_coords()
        mine = _index_of(me)
        sends = []
        for rel in range(1, N_DEV):
            to = _peer(me, rel)
            cp = pltpu.make_async_remote_copy(
                src_ref=in_ref.at[_index_of(to)], dst_ref=land.at[mine], send_sem=send1.at[rel - 1],
                recv_sem=recv1.at[rel - 1], device_id=to, device_id_type=MESH_ID)
            cp.start()
            sends.append(cp)
        land[mine] = in_ref[mine]
        for rel in range(1, N_DEV):
            frm = _peer(me, rel)
            pltpu.make_async_remote_copy(
                src_ref=in_ref.at[mine], dst_ref=land.at[_index_of(frm)], send_sem=send1.at[rel - 1],
                recv_sem=recv1.at[rel - 1], device_id=frm, device_id_type=MESH_ID).wait_recv()
        total = land[0]
        for d in range(1, N_DEV):
            total = total + land[d]
        out_ref[mine] = total
        for rel in range(1, N_DEV):
            to = _peer(me, rel)
            cp = pltpu.make_async_remote_copy(
                src_ref=out_ref.at[mine], dst_ref=out_ref.at[mine], send_sem=send2.at[rel - 1],
                recv_sem=recv2.at[rel - 1], device_id=to, device_id_type=MESH_ID)
            cp.start()
            sends.append(cp)
        for rel in range(1, N_DEV):
            frm = _peer(me, rel)
            pltpu.make_async_remote_copy(
                src_ref=out_ref.at[mine], dst_ref=out_ref.at[_index_of(frm)], send_sem=send2.at[rel - 1],
                recv_sem=recv2.at[rel - 1], device_id=frm, device_id_type=MESH_ID).wait_recv()
        for cp in sends:
            cp.wait_send()

    vmem = pl.BlockSpec(memory_space=pltpu.VMEM)
    return pl.pallas_call(
        body, name="all_reduce_small",
        in_specs=[vmem], out_specs=vmem,
        out_shape=jax.ShapeDtypeStruct(buf.shape, F32),
        scratch_shapes=[pltpu.VMEM(buf.shape, F32)] + [pltpu.SemaphoreType.DMA((N_DEV - 1,))] * 4,
        compiler_params=pltpu.CompilerParams(has_side_effects=True, vmem_limit_bytes=VMEM_LIMIT),
    )(buf)


_ADAM_C1 = 1.0 - ADAM_B1 ** ADAM_STEP
_ADAM_C2 = 1.0 - ADAM_B2 ** ADAM_STEP


def _adamw_math(w, g, m, v):
    m = ADAM_B1 * m + (1.0 - ADAM_B1) * g
    v = ADAM_B2 * v + (1.0 - ADAM_B2) * (g * g)
    m_hat = m / _ADAM_C1
    v_hat = v / _ADAM_C2
    delta = -ADAM_LR * (m_hat / (jnp.sqrt(v_hat) + ADAM_EPS) + ADAM_WD * w)
    return delta, m, v


def adamw_sharded(parts, w, m, v, *, tr, name):
    _, R, C = parts[0].shape

    def body(p0_ref, p1_ref, w_ref, m_ref, v_ref, g_ref, d_ref, nm_ref, nv_ref):
        def update(p_ref):
            g = p_ref[0].astype(F32)
            for d in range(1, N_DEV):
                g = g + p_ref[d].astype(F32)
            delta, nm, nv = _adamw_math(w_ref[0], g, m_ref[0], v_ref[0])
            g_ref[0] = g
            d_ref[0] = delta
            nm_ref[0] = nm
            nv_ref[0] = nv

        @pl.when(pl.program_id(0) == 0)
        def _():
            update(p0_ref)

        @pl.when(pl.program_id(0) == 1)
        def _():
            update(p1_ref)

    n_i = R // tr
    p_specs = [pl.BlockSpec((N_DEV, tr, C), lambda l, i: (0, jnp.where(l == 0, i, n_i - 1), 0)),
               pl.BlockSpec((N_DEV, tr, C), lambda l, i: (0, jnp.where(l == 1, i, 0), 0))]
    o_spec = pl.BlockSpec((1, tr, C), lambda l, i: (l, i, 0))
    return pl.pallas_call(
        body, name=name, grid=(DEPTH, n_i),
        in_specs=p_specs + [o_spec, o_spec, o_spec], out_specs=[o_spec] * 4,
        out_shape=[jax.ShapeDtypeStruct(w.shape, F32)] * 4,
        compiler_params=_params(("arbitrary", "arbitrary")),
    )(parts[0], parts[1], w, m, v)


def adamw_small(g, w, m, v):
    def body(g_ref, w_ref, m_ref, v_ref, d_ref, nm_ref, nv_ref):
        delta, nm, nv = _adamw_math(w_ref[...], g_ref[...], m_ref[...], v_ref[...])
        d_ref[...] = delta
        nm_ref[...] = nm
        nv_ref[...] = nv

    vmem = pl.BlockSpec(memory_space=pltpu.VMEM)
    return pl.pallas_call(
        body, name="adamw_small", in_specs=[vmem] * 4, out_specs=[vmem] * 3,
        out_shape=[jax.ShapeDtypeStruct(g.shape, F32)] * 3,
        compiler_params=pltpu.CompilerParams(vmem_limit_bytes=VMEM_LIMIT),
    )(g, w, m, v)


LANES = 128
SUBLANES = 8


def _pack(arrays, row_multiple):
    flat = jnp.concatenate([a.reshape(-1) for a in arrays])
    rows = -(-flat.shape[0] // (LANES * row_multiple)) * row_multiple
    return jnp.pad(flat, (0, rows * LANES - flat.shape[0])).reshape(rows, LANES)


def _unpack(buf, shapes):
    flat = buf.reshape(-1)
    out, off = [], 0
    for s in shapes:
        n = math.prod(s)
        out.append(flat[off:off + n].reshape(s))
        off += n
    return out


def _block_diag(w_pool):
    G, d, _ = w_pool.shape
    eye = jnp.eye(G, dtype=w_pool.dtype)
    return (eye[:, None, :, None] * w_pool[:, :, None, :]).reshape(G * d, G * d)


def kernel(x, norm1_g, w_in, conv_w, conv_b, conv_ln_g, conv_ln_b, w_pw, sg_ln_g, sg_ln_b, w_s, b_s, w_pool, pool_scale, w_out, norm2_g, w_gate_up, w_down, final_g, loss_target, m_norm1_g, m_w_in, m_conv_w, m_conv_b, m_conv_ln_g, m_conv_ln_b, m_w_pw, m_sg_ln_g, m_sg_ln_b, m_w_s, m_b_s, m_w_pool, m_pool_scale, m_w_out, m_norm2_g, m_w_gate_up, m_w_down, m_final_g, v_norm1_g, v_w_in, v_conv_w, v_conv_b, v_conv_ln_g, v_conv_ln_b, v_w_pw, v_sg_ln_g, v_sg_ln_b, v_w_s, v_b_s, v_w_pool, v_pool_scale, v_w_out, v_norm2_g, v_w_gate_up, v_w_down, v_final_g):
    b_loc, seq, _ = x.shape
    T = b_loc * seq
    tm_mix = min(256, seq)
    tm_ffn_fwd = min(512, T)
    tm_ffn_bwd = min(256, T)
    tm_head = min(512, T)
    tk = min(2048, T)
    tk_f32 = min(1024, T)
    th = D_FF // 2
    cw = conv_w.shape[2]
    my_index = _index_of(_my_coords())

    xf = x.reshape(T, D_MODEL)
    tgt = loss_target.reshape(T, D_MODEL)

    mixer_shards = [[w_in[l].T.astype(BF16), w_out[l].astype(BF16), w_pw[l].astype(BF16)] for l in range(DEPTH)]
    ffn_shards = [[w_gate_up[l].T.astype(BF16), w_down[l].astype(BF16)] for l in range(DEPTH)]

    tril = jnp.tril(jnp.ones((CHUNK, CHUNK), dtype=bool))
    layers = []
    for l in range(DEPTH):
        wm = jnp.where(tril[None], w_s[l], 0.0).astype(BF16)
        layers.append(dict(
            g1=norm1_g[l][None], convb=conv_b[l][None], clng=conv_ln_g[l][None], clnb=conv_ln_b[l][None],
            slng=sg_ln_g[l][None], slnb=sg_ln_b[l][None], wm=wm, wmT=jnp.swapaxes(wm, 1, 2),
            bias=jnp.repeat(b_s[l].T, HEAD_DIM_B, axis=1), wbd=_block_diag(w_pool[l]).astype(BF16),
            pscale=pool_scale[l][None], g2=norm2_g[l][None]))

    def set_mixer_weights(l, g_in, g_out, g_pw):
        layers[l].update(winT=g_in.reshape(D_IN, D_MODEL), wout=g_out.reshape(D_MODEL, D_MODEL), wpw=g_pw.reshape(D_A, D_A))

    def set_ffn_weights(l, g_gu, g_d):
        layers[l].update(wguT=g_gu.reshape(2, D_FF, D_MODEL), wd=g_d.reshape(D_FF, D_MODEL))

    first = run_comm(Gather(mixer_shards[0] + [conv_w.reshape(DEPTH * CONV_WIDTH, cw).T]), name="gather_first")
    set_mixer_weights(0, *first[:3])
    convw_full = first[3].reshape(D_A, DEPTH * CONV_WIDTH).T.reshape(DEPTH, CONV_WIDTH, D_A)
    for l in range(DEPTH):
        layers[l]["convw"] = convw_full[l]

    saved = []
    cur = xf
    for l in range(DEPTH):
        w = layers[l]
        (z, ycv, p, mix, x1), got = mixer_fwd(
            cur, w["g1"], w["winT"], w["convw"], w["convb"], w["clng"], w["clnb"], w["wpw"], w["slng"], w["slnb"], w["wm"],
            w["bias"], w["wbd"], w["pscale"], w["wout"], seq=seq, tm=tm_mix, comm=Gather(ffn_shards[l]))
        set_ffn_weights(l, *got)
        (x2, gu), got = ffn_fwd(x1, w["g2"], w["wguT"], w["wd"], tm=tm_ffn_fwd, th=th,
                                comm=Gather(mixer_shards[l + 1]) if l + 1 < DEPTH else None)
        if l + 1 < DEPTH:
            set_mixer_weights(l + 1, *got)
        saved.append((cur, z, ycv, p, mix, x1, gu))
        cur = x2
    loss_part, dx, dfg = head_fwd_bwd(cur, tgt, final_g[None], tm=tm_head)
    loss = lax.psum(loss_part[0, 0], ("x", "y", "c"))

    small = [None] * DEPTH
    parts = {}
    pending = None
    for l in reversed(range(DEPTH)):
        w = layers[l]
        x0, z, ycv, p, mix, x1, gu = saved[l]
        (dx1, h2, f, dgu, dg2), got = ffn_bwd(dx, x1, gu, w["g2"], w["wguT"], w["wd"], tm=tm_ffn_bwd, th=th,
                                              comm=Exchange(pending[1]) if pending else None)
        if pending:
            parts.update(zip(pending[0], got))
        gw_gu = wgrad(dgu, h2, tmo=th, tk=tk, name="wgrad_gate_up").reshape(N_DEV, 2 * D_FF // N_DEV, D_MODEL)
        gw_d = wgrad(f[None], dx, tmo=th, tk=tk_f32, name="wgrad_down").reshape(N_DEV, D_FF // N_DEV, D_MODEL)
        outs, got = mixer_bwd(
            dx1, x0, z, ycv, p, w["g1"], w["winT"], w["convw"], w["clng"], w["clnb"], w["wpw"], w["slng"], w["slnb"],
            w["wm"], w["wmT"], w["bias"], w["wbd"], w["pscale"], w["wout"], seq=seq, tm=tm_mix, comm=Exchange([gw_gu]))
        parts[("gu", l)], = got
        (dx, dz, h, s, dya, dg1, dconvw, dconvb, dclng, dclnb, dslng, dslnb, dwm, dbs, dwbd, dpscale) = outs
        gw_out = wgrad(mix[None], dx1, tmo=D_MODEL, tk=tk_f32, name="wgrad_out").reshape(N_DEV, D_MODEL // N_DEV, D_MODEL)
        gw_in = wgrad(dz[None], h, tmo=D_IN // 2, tk=tk, name="wgrad_in").reshape(N_DEV, D_IN // N_DEV, D_MODEL)
        gw_pw = wgrad(s[None], dya, tmo=D_A, tk=tk, name="wgrad_pw").reshape(N_DEV, D_A // N_DEV, D_A)
        pending = ([("d", l), ("in", l), ("out", l), ("pw", l)], [gw_d, gw_in, gw_out, gw_pw])
        small[l] = (dg1, dconvw, dconvb, dclng, dclnb, dslng, dslnb, dwm, dbs, dwbd, dpscale, dg2)
    grad_x = dx.reshape(x.shape)
    parts.update(zip(pending[0], run_comm(Exchange(pending[1]), name="exchange_last")))
    p_in, p_gu, p_d, p_out, p_pw = [[parts[(k, l)] for l in range(DEPTH)] for k in ("in", "gu", "d", "out", "pw")]

    small_arrays = [a for l in range(DEPTH) for a in small[l]] + [dfg]
    small_shapes = [a.shape for a in small_arrays]
    packed = _pack(small_arrays, N_DEV * SUBLANES)
    summed = all_reduce_small(packed.reshape(N_DEV, packed.shape[0] // N_DEV, LANES))
    sums = _unpack(summed, small_shapes)
    per_layer = len(small[0])
    g_small = {k: [] for k in ("norm1_g", "conv_w", "conv_b", "conv_ln_g", "conv_ln_b", "sg_ln_g", "sg_ln_b", "w_s", "b_s",
                               "w_pool", "pool_scale", "norm2_g")}
    for l in range(DEPTH):
        dg1, dconvw, dconvb, dclng, dclnb, dslng, dslnb, dwm, dbs, dwbd, dpscale, dg2 = sums[per_layer * l:per_layer * (l + 1)]
        g_small["norm1_g"].append(dg1[0])
        g_small["conv_w"].append(lax.dynamic_slice_in_dim(dconvw, my_index * cw, cw, axis=1))
        g_small["conv_b"].append(dconvb[0])
        g_small["conv_ln_g"].append(dclng[0])
        g_small["conv_ln_b"].append(dclnb[0])
        g_small["sg_ln_g"].append(dslng[0])
        g_small["sg_ln_b"].append(dslnb[0])
        g_small["w_s"].append(dwm)
        g_small["b_s"].append(dbs[:, :N_HEADS_B].T)
        g_small["w_pool"].append(jnp.stack([dwbd[g * GROUP_DIM_C:(g + 1) * GROUP_DIM_C, g * GROUP_DIM_C:(g + 1) * GROUP_DIM_C]
                                            for g in range(len(POOL_WINDOWS))]))
        g_small["pool_scale"].append(dpscale[0])
        g_small["norm2_g"].append(dg2[0])
    g_small = {k: jnp.stack(v) for k, v in g_small.items()}
    g_small["final_g"] = sums[-1][0]

    t = lambda a: jnp.swapaxes(a, 1, 2)
    g_w_in, d_w_in, nm_w_in, nv_w_in = map(t, adamw_sharded(p_in, t(w_in), t(m_w_in), t(v_w_in), tr=D_IN // N_DEV // 2,
                                                            name="adamw_w_in"))
    g_w_gu, d_w_gu, nm_w_gu, nv_w_gu = map(t, adamw_sharded(p_gu, t(w_gate_up), t(m_w_gate_up), t(v_w_gate_up),
                                                            tr=2 * D_FF // N_DEV // 4, name="adamw_w_gate_up"))
    g_w_d, d_w_d, nm_w_d, nv_w_d = adamw_sharded(p_d, w_down, m_w_down, v_w_down, tr=D_FF // N_DEV // 2, name="adamw_w_down")
    g_w_out, d_w_out, nm_w_out, nv_w_out = adamw_sharded(p_out, w_out, m_w_out, v_w_out, tr=D_MODEL // N_DEV,
                                                         name="adamw_w_out")
    g_w_pw, d_w_pw, nm_w_pw, nv_w_pw = adamw_sharded(p_pw, w_pw, m_w_pw, v_w_pw, tr=D_A // N_DEV, name="adamw_w_pw")

    small_names = ["norm1_g", "conv_w", "conv_b", "conv_ln_g", "conv_ln_b", "sg_ln_g", "sg_ln_b", "w_s", "b_s", "w_pool",
                   "pool_scale", "norm2_g", "final_g"]
    small_w = dict(norm1_g=norm1_g, conv_w=conv_w, conv_b=conv_b, conv_ln_g=conv_ln_g, conv_ln_b=conv_ln_b, sg_ln_g=sg_ln_g,
                   sg_ln_b=sg_ln_b, w_s=w_s, b_s=b_s, w_pool=w_pool, pool_scale=pool_scale, norm2_g=norm2_g, final_g=final_g)
    small_m = dict(norm1_g=m_norm1_g, conv_w=m_conv_w, conv_b=m_conv_b, conv_ln_g=m_conv_ln_g, conv_ln_b=m_conv_ln_b,
                   sg_ln_g=m_sg_ln_g, sg_ln_b=m_sg_ln_b, w_s=m_w_s, b_s=m_b_s, w_pool=m_w_pool, pool_scale=m_pool_scale,
                   norm2_g=m_norm2_g, final_g=m_final_g)
    small_v = dict(norm1_g=v_norm1_g, conv_w=v_conv_w, conv_b=v_conv_b, conv_ln_g=v_conv_ln_g, conv_ln_b=v_conv_ln_b,
                   sg_ln_g=v_sg_ln_g, sg_ln_b=v_sg_ln_b, w_s=v_w_s, b_s=v_b_s, w_pool=v_w_pool, pool_scale=v_pool_scale,
                   norm2_g=v_norm2_g, final_g=v_final_g)
    shapes = [small_w[k].shape for k in small_names]
    d_s, nm_s, nv_s = adamw_small(_pack([g_small[k] for k in small_names], SUBLANES),
                                  _pack([small_w[k] for k in small_names], SUBLANES),
                                  _pack([small_m[k] for k in small_names], SUBLANES),
                                  _pack([small_v[k] for k in small_names], SUBLANES))
    d_small = dict(zip(small_names, _unpack(d_s, shapes)))
    nm_small = dict(zip(small_names, _unpack(nm_s, shapes)))
    nv_small = dict(zip(small_names, _unpack(nv_s, shapes)))

    order = ["norm1_g", "w_in", "conv_w", "conv_b", "conv_ln_g", "conv_ln_b", "w_pw", "sg_ln_g", "sg_ln_b", "w_s", "b_s",
             "w_pool", "pool_scale", "w_out", "norm2_g", "w_gate_up", "w_down", "final_g"]
    grads = dict(g_small, w_in=g_w_in, w_pw=g_w_pw, w_out=g_w_out, w_gate_up=g_w_gu, w_down=g_w_d)
    deltas = dict(d_small, w_in=d_w_in, w_pw=d_w_pw, w_out=d_w_out, w_gate_up=d_w_gu, w_down=d_w_d)
    new_m = dict(nm_small, w_in=nm_w_in, w_pw=nm_w_pw, w_out=nm_w_out, w_gate_up=nm_w_gu, w_down=nm_w_d)
    new_v = dict(nv_small, w_in=nv_w_in, w_pw=nv_w_pw, w_out=nv_w_out, w_gate_up=nv_w_gu, w_down=nv_w_d)
    return (loss, grad_x, *[grads[k] for k in order], *[deltas[k] for k in order], *[new_m[k] for k in order],
            *[new_v[k] for k in order])
```

```python
import functools
import math

import jax
import jax.numpy as jnp
from jax import lax
from jax.experimental import pallas as pl
from jax.experimental.pallas import tpu as pltpu

F32 = jnp.float32
BF16 = jnp.bfloat16

D_MODEL = 1024
D_A = 384
D_B = 384
D_C = 256
D_IN = 2 * D_A + 2 * D_B + D_C
N_HEADS_B = 4
HEAD_DIM_B = 96
POOL_WINDOWS = (2, 4, 8, 16)
GROUP_DIM_C = 64
CONV_WIDTH = 31
CHUNK = 128
D_FF = 2816
RMS_EPS = 1e-6
LN_EPS = 1e-5
DEPTH = 2
N_DEV = 8

ADAM_LR = 0.001
ADAM_B1 = 0.9
ADAM_B2 = 0.999
ADAM_EPS = 1e-08
ADAM_WD = 0.01
ADAM_STEP = 10

LANES = 128
SUBLANES = 8

CONV_HALO = 32
POOL_HALO = 32
assert POOL_WINDOWS == (2, 4, 8, 16) and POOL_HALO == SUBLANES * len(POOL_WINDOWS)

VMEM_LIMIT = 56 * 1024 * 1024

MESH_ID = pl.DeviceIdType.MESH


def _dot(a, b):
    return jnp.dot(a, b, preferred_element_type=F32)


def _dot_nt(a, b):
    return lax.dot_general(a, b, (((1,), (1,)), ((), ())), preferred_element_type=F32)


def _dot_tn(a, b):
    return lax.dot_general(a, b, (((0,), (0,)), ((), ())), preferred_element_type=F32)


def _sigmoid(x):
    return 0.5 * jnp.tanh(0.5 * x) + 0.5


def _shifted_taps(buf, first_row, n_shifts, tm):
    for phase in range(min(SUBLANES, n_shifts)):
        shifts = list(range(phase, n_shifts, SUBLANES))
        span = buf[first_row + phase:first_row + shifts[-1] + tm, :]
        for s in shifts:
            yield s, span[s - phase:s - phase + tm, :]


def _window_sums_back(x_ref, bufs, n_rows):
    out, src, w = [], x_ref, 1
    for l in range(len(POOL_WINDOWS)):
        lo = SUBLANES * (l + 1)
        cur = src[lo:n_rows, :] + src[lo - w:n_rows - w, :]
        out.append(cur)
        if l < len(bufs):
            bufs[l][lo:n_rows, :] = cur
            src = bufs[l]
        w *= 2
    return out


def _window_sums_ahead(x_ref, bufs, n_rows):
    out, src, w = [], x_ref, 1
    for l in range(len(POOL_WINDOWS)):
        hi = n_rows - SUBLANES * (l + 1)
        cur = src[0:hi, :] + src[w:hi + w, :]
        out.append(cur)
        if l < len(bufs):
            bufs[l][0:hi, :] = cur
            src = bufs[l]
        w *= 2
    return out


_GELU_C = math.sqrt(2.0 / math.pi)


def _gelu_and_grad(x):
    x2 = x * x
    inner = _GELU_C * (x + 0.044715 * x2 * x)
    t = jnp.tanh(inner)
    g = 0.5 * x * (1.0 + t)
    dg = 0.5 * (1.0 + t) + 0.5 * x * (1.0 - t * t) * _GELU_C * (1.0 + 3.0 * 0.044715 * x2)
    return g, dg


def _ln_stats(x):
    mu = jnp.mean(x, axis=-1, keepdims=True)
    xc = x - mu
    var = jnp.mean(xc * xc, axis=-1, keepdims=True)
    rstd = lax.rsqrt(var + LN_EPS)
    return xc * rstd, rstd


def _ln_bwd(dy, xhat, rstd, g):
    dxhat = dy * g
    return rstd * (dxhat - jnp.mean(dxhat, axis=-1, keepdims=True)
                   - xhat * jnp.mean(dxhat * xhat, axis=-1, keepdims=True))


def _rms_bwd(dh, xn, r, g):
    dxn = dh * g
    return r * (dxn - xn * jnp.mean(dxn * xn, axis=-1, keepdims=True))


def _head_masks(width):
    lane = lax.broadcasted_iota(jnp.int32, (1, width), 1)
    return [(lane >= h * HEAD_DIM_B) & (lane < (h + 1) * HEAD_DIM_B) for h in range(N_HEADS_B)]


def _pool_select(vals, width):
    lane = lax.broadcasted_iota(jnp.int32, (1, width), 1)
    out = vals[-1]
    for g in range(len(vals) - 2, -1, -1):
        out = jnp.where(lane < (g + 1) * GROUP_DIM_C, vals[g], out)
    return out


def _pool_counts(pos):
    return _pool_select([jnp.minimum(pos + 1.0, float(w)) for w in POOL_WINDOWS], D_C)


def _full(shape):
    n = len(shape)
    return pl.BlockSpec(shape, lambda *_: (0,) * n)


def _params(sem):
    return pltpu.CompilerParams(dimension_semantics=sem, vmem_limit_bytes=VMEM_LIMIT)


def _my_coords():
    return lax.axis_index("x"), lax.axis_index("y"), lax.axis_index("c")


def _peer(me, rel):
    x, y, c = me
    bx, by, bc = (rel >> 2) & 1, (rel >> 1) & 1, rel & 1
    return (1 - x if bx else x, 1 - y if by else y, 1 - c if bc else c)


def _index_of(dev):
    return 4 * dev[0] + 2 * dev[1] + dev[2]


SIBLING = 1
OTHER_CHIPS = (2, 4, 6)


class Gather:
    def __init__(self, shards):
        n = len(shards)
        self.inputs = list(shards)
        self.out_shape = [jax.ShapeDtypeStruct((N_DEV,) + s.shape, s.dtype) for s in shards]
        self.scratch = [pltpu.SemaphoreType.DMA((N_DEV - 1, n)), pltpu.SemaphoreType.DMA((N_DEV - 1, n)),
                        pltpu.SemaphoreType.DMA((n,))]

    @staticmethod
    def _copy(src, dst, sems, rel, k, to):
        return pltpu.make_async_remote_copy(src_ref=src, dst_ref=dst, send_sem=sems[0].at[rel - 1, k],
                                            recv_sem=sems[1].at[rel - 1, k], device_id=to, device_id_type=MESH_ID)

    def start(self, ins, outs, sems):
        me = _my_coords()
        mine = _index_of(me)
        for k, src in enumerate(ins):
            pltpu.make_async_copy(src, outs[k].at[mine], sems[2].at[k]).start()
            for rel in (SIBLING,) + OTHER_CHIPS:
                self._copy(src, outs[k].at[mine], sems, rel, k, _peer(me, rel)).start()

    def finish(self, ins, outs, sems):
        me = _my_coords()
        mine = _index_of(me)
        sibling = _peer(me, SIBLING)
        for rel in OTHER_CHIPS:
            slot = _index_of(_peer(me, rel))
            for k in range(len(ins)):
                self._copy(ins[k], outs[k].at[slot], sems, rel, k, sibling).wait_recv()
                self._copy(outs[k].at[slot], outs[k].at[slot], sems, rel + 1, k, sibling).start()
        for rel in (SIBLING,) + tuple(r + 1 for r in OTHER_CHIPS):
            slot = _index_of(_peer(me, rel))
            for k in range(len(ins)):
                self._copy(ins[k], outs[k].at[slot], sems, rel, k, sibling).wait_recv()
        for rel in range(1, N_DEV):
            for k in range(len(ins)):
                self._copy(ins[k], outs[k].at[mine], sems, rel, k, sibling).wait_send()
        for k, src in enumerate(ins):
            pltpu.make_async_copy(src, outs[k].at[mine], sems[2].at[k]).wait()


class Exchange:
    def __init__(self, fulls):
        n = len(fulls)
        self.inputs = list(fulls)
        self.out_shape = [jax.ShapeDtypeStruct(f.shape, f.dtype) for f in fulls]
        self.scratch = [pltpu.SemaphoreType.DMA((N_DEV - 1, n)), pltpu.SemaphoreType.DMA((N_DEV - 1, n)),
                        pltpu.SemaphoreType.DMA((n,))]

    def start(self, ins, outs, sems):
        me = _my_coords()
        mine = _index_of(me)
        for k, src in enumerate(ins):
            pltpu.make_async_copy(src.at[mine], outs[k].at[mine], sems[2].at[k]).start()
            for rel in range(1, N_DEV):
                to = _peer(me, rel)
                Gather._copy(src.at[_index_of(to)], outs[k].at[mine], sems, rel, k, to).start()

    def finish(self, ins, outs, sems):
        me = _my_coords()
        mine = _index_of(me)
        for rel in range(1, N_DEV):
            frm = _peer(me, rel)
            for k, src in enumerate(ins):
                Gather._copy(src.at[mine], outs[k].at[_index_of(frm)], sems, rel, k, frm).wait_recv()
        for rel in range(1, N_DEV):
            for k, src in enumerate(ins):
                Gather._copy(src.at[mine], outs[k].at[mine], sems, rel, k, _peer(me, rel)).wait_send()
        for k, src in enumerate(ins):
            pltpu.make_async_copy(src.at[mine], outs[k].at[mine], sems[2].at[k]).wait()


def _hosted_call(body, *, name, grid, in_specs, out_specs, out_shape, scratch_shapes, args, comm=None):
    sem = ("arbitrary",) * len(grid)
    if comm is None:
        res = pl.pallas_call(body, name=name, grid=grid, in_specs=in_specs, out_specs=out_specs, out_shape=out_shape,
                             scratch_shapes=scratch_shapes, compiler_params=_params(sem))(*args)
        return list(res), []
    n_in, n_out, n_scr = len(in_specs), len(out_specs), len(scratch_shapes)
    n_cin, n_cout = len(comm.inputs), len(comm.out_shape)

    def hosted(*refs):
        ins, refs = refs[:n_in], refs[n_in:]
        cins, refs = refs[:n_cin], refs[n_cin:]
        outs, refs = refs[:n_out], refs[n_out:]
        couts, refs = refs[:n_cout], refs[n_cout:]
        scr, csems = refs[:n_scr], refs[n_scr:]
        ids = [pl.program_id(a) for a in range(len(grid))]
        first = functools.reduce(lambda a, b: a & b, [i == 0 for i in ids])
        last = functools.reduce(lambda a, b: a & b, [i == g - 1 for i, g in zip(ids, grid)])

        @pl.when(first)
        def _():
            comm.start(cins, couts, csems)

        body(*ins, *outs, *scr)

        @pl.when(last)
        def _():
            comm.finish(cins, couts, csems)

    any_spec = pl.BlockSpec(memory_space=pl.ANY)
    res = pl.pallas_call(
        hosted, name=name, grid=grid, in_specs=list(in_specs) + [any_spec] * n_cin,
        out_specs=list(out_specs) + [any_spec] * n_cout, out_shape=list(out_shape) + comm.out_shape,
        scratch_shapes=list(scratch_shapes) + comm.scratch,
        compiler_params=pltpu.CompilerParams(dimension_semantics=sem, vmem_limit_bytes=VMEM_LIMIT, has_side_effects=True),
    )(*args, *comm.inputs)
    return list(res[:n_out]), list(res[n_out:])


def run_comm(comm, *, name):
    n_cin, n_cout = len(comm.inputs), len(comm.out_shape)

    def body(*refs):
        cins, couts, csems = refs[:n_cin], refs[n_cin:n_cin + n_cout], refs[n_cin + n_cout:]
        comm.start(cins, couts, csems)
        comm.finish(cins, couts, csems)

    any_spec = pl.BlockSpec(memory_space=pl.ANY)
    return pl.pallas_call(
        body, name=name, in_specs=[any_spec] * n_cin, out_specs=[any_spec] * n_cout, out_shape=comm.out_shape,
        scratch_shapes=comm.scratch, compiler_params=pltpu.CompilerParams(has_side_effects=True),
    )(*comm.inputs)


def mixer_fwd(x, g1, winT, convw, convb, clng, clnb, wpw, slng, slnb, wm, bias, wbd, pscale, wout, *, seq, tm,
              comm=None):
    T = x.shape[0]
    tiles_per_seq = seq // tm
    n_chunks = tm // CHUNK

    def body(x_ref, g1_ref, winT_ref, convw_ref, convb_ref, clng_ref, clnb_ref, wpw_ref, slng_ref, slnb_ref,
             wm_ref, bias_ref, wbd_ref, pscale_ref, wout_ref,
             z_ref, ycv_ref, p_ref, mix_ref, x1_ref, ybuf, zcbuf, *pbufs):
        i = pl.program_id(0)
        tile_in_seq = i % tiles_per_seq

        @pl.when(tile_in_seq == 0)
        def _():
            ybuf[0:CONV_HALO, :] = jnp.zeros((CONV_HALO, D_A), F32)
            zcbuf[0:POOL_HALO, :] = jnp.zeros((POOL_HALO, D_C), F32)

        x = x_ref[...]
        r = lax.rsqrt(jnp.mean(x * x, axis=-1, keepdims=True) + RMS_EPS)
        h = (x * r * g1_ref[...]).astype(BF16)
        z = _dot_nt(h, winT_ref[...])
        z_ref[...] = z

        y = z[:, 0:D_A] * _sigmoid(z[:, D_A:2 * D_A])
        ybuf[CONV_HALO:CONV_HALO + tm, :] = y
        acc = jnp.zeros((tm, D_A), F32) + convb_ref[...]
        for k, rows in _shifted_taps(ybuf, CONV_HALO - (CONV_WIDTH - 1), CONV_WIDTH, tm):
            acc = acc + convw_ref[k:k + 1, :] * rows
        ybuf[0:CONV_HALO, :] = ybuf[tm:tm + CONV_HALO, :]
        ycv_ref[...] = acc
        xhat, _ = _ln_stats(acc)
        ln = xhat * clng_ref[...] + clnb_ref[...]
        s = ln * _sigmoid(ln)
        ya = _dot(s.astype(BF16), wpw_ref[...])

        gb, _ = _gelu_and_grad(z[:, 2 * D_A:2 * D_A + 2 * D_B])
        u = gb[:, 0:D_B]
        vhat, _ = _ln_stats(gb[:, D_B:2 * D_B])
        vn = vhat * slng_ref[...] + slnb_ref[...]
        masks = _head_masks(D_B)
        yb_parts = []
        for c in range(n_chunks):
            vn_c = vn[c * CHUNK:(c + 1) * CHUNK, :]
            sg = bias_ref[...]
            for hh in range(N_HEADS_B):
                sg = sg + _dot(wm_ref[hh], jnp.where(masks[hh], vn_c, 0.0).astype(BF16))
            yb_parts.append(u[c * CHUNK:(c + 1) * CHUNK, :] * sg)
        yb = jnp.concatenate(yb_parts, axis=0) if n_chunks > 1 else yb_parts[0]

        zc = z[:, 2 * D_A + 2 * D_B:D_IN]
        zcbuf[POOL_HALO:POOL_HALO + tm, :] = zc
        sums = [v[POOL_HALO - SUBLANES * (l + 1):POOL_HALO - SUBLANES * (l + 1) + tm, :]
                for l, v in enumerate(_window_sums_back(zcbuf, pbufs, POOL_HALO + tm))]
        zcbuf[0:POOL_HALO, :] = zcbuf[tm:tm + POOL_HALO, :]
        pos = (tile_in_seq * tm + lax.broadcasted_iota(jnp.int32, (tm, 1), 0)).astype(F32)
        p = _pool_select(sums, D_C) / _pool_counts(pos) - zc
        p_ref[...] = p
        yc = _dot(p.astype(BF16), wbd_ref[...]) * pscale_ref[...]

        mix = jnp.concatenate([ya, yb, yc], axis=1).astype(BF16)
        mix_ref[...] = mix
        x1_ref[...] = x + _dot(mix, wout_ref[...])

    row = lambda w: pl.BlockSpec((tm, w), lambda i: (i, 0))
    return _hosted_call(
        body, name="mixer_fwd", grid=(T // tm,),
        in_specs=[row(D_MODEL), _full((1, D_MODEL)), _full((D_IN, D_MODEL)), _full((CONV_WIDTH, D_A)),
                  _full((1, D_A)), _full((1, D_A)), _full((1, D_A)), _full((D_A, D_A)), _full((1, D_B)), _full((1, D_B)),
                  _full((N_HEADS_B, CHUNK, CHUNK)), _full((CHUNK, D_B)), _full((D_C, D_C)), _full((1, D_C)),
                  _full((D_MODEL, D_MODEL))],
        out_specs=[row(D_IN), row(D_A), row(D_C), row(D_MODEL), row(D_MODEL)],
        out_shape=[jax.ShapeDtypeStruct((T, D_IN), F32), jax.ShapeDtypeStruct((T, D_A), F32),
                   jax.ShapeDtypeStruct((T, D_C), F32), jax.ShapeDtypeStruct((T, D_MODEL), BF16),
                   jax.ShapeDtypeStruct((T, D_MODEL), F32)],
        scratch_shapes=[pltpu.VMEM((CONV_HALO + tm, D_A), F32)]
        + [pltpu.VMEM((POOL_HALO + tm, D_C), F32)] * len(POOL_WINDOWS),
        args=(x, g1, winT, convw, convb, clng, clnb, wpw, slng, slnb, wm, bias, wbd, pscale, wout), comm=comm)


def mixer_bwd(dx1, x, z, ycv, p, g1, winT, convw, clng, clnb, wpw, slng, slnb, wm, wmT, bias, wbd, pscale, wout,
              *, seq, tm, comm=None):
    T = x.shape[0]
    tiles_per_seq = seq // tm
    n_tiles = T // tm
    n_chunks = tm // CHUNK

    def body(dx1_ref, x_ref, z_ref, ycv_ref, p_ref, g1_ref, winT_ref, convw_ref, clng_ref, clnb_ref, wpw_ref,
             slng_ref, slnb_ref, wm_ref, wmT_ref, bias_ref, wbd_ref, pscale_ref, wout_ref,
             dx_ref, dz_ref, h_ref, s_ref, dya_ref,
             dg1_ref, dconvw_ref, dconvb_ref, dclng_ref, dclnb_ref, dslng_ref, dslnb_ref, dwm_ref, dbs_ref,
             dwbd_ref, dpscale_ref, dycbuf, dpcbuf, *pbufs):
        i = pl.program_id(0)
        tile_in_seq = (n_tiles - 1 - i) % tiles_per_seq

        @pl.when(i == 0)
        def _():
            for ref in (dg1_ref, dconvw_ref, dconvb_ref, dclng_ref, dclnb_ref, dslng_ref, dslnb_ref, dwm_ref,
                        dbs_ref, dwbd_ref, dpscale_ref):
                ref[...] = jnp.zeros(ref.shape, F32)

        @pl.when(tile_in_seq == tiles_per_seq - 1)
        def _():
            dycbuf[tm:tm + CONV_HALO, :] = jnp.zeros((CONV_HALO, D_A), F32)
            dpcbuf[tm:tm + POOL_HALO, :] = jnp.zeros((POOL_HALO, D_C), F32)

        dx1 = dx1_ref[...]
        z = z_ref[...]
        dmix = _dot_nt(dx1.astype(BF16), wout_ref[...])
        dya = dmix[:, 0:D_A]
        dyb = dmix[:, D_A:D_A + D_B]
        dyc = dmix[:, D_A + D_B:D_MODEL]

        p = p_ref[...]
        pb = p.astype(BF16)
        q = _dot(pb, wbd_ref[...])
        dpscale_ref[...] += jnp.sum(dyc * q, axis=0, keepdims=True)
        dq = (dyc * pscale_ref[...]).astype(BF16)
        dwbd_ref[...] += _dot_tn(pb, dq)
        dp = _dot_nt(dq, wbd_ref[...])
        pos = (tile_in_seq * tm + lax.broadcasted_iota(jnp.int32, (tm, 1), 0)).astype(F32)
        dpc = dp / _pool_counts(pos)
        dpcbuf[0:tm, :] = dpc
        sums = [v[0:tm, :] for v in _window_sums_ahead(dpcbuf, pbufs, tm + POOL_HALO)]
        dpcbuf[tm:tm + POOL_HALO, :] = dpcbuf[0:POOL_HALO, :]
        dzc = _pool_select(sums, D_C) - dp

        dya_ref[...] = dya.astype(BF16)
        ds = _dot_nt(dya.astype(BF16), wpw_ref[...])
        xhat, rstd = _ln_stats(ycv_ref[...])
        ln = xhat * clng_ref[...] + clnb_ref[...]
        sg = _sigmoid(ln)
        s_ref[...] = (ln * sg).astype(BF16)
        dln = ds * (sg * (1.0 + ln * (1.0 - sg)))
        dclng_ref[...] += jnp.sum(dln * xhat, axis=0, keepdims=True)
        dclnb_ref[...] += jnp.sum(dln, axis=0, keepdims=True)
        dycv = _ln_bwd(dln, xhat, rstd, clng_ref[...])
        dconvb_ref[...] += jnp.sum(dycv, axis=0, keepdims=True)
        a = z[:, 0:D_A]
        sgate = _sigmoid(z[:, D_A:2 * D_A])
        y = a * sgate
        dycbuf[0:tm, :] = dycv
        dy = jnp.zeros((tm, D_A), F32)
        for d, sh in _shifted_taps(dycbuf, 0, CONV_WIDTH, tm):
            k = CONV_WIDTH - 1 - d
            dy = dy + convw_ref[k:k + 1, :] * sh
            dconvw_ref[k:k + 1, :] += jnp.sum(y * sh, axis=0, keepdims=True)
        dycbuf[tm:tm + CONV_HALO, :] = dycbuf[0:CONV_HALO, :]
        da = dy * sgate
        dgate = dy * a * sgate * (1.0 - sgate)

        gb, dgb = _gelu_and_grad(z[:, 2 * D_A:2 * D_A + 2 * D_B])
        u = gb[:, 0:D_B]
        vhat, vrstd = _ln_stats(gb[:, D_B:2 * D_B])
        vn = vhat * slng_ref[...] + slnb_ref[...]
        masks = _head_masks(D_B)
        tril = (lax.broadcasted_iota(jnp.int32, (CHUNK, CHUNK), 0)
                >= lax.broadcasted_iota(jnp.int32, (CHUNK, CHUNK), 1))
        lane128 = lax.broadcasted_iota(jnp.int32, (1, CHUNK), 1)
        du_parts, dvn_parts = [], []
        for c in range(n_chunks):
            rows = slice(c * CHUNK, (c + 1) * CHUNK)
            vn_c = vn[rows, :]
            vh = [jnp.where(masks[hh], vn_c, 0.0).astype(BF16) for hh in range(N_HEADS_B)]
            sgc = bias_ref[...]
            for hh in range(N_HEADS_B):
                sgc = sgc + _dot(wm_ref[hh], vh[hh])
            dyb_c = dyb[rows, :]
            du_parts.append(dyb_c * sgc)
            dsg = dyb_c * u[rows, :]
            dvn_c = jnp.zeros((CHUNK, D_B), F32)
            dbs = jnp.zeros((CHUNK, CHUNK), F32)
            for hh in range(N_HEADS_B):
                dsg_h = jnp.where(masks[hh], dsg, 0.0)
                dsg_hb = dsg_h.astype(BF16)
                dwm_ref[hh] += jnp.where(tril, _dot_nt(dsg_hb, vh[hh]), 0.0)
                dvn_c = dvn_c + _dot(wmT_ref[hh], dsg_hb)
                dbs = dbs + jnp.where(lane128 == hh, jnp.sum(dsg_h, axis=1, keepdims=True), 0.0)
            dbs_ref[...] += dbs
            dvn_parts.append(dvn_c)
        du = jnp.concatenate(du_parts, axis=0) if n_chunks > 1 else du_parts[0]
        dvn = jnp.concatenate(dvn_parts, axis=0) if n_chunks > 1 else dvn_parts[0]
        dslng_ref[...] += jnp.sum(dvn * vhat, axis=0, keepdims=True)
        dslnb_ref[...] += jnp.sum(dvn, axis=0, keepdims=True)
        dv = _ln_bwd(dvn, vhat, vrstd, slng_ref[...])
        dzb = jnp.concatenate([du, dv], axis=1) * dgb

        dz = jnp.concatenate([da, dgate, dzb, dzc], axis=1).astype(BF16)
        dz_ref[...] = dz
        dh = _dot(dz, winT_ref[...])
        x = x_ref[...]
        r = lax.rsqrt(jnp.mean(x * x, axis=-1, keepdims=True) + RMS_EPS)
        xn = x * r
        h_ref[...] = (xn * g1_ref[...]).astype(BF16)
        dg1_ref[...] += jnp.sum(dh * xn, axis=0, keepdims=True)
        dx_ref[...] = dx1 + _rms_bwd(dh, xn, r, g1_ref[...])

    row = lambda w: pl.BlockSpec((tm, w), lambda i: (n_tiles - 1 - i, 0))
    acc_shapes = [(1, D_MODEL), (CONV_WIDTH, D_A), (1, D_A), (1, D_A), (1, D_A), (1, D_B), (1, D_B),
                  (N_HEADS_B, CHUNK, CHUNK), (CHUNK, CHUNK), (D_C, D_C), (1, D_C)]
    return _hosted_call(
        body, name="mixer_bwd", grid=(n_tiles,),
        in_specs=[row(D_MODEL), row(D_MODEL), row(D_IN), row(D_A), row(D_C),
                  _full((1, D_MODEL)), _full((D_IN, D_MODEL)), _full((CONV_WIDTH, D_A)), _full((1, D_A)), _full((1, D_A)),
                  _full((D_A, D_A)), _full((1, D_B)), _full((1, D_B)), _full((N_HEADS_B, CHUNK, CHUNK)),
                  _full((N_HEADS_B, CHUNK, CHUNK)), _full((CHUNK, D_B)), _full((D_C, D_C)), _full((1, D_C)),
                  _full((D_MODEL, D_MODEL))],
        out_specs=[row(D_MODEL), row(D_IN), row(D_MODEL), row(D_A), row(D_A)] + [_full(s) for s in acc_shapes],
        out_shape=[jax.ShapeDtypeStruct((T, D_MODEL), F32), jax.ShapeDtypeStruct((T, D_IN), BF16),
                   jax.ShapeDtypeStruct((T, D_MODEL), BF16), jax.ShapeDtypeStruct((T, D_A), BF16),
                   jax.ShapeDtypeStruct((T, D_A), BF16)] + [jax.ShapeDtypeStruct(s, F32) for s in acc_shapes],
        scratch_shapes=[pltpu.VMEM((tm + CONV_HALO, D_A), F32)]
        + [pltpu.VMEM((tm + POOL_HALO, D_C), F32)] * len(POOL_WINDOWS),
        args=(dx1, x, z, ycv, p, g1, winT, convw, clng, clnb, wpw, slng, slnb, wm, wmT, bias, wbd, pscale, wout),
        comm=comm)


def ffn_fwd(x1, g2, wguT, wd, *, tm, th, comm=None):
    T = x1.shape[0]
    n_h = D_FF // th

    def body(x1_ref, g2_ref, wgu_ref, wd_ref, x2_ref, gu_ref, h2_buf, acc):
        j = pl.program_id(1)

        @pl.when(j == 0)
        def _():
            x = x1_ref[...]
            r = lax.rsqrt(jnp.mean(x * x, axis=-1, keepdims=True) + RMS_EPS)
            h2_buf[...] = (x * r * g2_ref[...]).astype(BF16)
            acc[...] = x

        h2 = h2_buf[...]
        g = _dot_nt(h2, wgu_ref[0])
        u = _dot_nt(h2, wgu_ref[1])
        gu_ref[0] = g.astype(BF16)
        gu_ref[1] = u.astype(BF16)
        f = (g * _sigmoid(g) * u).astype(BF16)
        acc[...] += _dot(f, wd_ref[...])

        @pl.when(j == n_h - 1)
        def _():
            x2_ref[...] = acc[...]

    return _hosted_call(
        body, name="ffn_fwd", grid=(T // tm, n_h),
        in_specs=[pl.BlockSpec((tm, D_MODEL), lambda i, j: (i, 0)), _full((1, D_MODEL)),
                  pl.BlockSpec((2, th, D_MODEL), lambda i, j: (0, j, 0)),
                  pl.BlockSpec((th, D_MODEL), lambda i, j: (j, 0))],
        out_specs=[pl.BlockSpec((tm, D_MODEL), lambda i, j: (i, 0)),
                   pl.BlockSpec((2, tm, th), lambda i, j: (0, i, j))],
        out_shape=[jax.ShapeDtypeStruct((T, D_MODEL), F32), jax.ShapeDtypeStruct((2, T, D_FF), BF16)],
        scratch_shapes=[pltpu.VMEM((tm, D_MODEL), BF16), pltpu.VMEM((tm, D_MODEL), F32)],
        args=(x1, g2, wguT, wd), comm=comm)


def ffn_bwd(dx2, x1, gu, g2, wguT, wd, *, tm, th, comm=None):
    T = x1.shape[0]
    n_h = D_FF // th

    def body(dx2_ref, x1_ref, gu_ref, g2_ref, wgu_ref, wd_ref, dx1_ref, h2_ref, f_ref, dgu_ref, dg2_ref, acc):
        i = pl.program_id(0)
        j = pl.program_id(1)

        @pl.when((i == 0) & (j == 0))
        def _():
            dg2_ref[...] = jnp.zeros(dg2_ref.shape, F32)

        dx2 = dx2_ref[...]
        df = _dot_nt(dx2.astype(BF16), wd_ref[...])
        g = gu_ref[0].astype(F32)
        u = gu_ref[1].astype(F32)
        sg = _sigmoid(g)
        silu = g * sg
        f_ref[...] = (silu * u).astype(BF16)
        dgate = (df * u * (sg * (1.0 + g * (1.0 - sg)))).astype(BF16)
        dup = (df * silu).astype(BF16)
        dgu_ref[0] = dgate
        dgu_ref[1] = dup
        part = _dot(dgate, wgu_ref[0]) + _dot(dup, wgu_ref[1])

        @pl.when(j == 0)
        def _():
            acc[...] = part

        @pl.when(j > 0)
        def _():
            acc[...] += part

        @pl.when(j == n_h - 1)
        def _():
            x = x1_ref[...]
            r = lax.rsqrt(jnp.mean(x * x, axis=-1, keepdims=True) + RMS_EPS)
            xn = x * r
            dh = acc[...]
            h2_ref[...] = (xn * g2_ref[...]).astype(BF16)
            dg2_ref[...] += jnp.sum(dh * xn, axis=0, keepdims=True)
            dx1_ref[...] = dx2 + _rms_bwd(dh, xn, r, g2_ref[...])

    return _hosted_call(
        body, name="ffn_bwd", grid=(T // tm, n_h),
        in_specs=[pl.BlockSpec((tm, D_MODEL), lambda i, j: (i, 0)), pl.BlockSpec((tm, D_MODEL), lambda i, j: (i, 0)),
                  pl.BlockSpec((2, tm, th), lambda i, j: (0, i, j)), _full((1, D_MODEL)),
                  pl.BlockSpec((2, th, D_MODEL), lambda i, j: (0, j, 0)),
                  pl.BlockSpec((th, D_MODEL), lambda i, j: (j, 0))],
        out_specs=[pl.BlockSpec((tm, D_MODEL), lambda i, j: (i, 0)), pl.BlockSpec((tm, D_MODEL), lambda i, j: (i, 0)),
                   pl.BlockSpec((tm, th), lambda i, j: (i, j)), pl.BlockSpec((2, tm, th), lambda i, j: (0, i, j)),
                   _full((1, D_MODEL))],
        out_shape=[jax.ShapeDtypeStruct((T, D_MODEL), F32), jax.ShapeDtypeStruct((T, D_MODEL), BF16),
                   jax.ShapeDtypeStruct((T, D_FF), BF16), jax.ShapeDtypeStruct((2, T, D_FF), BF16),
                   jax.ShapeDtypeStruct((1, D_MODEL), F32)],
        scratch_shapes=[pltpu.VMEM((tm, D_MODEL), F32)],
        args=(dx2, x1, gu, g2, wguT, wd), comm=comm)


def head_fwd_bwd(x, target, fg, *, tm):
    T = x.shape[0]
    n_tiles = T // tm

    def body(x_ref, t_ref, fg_ref, loss_ref, dx_ref, dfg_ref, lacc):
        i = pl.program_id(0)

        @pl.when(i == 0)
        def _():
            lacc[...] = jnp.zeros(lacc.shape, F32)
            dfg_ref[...] = jnp.zeros(dfg_ref.shape, F32)

        x = x_ref[...]
        r = lax.rsqrt(jnp.mean(x * x, axis=-1, keepdims=True) + RMS_EPS)
        xn = x * r
        e = xn * fg_ref[...] - t_ref[...]
        lacc[...] += jnp.sum(e * e, axis=0, keepdims=True)
        dy = e * (1.0 / D_MODEL)
        dfg_ref[...] += jnp.sum(dy * xn, axis=0, keepdims=True)
        dx_ref[...] = _rms_bwd(dy, xn, r, fg_ref[...])

        @pl.when(i == n_tiles - 1)
        def _():
            loss_ref[...] = jnp.sum(lacc[...], axis=1, keepdims=True) * (0.5 / D_MODEL)

    row = pl.BlockSpec((tm, D_MODEL), lambda i: (i, 0))
    return pl.pallas_call(
        body, name="head_fwd_bwd", grid=(n_tiles,),
        in_specs=[row, row, _full((1, D_MODEL))],
        out_specs=[_full((1, 1)), row, _full((1, D_MODEL))],
        out_shape=[jax.ShapeDtypeStruct((1, 1), F32), jax.ShapeDtypeStruct((T, D_MODEL), F32),
                   jax.ShapeDtypeStruct((1, D_MODEL), F32)],
        scratch_shapes=[pltpu.VMEM((1, D_MODEL), F32)],
        compiler_params=_params(("arbitrary",)),
    )(x, target, fg)


def wgrad(a, b, *, tmo, tk, name, comm=None):
    G, T, M = a.shape
    N = b.shape[1]
    n_k = T // tk

    def body(a_ref, b_ref, o_ref, acc):
        k = pl.program_id(2)
        part = _dot_tn(a_ref[0].astype(BF16), b_ref[...].astype(BF16))
        if n_k == 1:
            o_ref[0] = part.astype(BF16)
            return

        @pl.when(k == 0)
        def _():
            acc[...] = part

        @pl.when((k > 0) & (k < n_k - 1))
        def _():
            acc[...] += part

        @pl.when(k == n_k - 1)
        def _():
            o_ref[0] = (acc[...] + part).astype(BF16)

    (out,), got = _hosted_call(
        body, name=name, grid=(G, M // tmo, n_k),
        in_specs=[pl.BlockSpec((1, tk, tmo), lambda g, m, k: (g, k, m)),
                  pl.BlockSpec((tk, N), lambda g, m, k: (k, 0))],
        out_specs=[pl.BlockSpec((1, tmo, N), lambda g, m, k: (g, m, 0))],
        out_shape=[jax.ShapeDtypeStruct((G, M, N), BF16)],
        scratch_shapes=[pltpu.VMEM((tmo, N), F32)],
        args=(a, b), comm=comm)
    return out, got


def all_reduce_small(buf):
    _, R, W = buf.shape

    def body(in_ref, out_ref, land, send1, recv1, send2, recv2):
        me = _my_coords()
        mine = _index_of(me)
        sends = []
        for rel in range(1, N_DEV):
            to = _peer(me, rel)
            cp = pltpu.make_async_remote_copy(
                src_ref=in_ref.at[_index_of(to)], dst_ref=land.at[mine], send_sem=send1.at[rel - 1],
                recv_sem=recv1.at[rel - 1], device_id=to, device_id_type=MESH_ID)
            cp.start()
            sends.append(cp)
        land[mine] = in_ref[mine]
        for rel in range(1, N_DEV):
            frm = _peer(me, rel)
            pltpu.make_async_remote_copy(
                src_ref=in_ref.at[mine], dst_ref=land.at[_index_of(frm)], send_sem=send1.at[rel - 1],
                recv_sem=recv1.at[rel - 1], device_id=frm, device_id_type=MESH_ID).wait_recv()
        total = land[0]
        for d in range(1, N_DEV):
            total = total + land[d]
        out_ref[mine] = total
        for rel in range(1, N_DEV):
            to = _peer(me, rel)
            cp = pltpu.make_async_remote_copy(
                src_ref=out_ref.at[mine], dst_ref=out_ref.at[mine], send_sem=send2.at[rel - 1],
                recv_sem=recv2.at[rel - 1], device_id=to, device_id_type=MESH_ID)
            cp.start()
            sends.append(cp)
        for rel in range(1, N_DEV):
            frm = _peer(me, rel)
            pltpu.make_async_remote_copy(
                src_ref=out_ref.at[mine], dst_ref=out_ref.at[_index_of(frm)], send_sem=send2.at[rel - 1],
                recv_sem=recv2.at[rel - 1], device_id=frm, device_id_type=MESH_ID).wait_recv()
        for cp in sends:
            cp.wait_send()

    vmem = pl.BlockSpec(memory_space=pltpu.VMEM)
    return pl.pallas_call(
        body, name="all_reduce_small",
        in_specs=[vmem], out_specs=vmem,
        out_shape=jax.ShapeDtypeStruct(buf.shape, F32),
        scratch_shapes=[pltpu.VMEM(buf.shape, F32)] + [pltpu.SemaphoreType.DMA((N_DEV - 1,))] * 4,
        compiler_params=pltpu.CompilerParams(has_side_effects=True, vmem_limit_bytes=VMEM_LIMIT),
    )(buf)


_ADAM_C1 = 1.0 - ADAM_B1 ** ADAM_STEP
_ADAM_C2 = 1.0 - ADAM_B2 ** ADAM_STEP


def _adamw_math(w, g, m, v):
    m = ADAM_B1 * m + (1.0 - ADAM_B1) * g
    v = ADAM_B2 * v + (1.0 - ADAM_B2) * (g * g)
    m_hat = m / _ADAM_C1
    v_hat = v / _ADAM_C2
    delta = -ADAM_LR * (m_hat / (jnp.sqrt(v_hat) + ADAM_EPS) + ADAM_WD * w)
    return delta, m, v


def adamw_sharded(parts, w, m, v, *, tr, name):
    _, R, C = parts[0].shape

    def body(p0_ref, p1_ref, w_ref, m_ref, v_ref, g_ref, d_ref, nm_ref, nv_ref):
        def update(p_ref):
            g = p_ref[0].astype(F32)
            for d in range(1, N_DEV):
                g = g + p_ref[d].astype(F32)
            delta, nm, nv = _adamw_math(w_ref[0], g, m_ref[0], v_ref[0])
            g_ref[0] = g
            d_ref[0] = delta
            nm_ref[0] = nm
            nv_ref[0] = nv

        @pl.when(pl.program_id(0) == 0)
        def _():
            update(p0_ref)

        @pl.when(pl.program_id(0) == 1)
        def _():
            update(p1_ref)

    n_i = R // tr
    p_specs = [pl.BlockSpec((N_DEV, tr, C), lambda l, i: (0, jnp.where(l == 0, i, n_i - 1), 0)),
               pl.BlockSpec((N_DEV, tr, C), lambda l, i: (0, jnp.where(l == 1, i, 0), 0))]
    o_spec = pl.BlockSpec((1, tr, C), lambda l, i: (l, i, 0))
    return pl.pallas_call(
        body, name=name, grid=(DEPTH, n_i),
        in_specs=p_specs + [o_spec, o_spec, o_spec], out_specs=[o_spec] * 4,
        out_shape=[jax.ShapeDtypeStruct(w.shape, F32)] * 4,
        compiler_params=_params(("arbitrary", "arbitrary")),
    )(parts[0], parts[1], w, m, v)


def adamw_small(gs, ws, ms, vs):
    n = len(gs)

    def body(*refs):
        g_refs, w_refs, m_refs, v_refs = refs[:n], refs[n:2 * n], refs[2 * n:3 * n], refs[3 * n:4 * n]
        d_refs, nm_refs, nv_refs = refs[4 * n:5 * n], refs[5 * n:6 * n], refs[6 * n:]
        for k in range(n):
            delta, nm, nv = _adamw_math(w_refs[k][...], g_refs[k][...], m_refs[k][...], v_refs[k][...])
            d_refs[k][...] = delta
            nm_refs[k][...] = nm
            nv_refs[k][...] = nv

    vmem = pl.BlockSpec(memory_space=pltpu.VMEM)
    res = pl.pallas_call(
        body, name="adamw_small", in_specs=[vmem] * (4 * n), out_specs=[vmem] * (3 * n),
        out_shape=[jax.ShapeDtypeStruct(w.shape, F32) for w in ws] * 3,
        compiler_params=pltpu.CompilerParams(vmem_limit_bytes=VMEM_LIMIT),
    )(*gs, *ws, *ms, *vs)
    return res[:n], res[n:2 * n], res[2 * n:]


def _pack(arrays, row_multiple):
    flat = jnp.concatenate([a.reshape(-1) for a in arrays])
    rows = -(-flat.shape[0] // (LANES * row_multiple)) * row_multiple
    return jnp.pad(flat, (0, rows * LANES - flat.shape[0])).reshape(rows, LANES)


def _unpack(buf, shapes):
    flat = buf.reshape(-1)
    out, off = [], 0
    for s in shapes:
        n = math.prod(s)
        out.append(flat[off:off + n].reshape(s))
        off += n
    return out


def _block_diag(w_pool):
    G, d, _ = w_pool.shape
    eye = jnp.eye(G, dtype=w_pool.dtype)
    return (eye[:, None, :, None] * w_pool[:, :, None, :]).reshape(G * d, G * d)


def kernel(x, norm1_g, w_in, conv_w, conv_b, conv_ln_g, conv_ln_b, w_pw, sg_ln_g, sg_ln_b, w_s, b_s, w_pool, pool_scale, w_out, norm2_g, w_gate_up, w_down, final_g, loss_target, m_norm1_g, m_w_in, m_conv_w, m_conv_b, m_conv_ln_g, m_conv_ln_b, m_w_pw, m_sg_ln_g, m_sg_ln_b, m_w_s, m_b_s, m_w_pool, m_pool_scale, m_w_out, m_norm2_g, m_w_gate_up, m_w_down, m_final_g, v_norm1_g, v_w_in, v_conv_w, v_conv_b, v_conv_ln_g, v_conv_ln_b, v_w_pw, v_sg_ln_g, v_sg_ln_b, v_w_s, v_b_s, v_w_pool, v_pool_scale, v_w_out, v_norm2_g, v_w_gate_up, v_w_down, v_final_g):
    b_loc, seq, _ = x.shape
    T = b_loc * seq
    tm_mix = min(256, seq)
    tm_ffn_fwd = min(512, T)
    tm_ffn_bwd = min(256, T)
    tm_head = min(512, T)
    tk = min(2048, T)
    tk_f32 = min(1024, T)
    th = D_FF // 2
    cw = conv_w.shape[2]
    my_index = _index_of(_my_coords())

    xf = x.reshape(T, D_MODEL)
    tgt = loss_target.reshape(T, D_MODEL)

    mixer_shards = [[w_in[l].T.astype(BF16), w_out[l].astype(BF16), w_pw[l].astype(BF16)] for l in range(DEPTH)]
    ffn_shards = [[w_gate_up[l].T.astype(BF16), w_down[l].astype(BF16)] for l in range(DEPTH)]

    tril = jnp.tril(jnp.ones((CHUNK, CHUNK), dtype=bool))
    layers = []
    for l in range(DEPTH):
        wm = jnp.where(tril[None], w_s[l], 0.0).astype(BF16)
        layers.append(dict(
            g1=norm1_g[l][None], convb=conv_b[l][None], clng=conv_ln_g[l][None], clnb=conv_ln_b[l][None],
            slng=sg_ln_g[l][None], slnb=sg_ln_b[l][None], wm=wm, wmT=jnp.swapaxes(wm, 1, 2),
            bias=jnp.repeat(b_s[l].T, HEAD_DIM_B, axis=1), wbd=_block_diag(w_pool[l]).astype(BF16),
            pscale=pool_scale[l][None], g2=norm2_g[l][None]))

    def set_mixer_weights(l, g_in, g_out, g_pw):
        layers[l].update(winT=g_in.reshape(D_IN, D_MODEL), wout=g_out.reshape(D_MODEL, D_MODEL), wpw=g_pw.reshape(D_A, D_A))

    def set_ffn_weights(l, g_gu, g_d):
        layers[l].update(wguT=g_gu.reshape(2, D_FF, D_MODEL), wd=g_d.reshape(D_FF, D_MODEL))

    first = run_comm(Gather(mixer_shards[0] + [conv_w.reshape(DEPTH * CONV_WIDTH, cw).T]), name="gather_first")
    set_mixer_weights(0, *first[:3])
    convw_full = first[3].reshape(D_A, DEPTH * CONV_WIDTH).T.reshape(DEPTH, CONV_WIDTH, D_A)
    for l in range(DEPTH):
        layers[l]["convw"] = convw_full[l]

    saved = []
    cur = xf
    for l in range(DEPTH):
        w = layers[l]
        (z, ycv, p, mix, x1), got = mixer_fwd(
            cur, w["g1"], w["winT"], w["convw"], w["convb"], w["clng"], w["clnb"], w["wpw"], w["slng"], w["slnb"], w["wm"],
            w["bias"], w["wbd"], w["pscale"], w["wout"], seq=seq, tm=tm_mix, comm=Gather(ffn_shards[l]))
        set_ffn_weights(l, *got)
        (x2, gu), got = ffn_fwd(x1, w["g2"], w["wguT"], w["wd"], tm=tm_ffn_fwd, th=th,
                                comm=Gather(mixer_shards[l + 1]) if l + 1 < DEPTH else None)
        if l + 1 < DEPTH:
            set_mixer_weights(l + 1, *got)
        saved.append((cur, z, ycv, p, mix, x1, gu))
        cur = x2
    loss_part, dx, dfg = head_fwd_bwd(cur, tgt, final_g[None], tm=tm_head)

    small = [None] * DEPTH
    parts = {}
    pending = None
    for l in reversed(range(DEPTH)):
        w = layers[l]
        x0, z, ycv, p, mix, x1, gu = saved[l]
        (dx1, h2, f, dgu, dg2), got = ffn_bwd(dx, x1, gu, w["g2"], w["wguT"], w["wd"], tm=tm_ffn_bwd, th=th,
                                              comm=Exchange(pending[1]) if pending else None)
        if pending:
            parts.update(zip(pending[0], got))
        gw_d, _ = wgrad(f[None], dx, tmo=th, tk=tk_f32, name="wgrad_down")
        gw_gu, got = wgrad(dgu, h2, tmo=th, tk=tk, name="wgrad_gate_up",
                           comm=Exchange([gw_d.reshape(N_DEV, D_FF // N_DEV, D_MODEL)]))
        parts[("d", l)], = got
        outs, got = mixer_bwd(
            dx1, x0, z, ycv, p, w["g1"], w["winT"], w["convw"], w["clng"], w["clnb"], w["wpw"], w["slng"], w["slnb"],
            w["wm"], w["wmT"], w["bias"], w["wbd"], w["pscale"], w["wout"], seq=seq, tm=tm_mix,
            comm=Exchange([gw_gu.reshape(N_DEV, 2 * D_FF // N_DEV, D_MODEL)]))
        parts[("gu", l)], = got
        (dx, dz, h, s, dya, dg1, dconvw, dconvb, dclng, dclnb, dslng, dslnb, dwm, dbs, dwbd, dpscale) = outs
        gw_out, _ = wgrad(mix[None], dx1, tmo=D_MODEL, tk=tk_f32, name="wgrad_out")
        gw_in, got = wgrad(dz[None], h, tmo=D_IN // 2, tk=tk, name="wgrad_in",
                           comm=Exchange([gw_out.reshape(N_DEV, D_MODEL // N_DEV, D_MODEL)]))
        parts[("out", l)], = got
        gw_pw, _ = wgrad(s[None], dya, tmo=D_A, tk=tk, name="wgrad_pw")
        pending = ([("in", l), ("pw", l)],
                   [gw_in.reshape(N_DEV, D_IN // N_DEV, D_MODEL), gw_pw.reshape(N_DEV, D_A // N_DEV, D_A)])
        small[l] = (dg1, dconvw, dconvb, dclng, dclnb, dslng, dslnb, dwm, dbs, dwbd, dpscale, dg2)
    grad_x = dx.reshape(x.shape)
    parts.update(zip(pending[0], run_comm(Exchange(pending[1]), name="exchange_last")))
    p_in, p_gu, p_d, p_out, p_pw = [[parts[(k, l)] for l in range(DEPTH)] for k in ("in", "gu", "d", "out", "pw")]

    small_arrays = [a for l in range(DEPTH) for a in small[l]] + [dfg, loss_part]
    small_shapes = [a.shape for a in small_arrays]
    packed = _pack(small_arrays, N_DEV * SUBLANES)
    summed = all_reduce_small(packed.reshape(N_DEV, packed.shape[0] // N_DEV, LANES))
    sums = _unpack(summed, small_shapes)
    per_layer = len(small[0])
    g_small = {k: [] for k in ("norm1_g", "conv_w", "conv_b", "conv_ln_g", "conv_ln_b", "sg_ln_g", "sg_ln_b", "w_s", "b_s",
                               "w_pool", "pool_scale", "norm2_g")}
    for l in range(DEPTH):
        dg1, dconvw, dconvb, dclng, dclnb, dslng, dslnb, dwm, dbs, dwbd, dpscale, dg2 = sums[per_layer * l:per_layer * (l + 1)]
        g_small["norm1_g"].append(dg1[0])
        g_small["conv_w"].append(lax.dynamic_slice_in_dim(dconvw, my_index * cw, cw, axis=1))
        g_small["conv_b"].append(dconvb[0])
        g_small["conv_ln_g"].append(dclng[0])
        g_small["conv_ln_b"].append(dclnb[0])
        g_small["sg_ln_g"].append(dslng[0])
        g_small["sg_ln_b"].append(dslnb[0])
        g_small["w_s"].append(dwm)
        g_small["b_s"].append(dbs[:, :N_HEADS_B].T)
        g_small["w_pool"].append(jnp.stack([dwbd[g * GROUP_DIM_C:(g + 1) * GROUP_DIM_C, g * GROUP_DIM_C:(g + 1) * GROUP_DIM_C]
                                            for g in range(len(POOL_WINDOWS))]))
        g_small["pool_scale"].append(dpscale[0])
        g_small["norm2_g"].append(dg2[0])
    g_small = {k: jnp.stack(v) for k, v in g_small.items()}
    g_small["final_g"] = sums[-2][0]
    loss = sums[-1][0, 0]

    t = lambda a: jnp.swapaxes(a, 1, 2)
    g_w_in, d_w_in, nm_w_in, nv_w_in = map(t, adamw_sharded(p_in, t(w_in), t(m_w_in), t(v_w_in), tr=D_IN // N_DEV // 2,
                                                            name="adamw_w_in"))
    g_w_gu, d_w_gu, nm_w_gu, nv_w_gu = map(t, adamw_sharded(p_gu, t(w_gate_up), t(m_w_gate_up), t(v_w_gate_up),
                                                            tr=2 * D_FF // N_DEV // 4, name="adamw_w_gate_up"))
    g_w_d, d_w_d, nm_w_d, nv_w_d = adamw_sharded(p_d, w_down, m_w_down, v_w_down, tr=D_FF // N_DEV // 2, name="adamw_w_down")
    g_w_out, d_w_out, nm_w_out, nv_w_out = adamw_sharded(p_out, w_out, m_w_out, v_w_out, tr=D_MODEL // N_DEV,
                                                         name="adamw_w_out")
    g_w_pw, d_w_pw, nm_w_pw, nv_w_pw = adamw_sharded(p_pw, w_pw, m_w_pw, v_w_pw, tr=D_A // N_DEV, name="adamw_w_pw")

    small_names = ["norm1_g", "conv_w", "conv_b", "conv_ln_g", "conv_ln_b", "sg_ln_g", "sg_ln_b", "w_s", "b_s", "w_pool",
                   "pool_scale", "norm2_g", "final_g"]
    small_w = dict(norm1_g=norm1_g, conv_w=conv_w, conv_b=conv_b, conv_ln_g=conv_ln_g, conv_ln_b=conv_ln_b, sg_ln_g=sg_ln_g,
                   sg_ln_b=sg_ln_b, w_s=w_s, b_s=b_s, w_pool=w_pool, pool_scale=pool_scale, norm2_g=norm2_g, final_g=final_g)
    small_m = dict(norm1_g=m_norm1_g, conv_w=m_conv_w, conv_b=m_conv_b, conv_ln_g=m_conv_ln_g, conv_ln_b=m_conv_ln_b,
                   sg_ln_g=m_sg_ln_g, sg_ln_b=m_sg_ln_b, w_s=m_w_s, b_s=m_b_s, w_pool=m_w_pool, pool_scale=m_pool_scale,
                   norm2_g=m_norm2_g, final_g=m_final_g)
    small_v = dict(norm1_g=v_norm1_g, conv_w=v_conv_w, conv_b=v_conv_b, conv_ln_g=v_conv_ln_g, conv_ln_b=v_conv_ln_b,
                   sg_ln_g=v_sg_ln_g, sg_ln_b=v_sg_ln_b, w_s=v_w_s, b_s=v_b_s, w_pool=v_w_pool, pool_scale=v_pool_scale,
                   norm2_g=v_norm2_g, final_g=v_final_g)
    two_d = lambda a: a[None] if a.ndim == 1 else a
    d_s, nm_s, nv_s = adamw_small(*[[two_d(d[k]) for k in small_names] for d in (g_small, small_w, small_m, small_v)])
    d_small = {k: a.reshape(small_w[k].shape) for k, a in zip(small_names, d_s)}
    nm_small = {k: a.reshape(small_w[k].shape) for k, a in zip(small_names, nm_s)}
    nv_small = {k: a.reshape(small_w[k].shape) for k, a in zip(small_names, nv_s)}

    order = ["norm1_g", "w_in", "conv_w", "conv_b", "conv_ln_g", "conv_ln_b", "w_pw", "sg_ln_g", "sg_ln_b", "w_s", "b_s",
             "w_pool", "pool_scale", "w_out", "norm2_g", "w_gate_up", "w_down", "final_g"]
    grads = dict(g_small, w_in=g_w_in, w_pw=g_w_pw, w_out=g_w_out, w_gate_up=g_w_gu, w_down=g_w_d)
    deltas = dict(d_small, w_in=d_w_in, w_pw=d_w_pw, w_out=d_w_out, w_gate_up=d_w_gu, w_down=d_w_d)
    new_m = dict(nm_small, w_in=nm_w_in, w_pw=nm_w_pw, w_out=nm_w_out, w_gate_up=nm_w_gu, w_down=nm_w_d)
    new_v = dict(nv_small, w_in=nv_w_in, w_pw=nv_w_pw, w_out=nv_w_out, w_gate_up=nv_w_gu, w_down=nv_w_d)
    return (loss, grad_x, *[grads[k] for k in order], *[deltas[k] for k in order], *[new_m[k] for k in order],
            *[new_v[k] for k in order])
```

```python
import functools
import math

import jax
import jax.numpy as jnp
from jax import lax
from jax.experimental import pallas as pl
from jax.experimental.pallas import tpu as pltpu

F32 = jnp.float32
BF16 = jnp.bfloat16

D_MODEL = 1024
D_A = 384
D_B = 384
D_C = 256
D_IN = 2 * D_A + 2 * D_B + D_C
N_HEADS_B = 4
HEAD_DIM_B = 96
POOL_WINDOWS = (2, 4, 8, 16)
GROUP_DIM_C = 64
CONV_WIDTH = 31
CHUNK = 128
D_FF = 2816
RMS_EPS = 1e-6
LN_EPS = 1e-5
DEPTH = 2
N_DEV = 8

ADAM_LR = 0.001
ADAM_B1 = 0.9
ADAM_B2 = 0.999
ADAM_EPS = 1e-08
ADAM_WD = 0.01
ADAM_STEP = 10

LANES = 128
SUBLANES = 8

CONV_HALO = 32
POOL_HALO = 32
assert POOL_WINDOWS == (2, 4, 8, 16) and POOL_HALO == SUBLANES * len(POOL_WINDOWS)

VMEM_LIMIT = 56 * 1024 * 1024

MESH_ID = pl.DeviceIdType.MESH


def _dot(a, b):
    return jnp.dot(a, b, preferred_element_type=F32)


def _dot_nt(a, b):
    return lax.dot_general(a, b, (((1,), (1,)), ((), ())), preferred_element_type=F32)


def _dot_tn(a, b):
    return lax.dot_general(a, b, (((0,), (0,)), ((), ())), preferred_element_type=F32)


def _sigmoid(x):
    return 0.5 * jnp.tanh(0.5 * x) + 0.5


def _shifted_taps(buf, first_row, n_shifts, tm):
    for phase in range(min(SUBLANES, n_shifts)):
        shifts = list(range(phase, n_shifts, SUBLANES))
        span = buf[first_row + phase:first_row + shifts[-1] + tm, :]
        for s in shifts:
            yield s, span[s - phase:s - phase + tm, :]


def _window_sums_back(x_ref, bufs, n_rows):
    out, src, w = [], x_ref, 1
    for l in range(len(POOL_WINDOWS)):
        lo = SUBLANES * (l + 1)
        cur = src[lo:n_rows, :] + src[lo - w:n_rows - w, :]
        out.append(cur)
        if l < len(bufs):
            bufs[l][lo:n_rows, :] = cur
            src = bufs[l]
        w *= 2
    return out


def _window_sums_ahead(x_ref, bufs, n_rows):
    out, src, w = [], x_ref, 1
    for l in range(len(POOL_WINDOWS)):
        hi = n_rows - SUBLANES * (l + 1)
        cur = src[0:hi, :] + src[w:hi + w, :]
        out.append(cur)
        if l < len(bufs):
            bufs[l][0:hi, :] = cur
            src = bufs[l]
        w *= 2
    return out


_GELU_C = math.sqrt(2.0 / math.pi)


def _gelu_and_grad(x):
    x2 = x * x
    inner = _GELU_C * (x + 0.044715 * x2 * x)
    t = jnp.tanh(inner)
    g = 0.5 * x * (1.0 + t)
    dg = 0.5 * (1.0 + t) + 0.5 * x * (1.0 - t * t) * _GELU_C * (1.0 + 3.0 * 0.044715 * x2)
    return g, dg


def _ln_stats(x):
    mu = jnp.mean(x, axis=-1, keepdims=True)
    xc = x - mu
    var = jnp.mean(xc * xc, axis=-1, keepdims=True)
    rstd = lax.rsqrt(var + LN_EPS)
    return xc * rstd, rstd


def _ln_bwd(dy, xhat, rstd, g):
    dxhat = dy * g
    return rstd * (dxhat - jnp.mean(dxhat, axis=-1, keepdims=True)
                   - xhat * jnp.mean(dxhat * xhat, axis=-1, keepdims=True))


def _rms_bwd(dh, xn, r, g):
    dxn = dh * g
    return r * (dxn - xn * jnp.mean(dxn * xn, axis=-1, keepdims=True))


def _head_masks(width):
    lane = lax.broadcasted_iota(jnp.int32, (1, width), 1)
    return [(lane >= h * HEAD_DIM_B) & (lane < (h + 1) * HEAD_DIM_B) for h in range(N_HEADS_B)]


def _pool_select(vals, width):
    lane = lax.broadcasted_iota(jnp.int32, (1, width), 1)
    out = vals[-1]
    for g in range(len(vals) - 2, -1, -1):
        out = jnp.where(lane < (g + 1) * GROUP_DIM_C, vals[g], out)
    return out


def _pool_counts(pos):
    return _pool_select([jnp.minimum(pos + 1.0, float(w)) for w in POOL_WINDOWS], D_C)


def _full(shape):
    n = len(shape)
    return pl.BlockSpec(shape, lambda *_: (0,) * n)


def _params(sem):
    return pltpu.CompilerParams(dimension_semantics=sem, vmem_limit_bytes=VMEM_LIMIT)


def _my_coords():
    return lax.axis_index("x"), lax.axis_index("y"), lax.axis_index("c")


def _peer(me, rel):
    x, y, c = me
    bx, by, bc = (rel >> 2) & 1, (rel >> 1) & 1, rel & 1
    return (1 - x if bx else x, 1 - y if by else y, 1 - c if bc else c)


def _index_of(dev):
    return 4 * dev[0] + 2 * dev[1] + dev[2]


SIBLING = 1
OTHER_CHIPS = (2, 4, 6)


class Gather:
    def __init__(self, shards):
        n = len(shards)
        self.inputs = list(shards)
        self.out_shape = [jax.ShapeDtypeStruct((N_DEV,) + s.shape, s.dtype) for s in shards]
        self.scratch = [pltpu.SemaphoreType.DMA((N_DEV - 1, n)), pltpu.SemaphoreType.DMA((N_DEV - 1, n)),
                        pltpu.SemaphoreType.DMA((n,))]

    @staticmethod
    def _copy(src, dst, sems, rel, k, to):
        return pltpu.make_async_remote_copy(src_ref=src, dst_ref=dst, send_sem=sems[0].at[rel - 1, k],
                                            recv_sem=sems[1].at[rel - 1, k], device_id=to, device_id_type=MESH_ID)

    def start(self, ins, outs, sems):
        me = _my_coords()
        mine = _index_of(me)
        for k, src in enumerate(ins):
            pltpu.make_async_copy(src, outs[k].at[mine], sems[2].at[k]).start()
            for rel in (SIBLING,) + OTHER_CHIPS:
                self._copy(src, outs[k].at[mine], sems, rel, k, _peer(me, rel)).start()

    def finish(self, ins, outs, sems):
        me = _my_coords()
        mine = _index_of(me)
        sibling = _peer(me, SIBLING)
        for rel in OTHER_CHIPS:
            slot = _index_of(_peer(me, rel))
            for k in range(len(ins)):
                self._copy(ins[k], outs[k].at[slot], sems, rel, k, sibling).wait_recv()
                self._copy(outs[k].at[slot], outs[k].at[slot], sems, rel + 1, k, sibling).start()
        for rel in (SIBLING,) + tuple(r + 1 for r in OTHER_CHIPS):
            slot = _index_of(_peer(me, rel))
            for k in range(len(ins)):
                self._copy(ins[k], outs[k].at[slot], sems, rel, k, sibling).wait_recv()
        for rel in range(1, N_DEV):
            for k in range(len(ins)):
                self._copy(ins[k], outs[k].at[mine], sems, rel, k, sibling).wait_send()
        for k, src in enumerate(ins):
            pltpu.make_async_copy(src, outs[k].at[mine], sems[2].at[k]).wait()


class Exchange:
    def __init__(self, fulls):
        n = len(fulls)
        self.inputs = list(fulls)
        self.out_shape = [jax.ShapeDtypeStruct(f.shape, f.dtype) for f in fulls]
        self.scratch = [pltpu.SemaphoreType.DMA((N_DEV - 1, n)), pltpu.SemaphoreType.DMA((N_DEV - 1, n)),
                        pltpu.SemaphoreType.DMA((n,))]

    def start(self, ins, outs, sems):
        me = _my_coords()
        mine = _index_of(me)
        for k, src in enumerate(ins):
            pltpu.make_async_copy(src.at[mine], outs[k].at[mine], sems[2].at[k]).start()
            for rel in range(1, N_DEV):
                to = _peer(me, rel)
                Gather._copy(src.at[_index_of(to)], outs[k].at[mine], sems, rel, k, to).start()

    def finish(self, ins, outs, sems):
        me = _my_coords()
        mine = _index_of(me)
        for rel in range(1, N_DEV):
            frm = _peer(me, rel)
            for k, src in enumerate(ins):
                Gather._copy(src.at[mine], outs[k].at[_index_of(frm)], sems, rel, k, frm).wait_recv()
        for rel in range(1, N_DEV):
            for k, src in enumerate(ins):
                Gather._copy(src.at[mine], outs[k].at[mine], sems, rel, k, _peer(me, rel)).wait_send()
        for k, src in enumerate(ins):
            pltpu.make_async_copy(src.at[mine], outs[k].at[mine], sems[2].at[k]).wait()


class Together:
    def __init__(self, plans):
        self.plans = list(plans)
        self.inputs = [a for p in self.plans for a in p.inputs]
        self.out_shape = [s for p in self.plans for s in p.out_shape]
        self.scratch = [s for p in self.plans for s in p.scratch]

    def _each(self, ins, outs, sems):
        i = o = s = 0
        for p in self.plans:
            ni, no, ns = len(p.inputs), len(p.out_shape), len(p.scratch)
            yield p, ins[i:i + ni], outs[o:o + no], sems[s:s + ns]
            i, o, s = i + ni, o + no, s + ns

    def start(self, ins, outs, sems):
        for p, pi, po, ps in self._each(ins, outs, sems):
            p.start(pi, po, ps)

    def finish(self, ins, outs, sems):
        for p, pi, po, ps in self._each(ins, outs, sems):
            p.finish(pi, po, ps)

    def split(self, results):
        out, o = [], 0
        for p in self.plans:
            out.append(results[o:o + len(p.out_shape)])
            o += len(p.out_shape)
        return out


def _hosted_call(body, *, name, grid, in_specs, out_specs, out_shape, scratch_shapes, args, comm=None):
    sem = ("arbitrary",) * len(grid)
    if comm is None:
        res = pl.pallas_call(body, name=name, grid=grid, in_specs=in_specs, out_specs=out_specs, out_shape=out_shape,
                             scratch_shapes=scratch_shapes, compiler_params=_params(sem))(*args)
        return list(res), []
    n_in, n_out, n_scr = len(in_specs), len(out_specs), len(scratch_shapes)
    n_cin, n_cout = len(comm.inputs), len(comm.out_shape)

    def hosted(*refs):
        ins, refs = refs[:n_in], refs[n_in:]
        cins, refs = refs[:n_cin], refs[n_cin:]
        outs, refs = refs[:n_out], refs[n_out:]
        couts, refs = refs[:n_cout], refs[n_cout:]
        scr, csems = refs[:n_scr], refs[n_scr:]
        ids = [pl.program_id(a) for a in range(len(grid))]
        first = functools.reduce(lambda a, b: a & b, [i == 0 for i in ids])
        last = functools.reduce(lambda a, b: a & b, [i == g - 1 for i, g in zip(ids, grid)])

        @pl.when(first)
        def _():
            comm.start(cins, couts, csems)

        body(*ins, *outs, *scr)

        @pl.when(last)
        def _():
            comm.finish(cins, couts, csems)

    any_spec = pl.BlockSpec(memory_space=pl.ANY)
    res = pl.pallas_call(
        hosted, name=name, grid=grid, in_specs=list(in_specs) + [any_spec] * n_cin,
        out_specs=list(out_specs) + [any_spec] * n_cout, out_shape=list(out_shape) + comm.out_shape,
        scratch_shapes=list(scratch_shapes) + comm.scratch,
        compiler_params=pltpu.CompilerParams(dimension_semantics=sem, vmem_limit_bytes=VMEM_LIMIT, has_side_effects=True),
    )(*args, *comm.inputs)
    return list(res[:n_out]), list(res[n_out:])


def run_comm(comm, *, name):
    n_cin, n_cout = len(comm.inputs), len(comm.out_shape)

    def body(*refs):
        cins, couts, csems = refs[:n_cin], refs[n_cin:n_cin + n_cout], refs[n_cin + n_cout:]
        comm.start(cins, couts, csems)
        comm.finish(cins, couts, csems)

    any_spec = pl.BlockSpec(memory_space=pl.ANY)
    return pl.pallas_call(
        body, name=name, in_specs=[any_spec] * n_cin, out_specs=[any_spec] * n_cout, out_shape=comm.out_shape,
        scratch_shapes=comm.scratch, compiler_params=pltpu.CompilerParams(has_side_effects=True),
    )(*comm.inputs)


def mixer_fwd(x, g1, winT, convw, convb, clng, clnb, wpw, slng, slnb, wm, bias, wbd, pscale, wout, *, seq, tm,
              comm=None):
    T = x.shape[0]
    tiles_per_seq = seq // tm
    n_chunks = tm // CHUNK

    def body(x_ref, g1_ref, winT_ref, convw_ref, convb_ref, clng_ref, clnb_ref, wpw_ref, slng_ref, slnb_ref,
             wm_ref, bias_ref, wbd_ref, pscale_ref, wout_ref,
             z_ref, ycv_ref, p_ref, mix_ref, x1_ref, ybuf, zcbuf, *pbufs):
        i = pl.program_id(0)
        tile_in_seq = i % tiles_per_seq

        @pl.when(tile_in_seq == 0)
        def _():
            ybuf[0:CONV_HALO, :] = jnp.zeros((CONV_HALO, D_A), F32)
            zcbuf[0:POOL_HALO, :] = jnp.zeros((POOL_HALO, D_C), F32)

        x = x_ref[...]
        r = lax.rsqrt(jnp.mean(x * x, axis=-1, keepdims=True) + RMS_EPS)
        h = (x * r * g1_ref[...]).astype(BF16)
        z = _dot_nt(h, winT_ref[...])
        z_ref[...] = z

        y = z[:, 0:D_A] * _sigmoid(z[:, D_A:2 * D_A])
        ybuf[CONV_HALO:CONV_HALO + tm, :] = y
        acc = jnp.zeros((tm, D_A), F32) + convb_ref[...]
        for k, rows in _shifted_taps(ybuf, CONV_HALO - (CONV_WIDTH - 1), CONV_WIDTH, tm):
            acc = acc + convw_ref[k:k + 1, :] * rows
        ybuf[0:CONV_HALO, :] = ybuf[tm:tm + CONV_HALO, :]
        ycv_ref[...] = acc
        xhat, _ = _ln_stats(acc)
        ln = xhat * clng_ref[...] + clnb_ref[...]
        s = ln * _sigmoid(ln)
        ya = _dot(s.astype(BF16), wpw_ref[...])

        gb, _ = _gelu_and_grad(z[:, 2 * D_A:2 * D_A + 2 * D_B])
        u = gb[:, 0:D_B]
        vhat, _ = _ln_stats(gb[:, D_B:2 * D_B])
        vn = vhat * slng_ref[...] + slnb_ref[...]
        masks = _head_masks(D_B)
        yb_parts = []
        for c in range(n_chunks):
            vn_c = vn[c * CHUNK:(c + 1) * CHUNK, :]
            sg = bias_ref[...]
            for hh in range(N_HEADS_B):
                sg = sg + _dot(wm_ref[hh], jnp.where(masks[hh], vn_c, 0.0).astype(BF16))
            yb_parts.append(u[c * CHUNK:(c + 1) * CHUNK, :] * sg)
        yb = jnp.concatenate(yb_parts, axis=0) if n_chunks > 1 else yb_parts[0]

        zc = z[:, 2 * D_A + 2 * D_B:D_IN]
        zcbuf[POOL_HALO:POOL_HALO + tm, :] = zc
        sums = [v[POOL_HALO - SUBLANES * (l + 1):POOL_HALO - SUBLANES * (l + 1) + tm, :]
                for l, v in enumerate(_window_sums_back(zcbuf, pbufs, POOL_HALO + tm))]
        zcbuf[0:POOL_HALO, :] = zcbuf[tm:tm + POOL_HALO, :]
        pos = (tile_in_seq * tm + lax.broadcasted_iota(jnp.int32, (tm, 1), 0)).astype(F32)
        p = _pool_select(sums, D_C) / _pool_counts(pos) - zc
        p_ref[...] = p
        yc = _dot(p.astype(BF16), wbd_ref[...]) * pscale_ref[...]

        mix = jnp.concatenate([ya, yb, yc], axis=1).astype(BF16)
        mix_ref[...] = mix
        x1_ref[...] = x + _dot(mix, wout_ref[...])

    row = lambda w: pl.BlockSpec((tm, w), lambda i: (i, 0))
    return _hosted_call(
        body, name="mixer_fwd", grid=(T // tm,),
        in_specs=[row(D_MODEL), _full((1, D_MODEL)), _full((D_IN, D_MODEL)), _full((CONV_WIDTH, D_A)),
                  _full((1, D_A)), _full((1, D_A)), _full((1, D_A)), _full((D_A, D_A)), _full((1, D_B)), _full((1, D_B)),
                  _full((N_HEADS_B, CHUNK, CHUNK)), _full((CHUNK, D_B)), _full((D_C, D_C)), _full((1, D_C)),
                  _full((D_MODEL, D_MODEL))],
        out_specs=[row(D_IN), row(D_A), row(D_C), row(D_MODEL), row(D_MODEL)],
        out_shape=[jax.ShapeDtypeStruct((T, D_IN), F32), jax.ShapeDtypeStruct((T, D_A), F32),
                   jax.ShapeDtypeStruct((T, D_C), F32), jax.ShapeDtypeStruct((T, D_MODEL), BF16),
                   jax.ShapeDtypeStruct((T, D_MODEL), F32)],
        scratch_shapes=[pltpu.VMEM((CONV_HALO + tm, D_A), F32)]
        + [pltpu.VMEM((POOL_HALO + tm, D_C), F32)] * len(POOL_WINDOWS),
        args=(x, g1, winT, convw, convb, clng, clnb, wpw, slng, slnb, wm, bias, wbd, pscale, wout), comm=comm)


def mixer_bwd(dx1, x, z, ycv, p, g1, winT, convw, clng, clnb, wpw, slng, slnb, wm, wmT, bias, wbd, pscale, wout,
              *, seq, tm, comm=None):
    T = x.shape[0]
    tiles_per_seq = seq // tm
    n_tiles = T // tm
    n_chunks = tm // CHUNK

    def body(dx1_ref, x_ref, z_ref, ycv_ref, p_ref, g1_ref, winT_ref, convw_ref, clng_ref, clnb_ref, wpw_ref,
             slng_ref, slnb_ref, wm_ref, wmT_ref, bias_ref, wbd_ref, pscale_ref, wout_ref,
             dx_ref, dz_ref, h_ref, s_ref, dya_ref,
             dg1_ref, dconvw_ref, dconvb_ref, dclng_ref, dclnb_ref, dslng_ref, dslnb_ref, dwm_ref, dbs_ref,
             dwbd_ref, dpscale_ref, dycbuf, dpcbuf, *pbufs):
        i = pl.program_id(0)
        tile_in_seq = (n_tiles - 1 - i) % tiles_per_seq

        @pl.when(i == 0)
        def _():
            for ref in (dg1_ref, dconvw_ref, dconvb_ref, dclng_ref, dclnb_ref, dslng_ref, dslnb_ref, dwm_ref,
                        dbs_ref, dwbd_ref, dpscale_ref):
                ref[...] = jnp.zeros(ref.shape, F32)

        @pl.when(tile_in_seq == tiles_per_seq - 1)
        def _():
            dycbuf[tm:tm + CONV_HALO, :] = jnp.zeros((CONV_HALO, D_A), F32)
            dpcbuf[tm:tm + POOL_HALO, :] = jnp.zeros((POOL_HALO, D_C), F32)

        dx1 = dx1_ref[...]
        z = z_ref[...]
        dmix = _dot_nt(dx1.astype(BF16), wout_ref[...])
        dya = dmix[:, 0:D_A]
        dyb = dmix[:, D_A:D_A + D_B]
        dyc = dmix[:, D_A + D_B:D_MODEL]

        p = p_ref[...]
        pb = p.astype(BF16)
        q = _dot(pb, wbd_ref[...])
        dpscale_ref[...] += jnp.sum(dyc * q, axis=0, keepdims=True)
        dq = (dyc * pscale_ref[...]).astype(BF16)
        dwbd_ref[...] += _dot_tn(pb, dq)
        dp = _dot_nt(dq, wbd_ref[...])
        pos = (tile_in_seq * tm + lax.broadcasted_iota(jnp.int32, (tm, 1), 0)).astype(F32)
        dpc = dp / _pool_counts(pos)
        dpcbuf[0:tm, :] = dpc
        sums = [v[0:tm, :] for v in _window_sums_ahead(dpcbuf, pbufs, tm + POOL_HALO)]
        dpcbuf[tm:tm + POOL_HALO, :] = dpcbuf[0:POOL_HALO, :]
        dzc = _pool_select(sums, D_C) - dp

        dya_ref[...] = dya.astype(BF16)
        ds = _dot_nt(dya.astype(BF16), wpw_ref[...])
        xhat, rstd = _ln_stats(ycv_ref[...])
        ln = xhat * clng_ref[...] + clnb_ref[...]
        sg = _sigmoid(ln)
        s_ref[...] = (ln * sg).astype(BF16)
        dln = ds * (sg * (1.0 + ln * (1.0 - sg)))
        dclng_ref[...] += jnp.sum(dln * xhat, axis=0, keepdims=True)
        dclnb_ref[...] += jnp.sum(dln, axis=0, keepdims=True)
        dycv = _ln_bwd(dln, xhat, rstd, clng_ref[...])
        dconvb_ref[...] += jnp.sum(dycv, axis=0, keepdims=True)
        a = z[:, 0:D_A]
        sgate = _sigmoid(z[:, D_A:2 * D_A])
        y = a * sgate
        dycbuf[0:tm, :] = dycv
        dy = jnp.zeros((tm, D_A), F32)
        for d, sh in _shifted_taps(dycbuf, 0, CONV_WIDTH, tm):
            k = CONV_WIDTH - 1 - d
            dy = dy + convw_ref[k:k + 1, :] * sh
            dconvw_ref[k:k + 1, :] += jnp.sum(y * sh, axis=0, keepdims=True)
        dycbuf[tm:tm + CONV_HALO, :] = dycbuf[0:CONV_HALO, :]
        da = dy * sgate
        dgate = dy * a * sgate * (1.0 - sgate)

        gb, dgb = _gelu_and_grad(z[:, 2 * D_A:2 * D_A + 2 * D_B])
        u = gb[:, 0:D_B]
        vhat, vrstd = _ln_stats(gb[:, D_B:2 * D_B])
        vn = vhat * slng_ref[...] + slnb_ref[...]
        masks = _head_masks(D_B)
        tril = (lax.broadcasted_iota(jnp.int32, (CHUNK, CHUNK), 0)
                >= lax.broadcasted_iota(jnp.int32, (CHUNK, CHUNK), 1))
        lane128 = lax.broadcasted_iota(jnp.int32, (1, CHUNK), 1)
        du_parts, dvn_parts = [], []
        for c in range(n_chunks):
            rows = slice(c * CHUNK, (c + 1) * CHUNK)
            vn_c = vn[rows, :]
            vh = [jnp.where(masks[hh], vn_c, 0.0).astype(BF16) for hh in range(N_HEADS_B)]
            sgc = bias_ref[...]
            for hh in range(N_HEADS_B):
                sgc = sgc + _dot(wm_ref[hh], vh[hh])
            dyb_c = dyb[rows, :]
            du_parts.append(dyb_c * sgc)
            dsg = dyb_c * u[rows, :]
            dvn_c = jnp.zeros((CHUNK, D_B), F32)
            dbs = jnp.zeros((CHUNK, CHUNK), F32)
            for hh in range(N_HEADS_B):
                dsg_h = jnp.where(masks[hh], dsg, 0.0)
                dsg_hb = dsg_h.astype(BF16)
                dwm_ref[hh] += jnp.where(tril, _dot_nt(dsg_hb, vh[hh]), 0.0)
                dvn_c = dvn_c + _dot(wmT_ref[hh], dsg_hb)
                dbs = dbs + jnp.where(lane128 == hh, jnp.sum(dsg_h, axis=1, keepdims=True), 0.0)
            dbs_ref[...] += dbs
            dvn_parts.append(dvn_c)
        du = jnp.concatenate(du_parts, axis=0) if n_chunks > 1 else du_parts[0]
        dvn = jnp.concatenate(dvn_parts, axis=0) if n_chunks > 1 else dvn_parts[0]
        dslng_ref[...] += jnp.sum(dvn * vhat, axis=0, keepdims=True)
        dslnb_ref[...] += jnp.sum(dvn, axis=0, keepdims=True)
        dv = _ln_bwd(dvn, vhat, vrstd, slng_ref[...])
        dzb = jnp.concatenate([du, dv], axis=1) * dgb

        dz = jnp.concatenate([da, dgate, dzb, dzc], axis=1).astype(BF16)
        dz_ref[...] = dz
        dh = _dot(dz, winT_ref[...])
        x = x_ref[...]
        r = lax.rsqrt(jnp.mean(x * x, axis=-1, keepdims=True) + RMS_EPS)
        xn = x * r
        h_ref[...] = (xn * g1_ref[...]).astype(BF16)
        dg1_ref[...] += jnp.sum(dh * xn, axis=0, keepdims=True)
        dx_ref[...] = dx1 + _rms_bwd(dh, xn, r, g1_ref[...])

    row = lambda w: pl.BlockSpec((tm, w), lambda i: (n_tiles - 1 - i, 0))
    acc_shapes = [(1, D_MODEL), (CONV_WIDTH, D_A), (1, D_A), (1, D_A), (1, D_A), (1, D_B), (1, D_B),
                  (N_HEADS_B, CHUNK, CHUNK), (CHUNK, CHUNK), (D_C, D_C), (1, D_C)]
    return _hosted_call(
        body, name="mixer_bwd", grid=(n_tiles,),
        in_specs=[row(D_MODEL), row(D_MODEL), row(D_IN), row(D_A), row(D_C),
                  _full((1, D_MODEL)), _full((D_IN, D_MODEL)), _full((CONV_WIDTH, D_A)), _full((1, D_A)), _full((1, D_A)),
                  _full((D_A, D_A)), _full((1, D_B)), _full((1, D_B)), _full((N_HEADS_B, CHUNK, CHUNK)),
                  _full((N_HEADS_B, CHUNK, CHUNK)), _full((CHUNK, D_B)), _full((D_C, D_C)), _full((1, D_C)),
                  _full((D_MODEL, D_MODEL))],
        out_specs=[row(D_MODEL), row(D_IN), row(D_MODEL), row(D_A), row(D_A)] + [_full(s) for s in acc_shapes],
        out_shape=[jax.ShapeDtypeStruct((T, D_MODEL), F32), jax.ShapeDtypeStruct((T, D_IN), BF16),
                   jax.ShapeDtypeStruct((T, D_MODEL), BF16), jax.ShapeDtypeStruct((T, D_A), BF16),
                   jax.ShapeDtypeStruct((T, D_A), BF16)] + [jax.ShapeDtypeStruct(s, F32) for s in acc_shapes],
        scratch_shapes=[pltpu.VMEM((tm + CONV_HALO, D_A), F32)]
        + [pltpu.VMEM((tm + POOL_HALO, D_C), F32)] * len(POOL_WINDOWS),
        args=(dx1, x, z, ycv, p, g1, winT, convw, clng, clnb, wpw, slng, slnb, wm, wmT, bias, wbd, pscale, wout),
        comm=comm)


def ffn_fwd(x1, g2, wguT, wd, *, tm, th, comm=None):
    T = x1.shape[0]
    n_h = D_FF // th

    def body(x1_ref, g2_ref, wgu_ref, wd_ref, x2_ref, gu_ref, h2_buf, acc):
        j = pl.program_id(1)

        @pl.when(j == 0)
        def _():
            x = x1_ref[...]
            r = lax.rsqrt(jnp.mean(x * x, axis=-1, keepdims=True) + RMS_EPS)
            h2_buf[...] = (x * r * g2_ref[...]).astype(BF16)
            acc[...] = x

        h2 = h2_buf[...]
        g = _dot_nt(h2, wgu_ref[0])
        u = _dot_nt(h2, wgu_ref[1])
        gu_ref[0] = g.astype(BF16)
        gu_ref[1] = u.astype(BF16)
        f = (g * _sigmoid(g) * u).astype(BF16)
        acc[...] += _dot(f, wd_ref[...])

        @pl.when(j == n_h - 1)
        def _():
            x2_ref[...] = acc[...]

    return _hosted_call(
        body, name="ffn_fwd", grid=(T // tm, n_h),
        in_specs=[pl.BlockSpec((tm, D_MODEL), lambda i, j: (i, 0)), _full((1, D_MODEL)),
                  pl.BlockSpec((2, th, D_MODEL), lambda i, j: (0, j, 0)),
                  pl.BlockSpec((th, D_MODEL), lambda i, j: (j, 0))],
        out_specs=[pl.BlockSpec((tm, D_MODEL), lambda i, j: (i, 0)),
                   pl.BlockSpec((2, tm, th), lambda i, j: (0, i, j))],
        out_shape=[jax.ShapeDtypeStruct((T, D_MODEL), F32), jax.ShapeDtypeStruct((2, T, D_FF), BF16)],
        scratch_shapes=[pltpu.VMEM((tm, D_MODEL), BF16), pltpu.VMEM((tm, D_MODEL), F32)],
        args=(x1, g2, wguT, wd), comm=comm)


def ffn_bwd(dx2, x1, gu, g2, wguT, wd, *, tm, th, comm=None):
    T = x1.shape[0]
    n_h = D_FF // th

    def body(dx2_ref, x1_ref, gu_ref, g2_ref, wgu_ref, wd_ref, dx1_ref, h2_ref, f_ref, dgu_ref, dg2_ref, acc):
        i = pl.program_id(0)
        j = pl.program_id(1)

        @pl.when((i == 0) & (j == 0))
        def _():
            dg2_ref[...] = jnp.zeros(dg2_ref.shape, F32)

        dx2 = dx2_ref[...]
        df = _dot_nt(dx2.astype(BF16), wd_ref[...])
        g = gu_ref[0].astype(F32)
        u = gu_ref[1].astype(F32)
        sg = _sigmoid(g)
        silu = g * sg
        f_ref[...] = (silu * u).astype(BF16)
        dgate = (df * u * (sg * (1.0 + g * (1.0 - sg)))).astype(BF16)
        dup = (df * silu).astype(BF16)
        dgu_ref[0] = dgate
        dgu_ref[1] = dup
        part = _dot(dgate, wgu_ref[0]) + _dot(dup, wgu_ref[1])

        @pl.when(j == 0)
        def _():
            acc[...] = part

        @pl.when(j > 0)
        def _():
            acc[...] += part

        @pl.when(j == n_h - 1)
        def _():
            x = x1_ref[...]
            r = lax.rsqrt(jnp.mean(x * x, axis=-1, keepdims=True) + RMS_EPS)
            xn = x * r
            dh = acc[...]
            h2_ref[...] = (xn * g2_ref[...]).astype(BF16)
            dg2_ref[...] += jnp.sum(dh * xn, axis=0, keepdims=True)
            dx1_ref[...] = dx2 + _rms_bwd(dh, xn, r, g2_ref[...])

    return _hosted_call(
        body, name="ffn_bwd", grid=(T // tm, n_h),
        in_specs=[pl.BlockSpec((tm, D_MODEL), lambda i, j: (i, 0)), pl.BlockSpec((tm, D_MODEL), lambda i, j: (i, 0)),
                  pl.BlockSpec((2, tm, th), lambda i, j: (0, i, j)), _full((1, D_MODEL)),
                  pl.BlockSpec((2, th, D_MODEL), lambda i, j: (0, j, 0)),
                  pl.BlockSpec((th, D_MODEL), lambda i, j: (j, 0))],
        out_specs=[pl.BlockSpec((tm, D_MODEL), lambda i, j: (i, 0)), pl.BlockSpec((tm, D_MODEL), lambda i, j: (i, 0)),
                   pl.BlockSpec((tm, th), lambda i, j: (i, j)), pl.BlockSpec((2, tm, th), lambda i, j: (0, i, j)),
                   _full((1, D_MODEL))],
        out_shape=[jax.ShapeDtypeStruct((T, D_MODEL), F32), jax.ShapeDtypeStruct((T, D_MODEL), BF16),
                   jax.ShapeDtypeStruct((T, D_FF), BF16), jax.ShapeDtypeStruct((2, T, D_FF), BF16),
                   jax.ShapeDtypeStruct((1, D_MODEL), F32)],
        scratch_shapes=[pltpu.VMEM((tm, D_MODEL), F32)],
        args=(dx2, x1, gu, g2, wguT, wd), comm=comm)


def head_fwd_bwd(x, target, fg, *, tm):
    T = x.shape[0]
    n_tiles = T // tm

    def body(x_ref, t_ref, fg_ref, loss_ref, dx_ref, dfg_ref, lacc):
        i = pl.program_id(0)

        @pl.when(i == 0)
        def _():
            lacc[...] = jnp.zeros(lacc.shape, F32)
            dfg_ref[...] = jnp.zeros(dfg_ref.shape, F32)

        x = x_ref[...]
        r = lax.rsqrt(jnp.mean(x * x, axis=-1, keepdims=True) + RMS_EPS)
        xn = x * r
        e = xn * fg_ref[...] - t_ref[...]
        lacc[...] += jnp.sum(e * e, axis=0, keepdims=True)
        dy = e * (1.0 / D_MODEL)
        dfg_ref[...] += jnp.sum(dy * xn, axis=0, keepdims=True)
        dx_ref[...] = _rms_bwd(dy, xn, r, fg_ref[...])

        @pl.when(i == n_tiles - 1)
        def _():
            loss_ref[...] = jnp.sum(lacc[...], axis=1, keepdims=True) * (0.5 / D_MODEL)

    row = pl.BlockSpec((tm, D_MODEL), lambda i: (i, 0))
    return pl.pallas_call(
        body, name="head_fwd_bwd", grid=(n_tiles,),
        in_specs=[row, row, _full((1, D_MODEL))],
        out_specs=[_full((1, 1)), row, _full((1, D_MODEL))],
        out_shape=[jax.ShapeDtypeStruct((1, 1), F32), jax.ShapeDtypeStruct((T, D_MODEL), F32),
                   jax.ShapeDtypeStruct((1, D_MODEL), F32)],
        scratch_shapes=[pltpu.VMEM((1, D_MODEL), F32)],
        compiler_params=_params(("arbitrary",)),
    )(x, target, fg)


def wgrad(a, b, *, tmo, tk, name, comm=None):
    G, T, M = a.shape
    N = b.shape[1]
    n_k = T // tk

    def body(a_ref, b_ref, o_ref, acc):
        k = pl.program_id(2)
        part = _dot_tn(a_ref[0].astype(BF16), b_ref[...].astype(BF16))
        if n_k == 1:
            o_ref[0] = part.astype(BF16)
            return

        @pl.when(k == 0)
        def _():
            acc[...] = part

        @pl.when((k > 0) & (k < n_k - 1))
        def _():
            acc[...] += part

        @pl.when(k == n_k - 1)
        def _():
            o_ref[0] = (acc[...] + part).astype(BF16)

    (out,), got = _hosted_call(
        body, name=name, grid=(G, M // tmo, n_k),
        in_specs=[pl.BlockSpec((1, tk, tmo), lambda g, m, k: (g, k, m)),
                  pl.BlockSpec((tk, N), lambda g, m, k: (k, 0))],
        out_specs=[pl.BlockSpec((1, tmo, N), lambda g, m, k: (g, m, 0))],
        out_shape=[jax.ShapeDtypeStruct((G, M, N), BF16)],
        scratch_shapes=[pltpu.VMEM((tmo, N), F32)],
        args=(a, b), comm=comm)
    return out, got


def sum_partials(gathered):
    n = len(gathered)

    def body(*refs):
        for in_ref, out_ref in zip(refs[:n], refs[n:]):
            total = in_ref[0]
            for d in range(1, N_DEV):
                total = total + in_ref[d]
            out_ref[...] = total

    vmem = pl.BlockSpec(memory_space=pltpu.VMEM)
    return pl.pallas_call(
        body, name="sum_partials", in_specs=[vmem] * n, out_specs=[vmem] * n,
        out_shape=[jax.ShapeDtypeStruct(g.shape[1:], F32) for g in gathered],
        compiler_params=pltpu.CompilerParams(vmem_limit_bytes=VMEM_LIMIT),
    )(*gathered)


_ADAM_C1 = 1.0 - ADAM_B1 ** ADAM_STEP
_ADAM_C2 = 1.0 - ADAM_B2 ** ADAM_STEP


def _adamw_math(w, g, m, v):
    m = ADAM_B1 * m + (1.0 - ADAM_B1) * g
    v = ADAM_B2 * v + (1.0 - ADAM_B2) * (g * g)
    m_hat = m / _ADAM_C1
    v_hat = v / _ADAM_C2
    delta = -ADAM_LR * (m_hat / (jnp.sqrt(v_hat) + ADAM_EPS) + ADAM_WD * w)
    return delta, m, v


def adamw_sharded(parts, w, m, v, *, tr, name, comm=None):
    _, R, C = parts[0].shape

    def body(p0_ref, p1_ref, w_ref, m_ref, v_ref, g_ref, d_ref, nm_ref, nv_ref):
        def update(p_ref):
            g = p_ref[0].astype(F32)
            for d in range(1, N_DEV):
                g = g + p_ref[d].astype(F32)
            delta, nm, nv = _adamw_math(w_ref[0], g, m_ref[0], v_ref[0])
            g_ref[0] = g
            d_ref[0] = delta
            nm_ref[0] = nm
            nv_ref[0] = nv

        @pl.when(pl.program_id(0) == 0)
        def _():
            update(p0_ref)

        @pl.when(pl.program_id(0) == 1)
        def _():
            update(p1_ref)

    n_i = R // tr
    p_specs = [pl.BlockSpec((N_DEV, tr, C), lambda l, i: (0, jnp.where(l == 0, i, n_i - 1), 0)),
               pl.BlockSpec((N_DEV, tr, C), lambda l, i: (0, jnp.where(l == 1, i, 0), 0))]
    o_spec = pl.BlockSpec((1, tr, C), lambda l, i: (l, i, 0))
    return _hosted_call(
        body, name=name, grid=(DEPTH, n_i),
        in_specs=p_specs + [o_spec, o_spec, o_spec], out_specs=[o_spec] * 4,
        out_shape=[jax.ShapeDtypeStruct(w.shape, F32)] * 4, scratch_shapes=[],
        args=(parts[0], parts[1], w, m, v), comm=comm)


def adamw_small(gs, ws, ms, vs):
    n = len(gs)

    def body(*refs):
        g_refs, w_refs, m_refs, v_refs = refs[:n], refs[n:2 * n], refs[2 * n:3 * n], refs[3 * n:4 * n]
        d_refs, nm_refs, nv_refs = refs[4 * n:5 * n], refs[5 * n:6 * n], refs[6 * n:]
        for k in range(n):
            delta, nm, nv = _adamw_math(w_refs[k][...], g_refs[k][...], m_refs[k][...], v_refs[k][...])
            d_refs[k][...] = delta
            nm_refs[k][...] = nm
            nv_refs[k][...] = nv

    vmem = pl.BlockSpec(memory_space=pltpu.VMEM)
    res = pl.pallas_call(
        body, name="adamw_small", in_specs=[vmem] * (4 * n), out_specs=[vmem] * (3 * n),
        out_shape=[jax.ShapeDtypeStruct(w.shape, F32) for w in ws] * 3,
        compiler_params=pltpu.CompilerParams(vmem_limit_bytes=VMEM_LIMIT),
    )(*gs, *ws, *ms, *vs)
    return res[:n], res[n:2 * n], res[2 * n:]


def _pack(arrays, row_multiple):
    flat = jnp.concatenate([a.reshape(-1) for a in arrays])
    rows = -(-flat.shape[0] // (LANES * row_multiple)) * row_multiple
    return jnp.pad(flat, (0, rows * LANES - flat.shape[0])).reshape(rows, LANES)


def _unpack(buf, shapes):
    flat = buf.reshape(-1)
    out, off = [], 0
    for s in shapes:
        n = math.prod(s)
        out.append(flat[off:off + n].reshape(s))
        off += n
    return out


def _block_diag(w_pool):
    G, d, _ = w_pool.shape
    eye = jnp.eye(G, dtype=w_pool.dtype)
    return (eye[:, None, :, None] * w_pool[:, :, None, :]).reshape(G * d, G * d)


def kernel(x, norm1_g, w_in, conv_w, conv_b, conv_ln_g, conv_ln_b, w_pw, sg_ln_g, sg_ln_b, w_s, b_s, w_pool, pool_scale, w_out, norm2_g, w_gate_up, w_down, final_g, loss_target, m_norm1_g, m_w_in, m_conv_w, m_conv_b, m_conv_ln_g, m_conv_ln_b, m_w_pw, m_sg_ln_g, m_sg_ln_b, m_w_s, m_b_s, m_w_pool, m_pool_scale, m_w_out, m_norm2_g, m_w_gate_up, m_w_down, m_final_g, v_norm1_g, v_w_in, v_conv_w, v_conv_b, v_conv_ln_g, v_conv_ln_b, v_w_pw, v_sg_ln_g, v_sg_ln_b, v_w_s, v_b_s, v_w_pool, v_pool_scale, v_w_out, v_norm2_g, v_w_gate_up, v_w_down, v_final_g):
    b_loc, seq, _ = x.shape
    T = b_loc * seq
    tm_mix = min(256, seq)
    tm_ffn_fwd = min(512, T)
    tm_ffn_bwd = min(256, T)
    tm_head = min(512, T)
    tk = min(2048, T)
    tk_f32 = min(1024, T)
    th = D_FF // 2
    cw = conv_w.shape[2]
    my_index = _index_of(_my_coords())

    xf = x.reshape(T, D_MODEL)
    tgt = loss_target.reshape(T, D_MODEL)

    mixer_shards = [[w_in[l].T.astype(BF16), w_out[l].astype(BF16), w_pw[l].astype(BF16)] for l in range(DEPTH)]
    ffn_shards = [[w_gate_up[l].T.astype(BF16), w_down[l].astype(BF16)] for l in range(DEPTH)]

    tril = jnp.tril(jnp.ones((CHUNK, CHUNK), dtype=bool))
    layers = []
    for l in range(DEPTH):
        wm = jnp.where(tril[None], w_s[l], 0.0).astype(BF16)
        layers.append(dict(
            g1=norm1_g[l][None], convb=conv_b[l][None], clng=conv_ln_g[l][None], clnb=conv_ln_b[l][None],
            slng=sg_ln_g[l][None], slnb=sg_ln_b[l][None], wm=wm, wmT=jnp.swapaxes(wm, 1, 2),
            bias=jnp.repeat(b_s[l].T, HEAD_DIM_B, axis=1), wbd=_block_diag(w_pool[l]).astype(BF16),
            pscale=pool_scale[l][None], g2=norm2_g[l][None]))

    def set_mixer_weights(l, g_in, g_out, g_pw):
        layers[l].update(winT=g_in.reshape(D_IN, D_MODEL), wout=g_out.reshape(D_MODEL, D_MODEL), wpw=g_pw.reshape(D_A, D_A))

    def set_ffn_weights(l, g_gu, g_d):
        layers[l].update(wguT=g_gu.reshape(2, D_FF, D_MODEL), wd=g_d.reshape(D_FF, D_MODEL))

    first = run_comm(Gather(mixer_shards[0] + [conv_w.reshape(DEPTH * CONV_WIDTH, cw).T]), name="gather_first")
    set_mixer_weights(0, *first[:3])
    convw_full = first[3].reshape(D_A, DEPTH * CONV_WIDTH).T.reshape(DEPTH, CONV_WIDTH, D_A)
    for l in range(DEPTH):
        layers[l]["convw"] = convw_full[l]

    saved = []
    cur = xf
    for l in range(DEPTH):
        w = layers[l]
        (z, ycv, p, mix, x1), got = mixer_fwd(
            cur, w["g1"], w["winT"], w["convw"], w["convb"], w["clng"], w["clnb"], w["wpw"], w["slng"], w["slnb"], w["wm"],
            w["bias"], w["wbd"], w["pscale"], w["wout"], seq=seq, tm=tm_mix,
            comm=Gather(ffn_shards[l] if l == 0 else ffn_shards[l][:1]))
        if l == 0:
            set_ffn_weights(l, *got)
        else:
            set_ffn_weights(l, got[0], early_wd)
        (x2, gu), got = ffn_fwd(x1, w["g2"], w["wguT"], w["wd"], tm=tm_ffn_fwd, th=th,
                                comm=Gather(mixer_shards[l + 1] + ffn_shards[l + 1][1:]) if l + 1 < DEPTH else None)
        if l + 1 < DEPTH:
            set_mixer_weights(l + 1, *got[:3])
            early_wd = got[3]
        saved.append((cur, z, ycv, p, mix, x1, gu))
        cur = x2
    loss_part, dx, dfg = head_fwd_bwd(cur, tgt, final_g[None], tm=tm_head)

    blocks = {"gu": (2 * D_FF // N_DEV, D_MODEL), "d": (D_FF // N_DEV, D_MODEL), "in": (D_IN // N_DEV, D_MODEL),
              "out": (D_MODEL // N_DEV, D_MODEL), "pw": (D_A // N_DEV, D_A)}
    by_device = lambda kind, g: g.reshape((N_DEV,) + blocks[kind])
    small = [None] * DEPTH
    parts = {}
    packs = [None] * DEPTH

    pending = None
    for l in reversed(range(DEPTH)):
        w = layers[l]
        x0, z, ycv, p, mix, x1, gu = saved[l]
        plan = Together([Exchange(pending[1]), Gather([pending[2]])]) if pending else None
        (dx1, h2, f, dgu, dg2), got = ffn_bwd(dx, x1, gu, w["g2"], w["wguT"], w["wd"], tm=tm_ffn_bwd, th=th, comm=plan)
        if pending:
            got_parts, (packs[l + 1],) = plan.split(got)
            parts.update(zip(pending[0], got_parts))
        gw_d, _ = wgrad(f[None], dx, tmo=th, tk=tk_f32, name="wgrad_down")
        last = l == 0
        gw_gu, got = wgrad(dgu, h2, tmo=th, tk=tk, name="wgrad_gate_up",
                           comm=Exchange([by_device("d", gw_d)]) if last else None)
        if last:
            parts[("d", l)], = got
        outs, got = mixer_bwd(
            dx1, x0, z, ycv, p, w["g1"], w["winT"], w["convw"], w["clng"], w["clnb"], w["wpw"], w["slng"], w["slnb"],
            w["wm"], w["wmT"], w["bias"], w["wbd"], w["pscale"], w["wout"], seq=seq, tm=tm_mix,
            comm=Exchange([by_device("gu", gw_gu)]))
        parts[("gu", l)], = got
        (dx, dz, h, s, dya, dg1, dconvw, dconvb, dclng, dclnb, dslng, dslnb, dwm, dbs, dwbd, dpscale) = outs
        small[l] = [dg1, dconvw, dconvb, dclng, dclnb, dslng, dslnb, dwm, dbs, dwbd, dpscale, dg2]
        if last:
            small[l] += [dfg, loss_part]
        pack = _pack(small[l], SUBLANES)
        if not last:
            gw_out, _ = wgrad(mix[None], dx1, tmo=D_MODEL, tk=tk_f32, name="wgrad_out")
            gw_in, got = wgrad(dz[None], h, tmo=D_IN // 2, tk=tk, name="wgrad_in", comm=Exchange([by_device("out", gw_out)]))
            parts[("out", l)], = got
            gw_pw, _ = wgrad(s[None], dya, tmo=D_A, tk=tk, name="wgrad_pw")
            pending = ([("d", l), ("in", l), ("pw", l)],
                       [by_device("d", gw_d), by_device("in", gw_in), by_device("pw", gw_pw)], pack)
    grad_x = dx.reshape(x.shape)

    gw_in, (packs[0],) = wgrad(dz[None], h, tmo=D_IN // 2, tk=tk, name="wgrad_in", comm=Gather([pack]))
    gw_out, got = wgrad(mix[None], dx1, tmo=D_MODEL, tk=tk_f32, name="wgrad_out", comm=Exchange([by_device("in", gw_in)]))
    parts[("in", 0)], = got
    gw_pw, _ = wgrad(s[None], dya, tmo=D_A, tk=tk, name="wgrad_pw")
    t = lambda a: jnp.swapaxes(a, 1, 2)
    (g_w_gu, d_w_gu, nm_w_gu, nv_w_gu), got = adamw_sharded(
        [parts[("gu", l)] for l in range(DEPTH)], t(w_gate_up), t(m_w_gate_up), t(v_w_gate_up), tr=2 * D_FF // N_DEV // 4,
        name="adamw_w_gate_up", comm=Exchange([by_device("out", gw_out), by_device("pw", gw_pw)]))
    parts[("out", 0)], parts[("pw", 0)] = got
    g_w_gu, d_w_gu, nm_w_gu, nv_w_gu = map(t, (g_w_gu, d_w_gu, nm_w_gu, nv_w_gu))
    p_in, p_d, p_out, p_pw = [[parts[(k, l)] for l in range(DEPTH)] for k in ("in", "d", "out", "pw")]

    sums = []
    for l, summed in enumerate(sum_partials(packs)):
        sums.append(_unpack(summed, [a.shape for a in small[l]]))
    loss = sums[0][-1][0, 0]
    dfg_sum = sums[0][-2]
    per_layer = 12
    sums = [a for l in range(DEPTH) for a in sums[l][:per_layer]]
    g_small = {k: [] for k in ("norm1_g", "conv_w", "conv_b", "conv_ln_g", "conv_ln_b", "sg_ln_g", "sg_ln_b", "w_s", "b_s",
                               "w_pool", "pool_scale", "norm2_g")}
    for l in range(DEPTH):
        dg1, dconvw, dconvb, dclng, dclnb, dslng, dslnb, dwm, dbs, dwbd, dpscale, dg2 = sums[per_layer * l:per_layer * (l + 1)]
        g_small["norm1_g"].append(dg1[0])
        g_small["conv_w"].append(lax.dynamic_slice_in_dim(dconvw, my_index * cw, cw, axis=1))
        g_small["conv_b"].append(dconvb[0])
        g_small["conv_ln_g"].append(dclng[0])
        g_small["conv_ln_b"].append(dclnb[0])
        g_small["sg_ln_g"].append(dslng[0])
        g_small["sg_ln_b"].append(dslnb[0])
        g_small["w_s"].append(dwm)
        g_small["b_s"].append(dbs[:, :N_HEADS_B].T)
        g_small["w_pool"].append(jnp.stack([dwbd[g * GROUP_DIM_C:(g + 1) * GROUP_DIM_C, g * GROUP_DIM_C:(g + 1) * GROUP_DIM_C]
                                            for g in range(len(POOL_WINDOWS))]))
        g_small["pool_scale"].append(dpscale[0])
        g_small["norm2_g"].append(dg2[0])
    g_small = {k: jnp.stack(v) for k, v in g_small.items()}
    g_small["final_g"] = dfg_sum[0]

    g_w_in, d_w_in, nm_w_in, nv_w_in = map(t, adamw_sharded(p_in, t(w_in), t(m_w_in), t(v_w_in), tr=D_IN // N_DEV // 2,
                                                            name="adamw_w_in")[0])
    g_w_d, d_w_d, nm_w_d, nv_w_d = adamw_sharded(p_d, w_down, m_w_down, v_w_down, tr=D_FF // N_DEV // 2,
                                                 name="adamw_w_down")[0]
    g_w_out, d_w_out, nm_w_out, nv_w_out = adamw_sharded(p_out, w_out, m_w_out, v_w_out, tr=D_MODEL // N_DEV,
                                                         name="adamw_w_out")[0]
    g_w_pw, d_w_pw, nm_w_pw, nv_w_pw = adamw_sharded(p_pw, w_pw, m_w_pw, v_w_pw, tr=D_A // N_DEV, name="adamw_w_pw")[0]

    small_names = ["norm1_g", "conv_w", "conv_b", "conv_ln_g", "conv_ln_b", "sg_ln_g", "sg_ln_b", "w_s", "b_s", "w_pool",
                   "pool_scale", "norm2_g", "final_g"]
    small_w = dict(norm1_g=norm1_g, conv_w=conv_w, conv_b=conv_b, conv_ln_g=conv_ln_g, conv_ln_b=conv_ln_b, sg_ln_g=sg_ln_g,
                   sg_ln_b=sg_ln_b, w_s=w_s, b_s=b_s, w_pool=w_pool, pool_scale=pool_scale, norm2_g=norm2_g, final_g=final_g)
    small_m = dict(norm1_g=m_norm1_g, conv_w=m_conv_w, conv_b=m_conv_b, conv_ln_g=m_conv_ln_g, conv_ln_b=m_conv_ln_b,
                   sg_ln_g=m_sg_ln_g, sg_ln_b=m_sg_ln_b, w_s=m_w_s, b_s=m_b_s, w_pool=m_w_pool, pool_scale=m_pool_scale,
                   norm2_g=m_norm2_g, final_g=m_final_g)
    small_v = dict(norm1_g=v_norm1_g, conv_w=v_conv_w, conv_b=v_conv_b, conv_ln_g=v_conv_ln_g, conv_ln_b=v_conv_ln_b,
                   sg_ln_g=v_sg_ln_g, sg_ln_b=v_sg_ln_b, w_s=v_w_s, b_s=v_b_s, w_pool=v_w_pool, pool_scale=v_pool_scale,
                   norm2_g=v_norm2_g, final_g=v_final_g)
    two_d = lambda a: a[None] if a.ndim == 1 else a
    d_s, nm_s, nv_s = adamw_small(*[[two_d(d[k]) for k in small_names] for d in (g_small, small_w, small_m, small_v)])
    d_small = {k: a.reshape(small_w[k].shape) for k, a in zip(small_names, d_s)}
    nm_small = {k: a.reshape(small_w[k].shape) for k, a in zip(small_names, nm_s)}
    nv_small = {k: a.reshape(small_w[k].shape) for k, a in zip(small_names, nv_s)}

    order = ["norm1_g", "w_in", "conv_w", "conv_b", "conv_ln_g", "conv_ln_b", "w_pw", "sg_ln_g", "sg_ln_b", "w_s", "b_s",
             "w_pool", "pool_scale", "w_out", "norm2_g", "w_gate_up", "w_down", "final_g"]
    grads = dict(g_small, w_in=g_w_in, w_pw=g_w_pw, w_out=g_w_out, w_gate_up=g_w_gu, w_down=g_w_d)
    deltas = dict(d_small, w_in=d_w_in, w_pw=d_w_pw, w_out=d_w_out, w_gate_up=d_w_gu, w_down=d_w_d)
    new_m = dict(nm_small, w_in=nm_w_in, w_pw=nm_w_pw, w_out=nm_w_out, w_gate_up=nm_w_gu, w_down=nm_w_d)
    new_v = dict(nv_small, w_in=nv_w_in, w_pw=nv_w_pw, w_out=nv_w_out, w_gate_up=nv_w_gu, w_down=nv_w_d)
    return (loss, grad_x, *[grads[k] for k in order], *[deltas[k] for k in order], *[new_m[k] for k in order],
            *[new_v[k] for k in order])
```

```python
import functools
import math

import jax
import jax.numpy as jnp
from jax import lax
from jax.experimental import pallas as pl
from jax.experimental.pallas import tpu as pltpu

F32 = jnp.float32
BF16 = jnp.bfloat16

D_MODEL = 1024
D_A = 384
D_B = 384
D_C = 256
D_IN = 2 * D_A + 2 * D_B + D_C
N_HEADS_B = 4
HEAD_DIM_B = 96
POOL_WINDOWS = (2, 4, 8, 16)
GROUP_DIM_C = 64
CONV_WIDTH = 31
CHUNK = 128
D_FF = 2816
RMS_EPS = 1e-6
LN_EPS = 1e-5
DEPTH = 2
N_DEV = 8

ADAM_LR = 0.001
ADAM_B1 = 0.9
ADAM_B2 = 0.999
ADAM_EPS = 1e-08
ADAM_WD = 0.01
ADAM_STEP = 10

LANES = 128
SUBLANES = 8

CONV_HALO = 32
POOL_HALO = 32
assert POOL_WINDOWS == (2, 4, 8, 16) and POOL_HALO == SUBLANES * len(POOL_WINDOWS)

VMEM_LIMIT = 56 * 1024 * 1024

MESH_ID = pl.DeviceIdType.MESH


def _dot(a, b):
    return jnp.dot(a, b, preferred_element_type=F32)


def _dot_nt(a, b):
    return lax.dot_general(a, b, (((1,), (1,)), ((), ())), preferred_element_type=F32)


def _dot_tn(a, b):
    return lax.dot_general(a, b, (((0,), (0,)), ((), ())), preferred_element_type=F32)


def _sigmoid(x):
    return 0.5 * jnp.tanh(0.5 * x) + 0.5


def _shifted_taps(buf, first_row, n_shifts, tm):
    for phase in range(min(SUBLANES, n_shifts)):
        shifts = list(range(phase, n_shifts, SUBLANES))
        span = buf[first_row + phase:first_row + shifts[-1] + tm, :]
        for s in shifts:
            yield s, span[s - phase:s - phase + tm, :]


def _window_sums_back(x_ref, bufs, n_rows):
    out, src, w = [], x_ref, 1
    for l in range(len(POOL_WINDOWS)):
        lo = SUBLANES * (l + 1)
        cur = src[lo:n_rows, :] + src[lo - w:n_rows - w, :]
        out.append(cur)
        if l < len(bufs):
            bufs[l][lo:n_rows, :] = cur
            src = bufs[l]
        w *= 2
    return out


def _window_sums_ahead(x_ref, bufs, n_rows):
    out, src, w = [], x_ref, 1
    for l in range(len(POOL_WINDOWS)):
        hi = n_rows - SUBLANES * (l + 1)
        cur = src[0:hi, :] + src[w:hi + w, :]
        out.append(cur)
        if l < len(bufs):
            bufs[l][0:hi, :] = cur
            src = bufs[l]
        w *= 2
    return out


_GELU_C = math.sqrt(2.0 / math.pi)


def _gelu_and_grad(x):
    x2 = x * x
    inner = _GELU_C * (x + 0.044715 * x2 * x)
    t = jnp.tanh(inner)
    g = 0.5 * x * (1.0 + t)
    dg = 0.5 * (1.0 + t) + 0.5 * x * (1.0 - t * t) * _GELU_C * (1.0 + 3.0 * 0.044715 * x2)
    return g, dg


def _ln_stats(x):
    mu = jnp.mean(x, axis=-1, keepdims=True)
    xc = x - mu
    var = jnp.mean(xc * xc, axis=-1, keepdims=True)
    rstd = lax.rsqrt(var + LN_EPS)
    return xc * rstd, rstd


def _ln_bwd(dy, xhat, rstd, g):
    dxhat = dy * g
    return rstd * (dxhat - jnp.mean(dxhat, axis=-1, keepdims=True)
                   - xhat * jnp.mean(dxhat * xhat, axis=-1, keepdims=True))


def _rms_bwd(dh, xn, r, g):
    dxn = dh * g
    return r * (dxn - xn * jnp.mean(dxn * xn, axis=-1, keepdims=True))


def _head_masks(width):
    lane = lax.broadcasted_iota(jnp.int32, (1, width), 1)
    return [(lane >= h * HEAD_DIM_B) & (lane < (h + 1) * HEAD_DIM_B) for h in range(N_HEADS_B)]


def _pool_select(vals, width):
    lane = lax.broadcasted_iota(jnp.int32, (1, width), 1)
    out = vals[-1]
    for g in range(len(vals) - 2, -1, -1):
        out = jnp.where(lane < (g + 1) * GROUP_DIM_C, vals[g], out)
    return out


def _pool_counts(pos):
    return _pool_select([jnp.minimum(pos + 1.0, float(w)) for w in POOL_WINDOWS], D_C)


def _full(shape):
    n = len(shape)
    return pl.BlockSpec(shape, lambda *_: (0,) * n)


def _params(sem):
    return pltpu.CompilerParams(dimension_semantics=sem, vmem_limit_bytes=VMEM_LIMIT)


def _my_coords():
    return lax.axis_index("x"), lax.axis_index("y"), lax.axis_index("c")


def _peer(me, rel):
    x, y, c = me
    bx, by, bc = (rel >> 2) & 1, (rel >> 1) & 1, rel & 1
    return (1 - x if bx else x, 1 - y if by else y, 1 - c if bc else c)


def _index_of(dev):
    return 4 * dev[0] + 2 * dev[1] + dev[2]


SIBLING = 1
OTHER_CHIPS = (2, 4, 6)


class Gather:
    def __init__(self, shards):
        n = len(shards)
        self.inputs = list(shards)
        self.out_shape = [jax.ShapeDtypeStruct((N_DEV,) + s.shape, s.dtype) for s in shards]
        self.scratch = [pltpu.SemaphoreType.DMA((N_DEV - 1, n)), pltpu.SemaphoreType.DMA((N_DEV - 1, n)),
                        pltpu.SemaphoreType.DMA((n,))]

    @staticmethod
    def _copy(src, dst, sems, rel, k, to):
        return pltpu.make_async_remote_copy(src_ref=src, dst_ref=dst, send_sem=sems[0].at[rel - 1, k],
                                            recv_sem=sems[1].at[rel - 1, k], device_id=to, device_id_type=MESH_ID)

    def start(self, ins, outs, sems):
        me = _my_coords()
        mine = _index_of(me)
        for k, src in enumerate(ins):
            pltpu.make_async_copy(src, outs[k].at[mine], sems[2].at[k]).start()
            for rel in (SIBLING,) + OTHER_CHIPS:
                self._copy(src, outs[k].at[mine], sems, rel, k, _peer(me, rel)).start()

    def finish(self, ins, outs, sems):
        me = _my_coords()
        mine = _index_of(me)
        sibling = _peer(me, SIBLING)
        for rel in OTHER_CHIPS:
            slot = _index_of(_peer(me, rel))
            for k in range(len(ins)):
                self._copy(ins[k], outs[k].at[slot], sems, rel, k, sibling).wait_recv()
                self._copy(outs[k].at[slot], outs[k].at[slot], sems, rel + 1, k, sibling).start()
        for rel in (SIBLING,) + tuple(r + 1 for r in OTHER_CHIPS):
            slot = _index_of(_peer(me, rel))
            for k in range(len(ins)):
                self._copy(ins[k], outs[k].at[slot], sems, rel, k, sibling).wait_recv()
        for rel in range(1, N_DEV):
            for k in range(len(ins)):
                self._copy(ins[k], outs[k].at[mine], sems, rel, k, sibling).wait_send()
        for k, src in enumerate(ins):
            pltpu.make_async_copy(src, outs[k].at[mine], sems[2].at[k]).wait()


class Exchange:
    def __init__(self, fulls):
        n = len(fulls)
        self.inputs = list(fulls)
        self.out_shape = [jax.ShapeDtypeStruct(f.shape, f.dtype) for f in fulls]
        self.scratch = [pltpu.SemaphoreType.DMA((N_DEV - 1, n)), pltpu.SemaphoreType.DMA((N_DEV - 1, n)),
                        pltpu.SemaphoreType.DMA((n,))]

    def start(self, ins, outs, sems):
        me = _my_coords()
        mine = _index_of(me)
        for k, src in enumerate(ins):
            pltpu.make_async_copy(src.at[mine], outs[k].at[mine], sems[2].at[k]).start()
            for rel in range(1, N_DEV):
                to = _peer(me, rel)
                Gather._copy(src.at[_index_of(to)], outs[k].at[mine], sems, rel, k, to).start()

    def finish(self, ins, outs, sems):
        me = _my_coords()
        mine = _index_of(me)
        for rel in range(1, N_DEV):
            frm = _peer(me, rel)
            for k, src in enumerate(ins):
                Gather._copy(src.at[mine], outs[k].at[_index_of(frm)], sems, rel, k, frm).wait_recv()
        for rel in range(1, N_DEV):
            for k, src in enumerate(ins):
                Gather._copy(src.at[mine], outs[k].at[mine], sems, rel, k, _peer(me, rel)).wait_send()
        for k, src in enumerate(ins):
            pltpu.make_async_copy(src.at[mine], outs[k].at[mine], sems[2].at[k]).wait()


class Together:
    def __init__(self, plans):
        self.plans = list(plans)
        self.inputs = [a for p in self.plans for a in p.inputs]
        self.out_shape = [s for p in self.plans for s in p.out_shape]
        self.scratch = [s for p in self.plans for s in p.scratch]

    def _each(self, ins, outs, sems):
        i = o = s = 0
        for p in self.plans:
            ni, no, ns = len(p.inputs), len(p.out_shape), len(p.scratch)
            yield p, ins[i:i + ni], outs[o:o + no], sems[s:s + ns]
            i, o, s = i + ni, o + no, s + ns

    def start(self, ins, outs, sems):
        for p, pi, po, ps in self._each(ins, outs, sems):
            p.start(pi, po, ps)

    def finish(self, ins, outs, sems):
        for p, pi, po, ps in self._each(ins, outs, sems):
            p.finish(pi, po, ps)

    def split(self, results):
        out, o = [], 0
        for p in self.plans:
            out.append(results[o:o + len(p.out_shape)])
            o += len(p.out_shape)
        return out


def _hosted_call(body, *, name, grid, in_specs, out_specs, out_shape, scratch_shapes, args, comm=None):
    sem = ("arbitrary",) * len(grid)
    if comm is None:
        res = pl.pallas_call(body, name=name, grid=grid, in_specs=in_specs, out_specs=out_specs, out_shape=out_shape,
                             scratch_shapes=scratch_shapes, compiler_params=_params(sem))(*args)
        return list(res), []
    n_in, n_out, n_scr = len(in_specs), len(out_specs), len(scratch_shapes)
    n_cin, n_cout = len(comm.inputs), len(comm.out_shape)

    def hosted(*refs):
        ins, refs = refs[:n_in], refs[n_in:]
        cins, refs = refs[:n_cin], refs[n_cin:]
        outs, refs = refs[:n_out], refs[n_out:]
        couts, refs = refs[:n_cout], refs[n_cout:]
        scr, csems = refs[:n_scr], refs[n_scr:]
        ids = [pl.program_id(a) for a in range(len(grid))]
        first = functools.reduce(lambda a, b: a & b, [i == 0 for i in ids])
        last = functools.reduce(lambda a, b: a & b, [i == g - 1 for i, g in zip(ids, grid)])

        @pl.when(first)
        def _():
            comm.start(cins, couts, csems)

        body(*ins, *outs, *scr)

        @pl.when(last)
        def _():
            comm.finish(cins, couts, csems)

    any_spec = pl.BlockSpec(memory_space=pl.ANY)
    res = pl.pallas_call(
        hosted, name=name, grid=grid, in_specs=list(in_specs) + [any_spec] * n_cin,
        out_specs=list(out_specs) + [any_spec] * n_cout, out_shape=list(out_shape) + comm.out_shape,
        scratch_shapes=list(scratch_shapes) + comm.scratch,
        compiler_params=pltpu.CompilerParams(dimension_semantics=sem, vmem_limit_bytes=VMEM_LIMIT, has_side_effects=True),
    )(*args, *comm.inputs)
    return list(res[:n_out]), list(res[n_out:])


def run_comm(comm, *, name):
    n_cin, n_cout = len(comm.inputs), len(comm.out_shape)

    def body(*refs):
        cins, couts, csems = refs[:n_cin], refs[n_cin:n_cin + n_cout], refs[n_cin + n_cout:]
        comm.start(cins, couts, csems)
        comm.finish(cins, couts, csems)

    any_spec = pl.BlockSpec(memory_space=pl.ANY)
    return pl.pallas_call(
        body, name=name, in_specs=[any_spec] * n_cin, out_specs=[any_spec] * n_cout, out_shape=comm.out_shape,
        scratch_shapes=comm.scratch, compiler_params=pltpu.CompilerParams(has_side_effects=True),
    )(*comm.inputs)


def mixer_fwd(x, g1, winT, convw, convb, clng, clnb, wpw, slng, slnb, wm, bias, wbd, pscale, wout, *, seq, tm,
              comm=None):
    T = x.shape[0]
    tiles_per_seq = seq // tm
    n_chunks = tm // CHUNK

    def body(x_ref, g1_ref, winT_ref, convw_ref, convb_ref, clng_ref, clnb_ref, wpw_ref, slng_ref, slnb_ref,
             wm_ref, bias_ref, wbd_ref, pscale_ref, wout_ref,
             z_ref, ycv_ref, p_ref, mix_ref, x1_ref, ybuf, zcbuf, *pbufs):
        i = pl.program_id(0)
        tile_in_seq = i % tiles_per_seq

        @pl.when(tile_in_seq == 0)
        def _():
            ybuf[0:CONV_HALO, :] = jnp.zeros((CONV_HALO, D_A), F32)
            zcbuf[0:POOL_HALO, :] = jnp.zeros((POOL_HALO, D_C), F32)

        x = x_ref[...]
        r = lax.rsqrt(jnp.mean(x * x, axis=-1, keepdims=True) + RMS_EPS)
        h = (x * r * g1_ref[...]).astype(BF16)
        z = _dot_nt(h, winT_ref[...])
        z_ref[...] = z

        y = z[:, 0:D_A] * _sigmoid(z[:, D_A:2 * D_A])
        ybuf[CONV_HALO:CONV_HALO + tm, :] = y
        acc = jnp.zeros((tm, D_A), F32) + convb_ref[...]
        for k, rows in _shifted_taps(ybuf, CONV_HALO - (CONV_WIDTH - 1), CONV_WIDTH, tm):
            acc = acc + convw_ref[k:k + 1, :] * rows
        ybuf[0:CONV_HALO, :] = ybuf[tm:tm + CONV_HALO, :]
        ycv_ref[...] = acc
        xhat, _ = _ln_stats(acc)
        ln = xhat * clng_ref[...] + clnb_ref[...]
        s = ln * _sigmoid(ln)
        ya = _dot(s.astype(BF16), wpw_ref[...])

        gb, _ = _gelu_and_grad(z[:, 2 * D_A:2 * D_A + 2 * D_B])
        u = gb[:, 0:D_B]
        vhat, _ = _ln_stats(gb[:, D_B:2 * D_B])
        vn = vhat * slng_ref[...] + slnb_ref[...]
        masks = _head_masks(D_B)
        yb_parts = []
        for c in range(n_chunks):
            vn_c = vn[c * CHUNK:(c + 1) * CHUNK, :]
            sg = bias_ref[...]
            for hh in range(N_HEADS_B):
                sg = sg + _dot(wm_ref[hh], jnp.where(masks[hh], vn_c, 0.0).astype(BF16))
            yb_parts.append(u[c * CHUNK:(c + 1) * CHUNK, :] * sg)
        yb = jnp.concatenate(yb_parts, axis=0) if n_chunks > 1 else yb_parts[0]

        zc = z[:, 2 * D_A + 2 * D_B:D_IN]
        zcbuf[POOL_HALO:POOL_HALO + tm, :] = zc
        sums = [v[POOL_HALO - SUBLANES * (l + 1):POOL_HALO - SUBLANES * (l + 1) + tm, :]
                for l, v in enumerate(_window_sums_back(zcbuf, pbufs, POOL_HALO + tm))]
        zcbuf[0:POOL_HALO, :] = zcbuf[tm:tm + POOL_HALO, :]
        pos = (tile_in_seq * tm + lax.broadcasted_iota(jnp.int32, (tm, 1), 0)).astype(F32)
        p = _pool_select(sums, D_C) / _pool_counts(pos) - zc
        p_ref[...] = p
        yc = _dot(p.astype(BF16), wbd_ref[...]) * pscale_ref[...]

        mix = jnp.concatenate([ya, yb, yc], axis=1).astype(BF16)
        mix_ref[...] = mix
        x1_ref[...] = x + _dot(mix, wout_ref[...])

    row = lambda w: pl.BlockSpec((tm, w), lambda i: (i, 0))
    return _hosted_call(
        body, name="mixer_fwd", grid=(T // tm,),
        in_specs=[row(D_MODEL), _full((1, D_MODEL)), _full((D_IN, D_MODEL)), _full((CONV_WIDTH, D_A)),
                  _full((1, D_A)), _full((1, D_A)), _full((1, D_A)), _full((D_A, D_A)), _full((1, D_B)), _full((1, D_B)),
                  _full((N_HEADS_B, CHUNK, CHUNK)), _full((CHUNK, D_B)), _full((D_C, D_C)), _full((1, D_C)),
                  _full((D_MODEL, D_MODEL))],
        out_specs=[row(D_IN), row(D_A), row(D_C), row(D_MODEL), row(D_MODEL)],
        out_shape=[jax.ShapeDtypeStruct((T, D_IN), F32), jax.ShapeDtypeStruct((T, D_A), F32),
                   jax.ShapeDtypeStruct((T, D_C), F32), jax.ShapeDtypeStruct((T, D_MODEL), BF16),
                   jax.ShapeDtypeStruct((T, D_MODEL), F32)],
        scratch_shapes=[pltpu.VMEM((CONV_HALO + tm, D_A), F32)]
        + [pltpu.VMEM((POOL_HALO + tm, D_C), F32)] * len(POOL_WINDOWS),
        args=(x, g1, winT, convw, convb, clng, clnb, wpw, slng, slnb, wm, bias, wbd, pscale, wout), comm=comm)


def mixer_bwd(dx1, x, z, ycv, p, g1, winT, convw, clng, clnb, wpw, slng, slnb, wm, wmT, bias, wbd, pscale, wout,
              *, seq, tm, comm=None):
    T = x.shape[0]
    tiles_per_seq = seq // tm
    n_tiles = T // tm
    n_chunks = tm // CHUNK

    def body(dx1_ref, x_ref, z_ref, ycv_ref, p_ref, g1_ref, winT_ref, convw_ref, clng_ref, clnb_ref, wpw_ref,
             slng_ref, slnb_ref, wm_ref, wmT_ref, bias_ref, wbd_ref, pscale_ref, wout_ref,
             dx_ref, dz_ref, h_ref, s_ref, dya_ref,
             dg1_ref, dconvw_ref, dconvb_ref, dclng_ref, dclnb_ref, dslng_ref, dslnb_ref, dwm_ref, dbs_ref,
             dwbd_ref, dpscale_ref, dycbuf, dpcbuf, *pbufs):
        i = pl.program_id(0)
        tile_in_seq = (n_tiles - 1 - i) % tiles_per_seq

        @pl.when(i == 0)
        def _():
            for ref in (dg1_ref, dconvw_ref, dconvb_ref, dclng_ref, dclnb_ref, dslng_ref, dslnb_ref, dwm_ref,
                        dbs_ref, dwbd_ref, dpscale_ref):
                ref[...] = jnp.zeros(ref.shape, F32)

        @pl.when(tile_in_seq == tiles_per_seq - 1)
        def _():
            dycbuf[tm:tm + CONV_HALO, :] = jnp.zeros((CONV_HALO, D_A), F32)
            dpcbuf[tm:tm + POOL_HALO, :] = jnp.zeros((POOL_HALO, D_C), F32)

        dx1 = dx1_ref[...]
        z = z_ref[...]
        dmix = _dot_nt(dx1.astype(BF16), wout_ref[...])
        dya = dmix[:, 0:D_A]
        dyb = dmix[:, D_A:D_A + D_B]
        dyc = dmix[:, D_A + D_B:D_MODEL]

        p = p_ref[...]
        pb = p.astype(BF16)
        q = _dot(pb, wbd_ref[...])
        dpscale_ref[...] += jnp.sum(dyc * q, axis=0, keepdims=True)
        dq = (dyc * pscale_ref[...]).astype(BF16)
        dwbd_ref[...] += _dot_tn(pb, dq)
        dp = _dot_nt(dq, wbd_ref[...])
        pos = (tile_in_seq * tm + lax.broadcasted_iota(jnp.int32, (tm, 1), 0)).astype(F32)
        dpc = dp / _pool_counts(pos)
        dpcbuf[0:tm, :] = dpc
        sums = [v[0:tm, :] for v in _window_sums_ahead(dpcbuf, pbufs, tm + POOL_HALO)]
        dpcbuf[tm:tm + POOL_HALO, :] = dpcbuf[0:POOL_HALO, :]
        dzc = _pool_select(sums, D_C) - dp

        dya_b = dya.astype(BF16)
        dya_ref[...] = dya_b
        ds = _dot_nt(dya_b, wpw_ref[...])
        xhat, rstd = _ln_stats(ycv_ref[...])
        ln = xhat * clng_ref[...] + clnb_ref[...]
        sg = _sigmoid(ln)
        s_ref[...] = (ln * sg).astype(BF16)
        dln = ds * (sg * (1.0 + ln * (1.0 - sg)))
        dclng_ref[...] += jnp.sum(dln * xhat, axis=0, keepdims=True)
        dclnb_ref[...] += jnp.sum(dln, axis=0, keepdims=True)
        dycv = _ln_bwd(dln, xhat, rstd, clng_ref[...])
        dconvb_ref[...] += jnp.sum(dycv, axis=0, keepdims=True)
        a = z[:, 0:D_A]
        sgate = _sigmoid(z[:, D_A:2 * D_A])
        y = a * sgate
        dycbuf[0:tm, :] = dycv
        dy = jnp.zeros((tm, D_A), F32)
        for d, sh in _shifted_taps(dycbuf, 0, CONV_WIDTH, tm):
            k = CONV_WIDTH - 1 - d
            dy = dy + convw_ref[k:k + 1, :] * sh
            dconvw_ref[k:k + 1, :] += jnp.sum(y * sh, axis=0, keepdims=True)
        dycbuf[tm:tm + CONV_HALO, :] = dycbuf[0:CONV_HALO, :]
        da = dy * sgate
        dgate = dy * a * sgate * (1.0 - sgate)

        gb, dgb = _gelu_and_grad(z[:, 2 * D_A:2 * D_A + 2 * D_B])
        u = gb[:, 0:D_B]
        vhat, vrstd = _ln_stats(gb[:, D_B:2 * D_B])
        vn = vhat * slng_ref[...] + slnb_ref[...]
        masks = _head_masks(D_B)
        tril = (lax.broadcasted_iota(jnp.int32, (CHUNK, CHUNK), 0)
                >= lax.broadcasted_iota(jnp.int32, (CHUNK, CHUNK), 1))
        lane128 = lax.broadcasted_iota(jnp.int32, (1, CHUNK), 1)
        du_parts, dvn_parts = [], []
        for c in range(n_chunks):
            rows = slice(c * CHUNK, (c + 1) * CHUNK)
            vn_c = vn[rows, :]
            vh = [jnp.where(masks[hh], vn_c, 0.0).astype(BF16) for hh in range(N_HEADS_B)]
            sgc = bias_ref[...]
            for hh in range(N_HEADS_B):
                sgc = sgc + _dot(wm_ref[hh], vh[hh])
            dyb_c = dyb[rows, :]
            du_parts.append(dyb_c * sgc)
            dsg = dyb_c * u[rows, :]
            dvn_c = jnp.zeros((CHUNK, D_B), F32)
            dbs = jnp.zeros((CHUNK, CHUNK), F32)
            for hh in range(N_HEADS_B):
                dsg_h = jnp.where(masks[hh], dsg, 0.0)
                dsg_hb = dsg_h.astype(BF16)
                dwm_ref[hh] += jnp.where(tril, _dot_nt(dsg_hb, vh[hh]), 0.0)
                dvn_c = dvn_c + _dot(wmT_ref[hh], dsg_hb)
                dbs = dbs + jnp.where(lane128 == hh, jnp.sum(dsg_h, axis=1, keepdims=True), 0.0)
            dbs_ref[...] += dbs
            dvn_parts.append(dvn_c)
        du = jnp.concatenate(du_parts, axis=0) if n_chunks > 1 else du_parts[0]
        dvn = jnp.concatenate(dvn_parts, axis=0) if n_chunks > 1 else dvn_parts[0]
        dslng_ref[...] += jnp.sum(dvn * vhat, axis=0, keepdims=True)
        dslnb_ref[...] += jnp.sum(dvn, axis=0, keepdims=True)
        dv = _ln_bwd(dvn, vhat, vrstd, slng_ref[...])
        dzb = jnp.concatenate([du, dv], axis=1) * dgb

        dz = jnp.concatenate([da, dgate, dzb, dzc], axis=1).astype(BF16)
        dz_ref[...] = dz
        dh = _dot(dz, winT_ref[...])
        x = x_ref[...]
        r = lax.rsqrt(jnp.mean(x * x, axis=-1, keepdims=True) + RMS_EPS)
        xn = x * r
        h_ref[...] = (xn * g1_ref[...]).astype(BF16)
        dg1_ref[...] += jnp.sum(dh * xn, axis=0, keepdims=True)
        dx_ref[...] = dx1 + _rms_bwd(dh, xn, r, g1_ref[...])

    row = lambda w: pl.BlockSpec((tm, w), lambda i: (n_tiles - 1 - i, 0))
    acc_shapes = [(1, D_MODEL), (CONV_WIDTH, D_A), (1, D_A), (1, D_A), (1, D_A), (1, D_B), (1, D_B),
                  (N_HEADS_B, CHUNK, CHUNK), (CHUNK, CHUNK), (D_C, D_C), (1, D_C)]
    return _hosted_call(
        body, name="mixer_bwd", grid=(n_tiles,),
        in_specs=[row(D_MODEL), row(D_MODEL), row(D_IN), row(D_A), row(D_C),
                  _full((1, D_MODEL)), _full((D_IN, D_MODEL)), _full((CONV_WIDTH, D_A)), _full((1, D_A)), _full((1, D_A)),
                  _full((D_A, D_A)), _full((1, D_B)), _full((1, D_B)), _full((N_HEADS_B, CHUNK, CHUNK)),
                  _full((N_HEADS_B, CHUNK, CHUNK)), _full((CHUNK, D_B)), _full((D_C, D_C)), _full((1, D_C)),
                  _full((D_MODEL, D_MODEL))],
        out_specs=[row(D_MODEL), row(D_IN), row(D_MODEL), row(D_A), row(D_A)] + [_full(s) for s in acc_shapes],
        out_shape=[jax.ShapeDtypeStruct((T, D_MODEL), F32), jax.ShapeDtypeStruct((T, D_IN), BF16),
                   jax.ShapeDtypeStruct((T, D_MODEL), BF16), jax.ShapeDtypeStruct((T, D_A), BF16),
                   jax.ShapeDtypeStruct((T, D_A), BF16)] + [jax.ShapeDtypeStruct(s, F32) for s in acc_shapes],
        scratch_shapes=[pltpu.VMEM((tm + CONV_HALO, D_A), F32)]
        + [pltpu.VMEM((tm + POOL_HALO, D_C), F32)] * len(POOL_WINDOWS),
        args=(dx1, x, z, ycv, p, g1, winT, convw, clng, clnb, wpw, slng, slnb, wm, wmT, bias, wbd, pscale, wout),
        comm=comm)


def ffn_fwd(x1, g2, wguT, wd, *, tm, th, comm=None):
    T = x1.shape[0]
    n_h = D_FF // th

    def body(x1_ref, g2_ref, wgu_ref, wd_ref, x2_ref, gu_ref, h2_buf, acc):
        j = pl.program_id(1)

        @pl.when(j == 0)
        def _():
            x = x1_ref[...]
            r = lax.rsqrt(jnp.mean(x * x, axis=-1, keepdims=True) + RMS_EPS)
            h2_buf[...] = (x * r * g2_ref[...]).astype(BF16)
            acc[...] = x

        h2 = h2_buf[...]
        rows = pl.ds(pl.multiple_of(j * th, th), th)
        g = _dot_nt(h2, wgu_ref[0, rows, :])
        u = _dot_nt(h2, wgu_ref[1, rows, :])
        gu_ref[0] = g.astype(BF16)
        gu_ref[1] = u.astype(BF16)
        f = (g * _sigmoid(g) * u).astype(BF16)
        acc[...] += _dot(f, wd_ref[rows, :])

        @pl.when(j == n_h - 1)
        def _():
            x2_ref[...] = acc[...]

    return _hosted_call(
        body, name="ffn_fwd", grid=(T // tm, n_h),
        in_specs=[pl.BlockSpec((tm, D_MODEL), lambda i, j: (i, 0)), _full((1, D_MODEL)),
                  _full((2, D_FF, D_MODEL)), _full((D_FF, D_MODEL))],
        out_specs=[pl.BlockSpec((tm, D_MODEL), lambda i, j: (i, 0)),
                   pl.BlockSpec((2, tm, th), lambda i, j: (0, i, j))],
        out_shape=[jax.ShapeDtypeStruct((T, D_MODEL), F32), jax.ShapeDtypeStruct((2, T, D_FF), BF16)],
        scratch_shapes=[pltpu.VMEM((tm, D_MODEL), BF16), pltpu.VMEM((tm, D_MODEL), F32)],
        args=(x1, g2, wguT, wd), comm=comm)


def ffn_bwd(dx2, x1, gu, g2, wguT, wd, *, tm, th, comm=None):
    T = x1.shape[0]
    n_h = D_FF // th

    def body(dx2_ref, x1_ref, gu_ref, g2_ref, wgu_ref, wd_ref, dx1_ref, h2_ref, f_ref, dgu_ref, dg2_ref, acc):
        i = pl.program_id(0)
        j = pl.program_id(1)

        @pl.when((i == 0) & (j == 0))
        def _():
            dg2_ref[...] = jnp.zeros(dg2_ref.shape, F32)

        dx2 = dx2_ref[...]
        rows = pl.ds(pl.multiple_of(j * th, th), th)
        df = _dot_nt(dx2.astype(BF16), wd_ref[rows, :])
        g = gu_ref[0].astype(F32)
        u = gu_ref[1].astype(F32)
        sg = _sigmoid(g)
        silu = g * sg
        f_ref[...] = (silu * u).astype(BF16)
        dgate = (df * u * (sg * (1.0 + g * (1.0 - sg)))).astype(BF16)
        dup = (df * silu).astype(BF16)
        dgu_ref[0] = dgate
        dgu_ref[1] = dup

        @pl.when(j == 0)
        def _():
            acc[...] = jnp.zeros(acc.shape, F32)

        acc[...] += _dot(dgate, wgu_ref[0, rows, :]) + _dot(dup, wgu_ref[1, rows, :])

        @pl.when(j == n_h - 1)
        def _():
            x = x1_ref[...]
            r = lax.rsqrt(jnp.mean(x * x, axis=-1, keepdims=True) + RMS_EPS)
            xn = x * r
            dh = acc[...]
            h2_ref[...] = (xn * g2_ref[...]).astype(BF16)
            dg2_ref[...] += jnp.sum(dh * xn, axis=0, keepdims=True)
            dx1_ref[...] = dx2 + _rms_bwd(dh, xn, r, g2_ref[...])

    return _hosted_call(
        body, name="ffn_bwd", grid=(T // tm, n_h),
        in_specs=[pl.BlockSpec((tm, D_MODEL), lambda i, j: (i, 0)), pl.BlockSpec((tm, D_MODEL), lambda i, j: (i, 0)),
                  pl.BlockSpec((2, tm, th), lambda i, j: (0, i, j)), _full((1, D_MODEL)),
                  _full((2, D_FF, D_MODEL)), _full((D_FF, D_MODEL))],
        out_specs=[pl.BlockSpec((tm, D_MODEL), lambda i, j: (i, 0)), pl.BlockSpec((tm, D_MODEL), lambda i, j: (i, 0)),
                   pl.BlockSpec((tm, th), lambda i, j: (i, j)), pl.BlockSpec((2, tm, th), lambda i, j: (0, i, j)),
                   _full((1, D_MODEL))],
        out_shape=[jax.ShapeDtypeStruct((T, D_MODEL), F32), jax.ShapeDtypeStruct((T, D_MODEL), BF16),
                   jax.ShapeDtypeStruct((T, D_FF), BF16), jax.ShapeDtypeStruct((2, T, D_FF), BF16),
                   jax.ShapeDtypeStruct((1, D_MODEL), F32)],
        scratch_shapes=[pltpu.VMEM((tm, D_MODEL), F32)],
        args=(dx2, x1, gu, g2, wguT, wd), comm=comm)


def head_fwd_bwd(x, target, fg, *, tm):
    T = x.shape[0]
    n_tiles = T // tm

    def body(x_ref, t_ref, fg_ref, loss_ref, dx_ref, dfg_ref, lacc):
        i = pl.program_id(0)

        @pl.when(i == 0)
        def _():
            lacc[...] = jnp.zeros(lacc.shape, F32)
            dfg_ref[...] = jnp.zeros(dfg_ref.shape, F32)

        x = x_ref[...]
        r = lax.rsqrt(jnp.mean(x * x, axis=-1, keepdims=True) + RMS_EPS)
        xn = x * r
        e = xn * fg_ref[...] - t_ref[...]
        lacc[...] += jnp.sum(e * e, axis=0, keepdims=True)
        dy = e * (1.0 / D_MODEL)
        dfg_ref[...] += jnp.sum(dy * xn, axis=0, keepdims=True)
        dx_ref[...] = _rms_bwd(dy, xn, r, fg_ref[...])

        @pl.when(i == n_tiles - 1)
        def _():
            loss_ref[...] = jnp.sum(lacc[...], axis=1, keepdims=True) * (0.5 / D_MODEL)

    row = pl.BlockSpec((tm, D_MODEL), lambda i: (i, 0))
    return pl.pallas_call(
        body, name="head_fwd_bwd", grid=(n_tiles,),
        in_specs=[row, row, _full((1, D_MODEL))],
        out_specs=[_full((1, 1)), row, _full((1, D_MODEL))],
        out_shape=[jax.ShapeDtypeStruct((1, 1), F32), jax.ShapeDtypeStruct((T, D_MODEL), F32),
                   jax.ShapeDtypeStruct((1, D_MODEL), F32)],
        scratch_shapes=[pltpu.VMEM((1, D_MODEL), F32)],
        compiler_params=_params(("arbitrary",)),
    )(x, target, fg)


def wgrad(a, b, *, tmo, tk, name, comm=None):
    G, T, M = a.shape
    N = b.shape[1]
    n_k = T // tk

    def body(a_ref, b_ref, o_ref, acc):
        k = pl.program_id(2)
        if n_k == 1:
            o_ref[0] = _dot_tn(a_ref[0].astype(BF16), b_ref[...].astype(BF16)).astype(BF16)
            return

        @pl.when(k == 0)
        def _():
            acc[...] = jnp.zeros(acc.shape, F32)

        acc[...] += _dot_tn(a_ref[0].astype(BF16), b_ref[...].astype(BF16))

        @pl.when(k == n_k - 1)
        def _():
            o_ref[0] = acc[...].astype(BF16)

    (out,), got = _hosted_call(
        body, name=name, grid=(G, M // tmo, n_k),
        in_specs=[pl.BlockSpec((1, tk, tmo), lambda g, m, k: (g, k, m)),
                  pl.BlockSpec((tk, N), lambda g, m, k: (k, 0))],
        out_specs=[pl.BlockSpec((1, tmo, N), lambda g, m, k: (g, m, 0))],
        out_shape=[jax.ShapeDtypeStruct((G, M, N), BF16)],
        scratch_shapes=[pltpu.VMEM((tmo, N), F32)],
        args=(a, b), comm=comm)
    return out, got


def sum_partials(gathered):
    n = len(gathered)

    def body(*refs):
        for in_ref, out_ref in zip(refs[:n], refs[n:]):
            total = in_ref[0]
            for d in range(1, N_DEV):
                total = total + in_ref[d]
            out_ref[...] = total

    vmem = pl.BlockSpec(memory_space=pltpu.VMEM)
    return pl.pallas_call(
        body, name="sum_partials", in_specs=[vmem] * n, out_specs=[vmem] * n,
        out_shape=[jax.ShapeDtypeStruct(g.shape[1:], F32) for g in gathered],
        compiler_params=pltpu.CompilerParams(vmem_limit_bytes=VMEM_LIMIT),
    )(*gathered)


_ADAM_C1 = 1.0 - ADAM_B1 ** ADAM_STEP
_ADAM_C2 = 1.0 - ADAM_B2 ** ADAM_STEP


def _adamw_math(w, g, m, v):
    m = ADAM_B1 * m + (1.0 - ADAM_B1) * g
    v = ADAM_B2 * v + (1.0 - ADAM_B2) * (g * g)
    m_hat = m / _ADAM_C1
    v_hat = v / _ADAM_C2
    delta = -ADAM_LR * (m_hat / (jnp.sqrt(v_hat) + ADAM_EPS) + ADAM_WD * w)
    return delta, m, v


def adamw_sharded(parts, w, m, v, *, tr, name, comm=None):
    _, R, C = parts[0].shape

    def body(p0_ref, p1_ref, w_ref, m_ref, v_ref, g_ref, d_ref, nm_ref, nv_ref):
        def update(p_ref):
            g = p_ref[0].astype(F32)
            for d in range(1, N_DEV):
                g = g + p_ref[d].astype(F32)
            delta, nm, nv = _adamw_math(w_ref[0], g, m_ref[0], v_ref[0])
            g_ref[0] = g
            d_ref[0] = delta
            nm_ref[0] = nm
            nv_ref[0] = nv

        @pl.when(pl.program_id(0) == 0)
        def _():
            update(p0_ref)

        @pl.when(pl.program_id(0) == 1)
        def _():
            update(p1_ref)

    n_i = R // tr
    p_specs = [pl.BlockSpec((N_DEV, tr, C), lambda l, i: (0, jnp.where(l == 0, i, n_i - 1), 0)),
               pl.BlockSpec((N_DEV, tr, C), lambda l, i: (0, jnp.where(l == 1, i, 0), 0))]
    o_spec = pl.BlockSpec((1, tr, C), lambda l, i: (l, i, 0))
    return _hosted_call(
        body, name=name, grid=(DEPTH, n_i),
        in_specs=p_specs + [o_spec, o_spec, o_spec], out_specs=[o_spec] * 4,
        out_shape=[jax.ShapeDtypeStruct(w.shape, F32)] * 4, scratch_shapes=[],
        args=(parts[0], parts[1], w, m, v), comm=comm)


def adamw_small(gs, ws, ms, vs):
    n = len(gs)

    def body(*refs):
        g_refs, w_refs, m_refs, v_refs = refs[:n], refs[n:2 * n], refs[2 * n:3 * n], refs[3 * n:4 * n]
        d_refs, nm_refs, nv_refs = refs[4 * n:5 * n], refs[5 * n:6 * n], refs[6 * n:]
        for k in range(n):
            delta, nm, nv = _adamw_math(w_refs[k][...], g_refs[k][...], m_refs[k][...], v_refs[k][...])
            d_refs[k][...] = delta
            nm_refs[k][...] = nm
            nv_refs[k][...] = nv

    vmem = pl.BlockSpec(memory_space=pltpu.VMEM)
    res = pl.pallas_call(
        body, name="adamw_small", in_specs=[vmem] * (4 * n), out_specs=[vmem] * (3 * n),
        out_shape=[jax.ShapeDtypeStruct(w.shape, F32) for w in ws] * 3,
        compiler_params=pltpu.CompilerParams(vmem_limit_bytes=VMEM_LIMIT),
    )(*gs, *ws, *ms, *vs)
    return res[:n], res[n:2 * n], res[2 * n:]


def _pack(arrays, row_multiple):
    flat = jnp.concatenate([a.reshape(-1) for a in arrays])
    rows = -(-flat.shape[0] // (LANES * row_multiple)) * row_multiple
    return jnp.pad(flat, (0, rows * LANES - flat.shape[0])).reshape(rows, LANES)


def _unpack(buf, shapes):
    flat = buf.reshape(-1)
    out, off = [], 0
    for s in shapes:
        n = math.prod(s)
        out.append(flat[off:off + n].reshape(s))
        off += n
    return out


def _block_diag(w_pool):
    G, d, _ = w_pool.shape
    eye = jnp.eye(G, dtype=w_pool.dtype)
    return (eye[:, None, :, None] * w_pool[:, :, None, :]).reshape(G * d, G * d)


def kernel(x, norm1_g, w_in, conv_w, conv_b, conv_ln_g, conv_ln_b, w_pw, sg_ln_g, sg_ln_b, w_s, b_s, w_pool, pool_scale, w_out, norm2_g, w_gate_up, w_down, final_g, loss_target, m_norm1_g, m_w_in, m_conv_w, m_conv_b, m_conv_ln_g, m_conv_ln_b, m_w_pw, m_sg_ln_g, m_sg_ln_b, m_w_s, m_b_s, m_w_pool, m_pool_scale, m_w_out, m_norm2_g, m_w_gate_up, m_w_down, m_final_g, v_norm1_g, v_w_in, v_conv_w, v_conv_b, v_conv_ln_g, v_conv_ln_b, v_w_pw, v_sg_ln_g, v_sg_ln_b, v_w_s, v_b_s, v_w_pool, v_pool_scale, v_w_out, v_norm2_g, v_w_gate_up, v_w_down, v_final_g):
    b_loc, seq, _ = x.shape
    T = b_loc * seq
    tm_mix = min(256, seq)
    tm_ffn_fwd = min(512, T)
    tm_ffn_bwd = min(256, T)
    tm_head = min(512, T)
    tk = min(2048, T)
    tk_f32 = min(1024, T)
    th = D_FF // 2
    cw = conv_w.shape[2]
    my_index = _index_of(_my_coords())

    xf = x.reshape(T, D_MODEL)
    tgt = loss_target.reshape(T, D_MODEL)

    mixer_shards = [[w_in[l].T.astype(BF16), w_out[l].astype(BF16), w_pw[l].astype(BF16)] for l in range(DEPTH)]
    ffn_shards = [[w_gate_up[l].T.astype(BF16), w_down[l].astype(BF16)] for l in range(DEPTH)]

    tril = jnp.tril(jnp.ones((CHUNK, CHUNK), dtype=bool))
    layers = []
    for l in range(DEPTH):
        wm = jnp.where(tril[None], w_s[l], 0.0).astype(BF16)
        layers.append(dict(
            g1=norm1_g[l][None], convb=conv_b[l][None], clng=conv_ln_g[l][None], clnb=conv_ln_b[l][None],
            slng=sg_ln_g[l][None], slnb=sg_ln_b[l][None], wm=wm, wmT=jnp.swapaxes(wm, 1, 2),
            bias=jnp.repeat(b_s[l].T, HEAD_DIM_B, axis=1), wbd=_block_diag(w_pool[l]).astype(BF16),
            pscale=pool_scale[l][None], g2=norm2_g[l][None]))

    def set_mixer_weights(l, g_in, g_out, g_pw):
        layers[l].update(winT=g_in.reshape(D_IN, D_MODEL), wout=g_out.reshape(D_MODEL, D_MODEL), wpw=g_pw.reshape(D_A, D_A))

    def set_ffn_weights(l, g_gu, g_d):
        layers[l].update(wguT=g_gu.reshape(2, D_FF, D_MODEL), wd=g_d.reshape(D_FF, D_MODEL))

    first = run_comm(Gather(mixer_shards[0] + [conv_w.reshape(DEPTH * CONV_WIDTH, cw).T]), name="gather_first")
    set_mixer_weights(0, *first[:3])
    convw_full = first[3].reshape(D_A, DEPTH * CONV_WIDTH).T.reshape(DEPTH, CONV_WIDTH, D_A)
    for l in range(DEPTH):
        layers[l]["convw"] = convw_full[l]

    saved = []
    cur = xf
    for l in range(DEPTH):
        w = layers[l]
        (z, ycv, p, mix, x1), got = mixer_fwd(
            cur, w["g1"], w["winT"], w["convw"], w["convb"], w["clng"], w["clnb"], w["wpw"], w["slng"], w["slnb"], w["wm"],
            w["bias"], w["wbd"], w["pscale"], w["wout"], seq=seq, tm=tm_mix,
            comm=Gather(ffn_shards[l] if l == 0 else ffn_shards[l][:1]))
        if l == 0:
            set_ffn_weights(l, *got)
        else:
            set_ffn_weights(l, got[0], early_wd)
        (x2, gu), got = ffn_fwd(x1, w["g2"], w["wguT"], w["wd"], tm=tm_ffn_fwd, th=th,
                                comm=Gather(mixer_shards[l + 1] + ffn_shards[l + 1][1:]) if l + 1 < DEPTH else None)
        if l + 1 < DEPTH:
            set_mixer_weights(l + 1, *got[:3])
            early_wd = got[3]
        saved.append((cur, z, ycv, p, mix, x1, gu))
        cur = x2
    loss_part, dx, dfg = head_fwd_bwd(cur, tgt, final_g[None], tm=tm_head)

    blocks = {"gu": (2 * D_FF // N_DEV, D_MODEL), "d": (D_FF // N_DEV, D_MODEL), "in": (D_IN // N_DEV, D_MODEL),
              "out": (D_MODEL // N_DEV, D_MODEL), "pw": (D_A // N_DEV, D_A)}
    by_device = lambda kind, g: g.reshape((N_DEV,) + blocks[kind])
    small = [None] * DEPTH
    parts = {}
    packs = [None] * DEPTH

    pending = None
    for l in reversed(range(DEPTH)):
        w = layers[l]
        x0, z, ycv, p, mix, x1, gu = saved[l]
        plan = Together([Exchange(pending[1]), Gather([pending[2]])]) if pending else None
        (dx1, h2, f, dgu, dg2), got = ffn_bwd(dx, x1, gu, w["g2"], w["wguT"], w["wd"], tm=tm_ffn_bwd, th=th, comm=plan)
        if pending:
            got_parts, (packs[l + 1],) = plan.split(got)
            parts.update(zip(pending[0], got_parts))
        gw_d, _ = wgrad(f[None], dx, tmo=th, tk=tk_f32, name="wgrad_down")
        last = l == 0
        gw_gu, got = wgrad(dgu, h2, tmo=th, tk=tk, name="wgrad_gate_up",
                           comm=Exchange([by_device("d", gw_d)]) if last else None)
        if last:
            parts[("d", l)], = got
        outs, got = mixer_bwd(
            dx1, x0, z, ycv, p, w["g1"], w["winT"], w["convw"], w["clng"], w["clnb"], w["wpw"], w["slng"], w["slnb"],
            w["wm"], w["wmT"], w["bias"], w["wbd"], w["pscale"], w["wout"], seq=seq, tm=tm_mix,
            comm=Exchange([by_device("gu", gw_gu)]))
        parts[("gu", l)], = got
        (dx, dz, h, s, dya, dg1, dconvw, dconvb, dclng, dclnb, dslng, dslnb, dwm, dbs, dwbd, dpscale) = outs
        small[l] = [dg1, dconvw, dconvb, dclng, dclnb, dslng, dslnb, dwm, dbs, dwbd, dpscale, dg2]
        if last:
            small[l] += [dfg, loss_part]
        pack = _pack(small[l], SUBLANES)
        gw_out, _ = wgrad(mix[None], dx1, tmo=D_MODEL, tk=tk_f32, name="wgrad_out")
        if last:
            gw_in, (packs[l],) = wgrad(dz[None], h, tmo=D_IN // 2, tk=tk, name="wgrad_in", comm=Gather([pack]))
        else:
            gw_in, got = wgrad(dz[None], h, tmo=D_IN // 2, tk=tk, name="wgrad_in", comm=Exchange([by_device("out", gw_out)]))
            parts[("out", l)], = got
        gw_pw, _ = wgrad(s[None], dya, tmo=D_A, tk=tk, name="wgrad_pw")
        if last:
            pending = ([("in", l), ("out", l), ("pw", l)],
                       [by_device("in", gw_in), by_device("out", gw_out), by_device("pw", gw_pw)])
        else:
            pending = ([("d", l), ("in", l), ("pw", l)],
                       [by_device("d", gw_d), by_device("in", gw_in), by_device("pw", gw_pw)], pack)
    grad_x = dx.reshape(x.shape)

    parts.update(zip(pending[0], run_comm(Exchange(pending[1]), name="exchange_last")))
    p_in, p_gu, p_d, p_out, p_pw = [[parts[(k, l)] for l in range(DEPTH)] for k in ("in", "gu", "d", "out", "pw")]

    sums = []
    for l, summed in enumerate(sum_partials(packs)):
        sums.append(_unpack(summed, [a.shape for a in small[l]]))
    loss = sums[0][-1][0, 0]
    dfg_sum = sums[0][-2]
    per_layer = 12
    sums = [a for l in range(DEPTH) for a in sums[l][:per_layer]]
    g_small = {k: [] for k in ("norm1_g", "conv_w", "conv_b", "conv_ln_g", "conv_ln_b", "sg_ln_g", "sg_ln_b", "w_s", "b_s",
                               "w_pool", "pool_scale", "norm2_g")}
    for l in range(DEPTH):
        dg1, dconvw, dconvb, dclng, dclnb, dslng, dslnb, dwm, dbs, dwbd, dpscale, dg2 = sums[per_layer * l:per_layer * (l + 1)]
        g_small["norm1_g"].append(dg1[0])
        g_small["conv_w"].append(lax.dynamic_slice_in_dim(dconvw, my_index * cw, cw, axis=1))
        g_small["conv_b"].append(dconvb[0])
        g_small["conv_ln_g"].append(dclng[0])
        g_small["conv_ln_b"].append(dclnb[0])
        g_small["sg_ln_g"].append(dslng[0])
        g_small["sg_ln_b"].append(dslnb[0])
        g_small["w_s"].append(dwm)
        g_small["b_s"].append(dbs[:, :N_HEADS_B].T)
        g_small["w_pool"].append(jnp.stack([dwbd[g * GROUP_DIM_C:(g + 1) * GROUP_DIM_C, g * GROUP_DIM_C:(g + 1) * GROUP_DIM_C]
                                            for g in range(len(POOL_WINDOWS))]))
        g_small["pool_scale"].append(dpscale[0])
        g_small["norm2_g"].append(dg2[0])
    g_small = {k: jnp.stack(v) for k, v in g_small.items()}
    g_small["final_g"] = dfg_sum[0]

    t = lambda a: jnp.swapaxes(a, 1, 2)
    g_w_in, d_w_in, nm_w_in, nv_w_in = map(t, adamw_sharded(p_in, t(w_in), t(m_w_in), t(v_w_in), tr=D_IN // N_DEV // 2,
                                                            name="adamw_w_in")[0])
    g_w_gu, d_w_gu, nm_w_gu, nv_w_gu = map(t, adamw_sharded(p_gu, t(w_gate_up), t(m_w_gate_up), t(v_w_gate_up),
                                                            tr=2 * D_FF // N_DEV // 4, name="adamw_w_gate_up")[0])
    g_w_d, d_w_d, nm_w_d, nv_w_d = adamw_sharded(p_d, w_down, m_w_down, v_w_down, tr=D_FF // N_DEV // 2,
                                                 name="adamw_w_down")[0]
    g_w_out, d_w_out, nm_w_out, nv_w_out = adamw_sharded(p_out, w_out, m_w_out, v_w_out, tr=D_MODEL // N_DEV,
                                                         name="adamw_w_out")[0]
    g_w_pw, d_w_pw, nm_w_pw, nv_w_pw = adamw_sharded(p_pw, w_pw, m_w_pw, v_w_pw, tr=D_A // N_DEV, name="adamw_w_pw")[0]

    small_names = ["norm1_g", "conv_w", "conv_b", "conv_ln_g", "conv_ln_b", "sg_ln_g", "sg_ln_b", "w_s", "b_s", "w_pool",
                   "pool_scale", "norm2_g", "final_g"]
    small_w = dict(norm1_g=norm1_g, conv_w=conv_w, conv_b=conv_b, conv_ln_g=conv_ln_g, conv_ln_b=conv_ln_b, sg_ln_g=sg_ln_g,
                   sg_ln_b=sg_ln_b, w_s=w_s, b_s=b_s, w_pool=w_pool, pool_scale=pool_scale, norm2_g=norm2_g, final_g=final_g)
    small_m = dict(norm1_g=m_norm1_g, conv_w=m_conv_w, conv_b=m_conv_b, conv_ln_g=m_conv_ln_g, conv_ln_b=m_conv_ln_b,
                   sg_ln_g=m_sg_ln_g, sg_ln_b=m_sg_ln_b, w_s=m_w_s, b_s=m_b_s, w_pool=m_w_pool, pool_scale=m_pool_scale,
                   norm2_g=m_norm2_g, final_g=m_final_g)
    small_v = dict(norm1_g=v_norm1_g, conv_w=v_conv_w, conv_b=v_conv_b, conv_ln_g=v_conv_ln_g, conv_ln_b=v_conv_ln_b,
                   sg_ln_g=v_sg_ln_g, sg_ln_b=v_sg_ln_b, w_s=v_w_s, b_s=v_b_s, w_pool=v_w_pool, pool_scale=v_pool_scale,
                   norm2_g=v_norm2_g, final_g=v_final_g)
    two_d = lambda a: a[None] if a.ndim == 1 else a
    d_s, nm_s, nv_s = adamw_small(*[[two_d(d[k]) for k in small_names] for d in (g_small, small_w, small_m, small_v)])
    d_small = {k: a.reshape(small_w[k].shape) for k, a in zip(small_names, d_s)}
    nm_small = {k: a.reshape(small_w[k].shape) for k, a in zip(small_names, nm_s)}
    nv_small = {k: a.reshape(small_w[k].shape) for k, a in zip(small_names, nv_s)}

    order = ["norm1_g", "w_in", "conv_w", "conv_b", "conv_ln_g", "conv_ln_b", "w_pw", "sg_ln_g", "sg_ln_b", "w_s", "b_s",
             "w_pool", "pool_scale", "w_out", "norm2_g", "w_gate_up", "w_down", "final_g"]
    grads = dict(g_small, w_in=g_w_in, w_pw=g_w_pw, w_out=g_w_out, w_gate_up=g_w_gu, w_down=g_w_d)
    deltas = dict(d_small, w_in=d_w_in, w_pw=d_w_pw, w_out=d_w_out, w_gate_up=d_w_gu, w_down=d_w_d)
    new_m = dict(nm_small, w_in=nm_w_in, w_pw=nm_w_pw, w_out=nm_w_out, w_gate_up=nm_w_gu, w_down=nm_w_d)
    new_v = dict(nv_small, w_in=nv_w_in, w_pw=nv_w_pw, w_out=nv_w_out, w_gate_up=nv_w_gu, w_down=nv_w_d)
    return (loss, grad_x, *[grads[k] for k in order], *[deltas[k] for k in order], *[new_m[k] for k in order],
            *[new_v[k] for k in order])
```

```python
import functools
import math

import jax
import jax.numpy as jnp
from jax import lax
from jax.experimental import pallas as pl
from jax.experimental.pallas import tpu as pltpu

F32 = jnp.float32
BF16 = jnp.bfloat16

D_MODEL = 1024
D_A = 384
D_B = 384
D_C = 256
D_IN = 2 * D_A + 2 * D_B + D_C
N_HEADS_B = 4
HEAD_DIM_B = 96
POOL_WINDOWS = (2, 4, 8, 16)
GROUP_DIM_C = 64
CONV_WIDTH = 31
CHUNK = 128
D_FF = 2816
RMS_EPS = 1e-6
LN_EPS = 1e-5
DEPTH = 2
N_DEV = 8

ADAM_LR = 0.001
ADAM_B1 = 0.9
ADAM_B2 = 0.999
ADAM_EPS = 1e-08
ADAM_WD = 0.01
ADAM_STEP = 10

LANES = 128
SUBLANES = 8

CONV_HALO = 32
POOL_HALO = 32
assert POOL_WINDOWS == (2, 4, 8, 16) and POOL_HALO == SUBLANES * len(POOL_WINDOWS)

VMEM_LIMIT = 56 * 1024 * 1024

MESH_ID = pl.DeviceIdType.MESH


def _dot(a, b):
    return jnp.dot(a, b, preferred_element_type=F32)


def _dot_nt(a, b):
    return lax.dot_general(a, b, (((1,), (1,)), ((), ())), preferred_element_type=F32)


def _dot_tn(a, b):
    return lax.dot_general(a, b, (((0,), (0,)), ((), ())), preferred_element_type=F32)


def _sigmoid(x):
    return 0.5 * jnp.tanh(0.5 * x) + 0.5


def _shifted_taps(buf, first_row, n_shifts, tm):
    for phase in range(min(SUBLANES, n_shifts)):
        shifts = list(range(phase, n_shifts, SUBLANES))
        span = buf[first_row + phase:first_row + shifts[-1] + tm, :]
        for s in shifts:
            yield s, span[s - phase:s - phase + tm, :]


def _window_sums_back(x_ref, bufs, n_rows):
    out, src, w = [], x_ref, 1
    for l in range(len(POOL_WINDOWS)):
        lo = SUBLANES * (l + 1)
        cur = src[lo:n_rows, :] + src[lo - w:n_rows - w, :]
        out.append(cur)
        if l < len(bufs):
            bufs[l][lo:n_rows, :] = cur
            src = bufs[l]
        w *= 2
    return out


def _window_sums_ahead(x_ref, bufs, n_rows):
    out, src, w = [], x_ref, 1
    for l in range(len(POOL_WINDOWS)):
        hi = n_rows - SUBLANES * (l + 1)
        cur = src[0:hi, :] + src[w:hi + w, :]
        out.append(cur)
        if l < len(bufs):
            bufs[l][0:hi, :] = cur
            src = bufs[l]
        w *= 2
    return out


_GELU_C = math.sqrt(2.0 / math.pi)


def _gelu_and_grad(x):
    x2 = x * x
    inner = _GELU_C * (x + 0.044715 * x2 * x)
    t = jnp.tanh(inner)
    g = 0.5 * x * (1.0 + t)
    dg = 0.5 * (1.0 + t) + 0.5 * x * (1.0 - t * t) * _GELU_C * (1.0 + 3.0 * 0.044715 * x2)
    return g, dg


def _ln_stats(x):
    mu = jnp.mean(x, axis=-1, keepdims=True)
    xc = x - mu
    var = jnp.mean(xc * xc, axis=-1, keepdims=True)
    rstd = lax.rsqrt(var + LN_EPS)
    return xc * rstd, rstd


def _ln_bwd(dy, xhat, rstd, g):
    dxhat = dy * g
    return rstd * (dxhat - jnp.mean(dxhat, axis=-1, keepdims=True)
                   - xhat * jnp.mean(dxhat * xhat, axis=-1, keepdims=True))


def _rms_bwd(dh, xn, r, g):
    dxn = dh * g
    return r * (dxn - xn * jnp.mean(dxn * xn, axis=-1, keepdims=True))


def _head_masks(width):
    lane = lax.broadcasted_iota(jnp.int32, (1, width), 1)
    return [(lane >= h * HEAD_DIM_B) & (lane < (h + 1) * HEAD_DIM_B) for h in range(N_HEADS_B)]


def _pool_select(vals, width):
    lane = lax.broadcasted_iota(jnp.int32, (1, width), 1)
    out = vals[-1]
    for g in range(len(vals) - 2, -1, -1):
        out = jnp.where(lane < (g + 1) * GROUP_DIM_C, vals[g], out)
    return out


def _pool_counts(pos):
    return _pool_select([jnp.minimum(pos + 1.0, float(w)) for w in POOL_WINDOWS], D_C)


def _full(shape):
    n = len(shape)
    return pl.BlockSpec(shape, lambda *_: (0,) * n)


def _params(sem):
    return pltpu.CompilerParams(dimension_semantics=sem, vmem_limit_bytes=VMEM_LIMIT)


def _my_coords():
    return lax.axis_index("x"), lax.axis_index("y"), lax.axis_index("c")


def _peer(me, rel):
    x, y, c = me
    bx, by, bc = (rel >> 2) & 1, (rel >> 1) & 1, rel & 1
    return (1 - x if bx else x, 1 - y if by else y, 1 - c if bc else c)


def _index_of(dev):
    return 4 * dev[0] + 2 * dev[1] + dev[2]


SIBLING = 1
OTHER_CHIPS = (2, 4, 6)


class Gather:
    def __init__(self, shards):
        n = len(shards)
        self.inputs = list(shards)
        self.out_shape = [jax.ShapeDtypeStruct((N_DEV,) + s.shape, s.dtype) for s in shards]
        self.scratch = [pltpu.SemaphoreType.DMA((N_DEV - 1, n)), pltpu.SemaphoreType.DMA((N_DEV - 1, n)),
                        pltpu.SemaphoreType.DMA((n,))]

    @staticmethod
    def _copy(src, dst, sems, rel, k, to):
        return pltpu.make_async_remote_copy(src_ref=src, dst_ref=dst, send_sem=sems[0].at[rel - 1, k],
                                            recv_sem=sems[1].at[rel - 1, k], device_id=to, device_id_type=MESH_ID)

    def start(self, ins, outs, sems):
        me = _my_coords()
        mine = _index_of(me)
        for k, src in enumerate(ins):
            pltpu.make_async_copy(src, outs[k].at[mine], sems[2].at[k]).start()
            for rel in (SIBLING,) + OTHER_CHIPS:
                self._copy(src, outs[k].at[mine], sems, rel, k, _peer(me, rel)).start()

    def finish(self, ins, outs, sems):
        me = _my_coords()
        mine = _index_of(me)
        sibling = _peer(me, SIBLING)
        for rel in OTHER_CHIPS:
            slot = _index_of(_peer(me, rel))
            for k in range(len(ins)):
                self._copy(ins[k], outs[k].at[slot], sems, rel, k, sibling).wait_recv()
                self._copy(outs[k].at[slot], outs[k].at[slot], sems, rel + 1, k, sibling).start()
        for rel in (SIBLING,) + tuple(r + 1 for r in OTHER_CHIPS):
            slot = _index_of(_peer(me, rel))
            for k in range(len(ins)):
                self._copy(ins[k], outs[k].at[slot], sems, rel, k, sibling).wait_recv()
        for rel in range(1, N_DEV):
            for k in range(len(ins)):
                self._copy(ins[k], outs[k].at[mine], sems, rel, k, sibling).wait_send()
        for k, src in enumerate(ins):
            pltpu.make_async_copy(src, outs[k].at[mine], sems[2].at[k]).wait()


class Exchange:
    def __init__(self, fulls):
        n = len(fulls)
        self.inputs = list(fulls)
        self.out_shape = [jax.ShapeDtypeStruct(f.shape, f.dtype) for f in fulls]
        self.scratch = [pltpu.SemaphoreType.DMA((N_DEV - 1, n)), pltpu.SemaphoreType.DMA((N_DEV - 1, n)),
                        pltpu.SemaphoreType.DMA((n,))]

    def start(self, ins, outs, sems):
        me = _my_coords()
        mine = _index_of(me)
        for k, src in enumerate(ins):
            pltpu.make_async_copy(src.at[mine], outs[k].at[mine], sems[2].at[k]).start()
            for rel in range(1, N_DEV):
                to = _peer(me, rel)
                Gather._copy(src.at[_index_of(to)], outs[k].at[mine], sems, rel, k, to).start()

    def finish(self, ins, outs, sems):
        me = _my_coords()
        mine = _index_of(me)
        for rel in range(1, N_DEV):
            frm = _peer(me, rel)
            for k, src in enumerate(ins):
                Gather._copy(src.at[mine], outs[k].at[_index_of(frm)], sems, rel, k, frm).wait_recv()
        for rel in range(1, N_DEV):
            for k, src in enumerate(ins):
                Gather._copy(src.at[mine], outs[k].at[mine], sems, rel, k, _peer(me, rel)).wait_send()
        for k, src in enumerate(ins):
            pltpu.make_async_copy(src.at[mine], outs[k].at[mine], sems[2].at[k]).wait()


class Together:
    def __init__(self, plans):
        self.plans = list(plans)
        self.inputs = [a for p in self.plans for a in p.inputs]
        self.out_shape = [s for p in self.plans for s in p.out_shape]
        self.scratch = [s for p in self.plans for s in p.scratch]

    def _each(self, ins, outs, sems):
        i = o = s = 0
        for p in self.plans:
            ni, no, ns = len(p.inputs), len(p.out_shape), len(p.scratch)
            yield p, ins[i:i + ni], outs[o:o + no], sems[s:s + ns]
            i, o, s = i + ni, o + no, s + ns

    def start(self, ins, outs, sems):
        for p, pi, po, ps in self._each(ins, outs, sems):
            p.start(pi, po, ps)

    def finish(self, ins, outs, sems):
        for p, pi, po, ps in self._each(ins, outs, sems):
            p.finish(pi, po, ps)

    def split(self, results):
        out, o = [], 0
        for p in self.plans:
            out.append(results[o:o + len(p.out_shape)])
            o += len(p.out_shape)
        return out


def _hosted_call(body, *, name, grid, in_specs, out_specs, out_shape, scratch_shapes, args, comm=None):
    sem = ("arbitrary",) * len(grid)
    if comm is None:
        res = pl.pallas_call(body, name=name, grid=grid, in_specs=in_specs, out_specs=out_specs, out_shape=out_shape,
                             scratch_shapes=scratch_shapes, compiler_params=_params(sem))(*args)
        return list(res), []
    n_in, n_out, n_scr = len(in_specs), len(out_specs), len(scratch_shapes)
    n_cin, n_cout = len(comm.inputs), len(comm.out_shape)

    def hosted(*refs):
        ins, refs = refs[:n_in], refs[n_in:]
        cins, refs = refs[:n_cin], refs[n_cin:]
        outs, refs = refs[:n_out], refs[n_out:]
        couts, refs = refs[:n_cout], refs[n_cout:]
        scr, csems = refs[:n_scr], refs[n_scr:]
        ids = [pl.program_id(a) for a in range(len(grid))]
        first = functools.reduce(lambda a, b: a & b, [i == 0 for i in ids])
        last = functools.reduce(lambda a, b: a & b, [i == g - 1 for i, g in zip(ids, grid)])

        @pl.when(first)
        def _():
            comm.start(cins, couts, csems)

        body(*ins, *outs, *scr)

        @pl.when(last)
        def _():
            comm.finish(cins, couts, csems)

    any_spec = pl.BlockSpec(memory_space=pl.ANY)
    res = pl.pallas_call(
        hosted, name=name, grid=grid, in_specs=list(in_specs) + [any_spec] * n_cin,
        out_specs=list(out_specs) + [any_spec] * n_cout, out_shape=list(out_shape) + comm.out_shape,
        scratch_shapes=list(scratch_shapes) + comm.scratch,
        compiler_params=pltpu.CompilerParams(dimension_semantics=sem, vmem_limit_bytes=VMEM_LIMIT, has_side_effects=True),
    )(*args, *comm.inputs)
    return list(res[:n_out]), list(res[n_out:])


def run_comm(comm, *, name):
    n_cin, n_cout = len(comm.inputs), len(comm.out_shape)

    def body(*refs):
        cins, couts, csems = refs[:n_cin], refs[n_cin:n_cin + n_cout], refs[n_cin + n_cout:]
        comm.start(cins, couts, csems)
        comm.finish(cins, couts, csems)

    any_spec = pl.BlockSpec(memory_space=pl.ANY)
    return pl.pallas_call(
        body, name=name, in_specs=[any_spec] * n_cin, out_specs=[any_spec] * n_cout, out_shape=comm.out_shape,
        scratch_shapes=comm.scratch, compiler_params=pltpu.CompilerParams(has_side_effects=True),
    )(*comm.inputs)


def mixer_fwd(x, g1, winT, convw, convb, clng, clnb, wpw, slng, slnb, wm, bias, wbd, pscale, wout, *, seq, tm,
              comm=None):
    T = x.shape[0]
    tiles_per_seq = seq // tm
    n_chunks = tm // CHUNK

    def body(x_ref, g1_ref, winT_ref, convw_ref, convb_ref, clng_ref, clnb_ref, wpw_ref, slng_ref, slnb_ref,
             wm_ref, bias_ref, wbd_ref, pscale_ref, wout_ref,
             z_ref, ycv_ref, p_ref, mix_ref, x1_ref, ybuf, zcbuf, *pbufs):
        i = pl.program_id(0)
        tile_in_seq = i % tiles_per_seq

        @pl.when(tile_in_seq == 0)
        def _():
            ybuf[0:CONV_HALO, :] = jnp.zeros((CONV_HALO, D_A), F32)
            zcbuf[0:POOL_HALO, :] = jnp.zeros((POOL_HALO, D_C), F32)

        x = x_ref[...]
        r = lax.rsqrt(jnp.mean(x * x, axis=-1, keepdims=True) + RMS_EPS)
        h = (x * r * g1_ref[...]).astype(BF16)
        z = _dot_nt(h, winT_ref[...])
        z_ref[...] = z

        y = z[:, 0:D_A] * _sigmoid(z[:, D_A:2 * D_A])
        ybuf[CONV_HALO:CONV_HALO + tm, :] = y
        acc = jnp.zeros((tm, D_A), F32) + convb_ref[...]
        for k, rows in _shifted_taps(ybuf, CONV_HALO - (CONV_WIDTH - 1), CONV_WIDTH, tm):
            acc = acc + convw_ref[k:k + 1, :] * rows
        ybuf[0:CONV_HALO, :] = ybuf[tm:tm + CONV_HALO, :]
        ycv_ref[...] = acc
        xhat, _ = _ln_stats(acc)
        ln = xhat * clng_ref[...] + clnb_ref[...]
        s = ln * _sigmoid(ln)
        ya = _dot(s.astype(BF16), wpw_ref[...])

        gb, _ = _gelu_and_grad(z[:, 2 * D_A:2 * D_A + 2 * D_B])
        u = gb[:, 0:D_B]
        vhat, _ = _ln_stats(gb[:, D_B:2 * D_B])
        vn = vhat * slng_ref[...] + slnb_ref[...]
        masks = _head_masks(D_B)
        yb_parts = []
        for c in range(n_chunks):
            vn_c = vn[c * CHUNK:(c + 1) * CHUNK, :]
            sg = bias_ref[...]
            for hh in range(N_HEADS_B):
                sg = sg + _dot(wm_ref[hh], jnp.where(masks[hh], vn_c, 0.0).astype(BF16))
            yb_parts.append(u[c * CHUNK:(c + 1) * CHUNK, :] * sg)
        yb = jnp.concatenate(yb_parts, axis=0) if n_chunks > 1 else yb_parts[0]

        zc = z[:, 2 * D_A + 2 * D_B:D_IN]
        zcbuf[POOL_HALO:POOL_HALO + tm, :] = zc
        sums = [v[POOL_HALO - SUBLANES * (l + 1):POOL_HALO - SUBLANES * (l + 1) + tm, :]
                for l, v in enumerate(_window_sums_back(zcbuf, pbufs, POOL_HALO + tm))]
        zcbuf[0:POOL_HALO, :] = zcbuf[tm:tm + POOL_HALO, :]
        pos = (tile_in_seq * tm + lax.broadcasted_iota(jnp.int32, (tm, 1), 0)).astype(F32)
        p = _pool_select(sums, D_C) / _pool_counts(pos) - zc
        p_ref[...] = p
        yc = _dot(p.astype(BF16), wbd_ref[...]) * pscale_ref[...]

        mix = jnp.concatenate([ya, yb, yc], axis=1).astype(BF16)
        mix_ref[...] = mix
        x1_ref[...] = x + _dot(mix, wout_ref[...])

    row = lambda w: pl.BlockSpec((tm, w), lambda i: (i, 0))
    return _hosted_call(
        body, name="mixer_fwd", grid=(T // tm,),
        in_specs=[row(D_MODEL), _full((1, D_MODEL)), _full((D_IN, D_MODEL)), _full((CONV_WIDTH, D_A)),
                  _full((1, D_A)), _full((1, D_A)), _full((1, D_A)), _full((D_A, D_A)), _full((1, D_B)), _full((1, D_B)),
                  _full((N_HEADS_B, CHUNK, CHUNK)), _full((CHUNK, D_B)), _full((D_C, D_C)), _full((1, D_C)),
                  _full((D_MODEL, D_MODEL))],
        out_specs=[row(D_IN), row(D_A), row(D_C), row(D_MODEL), row(D_MODEL)],
        out_shape=[jax.ShapeDtypeStruct((T, D_IN), F32), jax.ShapeDtypeStruct((T, D_A), F32),
                   jax.ShapeDtypeStruct((T, D_C), F32), jax.ShapeDtypeStruct((T, D_MODEL), BF16),
                   jax.ShapeDtypeStruct((T, D_MODEL), F32)],
        scratch_shapes=[pltpu.VMEM((CONV_HALO + tm, D_A), F32)]
        + [pltpu.VMEM((POOL_HALO + tm, D_C), F32)] * len(POOL_WINDOWS),
        args=(x, g1, winT, convw, convb, clng, clnb, wpw, slng, slnb, wm, bias, wbd, pscale, wout), comm=comm)


def mixer_bwd(dx1, x, z, ycv, p, g1, winT, convw, clng, clnb, wpw, slng, slnb, wm, wmT, bias, wbd, pscale, wout,
              *, seq, tm, comm=None):
    T = x.shape[0]
    tiles_per_seq = seq // tm
    n_tiles = T // tm
    n_chunks = tm // CHUNK

    def body(dx1_ref, x_ref, z_ref, ycv_ref, p_ref, g1_ref, winT_ref, convw_ref, clng_ref, clnb_ref, wpw_ref,
             slng_ref, slnb_ref, wm_ref, wmT_ref, bias_ref, wbd_ref, pscale_ref, wout_ref,
             dx_ref, dz_ref, h_ref, s_ref, dya_ref,
             dg1_ref, dconvw_ref, dconvb_ref, dclng_ref, dclnb_ref, dslng_ref, dslnb_ref, dwm_ref, dbs_ref,
             dwbd_ref, dpscale_ref, dycbuf, dpcbuf, *pbufs):
        i = pl.program_id(0)
        tile_in_seq = (n_tiles - 1 - i) % tiles_per_seq

        @pl.when(i == 0)
        def _():
            for ref in (dg1_ref, dconvw_ref, dconvb_ref, dclng_ref, dclnb_ref, dslng_ref, dslnb_ref, dwm_ref,
                        dbs_ref, dwbd_ref, dpscale_ref):
                ref[...] = jnp.zeros(ref.shape, F32)

        @pl.when(tile_in_seq == tiles_per_seq - 1)
        def _():
            dycbuf[tm:tm + CONV_HALO, :] = jnp.zeros((CONV_HALO, D_A), F32)
            dpcbuf[tm:tm + POOL_HALO, :] = jnp.zeros((POOL_HALO, D_C), F32)

        dx1 = dx1_ref[...]
        z = z_ref[...]
        dmix = _dot_nt(dx1.astype(BF16), wout_ref[...])
        dya = dmix[:, 0:D_A]
        dyb = dmix[:, D_A:D_A + D_B]
        dyc = dmix[:, D_A + D_B:D_MODEL]

        p = p_ref[...]
        pb = p.astype(BF16)
        q = _dot(pb, wbd_ref[...])
        dpscale_ref[...] += jnp.sum(dyc * q, axis=0, keepdims=True)
        dq = (dyc * pscale_ref[...]).astype(BF16)
        dwbd_ref[...] += _dot_tn(pb, dq)
        dp = _dot_nt(dq, wbd_ref[...])
        pos = (tile_in_seq * tm + lax.broadcasted_iota(jnp.int32, (tm, 1), 0)).astype(F32)
        dpc = dp / _pool_counts(pos)
        dpcbuf[0:tm, :] = dpc
        sums = [v[0:tm, :] for v in _window_sums_ahead(dpcbuf, pbufs, tm + POOL_HALO)]
        dpcbuf[tm:tm + POOL_HALO, :] = dpcbuf[0:POOL_HALO, :]
        dzc = _pool_select(sums, D_C) - dp

        dya_b = dya.astype(BF16)
        dya_ref[...] = dya_b
        ds = _dot_nt(dya_b, wpw_ref[...])
        xhat, rstd = _ln_stats(ycv_ref[...])
        ln = xhat * clng_ref[...] + clnb_ref[...]
        sg = _sigmoid(ln)
        s_ref[...] = (ln * sg).astype(BF16)
        dln = ds * (sg * (1.0 + ln * (1.0 - sg)))
        dclng_ref[...] += jnp.sum(dln * xhat, axis=0, keepdims=True)
        dclnb_ref[...] += jnp.sum(dln, axis=0, keepdims=True)
        dycv = _ln_bwd(dln, xhat, rstd, clng_ref[...])
        dconvb_ref[...] += jnp.sum(dycv, axis=0, keepdims=True)
        a = z[:, 0:D_A]
        sgate = _sigmoid(z[:, D_A:2 * D_A])
        y = a * sgate
        dycbuf[0:tm, :] = dycv
        dy = jnp.zeros((tm, D_A), F32)
        for d, sh in _shifted_taps(dycbuf, 0, CONV_WIDTH, tm):
            k = CONV_WIDTH - 1 - d
            dy = dy + convw_ref[k:k + 1, :] * sh
            dconvw_ref[k:k + 1, :] += jnp.sum(y * sh, axis=0, keepdims=True)
        dycbuf[tm:tm + CONV_HALO, :] = dycbuf[0:CONV_HALO, :]
        da = dy * sgate
        dgate = dy * a * sgate * (1.0 - sgate)

        gb, dgb = _gelu_and_grad(z[:, 2 * D_A:2 * D_A + 2 * D_B])
        u = gb[:, 0:D_B]
        vhat, vrstd = _ln_stats(gb[:, D_B:2 * D_B])
        vn = vhat * slng_ref[...] + slnb_ref[...]
        masks = _head_masks(D_B)
        tril = (lax.broadcasted_iota(jnp.int32, (CHUNK, CHUNK), 0)
                >= lax.broadcasted_iota(jnp.int32, (CHUNK, CHUNK), 1))
        lane128 = lax.broadcasted_iota(jnp.int32, (1, CHUNK), 1)
        du_parts, dvn_parts = [], []
        for c in range(n_chunks):
            rows = slice(c * CHUNK, (c + 1) * CHUNK)
            vn_c = vn[rows, :]
            vh = [jnp.where(masks[hh], vn_c, 0.0).astype(BF16) for hh in range(N_HEADS_B)]
            sgc = bias_ref[...]
            for hh in range(N_HEADS_B):
                sgc = sgc + _dot(wm_ref[hh], vh[hh])
            dyb_c = dyb[rows, :]
            du_parts.append(dyb_c * sgc)
            dsg = dyb_c * u[rows, :]
            dvn_c = jnp.zeros((CHUNK, D_B), F32)
            dbs = jnp.zeros((CHUNK, CHUNK), F32)
            for hh in range(N_HEADS_B):
                dsg_h = jnp.where(masks[hh], dsg, 0.0)
                dsg_hb = dsg_h.astype(BF16)
                dwm_ref[hh] += jnp.where(tril, _dot_nt(dsg_hb, vh[hh]), 0.0)
                dvn_c = dvn_c + _dot(wmT_ref[hh], dsg_hb)
                dbs = dbs + jnp.where(lane128 == hh, jnp.sum(dsg_h, axis=1, keepdims=True), 0.0)
            dbs_ref[...] += dbs
            dvn_parts.append(dvn_c)
        du = jnp.concatenate(du_parts, axis=0) if n_chunks > 1 else du_parts[0]
        dvn = jnp.concatenate(dvn_parts, axis=0) if n_chunks > 1 else dvn_parts[0]
        dslng_ref[...] += jnp.sum(dvn * vhat, axis=0, keepdims=True)
        dslnb_ref[...] += jnp.sum(dvn, axis=0, keepdims=True)
        dv = _ln_bwd(dvn, vhat, vrstd, slng_ref[...])
        dzb = jnp.concatenate([du, dv], axis=1) * dgb

        dz = jnp.concatenate([da, dgate, dzb, dzc], axis=1).astype(BF16)
        dz_ref[...] = dz
        dh = _dot(dz, winT_ref[...])
        x = x_ref[...]
        r = lax.rsqrt(jnp.mean(x * x, axis=-1, keepdims=True) + RMS_EPS)
        xn = x * r
        h_ref[...] = (xn * g1_ref[...]).astype(BF16)
        dg1_ref[...] += jnp.sum(dh * xn, axis=0, keepdims=True)
        dx_ref[...] = dx1 + _rms_bwd(dh, xn, r, g1_ref[...])

    row = lambda w: pl.BlockSpec((tm, w), lambda i: (n_tiles - 1 - i, 0))
    acc_shapes = [(1, D_MODEL), (CONV_WIDTH, D_A), (1, D_A), (1, D_A), (1, D_A), (1, D_B), (1, D_B),
                  (N_HEADS_B, CHUNK, CHUNK), (CHUNK, CHUNK), (D_C, D_C), (1, D_C)]
    return _hosted_call(
        body, name="mixer_bwd", grid=(n_tiles,),
        in_specs=[row(D_MODEL), row(D_MODEL), row(D_IN), row(D_A), row(D_C),
                  _full((1, D_MODEL)), _full((D_IN, D_MODEL)), _full((CONV_WIDTH, D_A)), _full((1, D_A)), _full((1, D_A)),
                  _full((D_A, D_A)), _full((1, D_B)), _full((1, D_B)), _full((N_HEADS_B, CHUNK, CHUNK)),
                  _full((N_HEADS_B, CHUNK, CHUNK)), _full((CHUNK, D_B)), _full((D_C, D_C)), _full((1, D_C)),
                  _full((D_MODEL, D_MODEL))],
        out_specs=[row(D_MODEL), row(D_IN), row(D_MODEL), row(D_A), row(D_A)] + [_full(s) for s in acc_shapes],
        out_shape=[jax.ShapeDtypeStruct((T, D_MODEL), F32), jax.ShapeDtypeStruct((T, D_IN), BF16),
                   jax.ShapeDtypeStruct((T, D_MODEL), BF16), jax.ShapeDtypeStruct((T, D_A), BF16),
                   jax.ShapeDtypeStruct((T, D_A), BF16)] + [jax.ShapeDtypeStruct(s, F32) for s in acc_shapes],
        scratch_shapes=[pltpu.VMEM((tm + CONV_HALO, D_A), F32)]
        + [pltpu.VMEM((tm + POOL_HALO, D_C), F32)] * len(POOL_WINDOWS),
        args=(dx1, x, z, ycv, p, g1, winT, convw, clng, clnb, wpw, slng, slnb, wm, wmT, bias, wbd, pscale, wout),
        comm=comm)


def ffn_fwd(x1, g2, wguT, wd, *, tm, th, comm=None):
    T = x1.shape[0]
    n_h = D_FF // th

    def body(x1_ref, g2_ref, wgu_ref, wd_ref, x2_ref, fac_ref, f_ref, h2_buf, acc):
        j = pl.program_id(1)

        @pl.when(j == 0)
        def _():
            x = x1_ref[...]
            r = lax.rsqrt(jnp.mean(x * x, axis=-1, keepdims=True) + RMS_EPS)
            h2_buf[...] = (x * r * g2_ref[...]).astype(BF16)
            acc[...] = x

        h2 = h2_buf[...]
        rows = pl.ds(pl.multiple_of(j * th, th), th)
        g = _dot_nt(h2, wgu_ref[0, rows, :])
        u = _dot_nt(h2, wgu_ref[1, rows, :])
        sg = _sigmoid(g)
        silu = g * sg
        fac_ref[0] = silu.astype(BF16)
        fac_ref[1] = (u * (sg * (1.0 + g * (1.0 - sg)))).astype(BF16)
        f = (silu * u).astype(BF16)
        f_ref[...] = f
        acc[...] += _dot(f, wd_ref[rows, :])

        @pl.when(j == n_h - 1)
        def _():
            x2_ref[...] = acc[...]

    return _hosted_call(
        body, name="ffn_fwd", grid=(T // tm, n_h),
        in_specs=[pl.BlockSpec((tm, D_MODEL), lambda i, j: (i, 0)), _full((1, D_MODEL)),
                  _full((2, D_FF, D_MODEL)), _full((D_FF, D_MODEL))],
        out_specs=[pl.BlockSpec((tm, D_MODEL), lambda i, j: (i, 0)),
                   pl.BlockSpec((2, tm, th), lambda i, j: (0, i, j)), pl.BlockSpec((tm, th), lambda i, j: (i, j))],
        out_shape=[jax.ShapeDtypeStruct((T, D_MODEL), F32), jax.ShapeDtypeStruct((2, T, D_FF), BF16),
                   jax.ShapeDtypeStruct((T, D_FF), BF16)],
        scratch_shapes=[pltpu.VMEM((tm, D_MODEL), BF16), pltpu.VMEM((tm, D_MODEL), F32)],
        args=(x1, g2, wguT, wd), comm=comm)


def ffn_bwd(dx2, x1, fac, g2, wguT, wd, *, tm, th, comm=None):
    T = x1.shape[0]
    n_h = D_FF // th

    def body(dx2_ref, x1_ref, fac_ref, g2_ref, wgu_ref, wd_ref, dx1_ref, h2_ref, dgu_ref, dg2_ref, acc):
        i = pl.program_id(0)
        j = pl.program_id(1)

        @pl.when((i == 0) & (j == 0))
        def _():
            dg2_ref[...] = jnp.zeros(dg2_ref.shape, F32)

        dx2 = dx2_ref[...]
        rows = pl.ds(pl.multiple_of(j * th, th), th)
        df = _dot_nt(dx2.astype(BF16), wd_ref[rows, :])
        dup = (df * fac_ref[0].astype(F32)).astype(BF16)
        dgate = (df * fac_ref[1].astype(F32)).astype(BF16)
        dgu_ref[0] = dgate
        dgu_ref[1] = dup

        @pl.when(j == 0)
        def _():
            acc[...] = jnp.zeros(acc.shape, F32)

        acc[...] += _dot(dgate, wgu_ref[0, rows, :]) + _dot(dup, wgu_ref[1, rows, :])

        @pl.when(j == n_h - 1)
        def _():
            x = x1_ref[...]
            r = lax.rsqrt(jnp.mean(x * x, axis=-1, keepdims=True) + RMS_EPS)
            xn = x * r
            dh = acc[...]
            h2_ref[...] = (xn * g2_ref[...]).astype(BF16)
            dg2_ref[...] += jnp.sum(dh * xn, axis=0, keepdims=True)
            dx1_ref[...] = dx2 + _rms_bwd(dh, xn, r, g2_ref[...])

    return _hosted_call(
        body, name="ffn_bwd", grid=(T // tm, n_h),
        in_specs=[pl.BlockSpec((tm, D_MODEL), lambda i, j: (i, 0)), pl.BlockSpec((tm, D_MODEL), lambda i, j: (i, 0)),
                  pl.BlockSpec((2, tm, th), lambda i, j: (0, i, j)), _full((1, D_MODEL)),
                  _full((2, D_FF, D_MODEL)), _full((D_FF, D_MODEL))],
        out_specs=[pl.BlockSpec((tm, D_MODEL), lambda i, j: (i, 0)), pl.BlockSpec((tm, D_MODEL), lambda i, j: (i, 0)),
                   pl.BlockSpec((2, tm, th), lambda i, j: (0, i, j)), _full((1, D_MODEL))],
        out_shape=[jax.ShapeDtypeStruct((T, D_MODEL), F32), jax.ShapeDtypeStruct((T, D_MODEL), BF16),
                   jax.ShapeDtypeStruct((2, T, D_FF), BF16), jax.ShapeDtypeStruct((1, D_MODEL), F32)],
        scratch_shapes=[pltpu.VMEM((tm, D_MODEL), F32)],
        args=(dx2, x1, fac, g2, wguT, wd), comm=comm)


def head_fwd_bwd(x, target, fg, *, tm):
    T = x.shape[0]
    n_tiles = T // tm

    def body(x_ref, t_ref, fg_ref, loss_ref, dx_ref, dfg_ref, lacc):
        i = pl.program_id(0)

        @pl.when(i == 0)
        def _():
            lacc[...] = jnp.zeros(lacc.shape, F32)
            dfg_ref[...] = jnp.zeros(dfg_ref.shape, F32)

        x = x_ref[...]
        r = lax.rsqrt(jnp.mean(x * x, axis=-1, keepdims=True) + RMS_EPS)
        xn = x * r
        e = xn * fg_ref[...] - t_ref[...]
        lacc[...] += jnp.sum(e * e, axis=0, keepdims=True)
        dy = e * (1.0 / D_MODEL)
        dfg_ref[...] += jnp.sum(dy * xn, axis=0, keepdims=True)
        dx_ref[...] = _rms_bwd(dy, xn, r, fg_ref[...])

        @pl.when(i == n_tiles - 1)
        def _():
            loss_ref[...] = jnp.sum(lacc[...], axis=1, keepdims=True) * (0.5 / D_MODEL)

    row = pl.BlockSpec((tm, D_MODEL), lambda i: (i, 0))
    return pl.pallas_call(
        body, name="head_fwd_bwd", grid=(n_tiles,),
        in_specs=[row, row, _full((1, D_MODEL))],
        out_specs=[_full((1, 1)), row, _full((1, D_MODEL))],
        out_shape=[jax.ShapeDtypeStruct((1, 1), F32), jax.ShapeDtypeStruct((T, D_MODEL), F32),
                   jax.ShapeDtypeStruct((1, D_MODEL), F32)],
        scratch_shapes=[pltpu.VMEM((1, D_MODEL), F32)],
        compiler_params=_params(("arbitrary",)),
    )(x, target, fg)


def wgrad(a, b, *, tmo, tk, name, comm=None):
    G, T, M = a.shape
    N = b.shape[1]
    n_k = T // tk

    def body(a_ref, b_ref, o_ref, acc):
        k = pl.program_id(2)
        if n_k == 1:
            o_ref[0] = _dot_tn(a_ref[0].astype(BF16), b_ref[...].astype(BF16)).astype(BF16)
            return

        @pl.when(k == 0)
        def _():
            acc[...] = jnp.zeros(acc.shape, F32)

        acc[...] += _dot_tn(a_ref[0].astype(BF16), b_ref[...].astype(BF16))

        @pl.when(k == n_k - 1)
        def _():
            o_ref[0] = acc[...].astype(BF16)

    (out,), got = _hosted_call(
        body, name=name, grid=(G, M // tmo, n_k),
        in_specs=[pl.BlockSpec((1, tk, tmo), lambda g, m, k: (g, k, m)),
                  pl.BlockSpec((tk, N), lambda g, m, k: (k, 0))],
        out_specs=[pl.BlockSpec((1, tmo, N), lambda g, m, k: (g, m, 0))],
        out_shape=[jax.ShapeDtypeStruct((G, M, N), BF16)],
        scratch_shapes=[pltpu.VMEM((tmo, N), F32)],
        args=(a, b), comm=comm)
    return out, got


def sum_partials(gathered):
    n = len(gathered)

    def body(*refs):
        for in_ref, out_ref in zip(refs[:n], refs[n:]):
            total = in_ref[0]
            for d in range(1, N_DEV):
                total = total + in_ref[d]
            out_ref[...] = total

    vmem = pl.BlockSpec(memory_space=pltpu.VMEM)
    return pl.pallas_call(
        body, name="sum_partials", in_specs=[vmem] * n, out_specs=[vmem] * n,
        out_shape=[jax.ShapeDtypeStruct(g.shape[1:], F32) for g in gathered],
        compiler_params=pltpu.CompilerParams(vmem_limit_bytes=VMEM_LIMIT),
    )(*gathered)


_ADAM_C1 = 1.0 - ADAM_B1 ** ADAM_STEP
_ADAM_C2 = 1.0 - ADAM_B2 ** ADAM_STEP


def _adamw_math(w, g, m, v):
    m = ADAM_B1 * m + (1.0 - ADAM_B1) * g
    v = ADAM_B2 * v + (1.0 - ADAM_B2) * (g * g)
    m_hat = m / _ADAM_C1
    v_hat = v / _ADAM_C2
    delta = -ADAM_LR * (m_hat / (jnp.sqrt(v_hat) + ADAM_EPS) + ADAM_WD * w)
    return delta, m, v


def adamw_sharded(parts, w, m, v, *, tr, name, comm=None):
    _, R, C = parts[0].shape

    def body(p0_ref, p1_ref, w_ref, m_ref, v_ref, g_ref, d_ref, nm_ref, nv_ref):
        def update(p_ref):
            g = p_ref[0].astype(F32)
            for d in range(1, N_DEV):
                g = g + p_ref[d].astype(F32)
            delta, nm, nv = _adamw_math(w_ref[0], g, m_ref[0], v_ref[0])
            g_ref[0] = g
            d_ref[0] = delta
            nm_ref[0] = nm
            nv_ref[0] = nv

        @pl.when(pl.program_id(0) == 0)
        def _():
            update(p0_ref)

        @pl.when(pl.program_id(0) == 1)
        def _():
            update(p1_ref)

    n_i = R // tr
    p_specs = [pl.BlockSpec((N_DEV, tr, C), lambda l, i: (0, jnp.where(l == 0, i, n_i - 1), 0)),
               pl.BlockSpec((N_DEV, tr, C), lambda l, i: (0, jnp.where(l == 1, i, 0), 0))]
    o_spec = pl.BlockSpec((1, tr, C), lambda l, i: (l, i, 0))
    return _hosted_call(
        body, name=name, grid=(DEPTH, n_i),
        in_specs=p_specs + [o_spec, o_spec, o_spec], out_specs=[o_spec] * 4,
        out_shape=[jax.ShapeDtypeStruct(w.shape, F32)] * 4, scratch_shapes=[],
        args=(parts[0], parts[1], w, m, v), comm=comm)


def adamw_small(gs, ws, ms, vs):
    n = len(gs)

    def body(*refs):
        g_refs, w_refs, m_refs, v_refs = refs[:n], refs[n:2 * n], refs[2 * n:3 * n], refs[3 * n:4 * n]
        d_refs, nm_refs, nv_refs = refs[4 * n:5 * n], refs[5 * n:6 * n], refs[6 * n:]
        for k in range(n):
            delta, nm, nv = _adamw_math(w_refs[k][...], g_refs[k][...], m_refs[k][...], v_refs[k][...])
            d_refs[k][...] = delta
            nm_refs[k][...] = nm
            nv_refs[k][...] = nv

    vmem = pl.BlockSpec(memory_space=pltpu.VMEM)
    res = pl.pallas_call(
        body, name="adamw_small", in_specs=[vmem] * (4 * n), out_specs=[vmem] * (3 * n),
        out_shape=[jax.ShapeDtypeStruct(w.shape, F32) for w in ws] * 3,
        compiler_params=pltpu.CompilerParams(vmem_limit_bytes=VMEM_LIMIT),
    )(*gs, *ws, *ms, *vs)
    return res[:n], res[n:2 * n], res[2 * n:]


def _pack(arrays, row_multiple):
    flat = jnp.concatenate([a.reshape(-1) for a in arrays])
    rows = -(-flat.shape[0] // (LANES * row_multiple)) * row_multiple
    return jnp.pad(flat, (0, rows * LANES - flat.shape[0])).reshape(rows, LANES)


def _unpack(buf, shapes):
    flat = buf.reshape(-1)
    out, off = [], 0
    for s in shapes:
        n = math.prod(s)
        out.append(flat[off:off + n].reshape(s))
        off += n
    return out


def _block_diag(w_pool):
    G, d, _ = w_pool.shape
    eye = jnp.eye(G, dtype=w_pool.dtype)
    return (eye[:, None, :, None] * w_pool[:, :, None, :]).reshape(G * d, G * d)


def kernel(x, norm1_g, w_in, conv_w, conv_b, conv_ln_g, conv_ln_b, w_pw, sg_ln_g, sg_ln_b, w_s, b_s, w_pool, pool_scale, w_out, norm2_g, w_gate_up, w_down, final_g, loss_target, m_norm1_g, m_w_in, m_conv_w, m_conv_b, m_conv_ln_g, m_conv_ln_b, m_w_pw, m_sg_ln_g, m_sg_ln_b, m_w_s, m_b_s, m_w_pool, m_pool_scale, m_w_out, m_norm2_g, m_w_gate_up, m_w_down, m_final_g, v_norm1_g, v_w_in, v_conv_w, v_conv_b, v_conv_ln_g, v_conv_ln_b, v_w_pw, v_sg_ln_g, v_sg_ln_b, v_w_s, v_b_s, v_w_pool, v_pool_scale, v_w_out, v_norm2_g, v_w_gate_up, v_w_down, v_final_g):
    b_loc, seq, _ = x.shape
    T = b_loc * seq
    tm_mix = min(256, seq)
    tm_ffn_fwd = min(512, T)
    tm_ffn_bwd = min(512, T)
    tm_head = min(512, T)
    tk = min(2048, T)
    tk_f32 = min(1024, T)
    th = D_FF // 2
    cw = conv_w.shape[2]
    my_index = _index_of(_my_coords())

    xf = x.reshape(T, D_MODEL)
    tgt = loss_target.reshape(T, D_MODEL)

    mixer_shards = [[w_in[l].T.astype(BF16), w_out[l].astype(BF16), w_pw[l].astype(BF16)] for l in range(DEPTH)]
    ffn_shards = [[w_gate_up[l].T.astype(BF16), w_down[l].astype(BF16)] for l in range(DEPTH)]

    tril = jnp.tril(jnp.ones((CHUNK, CHUNK), dtype=bool))
    layers = []
    for l in range(DEPTH):
        wm = jnp.where(tril[None], w_s[l], 0.0).astype(BF16)
        layers.append(dict(
            g1=norm1_g[l][None], convb=conv_b[l][None], clng=conv_ln_g[l][None], clnb=conv_ln_b[l][None],
            slng=sg_ln_g[l][None], slnb=sg_ln_b[l][None], wm=wm, wmT=jnp.swapaxes(wm, 1, 2),
            bias=jnp.repeat(b_s[l].T, HEAD_DIM_B, axis=1), wbd=_block_diag(w_pool[l]).astype(BF16),
            pscale=pool_scale[l][None], g2=norm2_g[l][None]))

    def set_mixer_weights(l, g_in, g_out, g_pw):
        layers[l].update(winT=g_in.reshape(D_IN, D_MODEL), wout=g_out.reshape(D_MODEL, D_MODEL), wpw=g_pw.reshape(D_A, D_A))

    def set_ffn_weights(l, g_gu, g_d):
        layers[l].update(wguT=g_gu.reshape(2, D_FF, D_MODEL), wd=g_d.reshape(D_FF, D_MODEL))

    first = run_comm(Gather(mixer_shards[0] + [conv_w.reshape(DEPTH * CONV_WIDTH, cw).T]), name="gather_first")
    set_mixer_weights(0, *first[:3])
    convw_full = first[3].reshape(D_A, DEPTH * CONV_WIDTH).T.reshape(DEPTH, CONV_WIDTH, D_A)
    for l in range(DEPTH):
        layers[l]["convw"] = convw_full[l]

    saved = []
    cur = xf
    for l in range(DEPTH):
        w = layers[l]
        (z, ycv, p, mix, x1), got = mixer_fwd(
            cur, w["g1"], w["winT"], w["convw"], w["convb"], w["clng"], w["clnb"], w["wpw"], w["slng"], w["slnb"], w["wm"],
            w["bias"], w["wbd"], w["pscale"], w["wout"], seq=seq, tm=tm_mix,
            comm=Gather(ffn_shards[l] if l == 0 else ffn_shards[l][:1]))
        if l == 0:
            set_ffn_weights(l, *got)
        else:
            set_ffn_weights(l, got[0], early_wd)
        (x2, fac, f), got = ffn_fwd(x1, w["g2"], w["wguT"], w["wd"], tm=tm_ffn_fwd, th=th,
                                comm=Gather(mixer_shards[l + 1] + ffn_shards[l + 1][1:]) if l + 1 < DEPTH else None)
        if l + 1 < DEPTH:
            set_mixer_weights(l + 1, *got[:3])
            early_wd = got[3]
        saved.append((cur, z, ycv, p, mix, x1, fac, f))
        cur = x2
    loss_part, dx, dfg = head_fwd_bwd(cur, tgt, final_g[None], tm=tm_head)

    blocks = {"gu": (2 * D_FF // N_DEV, D_MODEL), "d": (D_FF // N_DEV, D_MODEL), "in": (D_IN // N_DEV, D_MODEL),
              "out": (D_MODEL // N_DEV, D_MODEL), "pw": (D_A // N_DEV, D_A)}
    by_device = lambda kind, g: g.reshape((N_DEV,) + blocks[kind])
    small = [None] * DEPTH
    parts = {}
    packs = [None] * DEPTH

    pending = None
    for l in reversed(range(DEPTH)):
        w = layers[l]
        x0, z, ycv, p, mix, x1, fac, f = saved[l]
        plan = Together([Exchange(pending[1]), Gather([pending[2]])]) if pending else None
        (dx1, h2, dgu, dg2), got = ffn_bwd(dx, x1, fac, w["g2"], w["wguT"], w["wd"], tm=tm_ffn_bwd, th=th, comm=plan)
        if pending:
            got_parts, (packs[l + 1],) = plan.split(got)
            parts.update(zip(pending[0], got_parts))
        gw_d, _ = wgrad(f[None], dx, tmo=th, tk=tk_f32, name="wgrad_down")
        last = l == 0
        gw_gu, got = wgrad(dgu, h2, tmo=th, tk=tk, name="wgrad_gate_up",
                           comm=Exchange([by_device("d", gw_d)]) if last else None)
        if last:
            parts[("d", l)], = got
        outs, got = mixer_bwd(
            dx1, x0, z, ycv, p, w["g1"], w["winT"], w["convw"], w["clng"], w["clnb"], w["wpw"], w["slng"], w["slnb"],
            w["wm"], w["wmT"], w["bias"], w["wbd"], w["pscale"], w["wout"], seq=seq, tm=tm_mix,
            comm=Exchange([by_device("gu", gw_gu)]))
        parts[("gu", l)], = got
        (dx, dz, h, s, dya, dg1, dconvw, dconvb, dclng, dclnb, dslng, dslnb, dwm, dbs, dwbd, dpscale) = outs
        small[l] = [dg1, dconvw, dconvb, dclng, dclnb, dslng, dslnb, dwm, dbs, dwbd, dpscale, dg2]
        if last:
            small[l] += [dfg, loss_part]
        pack = _pack(small[l], SUBLANES)
        gw_out, _ = wgrad(mix[None], dx1, tmo=D_MODEL, tk=tk_f32, name="wgrad_out")
        if last:
            gw_in, (packs[l],) = wgrad(dz[None], h, tmo=D_IN // 2, tk=tk, name="wgrad_in", comm=Gather([pack]))
        else:
            gw_in, got = wgrad(dz[None], h, tmo=D_IN // 2, tk=tk, name="wgrad_in", comm=Exchange([by_device("out", gw_out)]))
            parts[("out", l)], = got
        gw_pw, _ = wgrad(s[None], dya, tmo=D_A, tk=tk, name="wgrad_pw")
        if last:
            pending = ([("in", l), ("out", l), ("pw", l)],
                       [by_device("in", gw_in), by_device("out", gw_out), by_device("pw", gw_pw)])
        else:
            pending = ([("d", l), ("in", l), ("pw", l)],
                       [by_device("d", gw_d), by_device("in", gw_in), by_device("pw", gw_pw)], pack)
    grad_x = dx.reshape(x.shape)

    parts.update(zip(pending[0], run_comm(Exchange(pending[1]), name="exchange_last")))
    p_in, p_gu, p_d, p_out, p_pw = [[parts[(k, l)] for l in range(DEPTH)] for k in ("in", "gu", "d", "out", "pw")]

    sums = []
    for l, summed in enumerate(sum_partials(packs)):
        sums.append(_unpack(summed, [a.shape for a in small[l]]))
    loss = sums[0][-1][0, 0]
    dfg_sum = sums[0][-2]
    per_layer = 12
    sums = [a for l in range(DEPTH) for a in sums[l][:per_layer]]
    g_small = {k: [] for k in ("norm1_g", "conv_w", "conv_b", "conv_ln_g", "conv_ln_b", "sg_ln_g", "sg_ln_b", "w_s", "b_s",
                               "w_pool", "pool_scale", "norm2_g")}
    for l in range(DEPTH):
        dg1, dconvw, dconvb, dclng, dclnb, dslng, dslnb, dwm, dbs, dwbd, dpscale, dg2 = sums[per_layer * l:per_layer * (l + 1)]
        g_small["norm1_g"].append(dg1[0])
        g_small["conv_w"].append(lax.dynamic_slice_in_dim(dconvw, my_index * cw, cw, axis=1))
        g_small["conv_b"].append(dconvb[0])
        g_small["conv_ln_g"].append(dclng[0])
        g_small["conv_ln_b"].append(dclnb[0])
        g_small["sg_ln_g"].append(dslng[0])
        g_small["sg_ln_b"].append(dslnb[0])
        g_small["w_s"].append(dwm)
        g_small["b_s"].append(dbs[:, :N_HEADS_B].T)
        g_small["w_pool"].append(jnp.stack([dwbd[g * GROUP_DIM_C:(g + 1) * GROUP_DIM_C, g * GROUP_DIM_C:(g + 1) * GROUP_DIM_C]
                                            for g in range(len(POOL_WINDOWS))]))
        g_small["pool_scale"].append(dpscale[0])
        g_small["norm2_g"].append(dg2[0])
    g_small = {k: jnp.stack(v) for k, v in g_small.items()}
    g_small["final_g"] = dfg_sum[0]

    t = lambda a: jnp.swapaxes(a, 1, 2)
    g_w_in, d_w_in, nm_w_in, nv_w_in = map(t, adamw_sharded(p_in, t(w_in), t(m_w_in), t(v_w_in), tr=D_IN // N_DEV // 2,
                                                            name="adamw_w_in")[0])
    g_w_gu, d_w_gu, nm_w_gu, nv_w_gu = map(t, adamw_sharded(p_gu, t(w_gate_up), t(m_w_gate_up), t(v_w_gate_up),
                                                            tr=2 * D_FF // N_DEV // 4, name="adamw_w_gate_up")[0])
    g_w_d, d_w_d, nm_w_d, nv_w_d = adamw_sharded(p_d, w_down, m_w_down, v_w_down, tr=D_FF // N_DEV // 2,
                                                 name="adamw_w_down")[0]
    g_w_out, d_w_out, nm_w_out, nv_w_out = adamw_sharded(p_out, w_out, m_w_out, v_w_out, tr=D_MODEL // N_DEV,
                                                         name="adamw_w_out")[0]
    g_w_pw, d_w_pw, nm_w_pw, nv_w_pw = adamw_sharded(p_pw, w_pw, m_w_pw, v_w_pw, tr=D_A // N_DEV, name="adamw_w_pw")[0]

    small_names = ["norm1_g", "conv_w", "conv_b", "conv_ln_g", "conv_ln_b", "sg_ln_g", "sg_ln_b", "w_s", "b_s", "w_pool",
                   "pool_scale", "norm2_g", "final_g"]
    small_w = dict(norm1_g=norm1_g, conv_w=conv_w, conv_b=conv_b, conv_ln_g=conv_ln_g, conv_ln_b=conv_ln_b, sg_ln_g=sg_ln_g,
                   sg_ln_b=sg_ln_b, w_s=w_s, b_s=b_s, w_pool=w_pool, pool_scale=pool_scale, norm2_g=norm2_g, final_g=final_g)
    small_m = dict(norm1_g=m_norm1_g, conv_w=m_conv_w, conv_b=m_conv_b, conv_ln_g=m_conv_ln_g, conv_ln_b=m_conv_ln_b,
                   sg_ln_g=m_sg_ln_g, sg_ln_b=m_sg_ln_b, w_s=m_w_s, b_s=m_b_s, w_pool=m_w_pool, pool_scale=m_pool_scale,
                   norm2_g=m_norm2_g, final_g=m_final_g)
    small_v = dict(norm1_g=v_norm1_g, conv_w=v_conv_w, conv_b=v_conv_b, conv_ln_g=v_conv_ln_g, conv_ln_b=v_conv_ln_b,
                   sg_ln_g=v_sg_ln_g, sg_ln_b=v_sg_ln_b, w_s=v_w_s, b_s=v_b_s, w_pool=v_w_pool, pool_scale=v_pool_scale,
                   norm2_g=v_norm2_g, final_g=v_final_g)
    two_d = lambda a: a[None] if a.ndim == 1 else a
    d_s, nm_s, nv_s = adamw_small(*[[two_d(d[k]) for k in small_names] for d in (g_small, small_w, small_m, small_v)])
    d_small = {k: a.reshape(small_w[k].shape) for k, a in zip(small_names, d_s)}
    nm_small = {k: a.reshape(small_w[k].shape) for k, a in zip(small_names, nm_s)}
    nv_small = {k: a.reshape(small_w[k].shape) for k, a in zip(small_names, nv_s)}

    order = ["norm1_g", "w_in", "conv_w", "conv_b", "conv_ln_g", "conv_ln_b", "w_pw", "sg_ln_g", "sg_ln_b", "w_s", "b_s",
             "w_pool", "pool_scale", "w_out", "norm2_g", "w_gate_up", "w_down", "final_g"]
    grads = dict(g_small, w_in=g_w_in, w_pw=g_w_pw, w_out=g_w_out, w_gate_up=g_w_gu, w_down=g_w_d)
    deltas = dict(d_small, w_in=d_w_in, w_pw=d_w_pw, w_out=d_w_out, w_gate_up=d_w_gu, w_down=d_w_d)
    new_m = dict(nm_small, w_in=nm_w_in, w_pw=nm_w_pw, w_out=nm_w_out, w_gate_up=nm_w_gu, w_down=nm_w_d)
    new_v = dict(nv_small, w_in=nv_w_in, w_pw=nv_w_pw, w_out=nv_w_out, w_gate_up=nv_w_gu, w_down=nv_w_d)
    return (loss, grad_x, *[grads[k] for k in order], *[deltas[k] for k in order], *[new_m[k] for k in order],
            *[new_v[k] for k in order])
```

```python
import functools
import math

import jax
import jax.numpy as jnp
from jax import lax
from jax.experimental import pallas as pl
from jax.experimental.pallas import tpu as pltpu

F32 = jnp.float32
BF16 = jnp.bfloat16

D_MODEL = 1024
D_A = 384
D_B = 384
D_C = 256
D_IN = 2 * D_A + 2 * D_B + D_C
N_HEADS_B = 4
HEAD_DIM_B = 96
POOL_WINDOWS = (2, 4, 8, 16)
GROUP_DIM_C = 64
CONV_WIDTH = 31
CHUNK = 128
D_FF = 2816
RMS_EPS = 1e-6
LN_EPS = 1e-5
DEPTH = 2
N_DEV = 8

ADAM_LR = 0.001
ADAM_B1 = 0.9
ADAM_B2 = 0.999
ADAM_EPS = 1e-08
ADAM_WD = 0.01
ADAM_STEP = 10

LANES = 128
SUBLANES = 8

CONV_HALO = 32
POOL_HALO = 32
assert POOL_WINDOWS == (2, 4, 8, 16) and POOL_HALO == SUBLANES * len(POOL_WINDOWS)

VMEM_LIMIT = 56 * 1024 * 1024

MESH_ID = pl.DeviceIdType.MESH


def _dot(a, b):
    return jnp.dot(a, b, preferred_element_type=F32)


def _dot_nt(a, b):
    return lax.dot_general(a, b, (((1,), (1,)), ((), ())), preferred_element_type=F32)


def _dot_tn(a, b):
    return lax.dot_general(a, b, (((0,), (0,)), ((), ())), preferred_element_type=F32)


def _sigmoid(x):
    return 0.5 * jnp.tanh(0.5 * x) + 0.5


def _shifted_taps(buf, first_row, n_shifts, tm):
    for phase in range(min(SUBLANES, n_shifts)):
        shifts = list(range(phase, n_shifts, SUBLANES))
        span = buf[first_row + phase:first_row + shifts[-1] + tm, :]
        for s in shifts:
            yield s, span[s - phase:s - phase + tm, :]


def _window_sums_back(x_ref, bufs, n_rows):
    out, src, w = [], x_ref, 1
    for l in range(len(POOL_WINDOWS)):
        lo = SUBLANES * (l + 1)
        cur = src[lo:n_rows, :] + src[lo - w:n_rows - w, :]
        out.append(cur)
        if l < len(bufs):
            bufs[l][lo:n_rows, :] = cur
            src = bufs[l]
        w *= 2
    return out


def _window_sums_ahead(x_ref, bufs, n_rows):
    out, src, w = [], x_ref, 1
    for l in range(len(POOL_WINDOWS)):
        hi = n_rows - SUBLANES * (l + 1)
        cur = src[0:hi, :] + src[w:hi + w, :]
        out.append(cur)
        if l < len(bufs):
            bufs[l][0:hi, :] = cur
            src = bufs[l]
        w *= 2
    return out


_GELU_C = math.sqrt(2.0 / math.pi)


def _gelu_and_grad(x):
    x2 = x * x
    inner = _GELU_C * (x + 0.044715 * x2 * x)
    t = jnp.tanh(inner)
    g = 0.5 * x * (1.0 + t)
    dg = 0.5 * (1.0 + t) + 0.5 * x * (1.0 - t * t) * _GELU_C * (1.0 + 3.0 * 0.044715 * x2)
    return g, dg


def _ln_stats(x):
    mu = jnp.mean(x, axis=-1, keepdims=True)
    xc = x - mu
    var = jnp.mean(xc * xc, axis=-1, keepdims=True)
    rstd = lax.rsqrt(var + LN_EPS)
    return xc * rstd, rstd


def _ln_bwd(dy, xhat, rstd, g):
    dxhat = dy * g
    return rstd * (dxhat - jnp.mean(dxhat, axis=-1, keepdims=True)
                   - xhat * jnp.mean(dxhat * xhat, axis=-1, keepdims=True))


def _rms_bwd(dh, xn, r, g):
    dxn = dh * g
    return r * (dxn - xn * jnp.mean(dxn * xn, axis=-1, keepdims=True))


def _head_masks(width):
    lane = lax.broadcasted_iota(jnp.int32, (1, width), 1)
    return [(lane >= h * HEAD_DIM_B) & (lane < (h + 1) * HEAD_DIM_B) for h in range(N_HEADS_B)]


def _pool_select(vals, width):
    lane = lax.broadcasted_iota(jnp.int32, (1, width), 1)
    out = vals[-1]
    for g in range(len(vals) - 2, -1, -1):
        out = jnp.where(lane < (g + 1) * GROUP_DIM_C, vals[g], out)
    return out


def _pool_counts(pos):
    return _pool_select([jnp.minimum(pos + 1.0, float(w)) for w in POOL_WINDOWS], D_C)


def _full(shape):
    n = len(shape)
    return pl.BlockSpec(shape, lambda *_: (0,) * n)


def _params(sem):
    return pltpu.CompilerParams(dimension_semantics=sem, vmem_limit_bytes=VMEM_LIMIT)


def _my_coords():
    return lax.axis_index("x"), lax.axis_index("y"), lax.axis_index("c")


def _peer(me, rel):
    x, y, c = me
    bx, by, bc = (rel >> 2) & 1, (rel >> 1) & 1, rel & 1
    return (1 - x if bx else x, 1 - y if by else y, 1 - c if bc else c)


def _index_of(dev):
    return 4 * dev[0] + 2 * dev[1] + dev[2]


FORWARD_LEAD = 2
SIBLING = 1
OTHER_CHIPS = (2, 4, 6)


class Gather:
    def __init__(self, shards):
        n = len(shards)
        self.inputs = list(shards)
        self.out_shape = [jax.ShapeDtypeStruct((N_DEV,) + s.shape, s.dtype) for s in shards]
        self.scratch = [pltpu.SemaphoreType.DMA((N_DEV - 1, n)), pltpu.SemaphoreType.DMA((N_DEV - 1, n)),
                        pltpu.SemaphoreType.DMA((n,))]

    @staticmethod
    def _copy(src, dst, sems, rel, k, to):
        return pltpu.make_async_remote_copy(src_ref=src, dst_ref=dst, send_sem=sems[0].at[rel - 1, k],
                                            recv_sem=sems[1].at[rel - 1, k], device_id=to, device_id_type=MESH_ID)

    def start(self, ins, outs, sems):
        me = _my_coords()
        mine = _index_of(me)
        for k, src in enumerate(ins):
            pltpu.make_async_copy(src, outs[k].at[mine], sems[2].at[k]).start()
            for rel in (SIBLING,) + OTHER_CHIPS:
                self._copy(src, outs[k].at[mine], sems, rel, k, _peer(me, rel)).start()

    def forward(self, ins, outs, sems):
        me = _my_coords()
        sibling = _peer(me, SIBLING)
        for rel in OTHER_CHIPS:
            slot = _index_of(_peer(me, rel))
            for k in range(len(ins)):
                self._copy(ins[k], outs[k].at[slot], sems, rel, k, sibling).wait_recv()
                self._copy(outs[k].at[slot], outs[k].at[slot], sems, rel + 1, k, sibling).start()

    def finish(self, ins, outs, sems):
        me = _my_coords()
        mine = _index_of(me)
        sibling = _peer(me, SIBLING)
        for rel in (SIBLING,) + tuple(r + 1 for r in OTHER_CHIPS):
            slot = _index_of(_peer(me, rel))
            for k in range(len(ins)):
                self._copy(ins[k], outs[k].at[slot], sems, rel, k, sibling).wait_recv()
        for rel in range(1, N_DEV):
            for k in range(len(ins)):
                self._copy(ins[k], outs[k].at[mine], sems, rel, k, sibling).wait_send()
        for k, src in enumerate(ins):
            pltpu.make_async_copy(src, outs[k].at[mine], sems[2].at[k]).wait()


class Exchange:
    def __init__(self, fulls):
        n = len(fulls)
        self.inputs = list(fulls)
        self.out_shape = [jax.ShapeDtypeStruct(f.shape, f.dtype) for f in fulls]
        self.scratch = [pltpu.SemaphoreType.DMA((N_DEV - 1, n)), pltpu.SemaphoreType.DMA((N_DEV - 1, n)),
                        pltpu.SemaphoreType.DMA((n,))]

    def start(self, ins, outs, sems):
        me = _my_coords()
        mine = _index_of(me)
        for k, src in enumerate(ins):
            pltpu.make_async_copy(src.at[mine], outs[k].at[mine], sems[2].at[k]).start()
            for rel in range(1, N_DEV):
                to = _peer(me, rel)
                Gather._copy(src.at[_index_of(to)], outs[k].at[mine], sems, rel, k, to).start()

    def forward(self, ins, outs, sems):
        pass

    def finish(self, ins, outs, sems):
        me = _my_coords()
        mine = _index_of(me)
        for rel in range(1, N_DEV):
            frm = _peer(me, rel)
            for k, src in enumerate(ins):
                Gather._copy(src.at[mine], outs[k].at[_index_of(frm)], sems, rel, k, frm).wait_recv()
        for rel in range(1, N_DEV):
            for k, src in enumerate(ins):
                Gather._copy(src.at[mine], outs[k].at[mine], sems, rel, k, _peer(me, rel)).wait_send()
        for k, src in enumerate(ins):
            pltpu.make_async_copy(src.at[mine], outs[k].at[mine], sems[2].at[k]).wait()


class Together:
    def __init__(self, plans):
        self.plans = list(plans)
        self.inputs = [a for p in self.plans for a in p.inputs]
        self.out_shape = [s for p in self.plans for s in p.out_shape]
        self.scratch = [s for p in self.plans for s in p.scratch]

    def _each(self, ins, outs, sems):
        i = o = s = 0
        for p in self.plans:
            ni, no, ns = len(p.inputs), len(p.out_shape), len(p.scratch)
            yield p, ins[i:i + ni], outs[o:o + no], sems[s:s + ns]
            i, o, s = i + ni, o + no, s + ns

    def start(self, ins, outs, sems):
        for p, pi, po, ps in self._each(ins, outs, sems):
            p.start(pi, po, ps)

    def forward(self, ins, outs, sems):
        for p, pi, po, ps in self._each(ins, outs, sems):
            p.forward(pi, po, ps)

    def finish(self, ins, outs, sems):
        for p, pi, po, ps in self._each(ins, outs, sems):
            p.finish(pi, po, ps)

    def split(self, results):
        out, o = [], 0
        for p in self.plans:
            out.append(results[o:o + len(p.out_shape)])
            o += len(p.out_shape)
        return out


def _hosted_call(body, *, name, grid, in_specs, out_specs, out_shape, scratch_shapes, args, comm=None):
    sem = ("arbitrary",) * len(grid)
    if comm is None:
        res = pl.pallas_call(body, name=name, grid=grid, in_specs=in_specs, out_specs=out_specs, out_shape=out_shape,
                             scratch_shapes=scratch_shapes, compiler_params=_params(sem))(*args)
        return list(res), []
    n_in, n_out, n_scr = len(in_specs), len(out_specs), len(scratch_shapes)
    n_cin, n_cout = len(comm.inputs), len(comm.out_shape)
    n_steps = math.prod(grid)

    def hosted(*refs):
        ins, refs = refs[:n_in], refs[n_in:]
        cins, refs = refs[:n_cin], refs[n_cin:]
        outs, refs = refs[:n_out], refs[n_out:]
        couts, refs = refs[:n_cout], refs[n_cout:]
        scr, csems = refs[:n_scr], refs[n_scr:]
        step = 0
        for a, g in enumerate(grid):
            step = step * g + pl.program_id(a)

        @pl.when(step == 0)
        def _():
            comm.start(cins, couts, csems)

        body(*ins, *outs, *scr)

        @pl.when(step == max(n_steps - 1 - FORWARD_LEAD, 0))
        def _():
            comm.forward(cins, couts, csems)

        @pl.when(step == n_steps - 1)
        def _():
            comm.finish(cins, couts, csems)

    any_spec = pl.BlockSpec(memory_space=pl.ANY)
    res = pl.pallas_call(
        hosted, name=name, grid=grid, in_specs=list(in_specs) + [any_spec] * n_cin,
        out_specs=list(out_specs) + [any_spec] * n_cout, out_shape=list(out_shape) + comm.out_shape,
        scratch_shapes=list(scratch_shapes) + comm.scratch,
        compiler_params=pltpu.CompilerParams(dimension_semantics=sem, vmem_limit_bytes=VMEM_LIMIT, has_side_effects=True),
    )(*args, *comm.inputs)
    return list(res[:n_out]), list(res[n_out:])


def run_comm(comm, *, name):
    n_cin, n_cout = len(comm.inputs), len(comm.out_shape)

    def body(*refs):
        cins, couts, csems = refs[:n_cin], refs[n_cin:n_cin + n_cout], refs[n_cin + n_cout:]
        comm.start(cins, couts, csems)
        comm.forward(cins, couts, csems)
        comm.finish(cins, couts, csems)

    any_spec = pl.BlockSpec(memory_space=pl.ANY)
    return pl.pallas_call(
        body, name=name, in_specs=[any_spec] * n_cin, out_specs=[any_spec] * n_cout, out_shape=comm.out_shape,
        scratch_shapes=comm.scratch, compiler_params=pltpu.CompilerParams(has_side_effects=True),
    )(*comm.inputs)


def mixer_fwd(x, g1, winT, convw, convb, clng, clnb, wpw, slng, slnb, wm, bias, wbd, pscale, wout, *, seq, tm,
              comm=None):
    T = x.shape[0]
    tiles_per_seq = seq // tm
    n_chunks = tm // CHUNK

    def body(x_ref, g1_ref, winT_ref, convw_ref, convb_ref, clng_ref, clnb_ref, wpw_ref, slng_ref, slnb_ref,
             wm_ref, bias_ref, wbd_ref, pscale_ref, wout_ref,
             z_ref, ycv_ref, p_ref, mix_ref, x1_ref, ybuf, zcbuf, *pbufs):
        i = pl.program_id(0)
        tile_in_seq = i % tiles_per_seq

        @pl.when(tile_in_seq == 0)
        def _():
            ybuf[0:CONV_HALO, :] = jnp.zeros((CONV_HALO, D_A), F32)
            zcbuf[0:POOL_HALO, :] = jnp.zeros((POOL_HALO, D_C), F32)

        x = x_ref[...]
        r = lax.rsqrt(jnp.mean(x * x, axis=-1, keepdims=True) + RMS_EPS)
        h = (x * r * g1_ref[...]).astype(BF16)
        z = _dot_nt(h, winT_ref[...])
        z_ref[...] = z

        y = z[:, 0:D_A] * _sigmoid(z[:, D_A:2 * D_A])
        ybuf[CONV_HALO:CONV_HALO + tm, :] = y
        acc = jnp.zeros((tm, D_A), F32) + convb_ref[...]
        for k, rows in _shifted_taps(ybuf, CONV_HALO - (CONV_WIDTH - 1), CONV_WIDTH, tm):
            acc = acc + convw_ref[k:k + 1, :] * rows
        ybuf[0:CONV_HALO, :] = ybuf[tm:tm + CONV_HALO, :]
        ycv_ref[...] = acc
        xhat, _ = _ln_stats(acc)
        ln = xhat * clng_ref[...] + clnb_ref[...]
        s = ln * _sigmoid(ln)
        ya = _dot(s.astype(BF16), wpw_ref[...])

        gb, _ = _gelu_and_grad(z[:, 2 * D_A:2 * D_A + 2 * D_B])
        u = gb[:, 0:D_B]
        vhat, _ = _ln_stats(gb[:, D_B:2 * D_B])
        vn = vhat * slng_ref[...] + slnb_ref[...]
        masks = _head_masks(D_B)
        yb_parts = []
        for c in range(n_chunks):
            vn_c = vn[c * CHUNK:(c + 1) * CHUNK, :]
            sg = bias_ref[...]
            for hh in range(N_HEADS_B):
                sg = sg + _dot(wm_ref[hh], jnp.where(masks[hh], vn_c, 0.0).astype(BF16))
            yb_parts.append(u[c * CHUNK:(c + 1) * CHUNK, :] * sg)
        yb = jnp.concatenate(yb_parts, axis=0) if n_chunks > 1 else yb_parts[0]

        zc = z[:, 2 * D_A + 2 * D_B:D_IN]
        zcbuf[POOL_HALO:POOL_HALO + tm, :] = zc
        sums = [v[POOL_HALO - SUBLANES * (l + 1):POOL_HALO - SUBLANES * (l + 1) + tm, :]
                for l, v in enumerate(_window_sums_back(zcbuf, pbufs, POOL_HALO + tm))]
        zcbuf[0:POOL_HALO, :] = zcbuf[tm:tm + POOL_HALO, :]
        pos = (tile_in_seq * tm + lax.broadcasted_iota(jnp.int32, (tm, 1), 0)).astype(F32)
        p = _pool_select(sums, D_C) / _pool_counts(pos) - zc
        p_ref[...] = p
        yc = _dot(p.astype(BF16), wbd_ref[...]) * pscale_ref[...]

        mix = jnp.concatenate([ya, yb, yc], axis=1).astype(BF16)
        mix_ref[...] = mix
        x1_ref[...] = x + _dot(mix, wout_ref[...])

    row = lambda w: pl.BlockSpec((tm, w), lambda i: (i, 0))
    return _hosted_call(
        body, name="mixer_fwd", grid=(T // tm,),
        in_specs=[row(D_MODEL), _full((1, D_MODEL)), _full((D_IN, D_MODEL)), _full((CONV_WIDTH, D_A)),
                  _full((1, D_A)), _full((1, D_A)), _full((1, D_A)), _full((D_A, D_A)), _full((1, D_B)), _full((1, D_B)),
                  _full((N_HEADS_B, CHUNK, CHUNK)), _full((CHUNK, D_B)), _full((D_C, D_C)), _full((1, D_C)),
                  _full((D_MODEL, D_MODEL))],
        out_specs=[row(D_IN), row(D_A), row(D_C), row(D_MODEL), row(D_MODEL)],
        out_shape=[jax.ShapeDtypeStruct((T, D_IN), F32), jax.ShapeDtypeStruct((T, D_A), F32),
                   jax.ShapeDtypeStruct((T, D_C), F32), jax.ShapeDtypeStruct((T, D_MODEL), BF16),
                   jax.ShapeDtypeStruct((T, D_MODEL), F32)],
        scratch_shapes=[pltpu.VMEM((CONV_HALO + tm, D_A), F32)]
        + [pltpu.VMEM((POOL_HALO + tm, D_C), F32)] * len(POOL_WINDOWS),
        args=(x, g1, winT, convw, convb, clng, clnb, wpw, slng, slnb, wm, bias, wbd, pscale, wout), comm=comm)


def mixer_bwd(dx1, x, z, ycv, p, g1, winT, convw, clng, clnb, wpw, slng, slnb, wm, wmT, bias, wbd, pscale, wout,
              *, seq, tm, comm=None):
    T = x.shape[0]
    tiles_per_seq = seq // tm
    n_tiles = T // tm
    n_chunks = tm // CHUNK

    def body(dx1_ref, x_ref, z_ref, ycv_ref, p_ref, g1_ref, winT_ref, convw_ref, clng_ref, clnb_ref, wpw_ref,
             slng_ref, slnb_ref, wm_ref, wmT_ref, bias_ref, wbd_ref, pscale_ref, wout_ref,
             dx_ref, dz_ref, h_ref, s_ref, dya_ref,
             dg1_ref, dconvw_ref, dconvb_ref, dclng_ref, dclnb_ref, dslng_ref, dslnb_ref, dwm_ref, dbs_ref,
             dwbd_ref, dpscale_ref, dycbuf, dpcbuf, *pbufs):
        i = pl.program_id(0)
        tile_in_seq = (n_tiles - 1 - i) % tiles_per_seq

        @pl.when(i == 0)
        def _():
            for ref in (dg1_ref, dconvw_ref, dconvb_ref, dclng_ref, dclnb_ref, dslng_ref, dslnb_ref, dwm_ref,
                        dbs_ref, dwbd_ref, dpscale_ref):
                ref[...] = jnp.zeros(ref.shape, F32)

        @pl.when(tile_in_seq == tiles_per_seq - 1)
        def _():
            dycbuf[tm:tm + CONV_HALO, :] = jnp.zeros((CONV_HALO, D_A), F32)
            dpcbuf[tm:tm + POOL_HALO, :] = jnp.zeros((POOL_HALO, D_C), F32)

        dx1 = dx1_ref[...]
        z = z_ref[...]
        dmix = _dot_nt(dx1.astype(BF16), wout_ref[...])
        dya = dmix[:, 0:D_A]
        dyb = dmix[:, D_A:D_A + D_B]
        dyc = dmix[:, D_A + D_B:D_MODEL]

        p = p_ref[...]
        pb = p.astype(BF16)
        q = _dot(pb, wbd_ref[...])
        dpscale_ref[...] += jnp.sum(dyc * q, axis=0, keepdims=True)
        dq = (dyc * pscale_ref[...]).astype(BF16)
        dwbd_ref[...] += _dot_tn(pb, dq)
        dp = _dot_nt(dq, wbd_ref[...])
        pos = (tile_in_seq * tm + lax.broadcasted_iota(jnp.int32, (tm, 1), 0)).astype(F32)
        dpc = dp / _pool_counts(pos)
        dpcbuf[0:tm, :] = dpc
        sums = [v[0:tm, :] for v in _window_sums_ahead(dpcbuf, pbufs, tm + POOL_HALO)]
        dpcbuf[tm:tm + POOL_HALO, :] = dpcbuf[0:POOL_HALO, :]
        dzc = _pool_select(sums, D_C) - dp

        dya_b = dya.astype(BF16)
        dya_ref[...] = dya_b
        ds = _dot_nt(dya_b, wpw_ref[...])
        xhat, rstd = _ln_stats(ycv_ref[...])
        ln = xhat * clng_ref[...] + clnb_ref[...]
        sg = _sigmoid(ln)
        s_ref[...] = (ln * sg).astype(BF16)
        dln = ds * (sg * (1.0 + ln * (1.0 - sg)))
        dclng_ref[...] += jnp.sum(dln * xhat, axis=0, keepdims=True)
        dclnb_ref[...] += jnp.sum(dln, axis=0, keepdims=True)
        dycv = _ln_bwd(dln, xhat, rstd, clng_ref[...])
        dconvb_ref[...] += jnp.sum(dycv, axis=0, keepdims=True)
        a = z[:, 0:D_A]
        sgate = _sigmoid(z[:, D_A:2 * D_A])
        y = a * sgate
        dycbuf[0:tm, :] = dycv
        dy = jnp.zeros((tm, D_A), F32)
        for d, sh in _shifted_taps(dycbuf, 0, CONV_WIDTH, tm):
            k = CONV_WIDTH - 1 - d
            dy = dy + convw_ref[k:k + 1, :] * sh
            dconvw_ref[k:k + 1, :] += jnp.sum(y * sh, axis=0, keepdims=True)
        dycbuf[tm:tm + CONV_HALO, :] = dycbuf[0:CONV_HALO, :]
        da = dy * sgate
        dgate = dy * a * sgate * (1.0 - sgate)

        gb, dgb = _gelu_and_grad(z[:, 2 * D_A:2 * D_A + 2 * D_B])
        u = gb[:, 0:D_B]
        vhat, vrstd = _ln_stats(gb[:, D_B:2 * D_B])
        vn = vhat * slng_ref[...] + slnb_ref[...]
        masks = _head_masks(D_B)
        tril = (lax.broadcasted_iota(jnp.int32, (CHUNK, CHUNK), 0)
                >= lax.broadcasted_iota(jnp.int32, (CHUNK, CHUNK), 1))
        lane128 = lax.broadcasted_iota(jnp.int32, (1, CHUNK), 1)
        du_parts, dvn_parts = [], []
        for c in range(n_chunks):
            rows = slice(c * CHUNK, (c + 1) * CHUNK)
            vn_c = vn[rows, :]
            vh = [jnp.where(masks[hh], vn_c, 0.0).astype(BF16) for hh in range(N_HEADS_B)]
            sgc = bias_ref[...]
            for hh in range(N_HEADS_B):
                sgc = sgc + _dot(wm_ref[hh], vh[hh])
            dyb_c = dyb[rows, :]
            du_parts.append(dyb_c * sgc)
            dsg = dyb_c * u[rows, :]
            dvn_c = jnp.zeros((CHUNK, D_B), F32)
            dbs = jnp.zeros((CHUNK, CHUNK), F32)
            for hh in range(N_HEADS_B):
                dsg_h = jnp.where(masks[hh], dsg, 0.0)
                dsg_hb = dsg_h.astype(BF16)
                dwm_ref[hh] += jnp.where(tril, _dot_nt(dsg_hb, vh[hh]), 0.0)
                dvn_c = dvn_c + _dot(wmT_ref[hh], dsg_hb)
                dbs = dbs + jnp.where(lane128 == hh, jnp.sum(dsg_h, axis=1, keepdims=True), 0.0)
            dbs_ref[...] += dbs
            dvn_parts.append(dvn_c)
        du = jnp.concatenate(du_parts, axis=0) if n_chunks > 1 else du_parts[0]
        dvn = jnp.concatenate(dvn_parts, axis=0) if n_chunks > 1 else dvn_parts[0]
        dslng_ref[...] += jnp.sum(dvn * vhat, axis=0, keepdims=True)
        dslnb_ref[...] += jnp.sum(dvn, axis=0, keepdims=True)
        dv = _ln_bwd(dvn, vhat, vrstd, slng_ref[...])
        dzb = jnp.concatenate([du, dv], axis=1) * dgb

        dz = jnp.concatenate([da, dgate, dzb, dzc], axis=1).astype(BF16)
        dz_ref[...] = dz
        dh = _dot(dz, winT_ref[...])
        x = x_ref[...]
        r = lax.rsqrt(jnp.mean(x * x, axis=-1, keepdims=True) + RMS_EPS)
        xn = x * r
        h_ref[...] = (xn * g1_ref[...]).astype(BF16)
        dg1_ref[...] += jnp.sum(dh * xn, axis=0, keepdims=True)
        dx_ref[...] = dx1 + _rms_bwd(dh, xn, r, g1_ref[...])

    row = lambda w: pl.BlockSpec((tm, w), lambda i: (n_tiles - 1 - i, 0))
    acc_shapes = [(1, D_MODEL), (CONV_WIDTH, D_A), (1, D_A), (1, D_A), (1, D_A), (1, D_B), (1, D_B),
                  (N_HEADS_B, CHUNK, CHUNK), (CHUNK, CHUNK), (D_C, D_C), (1, D_C)]
    return _hosted_call(
        body, name="mixer_bwd", grid=(n_tiles,),
        in_specs=[row(D_MODEL), row(D_MODEL), row(D_IN), row(D_A), row(D_C),
                  _full((1, D_MODEL)), _full((D_IN, D_MODEL)), _full((CONV_WIDTH, D_A)), _full((1, D_A)), _full((1, D_A)),
                  _full((D_A, D_A)), _full((1, D_B)), _full((1, D_B)), _full((N_HEADS_B, CHUNK, CHUNK)),
                  _full((N_HEADS_B, CHUNK, CHUNK)), _full((CHUNK, D_B)), _full((D_C, D_C)), _full((1, D_C)),
                  _full((D_MODEL, D_MODEL))],
        out_specs=[row(D_MODEL), row(D_IN), row(D_MODEL), row(D_A), row(D_A)] + [_full(s) for s in acc_shapes],
        out_shape=[jax.ShapeDtypeStruct((T, D_MODEL), F32), jax.ShapeDtypeStruct((T, D_IN), BF16),
                   jax.ShapeDtypeStruct((T, D_MODEL), BF16), jax.ShapeDtypeStruct((T, D_A), BF16),
                   jax.ShapeDtypeStruct((T, D_A), BF16)] + [jax.ShapeDtypeStruct(s, F32) for s in acc_shapes],
        scratch_shapes=[pltpu.VMEM((tm + CONV_HALO, D_A), F32)]
        + [pltpu.VMEM((tm + POOL_HALO, D_C), F32)] * len(POOL_WINDOWS),
        args=(dx1, x, z, ycv, p, g1, winT, convw, clng, clnb, wpw, slng, slnb, wm, wmT, bias, wbd, pscale, wout),
        comm=comm)


def ffn_fwd(x1, g2, wguT, wd, *, tm, th, comm=None):
    T = x1.shape[0]
    n_h = D_FF // th

    def body(x1_ref, g2_ref, wgu_ref, wd_ref, x2_ref, fac_ref, f_ref, h2_buf, acc):
        j = pl.program_id(1)

        @pl.when(j == 0)
        def _():
            x = x1_ref[...]
            r = lax.rsqrt(jnp.mean(x * x, axis=-1, keepdims=True) + RMS_EPS)
            h2_buf[...] = (x * r * g2_ref[...]).astype(BF16)
            acc[...] = x

        h2 = h2_buf[...]
        rows = pl.ds(pl.multiple_of(j * th, th), th)
        g = _dot_nt(h2, wgu_ref[0, rows, :])
        u = _dot_nt(h2, wgu_ref[1, rows, :])
        sg = _sigmoid(g)
        silu = g * sg
        fac_ref[0] = silu.astype(BF16)
        fac_ref[1] = (u * (sg * (1.0 + g * (1.0 - sg)))).astype(BF16)
        f = (silu * u).astype(BF16)
        f_ref[...] = f
        acc[...] += _dot(f, wd_ref[rows, :])

        @pl.when(j == n_h - 1)
        def _():
            x2_ref[...] = acc[...]

    return _hosted_call(
        body, name="ffn_fwd", grid=(T // tm, n_h),
        in_specs=[pl.BlockSpec((tm, D_MODEL), lambda i, j: (i, 0)), _full((1, D_MODEL)),
                  _full((2, D_FF, D_MODEL)), _full((D_FF, D_MODEL))],
        out_specs=[pl.BlockSpec((tm, D_MODEL), lambda i, j: (i, 0)),
                   pl.BlockSpec((2, tm, th), lambda i, j: (0, i, j)), pl.BlockSpec((tm, th), lambda i, j: (i, j))],
        out_shape=[jax.ShapeDtypeStruct((T, D_MODEL), F32), jax.ShapeDtypeStruct((2, T, D_FF), BF16),
                   jax.ShapeDtypeStruct((T, D_FF), BF16)],
        scratch_shapes=[pltpu.VMEM((tm, D_MODEL), BF16), pltpu.VMEM((tm, D_MODEL), F32)],
        args=(x1, g2, wguT, wd), comm=comm)


def ffn_bwd(dx2, x1, fac, g2, wguT, wd, *, tm, th, comm=None):
    T = x1.shape[0]
    n_h = D_FF // th

    def body(dx2_ref, x1_ref, fac_ref, g2_ref, wgu_ref, wd_ref, dx1_ref, h2_ref, dgu_ref, dg2_ref, acc):
        i = pl.program_id(0)
        j = pl.program_id(1)

        @pl.when((i == 0) & (j == 0))
        def _():
            dg2_ref[...] = jnp.zeros(dg2_ref.shape, F32)

        dx2 = dx2_ref[...]
        rows = pl.ds(pl.multiple_of(j * th, th), th)
        df = _dot_nt(dx2.astype(BF16), wd_ref[rows, :])
        dup = (df * fac_ref[0].astype(F32)).astype(BF16)
        dgate = (df * fac_ref[1].astype(F32)).astype(BF16)
        dgu_ref[0] = dgate
        dgu_ref[1] = dup

        @pl.when(j == 0)
        def _():
            acc[...] = jnp.zeros(acc.shape, F32)

        acc[...] += _dot(dgate, wgu_ref[0, rows, :]) + _dot(dup, wgu_ref[1, rows, :])

        @pl.when(j == n_h - 1)
        def _():
            x = x1_ref[...]
            r = lax.rsqrt(jnp.mean(x * x, axis=-1, keepdims=True) + RMS_EPS)
            xn = x * r
            dh = acc[...]
            h2_ref[...] = (xn * g2_ref[...]).astype(BF16)
            dg2_ref[...] += jnp.sum(dh * xn, axis=0, keepdims=True)
            dx1_ref[...] = dx2 + _rms_bwd(dh, xn, r, g2_ref[...])

    return _hosted_call(
        body, name="ffn_bwd", grid=(T // tm, n_h),
        in_specs=[pl.BlockSpec((tm, D_MODEL), lambda i, j: (i, 0)), pl.BlockSpec((tm, D_MODEL), lambda i, j: (i, 0)),
                  pl.BlockSpec((2, tm, th), lambda i, j: (0, i, j)), _full((1, D_MODEL)),
                  _full((2, D_FF, D_MODEL)), _full((D_FF, D_MODEL))],
        out_specs=[pl.BlockSpec((tm, D_MODEL), lambda i, j: (i, 0)), pl.BlockSpec((tm, D_MODEL), lambda i, j: (i, 0)),
                   pl.BlockSpec((2, tm, th), lambda i, j: (0, i, j)), _full((1, D_MODEL))],
        out_shape=[jax.ShapeDtypeStruct((T, D_MODEL), F32), jax.ShapeDtypeStruct((T, D_MODEL), BF16),
                   jax.ShapeDtypeStruct((2, T, D_FF), BF16), jax.ShapeDtypeStruct((1, D_MODEL), F32)],
        scratch_shapes=[pltpu.VMEM((tm, D_MODEL), F32)],
        args=(dx2, x1, fac, g2, wguT, wd), comm=comm)


def head_fwd_bwd(x, target, fg, *, tm):
    T = x.shape[0]
    n_tiles = T // tm

    def body(x_ref, t_ref, fg_ref, loss_ref, dx_ref, dfg_ref, lacc):
        i = pl.program_id(0)

        @pl.when(i == 0)
        def _():
            lacc[...] = jnp.zeros(lacc.shape, F32)
            dfg_ref[...] = jnp.zeros(dfg_ref.shape, F32)

        x = x_ref[...]
        r = lax.rsqrt(jnp.mean(x * x, axis=-1, keepdims=True) + RMS_EPS)
        xn = x * r
        e = xn * fg_ref[...] - t_ref[...]
        lacc[...] += jnp.sum(e * e, axis=0, keepdims=True)
        dy = e * (1.0 / D_MODEL)
        dfg_ref[...] += jnp.sum(dy * xn, axis=0, keepdims=True)
        dx_ref[...] = _rms_bwd(dy, xn, r, fg_ref[...])

        @pl.when(i == n_tiles - 1)
        def _():
            loss_ref[...] = jnp.sum(lacc[...], axis=1, keepdims=True) * (0.5 / D_MODEL)

    row = pl.BlockSpec((tm, D_MODEL), lambda i: (i, 0))
    return pl.pallas_call(
        body, name="head_fwd_bwd", grid=(n_tiles,),
        in_specs=[row, row, _full((1, D_MODEL))],
        out_specs=[_full((1, 1)), row, _full((1, D_MODEL))],
        out_shape=[jax.ShapeDtypeStruct((1, 1), F32), jax.ShapeDtypeStruct((T, D_MODEL), F32),
                   jax.ShapeDtypeStruct((1, D_MODEL), F32)],
        scratch_shapes=[pltpu.VMEM((1, D_MODEL), F32)],
        compiler_params=_params(("arbitrary",)),
    )(x, target, fg)


def wgrad(a, b, *, tmo, tk, name, comm=None):
    G, T, M = a.shape
    N = b.shape[1]
    n_k = T // tk

    def body(a_ref, b_ref, o_ref, acc):
        k = pl.program_id(2)
        if n_k == 1:
            o_ref[0] = _dot_tn(a_ref[0].astype(BF16), b_ref[...].astype(BF16)).astype(BF16)
            return

        @pl.when(k == 0)
        def _():
            acc[...] = jnp.zeros(acc.shape, F32)

        acc[...] += _dot_tn(a_ref[0].astype(BF16), b_ref[...].astype(BF16))

        @pl.when(k == n_k - 1)
        def _():
            o_ref[0] = acc[...].astype(BF16)

    (out,), got = _hosted_call(
        body, name=name, grid=(G, M // tmo, n_k),
        in_specs=[pl.BlockSpec((1, tk, tmo), lambda g, m, k: (g, k, m)),
                  pl.BlockSpec((tk, N), lambda g, m, k: (k, 0))],
        out_specs=[pl.BlockSpec((1, tmo, N), lambda g, m, k: (g, m, 0))],
        out_shape=[jax.ShapeDtypeStruct((G, M, N), BF16)],
        scratch_shapes=[pltpu.VMEM((tmo, N), F32)],
        args=(a, b), comm=comm)
    return out, got


def sum_partials(gathered):
    n = len(gathered)

    def body(*refs):
        for in_ref, out_ref in zip(refs[:n], refs[n:]):
            total = in_ref[0]
            for d in range(1, N_DEV):
                total = total + in_ref[d]
            out_ref[...] = total

    vmem = pl.BlockSpec(memory_space=pltpu.VMEM)
    return pl.pallas_call(
        body, name="sum_partials", in_specs=[vmem] * n, out_specs=[vmem] * n,
        out_shape=[jax.ShapeDtypeStruct(g.shape[1:], F32) for g in gathered],
        compiler_params=pltpu.CompilerParams(vmem_limit_bytes=VMEM_LIMIT),
    )(*gathered)


_ADAM_C1 = 1.0 - ADAM_B1 ** ADAM_STEP
_ADAM_C2 = 1.0 - ADAM_B2 ** ADAM_STEP


def _adamw_math(w, g, m, v):
    m = ADAM_B1 * m + (1.0 - ADAM_B1) * g
    v = ADAM_B2 * v + (1.0 - ADAM_B2) * (g * g)
    m_hat = m / _ADAM_C1
    v_hat = v / _ADAM_C2
    delta = -ADAM_LR * (m_hat / (jnp.sqrt(v_hat) + ADAM_EPS) + ADAM_WD * w)
    return delta, m, v


def adamw_sharded(parts, w, m, v, *, tr, name, comm=None):
    _, R, C = parts[0].shape

    def body(p0_ref, p1_ref, w_ref, m_ref, v_ref, g_ref, d_ref, nm_ref, nv_ref):
        def update(p_ref):
            g = p_ref[0].astype(F32)
            for d in range(1, N_DEV):
                g = g + p_ref[d].astype(F32)
            delta, nm, nv = _adamw_math(w_ref[0], g, m_ref[0], v_ref[0])
            g_ref[0] = g
            d_ref[0] = delta
            nm_ref[0] = nm
            nv_ref[0] = nv

        @pl.when(pl.program_id(0) == 0)
        def _():
            update(p0_ref)

        @pl.when(pl.program_id(0) == 1)
        def _():
            update(p1_ref)

    n_i = R // tr
    p_specs = [pl.BlockSpec((N_DEV, tr, C), lambda l, i: (0, jnp.where(l == 0, i, n_i - 1), 0)),
               pl.BlockSpec((N_DEV, tr, C), lambda l, i: (0, jnp.where(l == 1, i, 0), 0))]
    o_spec = pl.BlockSpec((1, tr, C), lambda l, i: (l, i, 0))
    return _hosted_call(
        body, name=name, grid=(DEPTH, n_i),
        in_specs=p_specs + [o_spec, o_spec, o_spec], out_specs=[o_spec] * 4,
        out_shape=[jax.ShapeDtypeStruct(w.shape, F32)] * 4, scratch_shapes=[],
        args=(parts[0], parts[1], w, m, v), comm=comm)


def adamw_small(gs, ws, ms, vs):
    n = len(gs)

    def body(*refs):
        g_refs, w_refs, m_refs, v_refs = refs[:n], refs[n:2 * n], refs[2 * n:3 * n], refs[3 * n:4 * n]
        d_refs, nm_refs, nv_refs = refs[4 * n:5 * n], refs[5 * n:6 * n], refs[6 * n:]
        for k in range(n):
            delta, nm, nv = _adamw_math(w_refs[k][...], g_refs[k][...], m_refs[k][...], v_refs[k][...])
            d_refs[k][...] = delta
            nm_refs[k][...] = nm
            nv_refs[k][...] = nv

    vmem = pl.BlockSpec(memory_space=pltpu.VMEM)
    res = pl.pallas_call(
        body, name="adamw_small", in_specs=[vmem] * (4 * n), out_specs=[vmem] * (3 * n),
        out_shape=[jax.ShapeDtypeStruct(w.shape, F32) for w in ws] * 3,
        compiler_params=pltpu.CompilerParams(vmem_limit_bytes=VMEM_LIMIT),
    )(*gs, *ws, *ms, *vs)
    return res[:n], res[n:2 * n], res[2 * n:]


def _pack(arrays, row_multiple):
    flat = jnp.concatenate([a.reshape(-1) for a in arrays])
    rows = -(-flat.shape[0] // (LANES * row_multiple)) * row_multiple
    return jnp.pad(flat, (0, rows * LANES - flat.shape[0])).reshape(rows, LANES)


def _unpack(buf, shapes):
    flat = buf.reshape(-1)
    out, off = [], 0
    for s in shapes:
        n = math.prod(s)
        out.append(flat[off:off + n].reshape(s))
        off += n
    return out


def _block_diag(w_pool):
    G, d, _ = w_pool.shape
    eye = jnp.eye(G, dtype=w_pool.dtype)
    return (eye[:, None, :, None] * w_pool[:, :, None, :]).reshape(G * d, G * d)


def kernel(x, norm1_g, w_in, conv_w, conv_b, conv_ln_g, conv_ln_b, w_pw, sg_ln_g, sg_ln_b, w_s, b_s, w_pool, pool_scale, w_out, norm2_g, w_gate_up, w_down, final_g, loss_target, m_norm1_g, m_w_in, m_conv_w, m_conv_b, m_conv_ln_g, m_conv_ln_b, m_w_pw, m_sg_ln_g, m_sg_ln_b, m_w_s, m_b_s, m_w_pool, m_pool_scale, m_w_out, m_norm2_g, m_w_gate_up, m_w_down, m_final_g, v_norm1_g, v_w_in, v_conv_w, v_conv_b, v_conv_ln_g, v_conv_ln_b, v_w_pw, v_sg_ln_g, v_sg_ln_b, v_w_s, v_b_s, v_w_pool, v_pool_scale, v_w_out, v_norm2_g, v_w_gate_up, v_w_down, v_final_g):
    b_loc, seq, _ = x.shape
    T = b_loc * seq
    tm_mix = min(256, seq)
    tm_ffn_fwd = min(512, T)
    tm_ffn_bwd = min(512, T)
    tm_head = min(512, T)
    tk = min(2048, T)
    tk_f32 = min(1024, T)
    th = D_FF // 2
    cw = conv_w.shape[2]
    my_index = _index_of(_my_coords())

    xf = x.reshape(T, D_MODEL)
    tgt = loss_target.reshape(T, D_MODEL)

    mixer_shards = [[w_in[l].T.astype(BF16), w_out[l].astype(BF16), w_pw[l].astype(BF16)] for l in range(DEPTH)]
    ffn_shards = [[w_gate_up[l].T.astype(BF16), w_down[l].astype(BF16)] for l in range(DEPTH)]

    tril = jnp.tril(jnp.ones((CHUNK, CHUNK), dtype=bool))
    layers = []
    for l in range(DEPTH):
        wm = jnp.where(tril[None], w_s[l], 0.0).astype(BF16)
        layers.append(dict(
            g1=norm1_g[l][None], convb=conv_b[l][None], clng=conv_ln_g[l][None], clnb=conv_ln_b[l][None],
            slng=sg_ln_g[l][None], slnb=sg_ln_b[l][None], wm=wm, wmT=jnp.swapaxes(wm, 1, 2),
            bias=jnp.repeat(b_s[l].T, HEAD_DIM_B, axis=1), wbd=_block_diag(w_pool[l]).astype(BF16),
            pscale=pool_scale[l][None], g2=norm2_g[l][None]))

    def set_mixer_weights(l, g_in, g_out, g_pw):
        layers[l].update(winT=g_in.reshape(D_IN, D_MODEL), wout=g_out.reshape(D_MODEL, D_MODEL), wpw=g_pw.reshape(D_A, D_A))

    def set_ffn_weights(l, g_gu, g_d):
        layers[l].update(wguT=g_gu.reshape(2, D_FF, D_MODEL), wd=g_d.reshape(D_FF, D_MODEL))

    first = run_comm(Gather(mixer_shards[0] + [conv_w.reshape(DEPTH * CONV_WIDTH, cw).T]), name="gather_first")
    set_mixer_weights(0, *first[:3])
    convw_full = first[3].reshape(D_A, DEPTH * CONV_WIDTH).T.reshape(DEPTH, CONV_WIDTH, D_A)
    for l in range(DEPTH):
        layers[l]["convw"] = convw_full[l]

    saved = []
    cur = xf
    for l in range(DEPTH):
        w = layers[l]
        (z, ycv, p, mix, x1), got = mixer_fwd(
            cur, w["g1"], w["winT"], w["convw"], w["convb"], w["clng"], w["clnb"], w["wpw"], w["slng"], w["slnb"], w["wm"],
            w["bias"], w["wbd"], w["pscale"], w["wout"], seq=seq, tm=tm_mix,
            comm=Gather(ffn_shards[l] if l == 0 else ffn_shards[l][:1]))
        if l == 0:
            set_ffn_weights(l, *got)
        else:
            set_ffn_weights(l, got[0], early_wd)
        (x2, fac, f), got = ffn_fwd(x1, w["g2"], w["wguT"], w["wd"], tm=tm_ffn_fwd, th=th,
                                comm=Gather(mixer_shards[l + 1] + ffn_shards[l + 1][1:]) if l + 1 < DEPTH else None)
        if l + 1 < DEPTH:
            set_mixer_weights(l + 1, *got[:3])
            early_wd = got[3]
        saved.append((cur, z, ycv, p, mix, x1, fac, f))
        cur = x2
    loss_part, dx, dfg = head_fwd_bwd(cur, tgt, final_g[None], tm=tm_head)

    blocks = {"gu": (2 * D_FF // N_DEV, D_MODEL), "d": (D_FF // N_DEV, D_MODEL), "in": (D_IN // N_DEV, D_MODEL),
              "out": (D_MODEL // N_DEV, D_MODEL), "pw": (D_A // N_DEV, D_A)}
    by_device = lambda kind, g: g.reshape((N_DEV,) + blocks[kind])
    small = [None] * DEPTH
    parts = {}
    packs = [None] * DEPTH

    pending = None
    for l in reversed(range(DEPTH)):
        w = layers[l]
        x0, z, ycv, p, mix, x1, fac, f = saved[l]
        plan = Together([Exchange(pending[1]), Gather([pending[2]])]) if pending else None
        (dx1, h2, dgu, dg2), got = ffn_bwd(dx, x1, fac, w["g2"], w["wguT"], w["wd"], tm=tm_ffn_bwd, th=th, comm=plan)
        if pending:
            got_parts, (packs[l + 1],) = plan.split(got)
            parts.update(zip(pending[0], got_parts))
        gw_d, _ = wgrad(f[None], dx, tmo=th, tk=tk_f32, name="wgrad_down")
        last = l == 0
        gw_gu, got = wgrad(dgu, h2, tmo=th, tk=tk, name="wgrad_gate_up", comm=Exchange([by_device("d", gw_d)]))
        parts[("d", l)], = got
        outs, got = mixer_bwd(
            dx1, x0, z, ycv, p, w["g1"], w["winT"], w["convw"], w["clng"], w["clnb"], w["wpw"], w["slng"], w["slnb"],
            w["wm"], w["wmT"], w["bias"], w["wbd"], w["pscale"], w["wout"], seq=seq, tm=tm_mix,
            comm=Exchange([by_device("gu", gw_gu)]))
        parts[("gu", l)], = got
        (dx, dz, h, s, dya, dg1, dconvw, dconvb, dclng, dclnb, dslng, dslnb, dwm, dbs, dwbd, dpscale) = outs
        small[l] = [dg1, dconvw, dconvb, dclng, dclnb, dslng, dslnb, dwm, dbs, dwbd, dpscale, dg2]
        if last:
            small[l] += [dfg, loss_part]
        pack = _pack(small[l], SUBLANES)
        gw_out, _ = wgrad(mix[None], dx1, tmo=D_MODEL, tk=tk_f32, name="wgrad_out")
        if last:
            gw_in, (packs[l],) = wgrad(dz[None], h, tmo=D_IN // 2, tk=tk, name="wgrad_in", comm=Gather([pack]))
        else:
            gw_in, got = wgrad(dz[None], h, tmo=D_IN // 2, tk=tk, name="wgrad_in", comm=Exchange([by_device("out", gw_out)]))
            parts[("out", l)], = got
        gw_pw, _ = wgrad(s[None], dya, tmo=D_A, tk=tk, name="wgrad_pw")
        if last:
            pending = ([("in", l), ("out", l), ("pw", l)],
                       [by_device("in", gw_in), by_device("out", gw_out), by_device("pw", gw_pw)])
        else:
            pending = ([("in", l), ("pw", l)], [by_device("in", gw_in), by_device("pw", gw_pw)], pack)
    grad_x = dx.reshape(x.shape)

    parts.update(zip(pending[0], run_comm(Exchange(pending[1]), name="exchange_last")))
    p_in, p_gu, p_d, p_out, p_pw = [[parts[(k, l)] for l in range(DEPTH)] for k in ("in", "gu", "d", "out", "pw")]

    sums = []
    for l, summed in enumerate(sum_partials(packs)):
        sums.append(_unpack(summed, [a.shape for a in small[l]]))
    loss = sums[0][-1][0, 0]
    dfg_sum = sums[0][-2]
    per_layer = 12
    sums = [a for l in range(DEPTH) for a in sums[l][:per_layer]]
    g_small = {k: [] for k in ("norm1_g", "conv_w", "conv_b", "conv_ln_g", "conv_ln_b", "sg_ln_g", "sg_ln_b", "w_s", "b_s",
                               "w_pool", "pool_scale", "norm2_g")}
    for l in range(DEPTH):
        dg1, dconvw, dconvb, dclng, dclnb, dslng, dslnb, dwm, dbs, dwbd, dpscale, dg2 = sums[per_layer * l:per_layer * (l + 1)]
        g_small["norm1_g"].append(dg1[0])
        g_small["conv_w"].append(lax.dynamic_slice_in_dim(dconvw, my_index * cw, cw, axis=1))
        g_small["conv_b"].append(dconvb[0])
        g_small["conv_ln_g"].append(dclng[0])
        g_small["conv_ln_b"].append(dclnb[0])
        g_small["sg_ln_g"].append(dslng[0])
        g_small["sg_ln_b"].append(dslnb[0])
        g_small["w_s"].append(dwm)
        g_small["b_s"].append(dbs[:, :N_HEADS_B].T)
        g_small["w_pool"].append(jnp.stack([dwbd[g * GROUP_DIM_C:(g + 1) * GROUP_DIM_C, g * GROUP_DIM_C:(g + 1) * GROUP_DIM_C]
                                            for g in range(len(POOL_WINDOWS))]))
        g_small["pool_scale"].append(dpscale[0])
        g_small["norm2_g"].append(dg2[0])
    g_small = {k: jnp.stack(v) for k, v in g_small.items()}
    g_small["final_g"] = dfg_sum[0]

    t = lambda a: jnp.swapaxes(a, 1, 2)
    g_w_in, d_w_in, nm_w_in, nv_w_in = map(t, adamw_sharded(p_in, t(w_in), t(m_w_in), t(v_w_in), tr=D_IN // N_DEV // 2,
                                                            name="adamw_w_in")[0])
    g_w_gu, d_w_gu, nm_w_gu, nv_w_gu = map(t, adamw_sharded(p_gu, t(w_gate_up), t(m_w_gate_up), t(v_w_gate_up),
                                                            tr=2 * D_FF // N_DEV // 4, name="adamw_w_gate_up")[0])
    g_w_d, d_w_d, nm_w_d, nv_w_d = adamw_sharded(p_d, w_down, m_w_down, v_w_down, tr=D_FF // N_DEV // 2,
                                                 name="adamw_w_down")[0]
    g_w_out, d_w_out, nm_w_out, nv_w_out = adamw_sharded(p_out, w_out, m_w_out, v_w_out, tr=D_MODEL // N_DEV,
                                                         name="adamw_w_out")[0]
    g_w_pw, d_w_pw, nm_w_pw, nv_w_pw = adamw_sharded(p_pw, w_pw, m_w_pw, v_w_pw, tr=D_A // N_DEV, name="adamw_w_pw")[0]

    small_names = ["norm1_g", "conv_w", "conv_b", "conv_ln_g", "conv_ln_b", "sg_ln_g", "sg_ln_b", "w_s", "b_s", "w_pool",
                   "pool_scale", "norm2_g", "final_g"]
    small_w = dict(norm1_g=norm1_g, conv_w=conv_w, conv_b=conv_b, conv_ln_g=conv_ln_g, conv_ln_b=conv_ln_b, sg_ln_g=sg_ln_g,
                   sg_ln_b=sg_ln_b, w_s=w_s, b_s=b_s, w_pool=w_pool, pool_scale=pool_scale, norm2_g=norm2_g, final_g=final_g)
    small_m = dict(norm1_g=m_norm1_g, conv_w=m_conv_w, conv_b=m_conv_b, conv_ln_g=m_conv_ln_g, conv_ln_b=m_conv_ln_b,
                   sg_ln_g=m_sg_ln_g, sg_ln_b=m_sg_ln_b, w_s=m_w_s, b_s=m_b_s, w_pool=m_w_pool, pool_scale=m_pool_scale,
                   norm2_g=m_norm2_g, final_g=m_final_g)
    small_v = dict(norm1_g=v_norm1_g, conv_w=v_conv_w, conv_b=v_conv_b, conv_ln_g=v_conv_ln_g, conv_ln_b=v_conv_ln_b,
                   sg_ln_g=v_sg_ln_g, sg_ln_b=v_sg_ln_b, w_s=v_w_s, b_s=v_b_s, w_pool=v_w_pool, pool_scale=v_pool_scale,
                   norm2_g=v_norm2_g, final_g=v_final_g)
    two_d = lambda a: a[None] if a.ndim == 1 else a
    d_s, nm_s, nv_s = adamw_small(*[[two_d(d[k]) for k in small_names] for d in (g_small, small_w, small_m, small_v)])
    d_small = {k: a.reshape(small_w[k].shape) for k, a in zip(small_names, d_s)}
    nm_small = {k: a.reshape(small_w[k].shape) for k, a in zip(small_names, nm_s)}
    nv_small = {k: a.reshape(small_w[k].shape) for k, a in zip(small_names, nv_s)}

    order = ["norm1_g", "w_in", "conv_w", "conv_b", "conv_ln_g", "conv_ln_b", "w_pw", "sg_ln_g", "sg_ln_b", "w_s", "b_s",
             "w_pool", "pool_scale", "w_out", "norm2_g", "w_gate_up", "w_down", "final_g"]
    grads = dict(g_small, w_in=g_w_in, w_pw=g_w_pw, w_out=g_w_out, w_gate_up=g_w_gu, w_down=g_w_d)
    deltas = dict(d_small, w_in=d_w_in, w_pw=d_w_pw, w_out=d_w_out, w_gate_up=d_w_gu, w_down=d_w_d)
    new_m = dict(nm_small, w_in=nm_w_in, w_pw=nm_w_pw, w_out=nm_w_out, w_gate_up=nm_w_gu, w_down=nm_w_d)
    new_v = dict(nv_small, w_in=nv_w_in, w_pw=nv_w_pw, w_out=nv_w_out, w_gate_up=nv_w_gu, w_down=nv_w_d)
    return (loss, grad_x, *[grads[k] for k in order], *[deltas[k] for k in order], *[new_m[k] for k in order],
            *[new_v[k] for k in order])
```

```python
import functools
import math

import jax
import jax.numpy as jnp
from jax import lax
from jax.experimental import pallas as pl
from jax.experimental.pallas import tpu as pltpu

F32 = jnp.float32
BF16 = jnp.bfloat16

D_MODEL = 1024
D_A = 384
D_B = 384
D_C = 256
D_IN = 2 * D_A + 2 * D_B + D_C
N_HEADS_B = 4
HEAD_DIM_B = 96
POOL_WINDOWS = (2, 4, 8, 16)
GROUP_DIM_C = 64
CONV_WIDTH = 31
CHUNK = 128
D_FF = 2816
RMS_EPS = 1e-6
LN_EPS = 1e-5
DEPTH = 2
N_DEV = 8

ADAM_LR = 0.001
ADAM_B1 = 0.9
ADAM_B2 = 0.999
ADAM_EPS = 1e-08
ADAM_WD = 0.01
ADAM_STEP = 10

LANES = 128
SUBLANES = 8

CONV_HALO = 32
POOL_HALO = 32
assert POOL_WINDOWS == (2, 4, 8, 16) and POOL_HALO == SUBLANES * len(POOL_WINDOWS)

VMEM_LIMIT = 56 * 1024 * 1024

MESH_ID = pl.DeviceIdType.MESH


def _dot(a, b):
    return jnp.dot(a, b, preferred_element_type=F32)


def _dot_nt(a, b):
    return lax.dot_general(a, b, (((1,), (1,)), ((), ())), preferred_element_type=F32)


def _dot_tn(a, b):
    return lax.dot_general(a, b, (((0,), (0,)), ((), ())), preferred_element_type=F32)


def _sigmoid(x):
    return 0.5 * jnp.tanh(0.5 * x) + 0.5


def _shifted_taps(buf, first_row, n_shifts, tm):
    for phase in range(min(SUBLANES, n_shifts)):
        shifts = list(range(phase, n_shifts, SUBLANES))
        span = buf[first_row + phase:first_row + shifts[-1] + tm, :]
        for s in shifts:
            yield s, span[s - phase:s - phase + tm, :]


def _window_sums_back(x_ref, bufs, n_rows):
    out, src, w = [], x_ref, 1
    for l in range(len(POOL_WINDOWS)):
        lo = SUBLANES * (l + 1)
        cur = src[lo:n_rows, :] + src[lo - w:n_rows - w, :]
        out.append(cur)
        if l < len(bufs):
            bufs[l][lo:n_rows, :] = cur
            src = bufs[l]
        w *= 2
    return out


def _window_sums_ahead(x_ref, bufs, n_rows):
    out, src, w = [], x_ref, 1
    for l in range(len(POOL_WINDOWS)):
        hi = n_rows - SUBLANES * (l + 1)
        cur = src[0:hi, :] + src[w:hi + w, :]
        out.append(cur)
        if l < len(bufs):
            bufs[l][0:hi, :] = cur
            src = bufs[l]
        w *= 2
    return out


_GELU_C = math.sqrt(2.0 / math.pi)


def _gelu_and_grad(x):
    x2 = x * x
    inner = _GELU_C * (x + 0.044715 * x2 * x)
    t = jnp.tanh(inner)
    g = 0.5 * x * (1.0 + t)
    dg = 0.5 * (1.0 + t) + 0.5 * x * (1.0 - t * t) * _GELU_C * (1.0 + 3.0 * 0.044715 * x2)
    return g, dg


def _ln_stats(x):
    mu = jnp.mean(x, axis=-1, keepdims=True)
    xc = x - mu
    var = jnp.mean(xc * xc, axis=-1, keepdims=True)
    rstd = lax.rsqrt(var + LN_EPS)
    return xc * rstd, rstd


def _ln_bwd(dy, xhat, rstd, g):
    dxhat = dy * g
    return rstd * (dxhat - jnp.mean(dxhat, axis=-1, keepdims=True)
                   - xhat * jnp.mean(dxhat * xhat, axis=-1, keepdims=True))


def _rms_bwd(dh, xn, r, g):
    dxn = dh * g
    return r * (dxn - xn * jnp.mean(dxn * xn, axis=-1, keepdims=True))


def _head_masks(width):
    lane = lax.broadcasted_iota(jnp.int32, (1, width), 1)
    return [(lane >= h * HEAD_DIM_B) & (lane < (h + 1) * HEAD_DIM_B) for h in range(N_HEADS_B)]


def _pool_select(vals, width):
    lane = lax.broadcasted_iota(jnp.int32, (1, width), 1)
    out = vals[-1]
    for g in range(len(vals) - 2, -1, -1):
        out = jnp.where(lane < (g + 1) * GROUP_DIM_C, vals[g], out)
    return out


def _pool_counts(pos):
    return _pool_select([jnp.minimum(pos + 1.0, float(w)) for w in POOL_WINDOWS], D_C)


def _full(shape):
    n = len(shape)
    return pl.BlockSpec(shape, lambda *_: (0,) * n)


def _params(sem):
    return pltpu.CompilerParams(dimension_semantics=sem, vmem_limit_bytes=VMEM_LIMIT)


def _my_coords():
    return lax.axis_index("x"), lax.axis_index("y"), lax.axis_index("c")


def _peer(me, rel):
    x, y, c = me
    bx, by, bc = (rel >> 2) & 1, (rel >> 1) & 1, rel & 1
    return (1 - x if bx else x, 1 - y if by else y, 1 - c if bc else c)


def _index_of(dev):
    return 4 * dev[0] + 2 * dev[1] + dev[2]


FORWARD_LEAD = 2
SIBLING = 1
OTHER_CHIPS = (2, 4, 6)


class Gather:
    def __init__(self, shards, lead=FORWARD_LEAD):
        n = len(shards)
        self.lead = lead
        self.inputs = list(shards)
        self.out_shape = [jax.ShapeDtypeStruct((N_DEV,) + s.shape, s.dtype) for s in shards]
        self.scratch = [pltpu.SemaphoreType.DMA((N_DEV - 1, n)), pltpu.SemaphoreType.DMA((N_DEV - 1, n)),
                        pltpu.SemaphoreType.DMA((n,))]

    @staticmethod
    def _copy(src, dst, sems, rel, k, to):
        return pltpu.make_async_remote_copy(src_ref=src, dst_ref=dst, send_sem=sems[0].at[rel - 1, k],
                                            recv_sem=sems[1].at[rel - 1, k], device_id=to, device_id_type=MESH_ID)

    def start(self, ins, outs, sems):
        me = _my_coords()
        mine = _index_of(me)
        for k, src in enumerate(ins):
            pltpu.make_async_copy(src, outs[k].at[mine], sems[2].at[k]).start()
            for rel in (SIBLING,) + OTHER_CHIPS:
                self._copy(src, outs[k].at[mine], sems, rel, k, _peer(me, rel)).start()

    def forward(self, ins, outs, sems):
        me = _my_coords()
        sibling = _peer(me, SIBLING)
        for rel in OTHER_CHIPS:
            slot = _index_of(_peer(me, rel))
            for k in range(len(ins)):
                self._copy(ins[k], outs[k].at[slot], sems, rel, k, sibling).wait_recv()
                self._copy(outs[k].at[slot], outs[k].at[slot], sems, rel + 1, k, sibling).start()

    def finish(self, ins, outs, sems):
        me = _my_coords()
        mine = _index_of(me)
        sibling = _peer(me, SIBLING)
        for rel in (SIBLING,) + tuple(r + 1 for r in OTHER_CHIPS):
            slot = _index_of(_peer(me, rel))
            for k in range(len(ins)):
                self._copy(ins[k], outs[k].at[slot], sems, rel, k, sibling).wait_recv()
        for rel in range(1, N_DEV):
            for k in range(len(ins)):
                self._copy(ins[k], outs[k].at[mine], sems, rel, k, sibling).wait_send()
        for k, src in enumerate(ins):
            pltpu.make_async_copy(src, outs[k].at[mine], sems[2].at[k]).wait()


class Exchange:
    lead = 0

    def __init__(self, fulls):
        n = len(fulls)
        self.inputs = list(fulls)
        self.out_shape = [jax.ShapeDtypeStruct(f.shape, f.dtype) for f in fulls]
        self.scratch = [pltpu.SemaphoreType.DMA((N_DEV - 1, n)), pltpu.SemaphoreType.DMA((N_DEV - 1, n)),
                        pltpu.SemaphoreType.DMA((n,))]

    def start(self, ins, outs, sems):
        me = _my_coords()
        mine = _index_of(me)
        for k, src in enumerate(ins):
            pltpu.make_async_copy(src.at[mine], outs[k].at[mine], sems[2].at[k]).start()
            for rel in range(1, N_DEV):
                to = _peer(me, rel)
                Gather._copy(src.at[_index_of(to)], outs[k].at[mine], sems, rel, k, to).start()

    def forward(self, ins, outs, sems):
        pass

    def finish(self, ins, outs, sems):
        me = _my_coords()
        mine = _index_of(me)
        for rel in range(1, N_DEV):
            frm = _peer(me, rel)
            for k, src in enumerate(ins):
                Gather._copy(src.at[mine], outs[k].at[_index_of(frm)], sems, rel, k, frm).wait_recv()
        for rel in range(1, N_DEV):
            for k, src in enumerate(ins):
                Gather._copy(src.at[mine], outs[k].at[mine], sems, rel, k, _peer(me, rel)).wait_send()
        for k, src in enumerate(ins):
            pltpu.make_async_copy(src.at[mine], outs[k].at[mine], sems[2].at[k]).wait()


class Together:
    def __init__(self, plans):
        self.plans = list(plans)
        self.lead = max(p.lead for p in self.plans)
        self.inputs = [a for p in self.plans for a in p.inputs]
        self.out_shape = [s for p in self.plans for s in p.out_shape]
        self.scratch = [s for p in self.plans for s in p.scratch]

    def _each(self, ins, outs, sems):
        i = o = s = 0
        for p in self.plans:
            ni, no, ns = len(p.inputs), len(p.out_shape), len(p.scratch)
            yield p, ins[i:i + ni], outs[o:o + no], sems[s:s + ns]
            i, o, s = i + ni, o + no, s + ns

    def start(self, ins, outs, sems):
        for p, pi, po, ps in self._each(ins, outs, sems):
            p.start(pi, po, ps)

    def forward(self, ins, outs, sems):
        for p, pi, po, ps in self._each(ins, outs, sems):
            p.forward(pi, po, ps)

    def finish(self, ins, outs, sems):
        for p, pi, po, ps in self._each(ins, outs, sems):
            p.finish(pi, po, ps)

    def split(self, results):
        out, o = [], 0
        for p in self.plans:
            out.append(results[o:o + len(p.out_shape)])
            o += len(p.out_shape)
        return out


def _hosted_call(body, *, name, grid, in_specs, out_specs, out_shape, scratch_shapes, args, comm=None):
    sem = ("arbitrary",) * len(grid)
    if comm is None:
        res = pl.pallas_call(body, name=name, grid=grid, in_specs=in_specs, out_specs=out_specs, out_shape=out_shape,
                             scratch_shapes=scratch_shapes, compiler_params=_params(sem))(*args)
        return list(res), []
    n_in, n_out, n_scr = len(in_specs), len(out_specs), len(scratch_shapes)
    n_cin, n_cout = len(comm.inputs), len(comm.out_shape)
    n_steps = math.prod(grid)

    def hosted(*refs):
        ins, refs = refs[:n_in], refs[n_in:]
        cins, refs = refs[:n_cin], refs[n_cin:]
        outs, refs = refs[:n_out], refs[n_out:]
        couts, refs = refs[:n_cout], refs[n_cout:]
        scr, csems = refs[:n_scr], refs[n_scr:]
        step = 0
        for a, g in enumerate(grid):
            step = step * g + pl.program_id(a)

        @pl.when(step == 0)
        def _():
            comm.start(cins, couts, csems)

        body(*ins, *outs, *scr)

        @pl.when(step == max(n_steps - 1 - comm.lead, 0))
        def _():
            comm.forward(cins, couts, csems)

        @pl.when(step == n_steps - 1)
        def _():
            comm.finish(cins, couts, csems)

    any_spec = pl.BlockSpec(memory_space=pl.ANY)
    res = pl.pallas_call(
        hosted, name=name, grid=grid, in_specs=list(in_specs) + [any_spec] * n_cin,
        out_specs=list(out_specs) + [any_spec] * n_cout, out_shape=list(out_shape) + comm.out_shape,
        scratch_shapes=list(scratch_shapes) + comm.scratch,
        compiler_params=pltpu.CompilerParams(dimension_semantics=sem, vmem_limit_bytes=VMEM_LIMIT, has_side_effects=True),
    )(*args, *comm.inputs)
    return list(res[:n_out]), list(res[n_out:])


def run_comm(comm, *, name):
    n_cin, n_cout = len(comm.inputs), len(comm.out_shape)

    def body(*refs):
        cins, couts, csems = refs[:n_cin], refs[n_cin:n_cin + n_cout], refs[n_cin + n_cout:]
        comm.start(cins, couts, csems)
        comm.forward(cins, couts, csems)
        comm.finish(cins, couts, csems)

    any_spec = pl.BlockSpec(memory_space=pl.ANY)
    return pl.pallas_call(
        body, name=name, in_specs=[any_spec] * n_cin, out_specs=[any_spec] * n_cout, out_shape=comm.out_shape,
        scratch_shapes=comm.scratch, compiler_params=pltpu.CompilerParams(has_side_effects=True),
    )(*comm.inputs)


def mixer_fwd(x, g1, winT, convw, convb, clng, clnb, wpw, slng, slnb, wm, bias, wbd, pscale, wout, *, seq, tm,
              comm=None):
    T = x.shape[0]
    tiles_per_seq = seq // tm
    n_chunks = tm // CHUNK

    def body(x_ref, g1_ref, winT_ref, convw_ref, convb_ref, clng_ref, clnb_ref, wpw_ref, slng_ref, slnb_ref,
             wm_ref, bias_ref, wbd_ref, pscale_ref, wout_ref,
             z_ref, ycv_ref, p_ref, mix_ref, x1_ref, ybuf, zcbuf, *pbufs):
        i = pl.program_id(0)
        tile_in_seq = i % tiles_per_seq

        @pl.when(tile_in_seq == 0)
        def _():
            ybuf[0:CONV_HALO, :] = jnp.zeros((CONV_HALO, D_A), F32)
            zcbuf[0:POOL_HALO, :] = jnp.zeros((POOL_HALO, D_C), F32)

        x = x_ref[...]
        r = lax.rsqrt(jnp.mean(x * x, axis=-1, keepdims=True) + RMS_EPS)
        h = (x * r * g1_ref[...]).astype(BF16)
        z = _dot_nt(h, winT_ref[...])
        z_ref[...] = z

        y = z[:, 0:D_A] * _sigmoid(z[:, D_A:2 * D_A])
        ybuf[CONV_HALO:CONV_HALO + tm, :] = y
        acc = jnp.zeros((tm, D_A), F32) + convb_ref[...]
        for k, rows in _shifted_taps(ybuf, CONV_HALO - (CONV_WIDTH - 1), CONV_WIDTH, tm):
            acc = acc + convw_ref[k:k + 1, :] * rows
        ybuf[0:CONV_HALO, :] = ybuf[tm:tm + CONV_HALO, :]
        ycv_ref[...] = acc
        xhat, _ = _ln_stats(acc)
        ln = xhat * clng_ref[...] + clnb_ref[...]
        s = ln * _sigmoid(ln)
        ya = _dot(s.astype(BF16), wpw_ref[...])

        gb, _ = _gelu_and_grad(z[:, 2 * D_A:2 * D_A + 2 * D_B])
        u = gb[:, 0:D_B]
        vhat, _ = _ln_stats(gb[:, D_B:2 * D_B])
        vn = vhat * slng_ref[...] + slnb_ref[...]
        masks = _head_masks(D_B)
        yb_parts = []
        for c in range(n_chunks):
            vn_c = vn[c * CHUNK:(c + 1) * CHUNK, :]
            sg = bias_ref[...]
            for hh in range(N_HEADS_B):
                sg = sg + _dot(wm_ref[hh], jnp.where(masks[hh], vn_c, 0.0).astype(BF16))
            yb_parts.append(u[c * CHUNK:(c + 1) * CHUNK, :] * sg)
        yb = jnp.concatenate(yb_parts, axis=0) if n_chunks > 1 else yb_parts[0]

        zc = z[:, 2 * D_A + 2 * D_B:D_IN]
        zcbuf[POOL_HALO:POOL_HALO + tm, :] = zc
        sums = [v[POOL_HALO - SUBLANES * (l + 1):POOL_HALO - SUBLANES * (l + 1) + tm, :]
                for l, v in enumerate(_window_sums_back(zcbuf, pbufs, POOL_HALO + tm))]
        zcbuf[0:POOL_HALO, :] = zcbuf[tm:tm + POOL_HALO, :]
        pos = (tile_in_seq * tm + lax.broadcasted_iota(jnp.int32, (tm, 1), 0)).astype(F32)
        p = _pool_select(sums, D_C) / _pool_counts(pos) - zc
        p_ref[...] = p
        yc = _dot(p.astype(BF16), wbd_ref[...]) * pscale_ref[...]

        mix = jnp.concatenate([ya, yb, yc], axis=1).astype(BF16)
        mix_ref[...] = mix
        x1_ref[...] = x + _dot(mix, wout_ref[...])

    row = lambda w: pl.BlockSpec((tm, w), lambda i: (i, 0))
    return _hosted_call(
        body, name="mixer_fwd", grid=(T // tm,),
        in_specs=[row(D_MODEL), _full((1, D_MODEL)), _full((D_IN, D_MODEL)), _full((CONV_WIDTH, D_A)),
                  _full((1, D_A)), _full((1, D_A)), _full((1, D_A)), _full((D_A, D_A)), _full((1, D_B)), _full((1, D_B)),
                  _full((N_HEADS_B, CHUNK, CHUNK)), _full((CHUNK, D_B)), _full((D_C, D_C)), _full((1, D_C)),
                  _full((D_MODEL, D_MODEL))],
        out_specs=[row(D_IN), row(D_A), row(D_C), row(D_MODEL), row(D_MODEL)],
        out_shape=[jax.ShapeDtypeStruct((T, D_IN), F32), jax.ShapeDtypeStruct((T, D_A), F32),
                   jax.ShapeDtypeStruct((T, D_C), F32), jax.ShapeDtypeStruct((T, D_MODEL), BF16),
                   jax.ShapeDtypeStruct((T, D_MODEL), F32)],
        scratch_shapes=[pltpu.VMEM((CONV_HALO + tm, D_A), F32)]
        + [pltpu.VMEM((POOL_HALO + tm, D_C), F32)] * len(POOL_WINDOWS),
        args=(x, g1, winT, convw, convb, clng, clnb, wpw, slng, slnb, wm, bias, wbd, pscale, wout), comm=comm)


def mixer_bwd(dx1, x, z, ycv, p, g1, winT, convw, clng, clnb, wpw, slng, slnb, wm, wmT, bias, wbd, pscale, wout,
              *, seq, tm, comm=None):
    T = x.shape[0]
    tiles_per_seq = seq // tm
    n_tiles = T // tm
    n_chunks = tm // CHUNK

    def body(dx1_ref, x_ref, z_ref, ycv_ref, p_ref, g1_ref, winT_ref, convw_ref, clng_ref, clnb_ref, wpw_ref,
             slng_ref, slnb_ref, wm_ref, wmT_ref, bias_ref, wbd_ref, pscale_ref, wout_ref,
             dx_ref, dz_ref, h_ref, s_ref, dya_ref,
             dg1_ref, dconvw_ref, dconvb_ref, dclng_ref, dclnb_ref, dslng_ref, dslnb_ref, dwm_ref, dbs_ref,
             dwbd_ref, dpscale_ref, dycbuf, dpcbuf, *pbufs):
        i = pl.program_id(0)
        tile_in_seq = (n_tiles - 1 - i) % tiles_per_seq

        @pl.when(i == 0)
        def _():
            for ref in (dg1_ref, dconvw_ref, dconvb_ref, dclng_ref, dclnb_ref, dslng_ref, dslnb_ref, dwm_ref,
                        dbs_ref, dwbd_ref, dpscale_ref):
                ref[...] = jnp.zeros(ref.shape, F32)

        @pl.when(tile_in_seq == tiles_per_seq - 1)
        def _():
            dycbuf[tm:tm + CONV_HALO, :] = jnp.zeros((CONV_HALO, D_A), F32)
            dpcbuf[tm:tm + POOL_HALO, :] = jnp.zeros((POOL_HALO, D_C), F32)

        dx1 = dx1_ref[...]
        z = z_ref[...]
        dmix = _dot_nt(dx1.astype(BF16), wout_ref[...])
        dya = dmix[:, 0:D_A]
        dyb = dmix[:, D_A:D_A + D_B]
        dyc = dmix[:, D_A + D_B:D_MODEL]

        p = p_ref[...]
        pb = p.astype(BF16)
        q = _dot(pb, wbd_ref[...])
        dpscale_ref[...] += jnp.sum(dyc * q, axis=0, keepdims=True)
        dq = (dyc * pscale_ref[...]).astype(BF16)
        dwbd_ref[...] += _dot_tn(pb, dq)
        dp = _dot_nt(dq, wbd_ref[...])
        pos = (tile_in_seq * tm + lax.broadcasted_iota(jnp.int32, (tm, 1), 0)).astype(F32)
        dpc = dp / _pool_counts(pos)
        dpcbuf[0:tm, :] = dpc
        sums = [v[0:tm, :] for v in _window_sums_ahead(dpcbuf, pbufs, tm + POOL_HALO)]
        dpcbuf[tm:tm + POOL_HALO, :] = dpcbuf[0:POOL_HALO, :]
        dzc = _pool_select(sums, D_C) - dp

        dya_b = dya.astype(BF16)
        dya_ref[...] = dya_b
        ds = _dot_nt(dya_b, wpw_ref[...])
        xhat, rstd = _ln_stats(ycv_ref[...])
        ln = xhat * clng_ref[...] + clnb_ref[...]
        sg = _sigmoid(ln)
        s_ref[...] = (ln * sg).astype(BF16)
        dln = ds * (sg * (1.0 + ln * (1.0 - sg)))
        dclng_ref[...] += jnp.sum(dln * xhat, axis=0, keepdims=True)
        dclnb_ref[...] += jnp.sum(dln, axis=0, keepdims=True)
        dycv = _ln_bwd(dln, xhat, rstd, clng_ref[...])
        dconvb_ref[...] += jnp.sum(dycv, axis=0, keepdims=True)
        a = z[:, 0:D_A]
        sgate = _sigmoid(z[:, D_A:2 * D_A])
        y = a * sgate
        dycbuf[0:tm, :] = dycv
        dy = jnp.zeros((tm, D_A), F32)
        for d, sh in _shifted_taps(dycbuf, 0, CONV_WIDTH, tm):
            k = CONV_WIDTH - 1 - d
            dy = dy + convw_ref[k:k + 1, :] * sh
            dconvw_ref[k:k + 1, :] += jnp.sum(y * sh, axis=0, keepdims=True)
        dycbuf[tm:tm + CONV_HALO, :] = dycbuf[0:CONV_HALO, :]
        da = dy * sgate
        dgate = dy * a * sgate * (1.0 - sgate)

        gb, dgb = _gelu_and_grad(z[:, 2 * D_A:2 * D_A + 2 * D_B])
        u = gb[:, 0:D_B]
        vhat, vrstd = _ln_stats(gb[:, D_B:2 * D_B])
        vn = vhat * slng_ref[...] + slnb_ref[...]
        masks = _head_masks(D_B)
        tril = (lax.broadcasted_iota(jnp.int32, (CHUNK, CHUNK), 0)
                >= lax.broadcasted_iota(jnp.int32, (CHUNK, CHUNK), 1))
        lane128 = lax.broadcasted_iota(jnp.int32, (1, CHUNK), 1)
        du_parts, dvn_parts = [], []
        for c in range(n_chunks):
            rows = slice(c * CHUNK, (c + 1) * CHUNK)
            vn_c = vn[rows, :]
            vh = [jnp.where(masks[hh], vn_c, 0.0).astype(BF16) for hh in range(N_HEADS_B)]
            sgc = bias_ref[...]
            for hh in range(N_HEADS_B):
                sgc = sgc + _dot(wm_ref[hh], vh[hh])
            dyb_c = dyb[rows, :]
            du_parts.append(dyb_c * sgc)
            dsg = dyb_c * u[rows, :]
            dvn_c = jnp.zeros((CHUNK, D_B), F32)
            dbs = jnp.zeros((CHUNK, CHUNK), F32)
            for hh in range(N_HEADS_B):
                dsg_h = jnp.where(masks[hh], dsg, 0.0)
                dsg_hb = dsg_h.astype(BF16)
                dwm_ref[hh] += jnp.where(tril, _dot_nt(dsg_hb, vh[hh]), 0.0)
                dvn_c = dvn_c + _dot(wmT_ref[hh], dsg_hb)
                dbs = dbs + jnp.where(lane128 == hh, jnp.sum(dsg_h, axis=1, keepdims=True), 0.0)
            dbs_ref[...] += dbs
            dvn_parts.append(dvn_c)
        du = jnp.concatenate(du_parts, axis=0) if n_chunks > 1 else du_parts[0]
        dvn = jnp.concatenate(dvn_parts, axis=0) if n_chunks > 1 else dvn_parts[0]
        dslng_ref[...] += jnp.sum(dvn * vhat, axis=0, keepdims=True)
        dslnb_ref[...] += jnp.sum(dvn, axis=0, keepdims=True)
        dv = _ln_bwd(dvn, vhat, vrstd, slng_ref[...])
        dzb = jnp.concatenate([du, dv], axis=1) * dgb

        dz = jnp.concatenate([da, dgate, dzb, dzc], axis=1).astype(BF16)
        dz_ref[...] = dz
        dh = _dot(dz, winT_ref[...])
        x = x_ref[...]
        r = lax.rsqrt(jnp.mean(x * x, axis=-1, keepdims=True) + RMS_EPS)
        xn = x * r
        h_ref[...] = (xn * g1_ref[...]).astype(BF16)
        dg1_ref[...] += jnp.sum(dh * xn, axis=0, keepdims=True)
        dx_ref[...] = dx1 + _rms_bwd(dh, xn, r, g1_ref[...])

    row = lambda w: pl.BlockSpec((tm, w), lambda i: (n_tiles - 1 - i, 0))
    acc_shapes = [(1, D_MODEL), (CONV_WIDTH, D_A), (1, D_A), (1, D_A), (1, D_A), (1, D_B), (1, D_B),
                  (N_HEADS_B, CHUNK, CHUNK), (CHUNK, CHUNK), (D_C, D_C), (1, D_C)]
    return _hosted_call(
        body, name="mixer_bwd", grid=(n_tiles,),
        in_specs=[row(D_MODEL), row(D_MODEL), row(D_IN), row(D_A), row(D_C),
                  _full((1, D_MODEL)), _full((D_IN, D_MODEL)), _full((CONV_WIDTH, D_A)), _full((1, D_A)), _full((1, D_A)),
                  _full((D_A, D_A)), _full((1, D_B)), _full((1, D_B)), _full((N_HEADS_B, CHUNK, CHUNK)),
                  _full((N_HEADS_B, CHUNK, CHUNK)), _full((CHUNK, D_B)), _full((D_C, D_C)), _full((1, D_C)),
                  _full((D_MODEL, D_MODEL))],
        out_specs=[row(D_MODEL), row(D_IN), row(D_MODEL), row(D_A), row(D_A)] + [_full(s) for s in acc_shapes],
        out_shape=[jax.ShapeDtypeStruct((T, D_MODEL), F32), jax.ShapeDtypeStruct((T, D_IN), BF16),
                   jax.ShapeDtypeStruct((T, D_MODEL), BF16), jax.ShapeDtypeStruct((T, D_A), BF16),
                   jax.ShapeDtypeStruct((T, D_A), BF16)] + [jax.ShapeDtypeStruct(s, F32) for s in acc_shapes],
        scratch_shapes=[pltpu.VMEM((tm + CONV_HALO, D_A), F32)]
        + [pltpu.VMEM((tm + POOL_HALO, D_C), F32)] * len(POOL_WINDOWS),
        args=(dx1, x, z, ycv, p, g1, winT, convw, clng, clnb, wpw, slng, slnb, wm, wmT, bias, wbd, pscale, wout),
        comm=comm)


def ffn_fwd(x1, g2, wguT, wd, *, tm, th, comm=None):
    T = x1.shape[0]
    n_h = D_FF // th

    def body(x1_ref, g2_ref, wgu_ref, wd_ref, x2_ref, fac_ref, f_ref, h2_buf, acc):
        j = pl.program_id(1)

        @pl.when(j == 0)
        def _():
            x = x1_ref[...]
            r = lax.rsqrt(jnp.mean(x * x, axis=-1, keepdims=True) + RMS_EPS)
            h2_buf[...] = (x * r * g2_ref[...]).astype(BF16)
            acc[...] = x

        h2 = h2_buf[...]
        rows = pl.ds(pl.multiple_of(j * th, th), th)
        g = _dot_nt(h2, wgu_ref[0, rows, :])
        u = _dot_nt(h2, wgu_ref[1, rows, :])
        sg = _sigmoid(g)
        silu = g * sg
        fac_ref[0] = silu.astype(BF16)
        fac_ref[1] = (u * (sg * (1.0 + g * (1.0 - sg)))).astype(BF16)
        f = (silu * u).astype(BF16)
        f_ref[...] = f
        acc[...] += _dot(f, wd_ref[rows, :])

        @pl.when(j == n_h - 1)
        def _():
            x2_ref[...] = acc[...]

    return _hosted_call(
        body, name="ffn_fwd", grid=(T // tm, n_h),
        in_specs=[pl.BlockSpec((tm, D_MODEL), lambda i, j: (i, 0)), _full((1, D_MODEL)),
                  _full((2, D_FF, D_MODEL)), _full((D_FF, D_MODEL))],
        out_specs=[pl.BlockSpec((tm, D_MODEL), lambda i, j: (i, 0)),
                   pl.BlockSpec((2, tm, th), lambda i, j: (0, i, j)), pl.BlockSpec((tm, th), lambda i, j: (i, j))],
        out_shape=[jax.ShapeDtypeStruct((T, D_MODEL), F32), jax.ShapeDtypeStruct((2, T, D_FF), BF16),
                   jax.ShapeDtypeStruct((T, D_FF), BF16)],
        scratch_shapes=[pltpu.VMEM((tm, D_MODEL), BF16), pltpu.VMEM((tm, D_MODEL), F32)],
        args=(x1, g2, wguT, wd), comm=comm)


def ffn_bwd(dx2, x1, fac, g2, wguT, wd, *, tm, th, comm=None):
    T = x1.shape[0]
    n_h = D_FF // th

    def body(dx2_ref, x1_ref, fac_ref, g2_ref, wgu_ref, wd_ref, dx1_ref, h2_ref, dgu_ref, dg2_ref, acc):
        i = pl.program_id(0)
        j = pl.program_id(1)

        @pl.when((i == 0) & (j == 0))
        def _():
            dg2_ref[...] = jnp.zeros(dg2_ref.shape, F32)

        dx2 = dx2_ref[...]
        rows = pl.ds(pl.multiple_of(j * th, th), th)
        df = _dot_nt(dx2.astype(BF16), wd_ref[rows, :])
        dup = (df * fac_ref[0].astype(F32)).astype(BF16)
        dgate = (df * fac_ref[1].astype(F32)).astype(BF16)
        dgu_ref[0] = dgate
        dgu_ref[1] = dup

        @pl.when(j == 0)
        def _():
            acc[...] = jnp.zeros(acc.shape, F32)

        acc[...] += _dot(dgate, wgu_ref[0, rows, :]) + _dot(dup, wgu_ref[1, rows, :])

        @pl.when(j == n_h - 1)
        def _():
            x = x1_ref[...]
            r = lax.rsqrt(jnp.mean(x * x, axis=-1, keepdims=True) + RMS_EPS)
            xn = x * r
            dh = acc[...]
            h2_ref[...] = (xn * g2_ref[...]).astype(BF16)
            dg2_ref[...] += jnp.sum(dh * xn, axis=0, keepdims=True)
            dx1_ref[...] = dx2 + _rms_bwd(dh, xn, r, g2_ref[...])

    return _hosted_call(
        body, name="ffn_bwd", grid=(T // tm, n_h),
        in_specs=[pl.BlockSpec((tm, D_MODEL), lambda i, j: (i, 0)), pl.BlockSpec((tm, D_MODEL), lambda i, j: (i, 0)),
                  pl.BlockSpec((2, tm, th), lambda i, j: (0, i, j)), _full((1, D_MODEL)),
                  _full((2, D_FF, D_MODEL)), _full((D_FF, D_MODEL))],
        out_specs=[pl.BlockSpec((tm, D_MODEL), lambda i, j: (i, 0)), pl.BlockSpec((tm, D_MODEL), lambda i, j: (i, 0)),
                   pl.BlockSpec((2, tm, th), lambda i, j: (0, i, j)), _full((1, D_MODEL))],
        out_shape=[jax.ShapeDtypeStruct((T, D_MODEL), F32), jax.ShapeDtypeStruct((T, D_MODEL), BF16),
                   jax.ShapeDtypeStruct((2, T, D_FF), BF16), jax.ShapeDtypeStruct((1, D_MODEL), F32)],
        scratch_shapes=[pltpu.VMEM((tm, D_MODEL), F32)],
        args=(dx2, x1, fac, g2, wguT, wd), comm=comm)


def head_fwd_bwd(x, target, fg, *, tm):
    T = x.shape[0]
    n_tiles = T // tm

    def body(x_ref, t_ref, fg_ref, loss_ref, dx_ref, dfg_ref, lacc):
        i = pl.program_id(0)

        @pl.when(i == 0)
        def _():
            lacc[...] = jnp.zeros(lacc.shape, F32)
            dfg_ref[...] = jnp.zeros(dfg_ref.shape, F32)

        x = x_ref[...]
        r = lax.rsqrt(jnp.mean(x * x, axis=-1, keepdims=True) + RMS_EPS)
        xn = x * r
        e = xn * fg_ref[...] - t_ref[...]
        lacc[...] += jnp.sum(e * e, axis=0, keepdims=True)
        dy = e * (1.0 / D_MODEL)
        dfg_ref[...] += jnp.sum(dy * xn, axis=0, keepdims=True)
        dx_ref[...] = _rms_bwd(dy, xn, r, fg_ref[...])

        @pl.when(i == n_tiles - 1)
        def _():
            loss_ref[...] = jnp.sum(lacc[...], axis=1, keepdims=True) * (0.5 / D_MODEL)

    row = pl.BlockSpec((tm, D_MODEL), lambda i: (i, 0))
    return pl.pallas_call(
        body, name="head_fwd_bwd", grid=(n_tiles,),
        in_specs=[row, row, _full((1, D_MODEL))],
        out_specs=[_full((1, 1)), row, _full((1, D_MODEL))],
        out_shape=[jax.ShapeDtypeStruct((1, 1), F32), jax.ShapeDtypeStruct((T, D_MODEL), F32),
                   jax.ShapeDtypeStruct((1, D_MODEL), F32)],
        scratch_shapes=[pltpu.VMEM((1, D_MODEL), F32)],
        compiler_params=_params(("arbitrary",)),
    )(x, target, fg)


def wgrad(a, b, *, tmo, tk, name, comm=None):
    G, T, M = a.shape
    N = b.shape[1]
    n_k = T // tk

    def body(a_ref, b_ref, o_ref, acc):
        k = pl.program_id(2)
        if n_k == 1:
            o_ref[0] = _dot_tn(a_ref[0].astype(BF16), b_ref[...].astype(BF16)).astype(BF16)
            return

        @pl.when(k == 0)
        def _():
            acc[...] = jnp.zeros(acc.shape, F32)

        acc[...] += _dot_tn(a_ref[0].astype(BF16), b_ref[...].astype(BF16))

        @pl.when(k == n_k - 1)
        def _():
            o_ref[0] = acc[...].astype(BF16)

    (out,), got = _hosted_call(
        body, name=name, grid=(G, M // tmo, n_k),
        in_specs=[pl.BlockSpec((1, tk, tmo), lambda g, m, k: (g, k, m)),
                  pl.BlockSpec((tk, N), lambda g, m, k: (k, 0))],
        out_specs=[pl.BlockSpec((1, tmo, N), lambda g, m, k: (g, m, 0))],
        out_shape=[jax.ShapeDtypeStruct((G, M, N), BF16)],
        scratch_shapes=[pltpu.VMEM((tmo, N), F32)],
        args=(a, b), comm=comm)
    return out, got


def sum_partials(gathered):
    n = len(gathered)

    def body(*refs):
        for in_ref, out_ref in zip(refs[:n], refs[n:]):
            total = in_ref[0]
            for d in range(1, N_DEV):
                total = total + in_ref[d]
            out_ref[...] = total

    vmem = pl.BlockSpec(memory_space=pltpu.VMEM)
    return pl.pallas_call(
        body, name="sum_partials", in_specs=[vmem] * n, out_specs=[vmem] * n,
        out_shape=[jax.ShapeDtypeStruct(g.shape[1:], F32) for g in gathered],
        compiler_params=pltpu.CompilerParams(vmem_limit_bytes=VMEM_LIMIT),
    )(*gathered)


_ADAM_C1 = 1.0 - ADAM_B1 ** ADAM_STEP
_ADAM_C2 = 1.0 - ADAM_B2 ** ADAM_STEP


def _adamw_math(w, g, m, v):
    m = ADAM_B1 * m + (1.0 - ADAM_B1) * g
    v = ADAM_B2 * v + (1.0 - ADAM_B2) * (g * g)
    m_hat = m / _ADAM_C1
    v_hat = v / _ADAM_C2
    delta = -ADAM_LR * (m_hat / (jnp.sqrt(v_hat) + ADAM_EPS) + ADAM_WD * w)
    return delta, m, v


def adamw_sharded(parts, w, m, v, *, tr, name, comm=None):
    _, R, C = parts[0].shape

    def body(p0_ref, p1_ref, w_ref, m_ref, v_ref, g_ref, d_ref, nm_ref, nv_ref):
        def update(p_ref):
            g = p_ref[0].astype(F32)
            for d in range(1, N_DEV):
                g = g + p_ref[d].astype(F32)
            delta, nm, nv = _adamw_math(w_ref[0], g, m_ref[0], v_ref[0])
            g_ref[0] = g
            d_ref[0] = delta
            nm_ref[0] = nm
            nv_ref[0] = nv

        @pl.when(pl.program_id(0) == 0)
        def _():
            update(p0_ref)

        @pl.when(pl.program_id(0) == 1)
        def _():
            update(p1_ref)

    n_i = R // tr
    p_specs = [pl.BlockSpec((N_DEV, tr, C), lambda l, i: (0, jnp.where(l == 0, i, n_i - 1), 0)),
               pl.BlockSpec((N_DEV, tr, C), lambda l, i: (0, jnp.where(l == 1, i, 0), 0))]
    o_spec = pl.BlockSpec((1, tr, C), lambda l, i: (l, i, 0))
    return _hosted_call(
        body, name=name, grid=(DEPTH, n_i),
        in_specs=p_specs + [o_spec, o_spec, o_spec], out_specs=[o_spec] * 4,
        out_shape=[jax.ShapeDtypeStruct(w.shape, F32)] * 4, scratch_shapes=[],
        args=(parts[0], parts[1], w, m, v), comm=comm)


def adamw_small(gs, ws, ms, vs):
    n = len(gs)

    def body(*refs):
        g_refs, w_refs, m_refs, v_refs = refs[:n], refs[n:2 * n], refs[2 * n:3 * n], refs[3 * n:4 * n]
        d_refs, nm_refs, nv_refs = refs[4 * n:5 * n], refs[5 * n:6 * n], refs[6 * n:]
        for k in range(n):
            delta, nm, nv = _adamw_math(w_refs[k][...], g_refs[k][...], m_refs[k][...], v_refs[k][...])
            d_refs[k][...] = delta
            nm_refs[k][...] = nm
            nv_refs[k][...] = nv

    vmem = pl.BlockSpec(memory_space=pltpu.VMEM)
    res = pl.pallas_call(
        body, name="adamw_small", in_specs=[vmem] * (4 * n), out_specs=[vmem] * (3 * n),
        out_shape=[jax.ShapeDtypeStruct(w.shape, F32) for w in ws] * 3,
        compiler_params=pltpu.CompilerParams(vmem_limit_bytes=VMEM_LIMIT),
    )(*gs, *ws, *ms, *vs)
    return res[:n], res[n:2 * n], res[2 * n:]


def _pack(arrays, row_multiple):
    flat = jnp.concatenate([a.reshape(-1) for a in arrays])
    rows = -(-flat.shape[0] // (LANES * row_multiple)) * row_multiple
    return jnp.pad(flat, (0, rows * LANES - flat.shape[0])).reshape(rows, LANES)


def _unpack(buf, shapes):
    flat = buf.reshape(-1)
    out, off = [], 0
    for s in shapes:
        n = math.prod(s)
        out.append(flat[off:off + n].reshape(s))
        off += n
    return out


def _block_diag(w_pool):
    G, d, _ = w_pool.shape
    eye = jnp.eye(G, dtype=w_pool.dtype)
    return (eye[:, None, :, None] * w_pool[:, :, None, :]).reshape(G * d, G * d)


def kernel(x, norm1_g, w_in, conv_w, conv_b, conv_ln_g, conv_ln_b, w_pw, sg_ln_g, sg_ln_b, w_s, b_s, w_pool, pool_scale, w_out, norm2_g, w_gate_up, w_down, final_g, loss_target, m_norm1_g, m_w_in, m_conv_w, m_conv_b, m_conv_ln_g, m_conv_ln_b, m_w_pw, m_sg_ln_g, m_sg_ln_b, m_w_s, m_b_s, m_w_pool, m_pool_scale, m_w_out, m_norm2_g, m_w_gate_up, m_w_down, m_final_g, v_norm1_g, v_w_in, v_conv_w, v_conv_b, v_conv_ln_g, v_conv_ln_b, v_w_pw, v_sg_ln_g, v_sg_ln_b, v_w_s, v_b_s, v_w_pool, v_pool_scale, v_w_out, v_norm2_g, v_w_gate_up, v_w_down, v_final_g):
    b_loc, seq, _ = x.shape
    T = b_loc * seq
    tm_mix = min(256, seq)
    tm_ffn_fwd = min(512, T)
    tm_ffn_bwd = min(512, T)
    tm_head = min(512, T)
    tk = min(2048, T)
    tk_f32 = min(1024, T)
    th = D_FF // 2
    cw = conv_w.shape[2]
    my_index = _index_of(_my_coords())

    xf = x.reshape(T, D_MODEL)
    tgt = loss_target.reshape(T, D_MODEL)

    mixer_shards = [[w_in[l].T.astype(BF16), w_out[l].astype(BF16), w_pw[l].astype(BF16)] for l in range(DEPTH)]
    ffn_shards = [[w_gate_up[l].T.astype(BF16), w_down[l].astype(BF16)] for l in range(DEPTH)]

    tril = jnp.tril(jnp.ones((CHUNK, CHUNK), dtype=bool))
    layers = []
    for l in range(DEPTH):
        wm = jnp.where(tril[None], w_s[l], 0.0).astype(BF16)
        layers.append(dict(
            g1=norm1_g[l][None], convb=conv_b[l][None], clng=conv_ln_g[l][None], clnb=conv_ln_b[l][None],
            slng=sg_ln_g[l][None], slnb=sg_ln_b[l][None], wm=wm, wmT=jnp.swapaxes(wm, 1, 2),
            bias=jnp.repeat(b_s[l].T, HEAD_DIM_B, axis=1), wbd=_block_diag(w_pool[l]).astype(BF16),
            pscale=pool_scale[l][None], g2=norm2_g[l][None]))

    def set_mixer_weights(l, g_in, g_out, g_pw):
        layers[l].update(winT=g_in.reshape(D_IN, D_MODEL), wout=g_out.reshape(D_MODEL, D_MODEL), wpw=g_pw.reshape(D_A, D_A))

    def set_ffn_weights(l, g_gu, g_d):
        layers[l].update(wguT=g_gu.reshape(2, D_FF, D_MODEL), wd=g_d.reshape(D_FF, D_MODEL))

    first = run_comm(Gather(mixer_shards[0] + [conv_w.reshape(DEPTH * CONV_WIDTH, cw).T]), name="gather_first")
    set_mixer_weights(0, *first[:3])
    convw_full = first[3].reshape(D_A, DEPTH * CONV_WIDTH).T.reshape(DEPTH, CONV_WIDTH, D_A)
    for l in range(DEPTH):
        layers[l]["convw"] = convw_full[l]

    saved = []
    cur = xf
    for l in range(DEPTH):
        w = layers[l]
        (z, ycv, p, mix, x1), got = mixer_fwd(
            cur, w["g1"], w["winT"], w["convw"], w["convb"], w["clng"], w["clnb"], w["wpw"], w["slng"], w["slnb"], w["wm"],
            w["bias"], w["wbd"], w["pscale"], w["wout"], seq=seq, tm=tm_mix,
            comm=Gather(ffn_shards[l], lead=0) if l == 0 else Gather(ffn_shards[l][:1]))
        if l == 0:
            set_ffn_weights(l, *got)
        else:
            set_ffn_weights(l, got[0], early_wd)
        (x2, fac, f), got = ffn_fwd(x1, w["g2"], w["wguT"], w["wd"], tm=tm_ffn_fwd, th=th,
                                comm=Gather(mixer_shards[l + 1] + ffn_shards[l + 1][1:]) if l + 1 < DEPTH else None)
        if l + 1 < DEPTH:
            set_mixer_weights(l + 1, *got[:3])
            early_wd = got[3]
        saved.append((cur, z, ycv, p, mix, x1, fac, f))
        cur = x2
    loss_part, dx, dfg = head_fwd_bwd(cur, tgt, final_g[None], tm=tm_head)

    blocks = {"gu": (2 * D_FF // N_DEV, D_MODEL), "d": (D_FF // N_DEV, D_MODEL), "in": (D_IN // N_DEV, D_MODEL),
              "out": (D_MODEL // N_DEV, D_MODEL), "pw": (D_A // N_DEV, D_A)}
    by_device = lambda kind, g: g.reshape((N_DEV,) + blocks[kind])
    small = [None] * DEPTH
    parts = {}
    packs = [None] * DEPTH

    pending = None
    for l in reversed(range(DEPTH)):
        w = layers[l]
        x0, z, ycv, p, mix, x1, fac, f = saved[l]
        plan = Together([Exchange(pending[1]), Gather([pending[2]])]) if pending else None
        (dx1, h2, dgu, dg2), got = ffn_bwd(dx, x1, fac, w["g2"], w["wguT"], w["wd"], tm=tm_ffn_bwd, th=th, comm=plan)
        if pending:
            got_parts, (packs[l + 1],) = plan.split(got)
            parts.update(zip(pending[0], got_parts))
        gw_d, _ = wgrad(f[None], dx, tmo=th, tk=tk_f32, name="wgrad_down")
        last = l == 0
        gw_gu, got = wgrad(dgu, h2, tmo=th, tk=tk, name="wgrad_gate_up", comm=Exchange([by_device("d", gw_d)]))
        parts[("d", l)], = got
        outs, got = mixer_bwd(
            dx1, x0, z, ycv, p, w["g1"], w["winT"], w["convw"], w["clng"], w["clnb"], w["wpw"], w["slng"], w["slnb"],
            w["wm"], w["wmT"], w["bias"], w["wbd"], w["pscale"], w["wout"], seq=seq, tm=tm_mix,
            comm=Exchange([by_device("gu", gw_gu)]))
        parts[("gu", l)], = got
        (dx, dz, h, s, dya, dg1, dconvw, dconvb, dclng, dclnb, dslng, dslnb, dwm, dbs, dwbd, dpscale) = outs
        small[l] = [dg1, dconvw, dconvb, dclng, dclnb, dslng, dslnb, dwm, dbs, dwbd, dpscale, dg2]
        if last:
            small[l] += [dfg, loss_part]
        pack = _pack(small[l], SUBLANES)
        if last:
            gw_in, _ = wgrad(dz[None], h, tmo=D_IN // 2, tk=tk, name="wgrad_in")
            plan = Together([Exchange([by_device("in", gw_in)]), Gather([pack], lead=1)])
            gw_out, got = wgrad(mix[None], dx1, tmo=D_MODEL, tk=tk_f32, name="wgrad_out", comm=plan)
            (parts[("in", l)],), (packs[l],) = plan.split(got)
            gw_pw, got = wgrad(s[None], dya, tmo=D_A, tk=tk, name="wgrad_pw", comm=Exchange([by_device("out", gw_out)]))
            parts[("out", l)], = got
            parts[("pw", l)], = run_comm(Exchange([by_device("pw", gw_pw)]), name="exchange_last")
        else:
            gw_out, _ = wgrad(mix[None], dx1, tmo=D_MODEL, tk=tk_f32, name="wgrad_out")
            gw_in, got = wgrad(dz[None], h, tmo=D_IN // 2, tk=tk, name="wgrad_in", comm=Exchange([by_device("out", gw_out)]))
            parts[("out", l)], = got
            gw_pw, _ = wgrad(s[None], dya, tmo=D_A, tk=tk, name="wgrad_pw")
            pending = ([("in", l), ("pw", l)], [by_device("in", gw_in), by_device("pw", gw_pw)], pack)
    grad_x = dx.reshape(x.shape)
    p_in, p_gu, p_d, p_out, p_pw = [[parts[(k, l)] for l in range(DEPTH)] for k in ("in", "gu", "d", "out", "pw")]

    sums = []
    for l, summed in enumerate(sum_partials(packs)):
        sums.append(_unpack(summed, [a.shape for a in small[l]]))
    loss = sums[0][-1][0, 0]
    dfg_sum = sums[0][-2]
    per_layer = 12
    sums = [a for l in range(DEPTH) for a in sums[l][:per_layer]]
    g_small = {k: [] for k in ("norm1_g", "conv_w", "conv_b", "conv_ln_g", "conv_ln_b", "sg_ln_g", "sg_ln_b", "w_s", "b_s",
                               "w_pool", "pool_scale", "norm2_g")}
    for l in range(DEPTH):
        dg1, dconvw, dconvb, dclng, dclnb, dslng, dslnb, dwm, dbs, dwbd, dpscale, dg2 = sums[per_layer * l:per_layer * (l + 1)]
        g_small["norm1_g"].append(dg1[0])
        g_small["conv_w"].append(lax.dynamic_slice_in_dim(dconvw, my_index * cw, cw, axis=1))
        g_small["conv_b"].append(dconvb[0])
        g_small["conv_ln_g"].append(dclng[0])
        g_small["conv_ln_b"].append(dclnb[0])
        g_small["sg_ln_g"].append(dslng[0])
        g_small["sg_ln_b"].append(dslnb[0])
        g_small["w_s"].append(dwm)
        g_small["b_s"].append(dbs[:, :N_HEADS_B].T)
        g_small["w_pool"].append(jnp.stack([dwbd[g * GROUP_DIM_C:(g + 1) * GROUP_DIM_C, g * GROUP_DIM_C:(g + 1) * GROUP_DIM_C]
                                            for g in range(len(POOL_WINDOWS))]))
        g_small["pool_scale"].append(dpscale[0])
        g_small["norm2_g"].append(dg2[0])
    g_small = {k: jnp.stack(v) for k, v in g_small.items()}
    g_small["final_g"] = dfg_sum[0]

    t = lambda a: jnp.swapaxes(a, 1, 2)
    g_w_in, d_w_in, nm_w_in, nv_w_in = map(t, adamw_sharded(p_in, t(w_in), t(m_w_in), t(v_w_in), tr=D_IN // N_DEV // 2,
                                                            name="adamw_w_in")[0])
    g_w_gu, d_w_gu, nm_w_gu, nv_w_gu = map(t, adamw_sharded(p_gu, t(w_gate_up), t(m_w_gate_up), t(v_w_gate_up),
                                                            tr=2 * D_FF // N_DEV // 4, name="adamw_w_gate_up")[0])
    g_w_d, d_w_d, nm_w_d, nv_w_d = adamw_sharded(p_d, w_down, m_w_down, v_w_down, tr=D_FF // N_DEV // 2,
                                                 name="adamw_w_down")[0]
    g_w_out, d_w_out, nm_w_out, nv_w_out = adamw_sharded(p_out, w_out, m_w_out, v_w_out, tr=D_MODEL // N_DEV,
                                                         name="adamw_w_out")[0]
    g_w_pw, d_w_pw, nm_w_pw, nv_w_pw = adamw_sharded(p_pw, w_pw, m_w_pw, v_w_pw, tr=D_A // N_DEV, name="adamw_w_pw")[0]

    small_names = ["norm1_g", "conv_w", "conv_b", "conv_ln_g", "conv_ln_b", "sg_ln_g", "sg_ln_b", "w_s", "b_s", "w_pool",
                   "pool_scale", "norm2_g", "final_g"]
    small_w = dict(norm1_g=norm1_g, conv_w=conv_w, conv_b=conv_b, conv_ln_g=conv_ln_g, conv_ln_b=conv_ln_b, sg_ln_g=sg_ln_g,
                   sg_ln_b=sg_ln_b, w_s=w_s, b_s=b_s, w_pool=w_pool, pool_scale=pool_scale, norm2_g=norm2_g, final_g=final_g)
    small_m = dict(norm1_g=m_norm1_g, conv_w=m_conv_w, conv_b=m_conv_b, conv_ln_g=m_conv_ln_g, conv_ln_b=m_conv_ln_b,
                   sg_ln_g=m_sg_ln_g, sg_ln_b=m_sg_ln_b, w_s=m_w_s, b_s=m_b_s, w_pool=m_w_pool, pool_scale=m_pool_scale,
                   norm2_g=m_norm2_g, final_g=m_final_g)
    small_v = dict(norm1_g=v_norm1_g, conv_w=v_conv_w, conv_b=v_conv_b, conv_ln_g=v_conv_ln_g, conv_ln_b=v_conv_ln_b,
                   sg_ln_g=v_sg_ln_g, sg_ln_b=v_sg_ln_b, w_s=v_w_s, b_s=v_b_s, w_pool=v_w_pool, pool_scale=v_pool_scale,
                   norm2_g=v_norm2_g, final_g=v_final_g)
    two_d = lambda a: a[None] if a.ndim == 1 else a
    d_s, nm_s, nv_s = adamw_small(*[[two_d(d[k]) for k in small_names] for d in (g_small, small_w, small_m, small_v)])
    d_small = {k: a.reshape(small_w[k].shape) for k, a in zip(small_names, d_s)}
    nm_small = {k: a.reshape(small_w[k].shape) for k, a in zip(small_names, nm_s)}
    nv_small = {k: a.reshape(small_w[k].shape) for k, a in zip(small_names, nv_s)}

    order = ["norm1_g", "w_in", "conv_w", "conv_b", "conv_ln_g", "conv_ln_b", "w_pw", "sg_ln_g", "sg_ln_b", "w_s", "b_s",
             "w_pool", "pool_scale", "w_out", "norm2_g", "w_gate_up", "w_down", "final_g"]
    grads = dict(g_small, w_in=g_w_in, w_pw=g_w_pw, w_out=g_w_out, w_gate_up=g_w_gu, w_down=g_w_d)
    deltas = dict(d_small, w_in=d_w_in, w_pw=d_w_pw, w_out=d_w_out, w_gate_up=d_w_gu, w_down=d_w_d)
    new_m = dict(nm_small, w_in=nm_w_in, w_pw=nm_w_pw, w_out=nm_w_out, w_gate_up=nm_w_gu, w_down=nm_w_d)
    new_v = dict(nv_small, w_in=nv_w_in, w_pw=nv_w_pw, w_out=nv_w_out, w_gate_up=nv_w_gu, w_down=nv_w_d)
    return (loss, grad_x, *[grads[k] for k in order], *[deltas[k] for k in order], *[new_m[k] for k in order],
            *[new_v[k] for k in order])
```

```python
import functools
import math

import jax
import jax.numpy as jnp
from jax import lax
from jax.experimental import pallas as pl
from jax.experimental.pallas import tpu as pltpu

F32 = jnp.float32
BF16 = jnp.bfloat16

D_MODEL = 1024
D_A = 384
D_B = 384
D_C = 256
D_IN = 2 * D_A + 2 * D_B + D_C
N_HEADS_B = 4
HEAD_DIM_B = 96
POOL_WINDOWS = (2, 4, 8, 16)
GROUP_DIM_C = 64
CONV_WIDTH = 31
CHUNK = 128
D_FF = 2816
RMS_EPS = 1e-6
LN_EPS = 1e-5
DEPTH = 2
N_DEV = 8

ADAM_LR = 0.001
ADAM_B1 = 0.9
ADAM_B2 = 0.999
ADAM_EPS = 1e-08
ADAM_WD = 0.01
ADAM_STEP = 10

LANES = 128
SUBLANES = 8

CONV_HALO = 32
POOL_HALO = 32
assert POOL_WINDOWS == (2, 4, 8, 16) and POOL_HALO == SUBLANES * len(POOL_WINDOWS)

VMEM_LIMIT = 56 * 1024 * 1024

MESH_ID = pl.DeviceIdType.MESH


def _dot(a, b):
    return jnp.dot(a, b, preferred_element_type=F32)


def _dot_nt(a, b):
    return lax.dot_general(a, b, (((1,), (1,)), ((), ())), preferred_element_type=F32)


def _dot_tn(a, b):
    return lax.dot_general(a, b, (((0,), (0,)), ((), ())), preferred_element_type=F32)


def _sigmoid(x):
    return 0.5 * jnp.tanh(0.5 * x) + 0.5


def _shifted_taps(buf, first_row, n_shifts, tm):
    for phase in range(min(SUBLANES, n_shifts)):
        shifts = list(range(phase, n_shifts, SUBLANES))
        span = buf[first_row + phase:first_row + shifts[-1] + tm, :]
        for s in shifts:
            yield s, span[s - phase:s - phase + tm, :]


def _window_sums_back(x_ref, bufs, n_rows):
    out, src, w = [], x_ref, 1
    for l in range(len(POOL_WINDOWS)):
        lo = SUBLANES * (l + 1)
        cur = src[lo:n_rows, :] + src[lo - w:n_rows - w, :]
        out.append(cur)
        if l < len(bufs):
            bufs[l][lo:n_rows, :] = cur
            src = bufs[l]
        w *= 2
    return out


def _window_sums_ahead(x_ref, bufs, n_rows):
    out, src, w = [], x_ref, 1
    for l in range(len(POOL_WINDOWS)):
        hi = n_rows - SUBLANES * (l + 1)
        cur = src[0:hi, :] + src[w:hi + w, :]
        out.append(cur)
        if l < len(bufs):
            bufs[l][0:hi, :] = cur
            src = bufs[l]
        w *= 2
    return out


_GELU_C = math.sqrt(2.0 / math.pi)


def _gelu_and_grad(x):
    x2 = x * x
    inner = _GELU_C * (x + 0.044715 * x2 * x)
    t = jnp.tanh(inner)
    g = 0.5 * x * (1.0 + t)
    dg = 0.5 * (1.0 + t) + 0.5 * x * (1.0 - t * t) * _GELU_C * (1.0 + 3.0 * 0.044715 * x2)
    return g, dg


def _ln_stats(x):
    mu = jnp.mean(x, axis=-1, keepdims=True)
    xc = x - mu
    var = jnp.mean(xc * xc, axis=-1, keepdims=True)
    rstd = lax.rsqrt(var + LN_EPS)
    return xc * rstd, rstd


def _ln_bwd(dy, xhat, rstd, g):
    dxhat = dy * g
    return rstd * (dxhat - jnp.mean(dxhat, axis=-1, keepdims=True)
                   - xhat * jnp.mean(dxhat * xhat, axis=-1, keepdims=True))


def _rms_bwd(dh, xn, r, g):
    dxn = dh * g
    return r * (dxn - xn * jnp.mean(dxn * xn, axis=-1, keepdims=True))


def _head_masks(width):
    lane = lax.broadcasted_iota(jnp.int32, (1, width), 1)
    return [(lane >= h * HEAD_DIM_B) & (lane < (h + 1) * HEAD_DIM_B) for h in range(N_HEADS_B)]


def _pool_select(vals, width):
    lane = lax.broadcasted_iota(jnp.int32, (1, width), 1)
    out = vals[-1]
    for g in range(len(vals) - 2, -1, -1):
        out = jnp.where(lane < (g + 1) * GROUP_DIM_C, vals[g], out)
    return out


def _pool_counts(pos):
    return _pool_select([jnp.minimum(pos + 1.0, float(w)) for w in POOL_WINDOWS], D_C)


def _full(shape):
    n = len(shape)
    return pl.BlockSpec(shape, lambda *_: (0,) * n)


def _params(sem):
    return pltpu.CompilerParams(dimension_semantics=sem, vmem_limit_bytes=VMEM_LIMIT)


def _my_coords():
    return lax.axis_index("x"), lax.axis_index("y"), lax.axis_index("c")


def _peer(me, rel):
    x, y, c = me
    bx, by, bc = (rel >> 2) & 1, (rel >> 1) & 1, rel & 1
    return (1 - x if bx else x, 1 - y if by else y, 1 - c if bc else c)


def _index_of(dev):
    return 4 * dev[0] + 2 * dev[1] + dev[2]


FORWARD_LEAD = 2
SIBLING = 1
OTHER_CHIPS = (2, 4, 6)


class Gather:
    def __init__(self, shards, lead=FORWARD_LEAD):
        n = len(shards)
        self.lead = lead
        self.inputs = list(shards)
        self.out_shape = [jax.ShapeDtypeStruct((N_DEV,) + s.shape, s.dtype) for s in shards]
        self.scratch = [pltpu.SemaphoreType.DMA((N_DEV - 1, n)), pltpu.SemaphoreType.DMA((N_DEV - 1, n)),
                        pltpu.SemaphoreType.DMA((n,))]

    @staticmethod
    def _copy(src, dst, sems, rel, k, to):
        return pltpu.make_async_remote_copy(src_ref=src, dst_ref=dst, send_sem=sems[0].at[rel - 1, k],
                                            recv_sem=sems[1].at[rel - 1, k], device_id=to, device_id_type=MESH_ID)

    def start(self, ins, outs, sems):
        me = _my_coords()
        mine = _index_of(me)
        for k, src in enumerate(ins):
            pltpu.make_async_copy(src, outs[k].at[mine], sems[2].at[k]).start()
            for rel in (SIBLING,) + OTHER_CHIPS:
                self._copy(src, outs[k].at[mine], sems, rel, k, _peer(me, rel)).start()

    def forward(self, ins, outs, sems):
        me = _my_coords()
        sibling = _peer(me, SIBLING)
        for rel in OTHER_CHIPS:
            slot = _index_of(_peer(me, rel))
            for k in range(len(ins)):
                self._copy(ins[k], outs[k].at[slot], sems, rel, k, sibling).wait_recv()
                self._copy(outs[k].at[slot], outs[k].at[slot], sems, rel + 1, k, sibling).start()

    def finish(self, ins, outs, sems):
        me = _my_coords()
        mine = _index_of(me)
        sibling = _peer(me, SIBLING)
        for rel in (SIBLING,) + tuple(r + 1 for r in OTHER_CHIPS):
            slot = _index_of(_peer(me, rel))
            for k in range(len(ins)):
                self._copy(ins[k], outs[k].at[slot], sems, rel, k, sibling).wait_recv()
        for rel in range(1, N_DEV):
            for k in range(len(ins)):
                self._copy(ins[k], outs[k].at[mine], sems, rel, k, sibling).wait_send()
        for k, src in enumerate(ins):
            pltpu.make_async_copy(src, outs[k].at[mine], sems[2].at[k]).wait()


class Exchange:
    lead = 0

    def __init__(self, fulls):
        n = len(fulls)
        self.inputs = list(fulls)
        self.out_shape = [jax.ShapeDtypeStruct(f.shape, f.dtype) for f in fulls]
        self.scratch = [pltpu.SemaphoreType.DMA((N_DEV - 1, n)), pltpu.SemaphoreType.DMA((N_DEV - 1, n)),
                        pltpu.SemaphoreType.DMA((n,))]

    def start(self, ins, outs, sems):
        me = _my_coords()
        mine = _index_of(me)
        for k, src in enumerate(ins):
            pltpu.make_async_copy(src.at[mine], outs[k].at[mine], sems[2].at[k]).start()
            for rel in range(1, N_DEV):
                to = _peer(me, rel)
                Gather._copy(src.at[_index_of(to)], outs[k].at[mine], sems, rel, k, to).start()

    def forward(self, ins, outs, sems):
        pass

    def finish(self, ins, outs, sems):
        me = _my_coords()
        mine = _index_of(me)
        for rel in range(1, N_DEV):
            frm = _peer(me, rel)
            for k, src in enumerate(ins):
                Gather._copy(src.at[mine], outs[k].at[_index_of(frm)], sems, rel, k, frm).wait_recv()
        for rel in range(1, N_DEV):
            for k, src in enumerate(ins):
                Gather._copy(src.at[mine], outs[k].at[mine], sems, rel, k, _peer(me, rel)).wait_send()
        for k, src in enumerate(ins):
            pltpu.make_async_copy(src.at[mine], outs[k].at[mine], sems[2].at[k]).wait()


class Together:
    def __init__(self, plans):
        self.plans = list(plans)
        self.lead = max(p.lead for p in self.plans)
        self.inputs = [a for p in self.plans for a in p.inputs]
        self.out_shape = [s for p in self.plans for s in p.out_shape]
        self.scratch = [s for p in self.plans for s in p.scratch]

    def _each(self, ins, outs, sems):
        i = o = s = 0
        for p in self.plans:
            ni, no, ns = len(p.inputs), len(p.out_shape), len(p.scratch)
            yield p, ins[i:i + ni], outs[o:o + no], sems[s:s + ns]
            i, o, s = i + ni, o + no, s + ns

    def start(self, ins, outs, sems):
        for p, pi, po, ps in self._each(ins, outs, sems):
            p.start(pi, po, ps)

    def forward(self, ins, outs, sems):
        for p, pi, po, ps in self._each(ins, outs, sems):
            p.forward(pi, po, ps)

    def finish(self, ins, outs, sems):
        for p, pi, po, ps in self._each(ins, outs, sems):
            p.finish(pi, po, ps)

    def split(self, results):
        out, o = [], 0
        for p in self.plans:
            out.append(results[o:o + len(p.out_shape)])
            o += len(p.out_shape)
        return out


def _hosted_call(body, *, name, grid, in_specs, out_specs, out_shape, scratch_shapes, args, comm=None):
    sem = ("arbitrary",) * len(grid)
    if comm is None:
        res = pl.pallas_call(body, name=name, grid=grid, in_specs=in_specs, out_specs=out_specs, out_shape=out_shape,
                             scratch_shapes=scratch_shapes, compiler_params=_params(sem))(*args)
        return list(res), []
    n_in, n_out, n_scr = len(in_specs), len(out_specs), len(scratch_shapes)
    n_cin, n_cout = len(comm.inputs), len(comm.out_shape)
    n_steps = math.prod(grid)

    def hosted(*refs):
        ins, refs = refs[:n_in], refs[n_in:]
        cins, refs = refs[:n_cin], refs[n_cin:]
        outs, refs = refs[:n_out], refs[n_out:]
        couts, refs = refs[:n_cout], refs[n_cout:]
        scr, csems = refs[:n_scr], refs[n_scr:]
        step = 0
        for a, g in enumerate(grid):
            step = step * g + pl.program_id(a)

        @pl.when(step == 0)
        def _():
            comm.start(cins, couts, csems)

        body(*ins, *outs, *scr)

        @pl.when(step == max(n_steps - 1 - comm.lead, 0))
        def _():
            comm.forward(cins, couts, csems)

        @pl.when(step == n_steps - 1)
        def _():
            comm.finish(cins, couts, csems)

    any_spec = pl.BlockSpec(memory_space=pl.ANY)
    res = pl.pallas_call(
        hosted, name=name, grid=grid, in_specs=list(in_specs) + [any_spec] * n_cin,
        out_specs=list(out_specs) + [any_spec] * n_cout, out_shape=list(out_shape) + comm.out_shape,
        scratch_shapes=list(scratch_shapes) + comm.scratch,
        compiler_params=pltpu.CompilerParams(dimension_semantics=sem, vmem_limit_bytes=VMEM_LIMIT, has_side_effects=True),
    )(*args, *comm.inputs)
    return list(res[:n_out]), list(res[n_out:])


def run_comm(comm, *, name):
    n_cin, n_cout = len(comm.inputs), len(comm.out_shape)

    def body(*refs):
        cins, couts, csems = refs[:n_cin], refs[n_cin:n_cin + n_cout], refs[n_cin + n_cout:]
        comm.start(cins, couts, csems)
        comm.forward(cins, couts, csems)
        comm.finish(cins, couts, csems)

    any_spec = pl.BlockSpec(memory_space=pl.ANY)
    return pl.pallas_call(
        body, name=name, in_specs=[any_spec] * n_cin, out_specs=[any_spec] * n_cout, out_shape=comm.out_shape,
        scratch_shapes=comm.scratch, compiler_params=pltpu.CompilerParams(has_side_effects=True),
    )(*comm.inputs)


def mixer_fwd(x, g1, winT, convw, convb, clng, clnb, wpw, slng, slnb, wm, bias, wbd, pscale, wout, *, seq, tm,
              comm=None):
    T = x.shape[0]
    tiles_per_seq = seq // tm
    n_chunks = tm // CHUNK

    def body(x_ref, g1_ref, winT_ref, convw_ref, convb_ref, clng_ref, clnb_ref, wpw_ref, slng_ref, slnb_ref,
             wm_ref, bias_ref, wbd_ref, pscale_ref, wout_ref,
             z_ref, ycv_ref, p_ref, mix_ref, x1_ref, ybuf, zcbuf, *pbufs):
        i = pl.program_id(0)
        tile_in_seq = i % tiles_per_seq

        @pl.when(tile_in_seq == 0)
        def _():
            ybuf[0:CONV_HALO, :] = jnp.zeros((CONV_HALO, D_A), F32)
            zcbuf[0:POOL_HALO, :] = jnp.zeros((POOL_HALO, D_C), F32)

        x = x_ref[...]
        r = lax.rsqrt(jnp.mean(x * x, axis=-1, keepdims=True) + RMS_EPS)
        h = (x * r * g1_ref[...]).astype(BF16)
        z = _dot_nt(h, winT_ref[...])
        z_ref[...] = z

        y = z[:, 0:D_A] * _sigmoid(z[:, D_A:2 * D_A])
        ybuf[CONV_HALO:CONV_HALO + tm, :] = y
        acc = jnp.zeros((tm, D_A), F32) + convb_ref[...]
        for k, rows in _shifted_taps(ybuf, CONV_HALO - (CONV_WIDTH - 1), CONV_WIDTH, tm):
            acc = acc + convw_ref[k:k + 1, :] * rows
        ybuf[0:CONV_HALO, :] = ybuf[tm:tm + CONV_HALO, :]
        ycv_ref[...] = acc
        xhat, _ = _ln_stats(acc)
        ln = xhat * clng_ref[...] + clnb_ref[...]
        s = ln * _sigmoid(ln)
        ya = _dot(s.astype(BF16), wpw_ref[...])

        gb, _ = _gelu_and_grad(z[:, 2 * D_A:2 * D_A + 2 * D_B])
        u = gb[:, 0:D_B]
        vhat, _ = _ln_stats(gb[:, D_B:2 * D_B])
        vn = vhat * slng_ref[...] + slnb_ref[...]
        masks = _head_masks(D_B)
        yb_parts = []
        for c in range(n_chunks):
            vn_c = vn[c * CHUNK:(c + 1) * CHUNK, :]
            sg = bias_ref[...]
            for hh in range(N_HEADS_B):
                sg = sg + _dot(wm_ref[hh], jnp.where(masks[hh], vn_c, 0.0).astype(BF16))
            yb_parts.append(u[c * CHUNK:(c + 1) * CHUNK, :] * sg)
        yb = jnp.concatenate(yb_parts, axis=0) if n_chunks > 1 else yb_parts[0]

        zc = z[:, 2 * D_A + 2 * D_B:D_IN]
        zcbuf[POOL_HALO:POOL_HALO + tm, :] = zc
        sums = [v[POOL_HALO - SUBLANES * (l + 1):POOL_HALO - SUBLANES * (l + 1) + tm, :]
                for l, v in enumerate(_window_sums_back(zcbuf, pbufs, POOL_HALO + tm))]
        zcbuf[0:POOL_HALO, :] = zcbuf[tm:tm + POOL_HALO, :]
        pos = (tile_in_seq * tm + lax.broadcasted_iota(jnp.int32, (tm, 1), 0)).astype(F32)
        p = _pool_select(sums, D_C) / _pool_counts(pos) - zc
        p_ref[...] = p
        yc = _dot(p.astype(BF16), wbd_ref[...]) * pscale_ref[...]

        mix = jnp.concatenate([ya, yb, yc], axis=1).astype(BF16)
        mix_ref[...] = mix
        x1_ref[...] = x + _dot(mix, wout_ref[...])

    row = lambda w: pl.BlockSpec((tm, w), lambda i: (i, 0))
    return _hosted_call(
        body, name="mixer_fwd", grid=(T // tm,),
        in_specs=[row(D_MODEL), _full((1, D_MODEL)), _full((D_IN, D_MODEL)), _full((CONV_WIDTH, D_A)),
                  _full((1, D_A)), _full((1, D_A)), _full((1, D_A)), _full((D_A, D_A)), _full((1, D_B)), _full((1, D_B)),
                  _full((N_HEADS_B, CHUNK, CHUNK)), _full((CHUNK, D_B)), _full((D_C, D_C)), _full((1, D_C)),
                  _full((D_MODEL, D_MODEL))],
        out_specs=[row(D_IN), row(D_A), row(D_C), row(D_MODEL), row(D_MODEL)],
        out_shape=[jax.ShapeDtypeStruct((T, D_IN), F32), jax.ShapeDtypeStruct((T, D_A), F32),
                   jax.ShapeDtypeStruct((T, D_C), F32), jax.ShapeDtypeStruct((T, D_MODEL), BF16),
                   jax.ShapeDtypeStruct((T, D_MODEL), F32)],
        scratch_shapes=[pltpu.VMEM((CONV_HALO + tm, D_A), F32)]
        + [pltpu.VMEM((POOL_HALO + tm, D_C), F32)] * len(POOL_WINDOWS),
        args=(x, g1, winT, convw, convb, clng, clnb, wpw, slng, slnb, wm, bias, wbd, pscale, wout), comm=comm)


def mixer_bwd(dx1, x, z, ycv, p, g1, winT, convw, clng, clnb, wpw, slng, slnb, wm, wmT, bias, wbd, pscale, wout,
              *, seq, tm, comm=None):
    T = x.shape[0]
    tiles_per_seq = seq // tm
    n_tiles = T // tm
    n_chunks = tm // CHUNK

    def body(dx1_ref, x_ref, z_ref, ycv_ref, p_ref, g1_ref, winT_ref, convw_ref, clng_ref, clnb_ref, wpw_ref,
             slng_ref, slnb_ref, wm_ref, wmT_ref, bias_ref, wbd_ref, pscale_ref, wout_ref,
             dx_ref, dz_ref, h_ref, s_ref, dya_ref,
             dg1_ref, dconvw_ref, dconvb_ref, dclng_ref, dclnb_ref, dslng_ref, dslnb_ref, dwm_ref, dbs_ref,
             dwbd_ref, dpscale_ref, dycbuf, dpcbuf, *pbufs):
        i = pl.program_id(0)
        tile_in_seq = (n_tiles - 1 - i) % tiles_per_seq

        @pl.when(i == 0)
        def _():
            for ref in (dg1_ref, dconvw_ref, dconvb_ref, dclng_ref, dclnb_ref, dslng_ref, dslnb_ref, dwm_ref,
                        dbs_ref, dwbd_ref, dpscale_ref):
                ref[...] = jnp.zeros(ref.shape, F32)

        @pl.when(tile_in_seq == tiles_per_seq - 1)
        def _():
            dycbuf[tm:tm + CONV_HALO, :] = jnp.zeros((CONV_HALO, D_A), F32)
            dpcbuf[tm:tm + POOL_HALO, :] = jnp.zeros((POOL_HALO, D_C), F32)

        dx1 = dx1_ref[...]
        z = z_ref[...]
        dmix = _dot_nt(dx1.astype(BF16), wout_ref[...])
        dya = dmix[:, 0:D_A]
        dyb = dmix[:, D_A:D_A + D_B]
        dyc = dmix[:, D_A + D_B:D_MODEL]

        p = p_ref[...]
        pb = p.astype(BF16)
        q = _dot(pb, wbd_ref[...])
        dpscale_ref[...] += jnp.sum(dyc * q, axis=0, keepdims=True)
        dq = (dyc * pscale_ref[...]).astype(BF16)
        dwbd_ref[...] += _dot_tn(pb, dq)
        dp = _dot_nt(dq, wbd_ref[...])
        pos = (tile_in_seq * tm + lax.broadcasted_iota(jnp.int32, (tm, 1), 0)).astype(F32)
        dpc = dp / _pool_counts(pos)
        dpcbuf[0:tm, :] = dpc
        sums = [v[0:tm, :] for v in _window_sums_ahead(dpcbuf, pbufs, tm + POOL_HALO)]
        dpcbuf[tm:tm + POOL_HALO, :] = dpcbuf[0:POOL_HALO, :]
        dzc = _pool_select(sums, D_C) - dp

        dya_b = dya.astype(BF16)
        dya_ref[...] = dya_b
        ds = _dot_nt(dya_b, wpw_ref[...])
        xhat, rstd = _ln_stats(ycv_ref[...])
        ln = xhat * clng_ref[...] + clnb_ref[...]
        sg = _sigmoid(ln)
        s_ref[...] = (ln * sg).astype(BF16)
        dln = ds * (sg * (1.0 + ln * (1.0 - sg)))
        dclng_ref[...] += jnp.sum(dln * xhat, axis=0, keepdims=True)
        dclnb_ref[...] += jnp.sum(dln, axis=0, keepdims=True)
        dycv = _ln_bwd(dln, xhat, rstd, clng_ref[...])
        dconvb_ref[...] += jnp.sum(dycv, axis=0, keepdims=True)
        a = z[:, 0:D_A]
        sgate = _sigmoid(z[:, D_A:2 * D_A])
        y = a * sgate
        dycbuf[0:tm, :] = dycv
        dy = jnp.zeros((tm, D_A), F32)
        for d, sh in _shifted_taps(dycbuf, 0, CONV_WIDTH, tm):
            k = CONV_WIDTH - 1 - d
            dy = dy + convw_ref[k:k + 1, :] * sh
            dconvw_ref[k:k + 1, :] += jnp.sum(y * sh, axis=0, keepdims=True)
        dycbuf[tm:tm + CONV_HALO, :] = dycbuf[0:CONV_HALO, :]
        da = dy * sgate
        dgate = dy * a * sgate * (1.0 - sgate)

        gb, dgb = _gelu_and_grad(z[:, 2 * D_A:2 * D_A + 2 * D_B])
        u = gb[:, 0:D_B]
        vhat, vrstd = _ln_stats(gb[:, D_B:2 * D_B])
        vn = vhat * slng_ref[...] + slnb_ref[...]
        masks = _head_masks(D_B)
        tril = (lax.broadcasted_iota(jnp.int32, (CHUNK, CHUNK), 0)
                >= lax.broadcasted_iota(jnp.int32, (CHUNK, CHUNK), 1))
        lane128 = lax.broadcasted_iota(jnp.int32, (1, CHUNK), 1)
        du_parts, dvn_parts = [], []
        for c in range(n_chunks):
            rows = slice(c * CHUNK, (c + 1) * CHUNK)
            vn_c = vn[rows, :]
            vh = [jnp.where(masks[hh], vn_c, 0.0).astype(BF16) for hh in range(N_HEADS_B)]
            sgc = bias_ref[...]
            for hh in range(N_HEADS_B):
                sgc = sgc + _dot(wm_ref[hh], vh[hh])
            dyb_c = dyb[rows, :]
            du_parts.append(dyb_c * sgc)
            dsg = dyb_c * u[rows, :]
            dvn_c = jnp.zeros((CHUNK, D_B), F32)
            dbs = jnp.zeros((CHUNK, CHUNK), F32)
            for hh in range(N_HEADS_B):
                dsg_h = jnp.where(masks[hh], dsg, 0.0)
                dsg_hb = dsg_h.astype(BF16)
                dwm_ref[hh] += jnp.where(tril, _dot_nt(dsg_hb, vh[hh]), 0.0)
                dvn_c = dvn_c + _dot(wmT_ref[hh], dsg_hb)
                dbs = dbs + jnp.where(lane128 == hh, jnp.sum(dsg_h, axis=1, keepdims=True), 0.0)
            dbs_ref[...] += dbs
            dvn_parts.append(dvn_c)
        du = jnp.concatenate(du_parts, axis=0) if n_chunks > 1 else du_parts[0]
        dvn = jnp.concatenate(dvn_parts, axis=0) if n_chunks > 1 else dvn_parts[0]
        dslng_ref[...] += jnp.sum(dvn * vhat, axis=0, keepdims=True)
        dslnb_ref[...] += jnp.sum(dvn, axis=0, keepdims=True)
        dv = _ln_bwd(dvn, vhat, vrstd, slng_ref[...])
        dzb = jnp.concatenate([du, dv], axis=1) * dgb

        dz = jnp.concatenate([da, dgate, dzb, dzc], axis=1).astype(BF16)
        dz_ref[...] = dz
        dh = _dot(dz, winT_ref[...])
        x = x_ref[...]
        r = lax.rsqrt(jnp.mean(x * x, axis=-1, keepdims=True) + RMS_EPS)
        xn = x * r
        h_ref[...] = (xn * g1_ref[...]).astype(BF16)
        dg1_ref[...] += jnp.sum(dh * xn, axis=0, keepdims=True)
        dx_ref[...] = dx1 + _rms_bwd(dh, xn, r, g1_ref[...])

    row = lambda w: pl.BlockSpec((tm, w), lambda i: (n_tiles - 1 - i, 0))
    acc_shapes = [(1, D_MODEL), (CONV_WIDTH, D_A), (1, D_A), (1, D_A), (1, D_A), (1, D_B), (1, D_B),
                  (N_HEADS_B, CHUNK, CHUNK), (CHUNK, CHUNK), (D_C, D_C), (1, D_C)]
    return _hosted_call(
        body, name="mixer_bwd", grid=(n_tiles,),
        in_specs=[row(D_MODEL), row(D_MODEL), row(D_IN), row(D_A), row(D_C),
                  _full((1, D_MODEL)), _full((D_IN, D_MODEL)), _full((CONV_WIDTH, D_A)), _full((1, D_A)), _full((1, D_A)),
                  _full((D_A, D_A)), _full((1, D_B)), _full((1, D_B)), _full((N_HEADS_B, CHUNK, CHUNK)),
                  _full((N_HEADS_B, CHUNK, CHUNK)), _full((CHUNK, D_B)), _full((D_C, D_C)), _full((1, D_C)),
                  _full((D_MODEL, D_MODEL))],
        out_specs=[row(D_MODEL), row(D_IN), row(D_MODEL), row(D_A), row(D_A)] + [_full(s) for s in acc_shapes],
        out_shape=[jax.ShapeDtypeStruct((T, D_MODEL), F32), jax.ShapeDtypeStruct((T, D_IN), BF16),
                   jax.ShapeDtypeStruct((T, D_MODEL), BF16), jax.ShapeDtypeStruct((T, D_A), BF16),
                   jax.ShapeDtypeStruct((T, D_A), BF16)] + [jax.ShapeDtypeStruct(s, F32) for s in acc_shapes],
        scratch_shapes=[pltpu.VMEM((tm + CONV_HALO, D_A), F32)]
        + [pltpu.VMEM((tm + POOL_HALO, D_C), F32)] * len(POOL_WINDOWS),
        args=(dx1, x, z, ycv, p, g1, winT, convw, clng, clnb, wpw, slng, slnb, wm, wmT, bias, wbd, pscale, wout),
        comm=comm)


def ffn_fwd(x1, g2, wguT, wd, *, tm, th, comm=None):
    T = x1.shape[0]
    n_h = D_FF // th

    def body(x1_ref, g2_ref, wgu_ref, wd_ref, x2_ref, fac_ref, f_ref, h2_buf, acc):
        j = pl.program_id(1)

        @pl.when(j == 0)
        def _():
            x = x1_ref[...]
            r = lax.rsqrt(jnp.mean(x * x, axis=-1, keepdims=True) + RMS_EPS)
            h2_buf[...] = (x * r * g2_ref[...]).astype(BF16)
            acc[...] = x

        h2 = h2_buf[...]
        g = _dot_nt(h2, wgu_ref[0])
        u = _dot_nt(h2, wgu_ref[1])
        sg = _sigmoid(g)
        silu = g * sg
        fac_ref[0] = silu.astype(BF16)
        fac_ref[1] = (u * (sg * (1.0 + g * (1.0 - sg)))).astype(BF16)
        f = (silu * u).astype(BF16)
        f_ref[...] = f
        acc[...] += _dot(f, wd_ref[...])

        @pl.when(j == n_h - 1)
        def _():
            x2_ref[...] = acc[...]

    return _hosted_call(
        body, name="ffn_fwd", grid=(T // tm, n_h),
        in_specs=[pl.BlockSpec((tm, D_MODEL), lambda i, j: (i, 0)), _full((1, D_MODEL)),
                  pl.BlockSpec((2, th, D_MODEL), lambda i, j: (0, j, 0)), pl.BlockSpec((th, D_MODEL), lambda i, j: (j, 0))],
        out_specs=[pl.BlockSpec((tm, D_MODEL), lambda i, j: (i, 0)),
                   pl.BlockSpec((2, tm, th), lambda i, j: (0, i, j)), pl.BlockSpec((tm, th), lambda i, j: (i, j))],
        out_shape=[jax.ShapeDtypeStruct((T, D_MODEL), F32), jax.ShapeDtypeStruct((2, T, D_FF), BF16),
                   jax.ShapeDtypeStruct((T, D_FF), BF16)],
        scratch_shapes=[pltpu.VMEM((tm, D_MODEL), BF16), pltpu.VMEM((tm, D_MODEL), F32)],
        args=(x1, g2, wguT, wd), comm=comm)


def ffn_bwd(dx2, x1, fac, g2, wguT, wd, *, tm, th, comm=None):
    T = x1.shape[0]
    n_h = D_FF // th

    def body(dx2_ref, x1_ref, fac_ref, g2_ref, wgu_ref, wd_ref, dx1_ref, h2_ref, dgu_ref, dg2_ref, acc):
        i = pl.program_id(0)
        j = pl.program_id(1)

        @pl.when((i == 0) & (j == 0))
        def _():
            dg2_ref[...] = jnp.zeros(dg2_ref.shape, F32)

        dx2 = dx2_ref[...]
        rows = pl.ds(pl.multiple_of(j * th, th), th)
        df = _dot_nt(dx2.astype(BF16), wd_ref[rows, :])
        dup = (df * fac_ref[0].astype(F32)).astype(BF16)
        dgate = (df * fac_ref[1].astype(F32)).astype(BF16)
        dgu_ref[0] = dgate
        dgu_ref[1] = dup

        @pl.when(j == 0)
        def _():
            acc[...] = jnp.zeros(acc.shape, F32)

        acc[...] += _dot(dgate, wgu_ref[0, rows, :]) + _dot(dup, wgu_ref[1, rows, :])

        @pl.when(j == n_h - 1)
        def _():
            x = x1_ref[...]
            r = lax.rsqrt(jnp.mean(x * x, axis=-1, keepdims=True) + RMS_EPS)
            xn = x * r
            dh = acc[...]
            h2_ref[...] = (xn * g2_ref[...]).astype(BF16)
            dg2_ref[...] += jnp.sum(dh * xn, axis=0, keepdims=True)
            dx1_ref[...] = dx2 + _rms_bwd(dh, xn, r, g2_ref[...])

    return _hosted_call(
        body, name="ffn_bwd", grid=(T // tm, n_h),
        in_specs=[pl.BlockSpec((tm, D_MODEL), lambda i, j: (i, 0)), pl.BlockSpec((tm, D_MODEL), lambda i, j: (i, 0)),
                  pl.BlockSpec((2, tm, th), lambda i, j: (0, i, j)), _full((1, D_MODEL)),
                  _full((2, D_FF, D_MODEL)), _full((D_FF, D_MODEL))],
        out_specs=[pl.BlockSpec((tm, D_MODEL), lambda i, j: (i, 0)), pl.BlockSpec((tm, D_MODEL), lambda i, j: (i, 0)),
                   pl.BlockSpec((2, tm, th), lambda i, j: (0, i, j)), _full((1, D_MODEL))],
        out_shape=[jax.ShapeDtypeStruct((T, D_MODEL), F32), jax.ShapeDtypeStruct((T, D_MODEL), BF16),
                   jax.ShapeDtypeStruct((2, T, D_FF), BF16), jax.ShapeDtypeStruct((1, D_MODEL), F32)],
        scratch_shapes=[pltpu.VMEM((tm, D_MODEL), F32)],
        args=(dx2, x1, fac, g2, wguT, wd), comm=comm)


def head_fwd_bwd(x, target, fg, *, tm):
    T = x.shape[0]
    n_tiles = T // tm

    def body(x_ref, t_ref, fg_ref, loss_ref, dx_ref, dfg_ref, lacc):
        i = pl.program_id(0)

        @pl.when(i == 0)
        def _():
            lacc[...] = jnp.zeros(lacc.shape, F32)
            dfg_ref[...] = jnp.zeros(dfg_ref.shape, F32)

        x = x_ref[...]
        r = lax.rsqrt(jnp.mean(x * x, axis=-1, keepdims=True) + RMS_EPS)
        xn = x * r
        e = xn * fg_ref[...] - t_ref[...]
        lacc[...] += jnp.sum(e * e, axis=0, keepdims=True)
        dy = e * (1.0 / D_MODEL)
        dfg_ref[...] += jnp.sum(dy * xn, axis=0, keepdims=True)
        dx_ref[...] = _rms_bwd(dy, xn, r, fg_ref[...])

        @pl.when(i == n_tiles - 1)
        def _():
            loss_ref[...] = jnp.sum(lacc[...], axis=1, keepdims=True) * (0.5 / D_MODEL)

    row = pl.BlockSpec((tm, D_MODEL), lambda i: (i, 0))
    return pl.pallas_call(
        body, name="head_fwd_bwd", grid=(n_tiles,),
        in_specs=[row, row, _full((1, D_MODEL))],
        out_specs=[_full((1, 1)), row, _full((1, D_MODEL))],
        out_shape=[jax.ShapeDtypeStruct((1, 1), F32), jax.ShapeDtypeStruct((T, D_MODEL), F32),
                   jax.ShapeDtypeStruct((1, D_MODEL), F32)],
        scratch_shapes=[pltpu.VMEM((1, D_MODEL), F32)],
        compiler_params=_params(("arbitrary",)),
    )(x, target, fg)


def wgrad(a, b, *, tmo, tk, name, comm=None):
    G, T, M = a.shape
    N = b.shape[1]
    n_k = T // tk

    def body(a_ref, b_ref, o_ref, acc):
        k = pl.program_id(2)
        if n_k == 1:
            o_ref[0] = _dot_tn(a_ref[0].astype(BF16), b_ref[...].astype(BF16)).astype(BF16)
            return

        @pl.when(k == 0)
        def _():
            acc[...] = jnp.zeros(acc.shape, F32)

        acc[...] += _dot_tn(a_ref[0].astype(BF16), b_ref[...].astype(BF16))

        @pl.when(k == n_k - 1)
        def _():
            o_ref[0] = acc[...].astype(BF16)

    (out,), got = _hosted_call(
        body, name=name, grid=(G, M // tmo, n_k),
        in_specs=[pl.BlockSpec((1, tk, tmo), lambda g, m, k: (g, k, m)),
                  pl.BlockSpec((tk, N), lambda g, m, k: (k, 0))],
        out_specs=[pl.BlockSpec((1, tmo, N), lambda g, m, k: (g, m, 0))],
        out_shape=[jax.ShapeDtypeStruct((G, M, N), BF16)],
        scratch_shapes=[pltpu.VMEM((tmo, N), F32)],
        args=(a, b), comm=comm)
    return out, got


def sum_partials(gathered):
    n = len(gathered)

    def body(*refs):
        for in_ref, out_ref in zip(refs[:n], refs[n:]):
            total = in_ref[0].astype(F32)
            for d in range(1, N_DEV):
                total = total + in_ref[d].astype(F32)
            out_ref[...] = total

    vmem = pl.BlockSpec(memory_space=pltpu.VMEM)
    return pl.pallas_call(
        body, name="sum_partials", in_specs=[vmem] * n, out_specs=[vmem] * n,
        out_shape=[jax.ShapeDtypeStruct(g.shape[1:], F32) for g in gathered],
        compiler_params=pltpu.CompilerParams(vmem_limit_bytes=VMEM_LIMIT),
    )(*gathered)


_ADAM_C1 = 1.0 - ADAM_B1 ** ADAM_STEP
_ADAM_C2 = 1.0 - ADAM_B2 ** ADAM_STEP


def _adamw_math(w, g, m, v):
    m = ADAM_B1 * m + (1.0 - ADAM_B1) * g
    v = ADAM_B2 * v + (1.0 - ADAM_B2) * (g * g)
    m_hat = m / _ADAM_C1
    v_hat = v / _ADAM_C2
    delta = -ADAM_LR * (m_hat / (jnp.sqrt(v_hat) + ADAM_EPS) + ADAM_WD * w)
    return delta, m, v


def adamw_sharded(parts, w, m, v, *, tr, name, comm=None):
    _, R, C = parts[0].shape

    def body(p0_ref, p1_ref, w_ref, m_ref, v_ref, g_ref, d_ref, nm_ref, nv_ref):
        def update(p_ref):
            g = p_ref[0].astype(F32)
            for d in range(1, N_DEV):
                g = g + p_ref[d].astype(F32)
            delta, nm, nv = _adamw_math(w_ref[0], g, m_ref[0], v_ref[0])
            g_ref[0] = g
            d_ref[0] = delta
            nm_ref[0] = nm
            nv_ref[0] = nv

        @pl.when(pl.program_id(0) == 0)
        def _():
            update(p0_ref)

        @pl.when(pl.program_id(0) == 1)
        def _():
            update(p1_ref)

    n_i = R // tr
    p_specs = [pl.BlockSpec((N_DEV, tr, C), lambda l, i: (0, jnp.where(l == 0, i, n_i - 1), 0)),
               pl.BlockSpec((N_DEV, tr, C), lambda l, i: (0, jnp.where(l == 1, i, 0), 0))]
    o_spec = pl.BlockSpec((1, tr, C), lambda l, i: (l, i, 0))
    return _hosted_call(
        body, name=name, grid=(DEPTH, n_i),
        in_specs=p_specs + [o_spec, o_spec, o_spec], out_specs=[o_spec] * 4,
        out_shape=[jax.ShapeDtypeStruct(w.shape, F32)] * 4, scratch_shapes=[],
        args=(parts[0], parts[1], w, m, v), comm=comm)


def adamw_small(gs, ws, ms, vs):
    n = len(gs)

    def body(*refs):
        g_refs, w_refs, m_refs, v_refs = refs[:n], refs[n:2 * n], refs[2 * n:3 * n], refs[3 * n:4 * n]
        d_refs, nm_refs, nv_refs = refs[4 * n:5 * n], refs[5 * n:6 * n], refs[6 * n:]
        for k in range(n):
            delta, nm, nv = _adamw_math(w_refs[k][...], g_refs[k][...], m_refs[k][...], v_refs[k][...])
            d_refs[k][...] = delta
            nm_refs[k][...] = nm
            nv_refs[k][...] = nv

    vmem = pl.BlockSpec(memory_space=pltpu.VMEM)
    res = pl.pallas_call(
        body, name="adamw_small", in_specs=[vmem] * (4 * n), out_specs=[vmem] * (3 * n),
        out_shape=[jax.ShapeDtypeStruct(w.shape, F32) for w in ws] * 3,
        compiler_params=pltpu.CompilerParams(vmem_limit_bytes=VMEM_LIMIT),
    )(*gs, *ws, *ms, *vs)
    return res[:n], res[n:2 * n], res[2 * n:]


def _pack(arrays, row_multiple):
    flat = jnp.concatenate([a.reshape(-1) for a in arrays])
    rows = -(-flat.shape[0] // (LANES * row_multiple)) * row_multiple
    return jnp.pad(flat, (0, rows * LANES - flat.shape[0])).reshape(rows, LANES)


def _unpack(buf, shapes):
    flat = buf.reshape(-1)
    out, off = [], 0
    for s in shapes:
        n = math.prod(s)
        out.append(flat[off:off + n].reshape(s))
        off += n
    return out


def _block_diag(w_pool):
    G, d, _ = w_pool.shape
    eye = jnp.eye(G, dtype=w_pool.dtype)
    return (eye[:, None, :, None] * w_pool[:, :, None, :]).reshape(G * d, G * d)


def kernel(x, norm1_g, w_in, conv_w, conv_b, conv_ln_g, conv_ln_b, w_pw, sg_ln_g, sg_ln_b, w_s, b_s, w_pool, pool_scale, w_out, norm2_g, w_gate_up, w_down, final_g, loss_target, m_norm1_g, m_w_in, m_conv_w, m_conv_b, m_conv_ln_g, m_conv_ln_b, m_w_pw, m_sg_ln_g, m_sg_ln_b, m_w_s, m_b_s, m_w_pool, m_pool_scale, m_w_out, m_norm2_g, m_w_gate_up, m_w_down, m_final_g, v_norm1_g, v_w_in, v_conv_w, v_conv_b, v_conv_ln_g, v_conv_ln_b, v_w_pw, v_sg_ln_g, v_sg_ln_b, v_w_s, v_b_s, v_w_pool, v_pool_scale, v_w_out, v_norm2_g, v_w_gate_up, v_w_down, v_final_g):
    b_loc, seq, _ = x.shape
    T = b_loc * seq
    tm_mix = min(256, seq)
    tm_ffn_fwd = min(512, T)
    tm_ffn_bwd = min(512, T)
    tm_head = min(512, T)
    tk = min(2048, T)
    tk_f32 = min(1024, T)
    th = D_FF // 2
    cw = conv_w.shape[2]
    my_index = _index_of(_my_coords())

    xf = x.reshape(T, D_MODEL)
    tgt = loss_target.reshape(T, D_MODEL)

    mixer_shards = [[w_in[l].T.astype(BF16), w_out[l].astype(BF16), w_pw[l].astype(BF16)] for l in range(DEPTH)]
    ffn_shards = [[w_gate_up[l].T.astype(BF16), w_down[l].astype(BF16)] for l in range(DEPTH)]

    tril = jnp.tril(jnp.ones((CHUNK, CHUNK), dtype=bool))
    layers = []
    for l in range(DEPTH):
        wm = jnp.where(tril[None], w_s[l], 0.0).astype(BF16)
        layers.append(dict(
            g1=norm1_g[l][None], convb=conv_b[l][None], clng=conv_ln_g[l][None], clnb=conv_ln_b[l][None],
            slng=sg_ln_g[l][None], slnb=sg_ln_b[l][None], wm=wm, wmT=jnp.swapaxes(wm, 1, 2),
            bias=jnp.repeat(b_s[l].T, HEAD_DIM_B, axis=1), wbd=_block_diag(w_pool[l]).astype(BF16),
            pscale=pool_scale[l][None], g2=norm2_g[l][None]))

    def set_mixer_weights(l, g_in, g_out, g_pw):
        layers[l].update(winT=g_in.reshape(D_IN, D_MODEL), wout=g_out.reshape(D_MODEL, D_MODEL), wpw=g_pw.reshape(D_A, D_A))

    def set_ffn_weights(l, g_gu, g_d):
        layers[l].update(wguT=g_gu.reshape(2, D_FF, D_MODEL), wd=g_d.reshape(D_FF, D_MODEL))

    first = run_comm(Gather(mixer_shards[0] + [conv_w.reshape(DEPTH * CONV_WIDTH, cw).T]), name="gather_first")
    set_mixer_weights(0, *first[:3])
    convw_full = first[3].reshape(D_A, DEPTH * CONV_WIDTH).T.reshape(DEPTH, CONV_WIDTH, D_A)
    for l in range(DEPTH):
        layers[l]["convw"] = convw_full[l]

    saved = []
    cur = xf
    for l in range(DEPTH):
        w = layers[l]
        (z, ycv, p, mix, x1), got = mixer_fwd(
            cur, w["g1"], w["winT"], w["convw"], w["convb"], w["clng"], w["clnb"], w["wpw"], w["slng"], w["slnb"], w["wm"],
            w["bias"], w["wbd"], w["pscale"], w["wout"], seq=seq, tm=tm_mix,
            comm=Gather(ffn_shards[l], lead=0) if l == 0 else Gather(ffn_shards[l][:1]))
        if l == 0:
            set_ffn_weights(l, *got)
        else:
            set_ffn_weights(l, got[0], early_wd)
        (x2, fac, f), got = ffn_fwd(x1, w["g2"], w["wguT"], w["wd"], tm=tm_ffn_fwd, th=th,
                                comm=Gather(mixer_shards[l + 1] + ffn_shards[l + 1][1:]) if l + 1 < DEPTH else None)
        if l + 1 < DEPTH:
            set_mixer_weights(l + 1, *got[:3])
            early_wd = got[3]
        saved.append((cur, z, ycv, p, mix, x1, fac, f))
        cur = x2
    loss_part, dx, dfg = head_fwd_bwd(cur, tgt, final_g[None], tm=tm_head)

    blocks = {"gu": (2 * D_FF // N_DEV, D_MODEL), "d": (D_FF // N_DEV, D_MODEL), "in": (D_IN // N_DEV, D_MODEL),
              "out": (D_MODEL // N_DEV, D_MODEL), "pw": (D_A // N_DEV, D_A)}
    by_device = lambda kind, g: g.reshape((N_DEV,) + blocks[kind])
    small = [None] * DEPTH
    parts = {}
    packs = [None] * DEPTH

    pending = None
    for l in reversed(range(DEPTH)):
        w = layers[l]
        x0, z, ycv, p, mix, x1, fac, f = saved[l]
        plan = Together([Exchange(pending[1]), Gather([pending[2]])]) if pending else None
        (dx1, h2, dgu, dg2), got = ffn_bwd(dx, x1, fac, w["g2"], w["wguT"], w["wd"], tm=tm_ffn_bwd, th=th, comm=plan)
        if pending:
            got_parts, (packs[l + 1],) = plan.split(got)
            parts.update(zip(pending[0], got_parts))
        gw_d, _ = wgrad(f[None], dx, tmo=th, tk=tk_f32, name="wgrad_down")
        last = l == 0
        gw_gu, got = wgrad(dgu, h2, tmo=th, tk=tk, name="wgrad_gate_up", comm=Exchange([by_device("d", gw_d)]))
        parts[("d", l)], = got
        outs, got = mixer_bwd(
            dx1, x0, z, ycv, p, w["g1"], w["winT"], w["convw"], w["clng"], w["clnb"], w["wpw"], w["slng"], w["slnb"],
            w["wm"], w["wmT"], w["bias"], w["wbd"], w["pscale"], w["wout"], seq=seq, tm=tm_mix,
            comm=Exchange([by_device("gu", gw_gu)]))
        parts[("gu", l)], = got
        (dx, dz, h, s, dya, dg1, dconvw, dconvb, dclng, dclnb, dslng, dslnb, dwm, dbs, dwbd, dpscale) = outs
        small[l] = [dg1, dconvw, dconvb, dclng, dclnb, dslng, dslnb, dwm, dbs, dwbd, dpscale, dg2]
        pack = _pack(small[l], 2 * SUBLANES).astype(BF16)
        if last:
            gw_in, _ = wgrad(dz[None], h, tmo=D_IN // 2, tk=tk, name="wgrad_in")
            plan = Together([Exchange([by_device("in", gw_in)]),
                             Gather([pack, _pack([dfg, loss_part], SUBLANES)], lead=1)])
            gw_out, got = wgrad(mix[None], dx1, tmo=D_MODEL, tk=tk_f32, name="wgrad_out", comm=plan)
            (parts[("in", l)],), (packs[l], head_pack) = plan.split(got)
            gw_pw, got = wgrad(s[None], dya, tmo=D_A, tk=tk, name="wgrad_pw", comm=Exchange([by_device("out", gw_out)]))
            parts[("out", l)], = got
            parts[("pw", l)], = run_comm(Exchange([by_device("pw", gw_pw)]), name="exchange_last")
        else:
            gw_out, _ = wgrad(mix[None], dx1, tmo=D_MODEL, tk=tk_f32, name="wgrad_out")
            gw_in, got = wgrad(dz[None], h, tmo=D_IN // 2, tk=tk, name="wgrad_in", comm=Exchange([by_device("out", gw_out)]))
            parts[("out", l)], = got
            gw_pw, _ = wgrad(s[None], dya, tmo=D_A, tk=tk, name="wgrad_pw")
            pending = ([("in", l), ("pw", l)], [by_device("in", gw_in), by_device("pw", gw_pw)], pack)
    grad_x = dx.reshape(x.shape)
    p_in, p_gu, p_d, p_out, p_pw = [[parts[(k, l)] for l in range(DEPTH)] for k in ("in", "gu", "d", "out", "pw")]

    summed = sum_partials(packs + [head_pack])
    dfg_sum, loss_sum = _unpack(summed[DEPTH], [dfg.shape, loss_part.shape])
    loss = loss_sum[0, 0]
    per_layer = len(small[0])
    sums = [a for l in range(DEPTH) for a in _unpack(summed[l], [a.shape for a in small[l]])]
    g_small = {k: [] for k in ("norm1_g", "conv_w", "conv_b", "conv_ln_g", "conv_ln_b", "sg_ln_g", "sg_ln_b", "w_s", "b_s",
                               "w_pool", "pool_scale", "norm2_g")}
    for l in range(DEPTH):
        dg1, dconvw, dconvb, dclng, dclnb, dslng, dslnb, dwm, dbs, dwbd, dpscale, dg2 = sums[per_layer * l:per_layer * (l + 1)]
        g_small["norm1_g"].append(dg1[0])
        g_small["conv_w"].append(lax.dynamic_slice_in_dim(dconvw, my_index * cw, cw, axis=1))
        g_small["conv_b"].append(dconvb[0])
        g_small["conv_ln_g"].append(dclng[0])
        g_small["conv_ln_b"].append(dclnb[0])
        g_small["sg_ln_g"].append(dslng[0])
        g_small["sg_ln_b"].append(dslnb[0])
        g_small["w_s"].append(dwm)
        g_small["b_s"].append(dbs[:, :N_HEADS_B].T)
        g_small["w_pool"].append(jnp.stack([dwbd[g * GROUP_DIM_C:(g + 1) * GROUP_DIM_C, g * GROUP_DIM_C:(g + 1) * GROUP_DIM_C]
                                            for g in range(len(POOL_WINDOWS))]))
        g_small["pool_scale"].append(dpscale[0])
        g_small["norm2_g"].append(dg2[0])
    g_small = {k: jnp.stack(v) for k, v in g_small.items()}
    g_small["final_g"] = dfg_sum[0]

    t = lambda a: jnp.swapaxes(a, 1, 2)
    g_w_in, d_w_in, nm_w_in, nv_w_in = map(t, adamw_sharded(p_in, t(w_in), t(m_w_in), t(v_w_in), tr=D_IN // N_DEV // 2,
                                                            name="adamw_w_in")[0])
    g_w_gu, d_w_gu, nm_w_gu, nv_w_gu = map(t, adamw_sharded(p_gu, t(w_gate_up), t(m_w_gate_up), t(v_w_gate_up),
                                                            tr=2 * D_FF // N_DEV // 4, name="adamw_w_gate_up")[0])
    g_w_d, d_w_d, nm_w_d, nv_w_d = adamw_sharded(p_d, w_down, m_w_down, v_w_down, tr=D_FF // N_DEV // 2,
                                                 name="adamw_w_down")[0]
    g_w_out, d_w_out, nm_w_out, nv_w_out = adamw_sharded(p_out, w_out, m_w_out, v_w_out, tr=D_MODEL // N_DEV,
                                                         name="adamw_w_out")[0]
    g_w_pw, d_w_pw, nm_w_pw, nv_w_pw = adamw_sharded(p_pw, w_pw, m_w_pw, v_w_pw, tr=D_A // N_DEV, name="adamw_w_pw")[0]

    small_names = ["norm1_g", "conv_w", "conv_b", "conv_ln_g", "conv_ln_b", "sg_ln_g", "sg_ln_b", "w_s", "b_s", "w_pool",
                   "pool_scale", "norm2_g", "final_g"]
    small_w = dict(norm1_g=norm1_g, conv_w=conv_w, conv_b=conv_b, conv_ln_g=conv_ln_g, conv_ln_b=conv_ln_b, sg_ln_g=sg_ln_g,
                   sg_ln_b=sg_ln_b, w_s=w_s, b_s=b_s, w_pool=w_pool, pool_scale=pool_scale, norm2_g=norm2_g, final_g=final_g)
    small_m = dict(norm1_g=m_norm1_g, conv_w=m_conv_w, conv_b=m_conv_b, conv_ln_g=m_conv_ln_g, conv_ln_b=m_conv_ln_b,
                   sg_ln_g=m_sg_ln_g, sg_ln_b=m_sg_ln_b, w_s=m_w_s, b_s=m_b_s, w_pool=m_w_pool, pool_scale=m_pool_scale,
                   norm2_g=m_norm2_g, final_g=m_final_g)
    small_v = dict(norm1_g=v_norm1_g, conv_w=v_conv_w, conv_b=v_conv_b, conv_ln_g=v_conv_ln_g, conv_ln_b=v_conv_ln_b,
                   sg_ln_g=v_sg_ln_g, sg_ln_b=v_sg_ln_b, w_s=v_w_s, b_s=v_b_s, w_pool=v_w_pool, pool_scale=v_pool_scale,
                   norm2_g=v_norm2_g, final_g=v_final_g)
    two_d = lambda a: a[None] if a.ndim == 1 else a
    d_s, nm_s, nv_s = adamw_small(*[[two_d(d[k]) for k in small_names] for d in (g_small, small_w, small_m, small_v)])
    d_small = {k: a.reshape(small_w[k].shape) for k, a in zip(small_names, d_s)}
    nm_small = {k: a.reshape(small_w[k].shape) for k, a in zip(small_names, nm_s)}
    nv_small = {k: a.reshape(small_w[k].shape) for k, a in zip(small_names, nv_s)}

    order = ["norm1_g", "w_in", "conv_w", "conv_b", "conv_ln_g", "conv_ln_b", "w_pw", "sg_ln_g", "sg_ln_b", "w_s", "b_s",
             "w_pool", "pool_scale", "w_out", "norm2_g", "w_gate_up", "w_down", "final_g"]
    grads = dict(g_small, w_in=g_w_in, w_pw=g_w_pw, w_out=g_w_out, w_gate_up=g_w_gu, w_down=g_w_d)
    deltas = dict(d_small, w_in=d_w_in, w_pw=d_w_pw, w_out=d_w_out, w_gate_up=d_w_gu, w_down=d_w_d)
    new_m = dict(nm_small, w_in=nm_w_in, w_pw=nm_w_pw, w_out=nm_w_out, w_gate_up=nm_w_gu, w_down=nm_w_d)
    new_v = dict(nv_small, w_in=nv_w_in, w_pw=nv_w_pw, w_out=nv_w_out, w_gate_up=nv_w_gu, w_down=nv_w_d)
    return (loss, grad_x, *[grads[k] for k in order], *[deltas[k] for k in order], *[new_m[k] for k in order],
            *[new_v[k] for k in order])
```

```python
import math

import jax
import jax.numpy as jnp
from jax import lax
from jax.experimental import pallas as pl
from jax.experimental.pallas import tpu as pltpu

F32 = jnp.float32
BF16 = jnp.bfloat16

D_MODEL = 1024
D_A = 384
D_B = 384
D_C = 256
D_IN = 2 * D_A + 2 * D_B + D_C
N_HEADS_B = 4
HEAD_DIM_B = 96
POOL_WINDOWS = (2, 4, 8, 16)
GROUP_DIM_C = 64
CONV_WIDTH = 31
CHUNK = 128
D_FF = 2816
RMS_EPS = 1e-6
LN_EPS = 1e-5
DEPTH = 2
N_DEV = 8

ADAM_LR = 0.001
ADAM_B1 = 0.9
ADAM_B2 = 0.999
ADAM_EPS = 1e-08
ADAM_WD = 0.01
ADAM_STEP = 10

LANES = 128
SUBLANES = 8

CONV_HALO = 32
POOL_HALO = 32
assert POOL_WINDOWS == (2, 4, 8, 16) and POOL_HALO == SUBLANES * len(POOL_WINDOWS)

VMEM_LIMIT = 56 * 1024 * 1024

MESH_ID = pl.DeviceIdType.MESH


def _dot(a, b):
    return jnp.dot(a, b, preferred_element_type=F32)


def _dot_nt(a, b):
    return lax.dot_general(a, b, (((1,), (1,)), ((), ())), preferred_element_type=F32)


def _dot_tn(a, b):
    return lax.dot_general(a, b, (((0,), (0,)), ((), ())), preferred_element_type=F32)


def _sigmoid(x):
    return 0.5 * jnp.tanh(0.5 * x) + 0.5


def _shifted_taps(buf, first_row, n_shifts, tm):
    for phase in range(min(SUBLANES, n_shifts)):
        shifts = list(range(phase, n_shifts, SUBLANES))
        span = buf[first_row + phase:first_row + shifts[-1] + tm, :]
        for s in shifts:
            yield s, span[s - phase:s - phase + tm, :]


def _window_sums_back(x_ref, bufs, n_rows):
    out, src, w = [], x_ref, 1
    for l in range(len(POOL_WINDOWS)):
        lo = SUBLANES * (l + 1)
        cur = src[lo:n_rows, :] + src[lo - w:n_rows - w, :]
        out.append(cur)
        if l < len(bufs):
            bufs[l][lo:n_rows, :] = cur
            src = bufs[l]
        w *= 2
    return out


def _window_sums_ahead(x_ref, bufs, n_rows):
    out, src, w = [], x_ref, 1
    for l in range(len(POOL_WINDOWS)):
        hi = n_rows - SUBLANES * (l + 1)
        cur = src[0:hi, :] + src[w:hi + w, :]
        out.append(cur)
        if l < len(bufs):
            bufs[l][0:hi, :] = cur
            src = bufs[l]
        w *= 2
    return out


_GELU_C = math.sqrt(2.0 / math.pi)


def _gelu_and_grad(x):
    x2 = x * x
    inner = _GELU_C * (x + 0.044715 * x2 * x)
    t = jnp.tanh(inner)
    g = 0.5 * x * (1.0 + t)
    dg = 0.5 * (1.0 + t) + 0.5 * x * (1.0 - t * t) * _GELU_C * (1.0 + 3.0 * 0.044715 * x2)
    return g, dg


def _ln_stats(x):
    mu = jnp.mean(x, axis=-1, keepdims=True)
    xc = x - mu
    var = jnp.mean(xc * xc, axis=-1, keepdims=True)
    rstd = lax.rsqrt(var + LN_EPS)
    return xc * rstd, rstd


def _ln_bwd(dy, xhat, rstd, g):
    dxhat = dy * g
    return rstd * (dxhat - jnp.mean(dxhat, axis=-1, keepdims=True)
                   - xhat * jnp.mean(dxhat * xhat, axis=-1, keepdims=True))


def _rms_bwd(dh, xn, r, g):
    dxn = dh * g
    return r * (dxn - xn * jnp.mean(dxn * xn, axis=-1, keepdims=True))


def _head_masks(width):
    lane = lax.broadcasted_iota(jnp.int32, (1, width), 1)
    return [(lane >= h * HEAD_DIM_B) & (lane < (h + 1) * HEAD_DIM_B) for h in range(N_HEADS_B)]


def _pool_select(vals, width):
    lane = lax.broadcasted_iota(jnp.int32, (1, width), 1)
    out = vals[-1]
    for g in range(len(vals) - 2, -1, -1):
        out = jnp.where(lane < (g + 1) * GROUP_DIM_C, vals[g], out)
    return out


def _pool_counts(pos):
    return _pool_select([jnp.minimum(pos + 1.0, float(w)) for w in POOL_WINDOWS], D_C)


def _full(shape):
    n = len(shape)
    return pl.BlockSpec(shape, lambda *_: (0,) * n)


def _params(sem):
    return pltpu.CompilerParams(dimension_semantics=sem, vmem_limit_bytes=VMEM_LIMIT)


def _my_coords():
    return lax.axis_index("x"), lax.axis_index("y"), lax.axis_index("c")


def _peer(me, rel):
    x, y, c = me
    bx, by, bc = (rel >> 2) & 1, (rel >> 1) & 1, rel & 1
    return (1 - x if bx else x, 1 - y if by else y, 1 - c if bc else c)


def _index_of(dev):
    return 4 * dev[0] + 2 * dev[1] + dev[2]


FORWARD_LEAD = 2
SIBLING = 1
OTHER_CHIPS = (2, 4, 6)


class Gather:
    def __init__(self, shards, lead=FORWARD_LEAD):
        n = len(shards)
        self.lead = lead
        self.inputs = list(shards)
        self.out_shape = [jax.ShapeDtypeStruct((N_DEV,) + s.shape, s.dtype) for s in shards]
        self.scratch = [pltpu.SemaphoreType.DMA((N_DEV - 1, n)), pltpu.SemaphoreType.DMA((N_DEV - 1, n)),
                        pltpu.SemaphoreType.DMA((n,))]

    @staticmethod
    def _copy(src, dst, sems, rel, k, to):
        return pltpu.make_async_remote_copy(src_ref=src, dst_ref=dst, send_sem=sems[0].at[rel - 1, k],
                                            recv_sem=sems[1].at[rel - 1, k], device_id=to, device_id_type=MESH_ID)

    def start(self, ins, outs, sems):
        me = _my_coords()
        mine = _index_of(me)
        for k, src in enumerate(ins):
            pltpu.make_async_copy(src, outs[k].at[mine], sems[2].at[k]).start()
            for rel in (SIBLING,) + OTHER_CHIPS:
                self._copy(src, outs[k].at[mine], sems, rel, k, _peer(me, rel)).start()

    def forward(self, ins, outs, sems):
        me = _my_coords()
        sibling = _peer(me, SIBLING)
        for rel in OTHER_CHIPS:
            slot = _index_of(_peer(me, rel))
            for k in range(len(ins)):
                self._copy(ins[k], outs[k].at[slot], sems, rel, k, sibling).wait_recv()
                self._copy(outs[k].at[slot], outs[k].at[slot], sems, rel + 1, k, sibling).start()

    def finish(self, ins, outs, sems):
        me = _my_coords()
        mine = _index_of(me)
        sibling = _peer(me, SIBLING)
        for rel in (SIBLING,) + tuple(r + 1 for r in OTHER_CHIPS):
            slot = _index_of(_peer(me, rel))
            for k in range(len(ins)):
                self._copy(ins[k], outs[k].at[slot], sems, rel, k, sibling).wait_recv()
        for rel in range(1, N_DEV):
            for k in range(len(ins)):
                self._copy(ins[k], outs[k].at[mine], sems, rel, k, sibling).wait_send()
        for k, src in enumerate(ins):
            pltpu.make_async_copy(src, outs[k].at[mine], sems[2].at[k]).wait()


class Exchange:
    lead = 0

    def __init__(self, fulls):
        n = len(fulls)
        self.inputs = list(fulls)
        self.out_shape = [jax.ShapeDtypeStruct(f.shape, f.dtype) for f in fulls]
        self.scratch = [pltpu.SemaphoreType.DMA((N_DEV - 1, n)), pltpu.SemaphoreType.DMA((N_DEV - 1, n)),
                        pltpu.SemaphoreType.DMA((n,))]

    def start(self, ins, outs, sems):
        me = _my_coords()
        mine = _index_of(me)
        for k, src in enumerate(ins):
            pltpu.make_async_copy(src.at[mine], outs[k].at[mine], sems[2].at[k]).start()
            for rel in range(1, N_DEV):
                to = _peer(me, rel)
                Gather._copy(src.at[_index_of(to)], outs[k].at[mine], sems, rel, k, to).start()

    def forward(self, ins, outs, sems):
        pass

    def finish(self, ins, outs, sems):
        me = _my_coords()
        mine = _index_of(me)
        for rel in range(1, N_DEV):
            frm = _peer(me, rel)
            for k, src in enumerate(ins):
                Gather._copy(src.at[mine], outs[k].at[_index_of(frm)], sems, rel, k, frm).wait_recv()
        for rel in range(1, N_DEV):
            for k, src in enumerate(ins):
                Gather._copy(src.at[mine], outs[k].at[mine], sems, rel, k, _peer(me, rel)).wait_send()
        for k, src in enumerate(ins):
            pltpu.make_async_copy(src.at[mine], outs[k].at[mine], sems[2].at[k]).wait()


class Together:
    def __init__(self, plans):
        self.plans = list(plans)
        self.lead = max(p.lead for p in self.plans)
        self.inputs = [a for p in self.plans for a in p.inputs]
        self.out_shape = [s for p in self.plans for s in p.out_shape]
        self.scratch = [s for p in self.plans for s in p.scratch]

    def _each(self, ins, outs, sems):
        i = o = s = 0
        for p in self.plans:
            ni, no, ns = len(p.inputs), len(p.out_shape), len(p.scratch)
            yield p, ins[i:i + ni], outs[o:o + no], sems[s:s + ns]
            i, o, s = i + ni, o + no, s + ns

    def start(self, ins, outs, sems):
        for p, pi, po, ps in self._each(ins, outs, sems):
            p.start(pi, po, ps)

    def forward(self, ins, outs, sems):
        for p, pi, po, ps in self._each(ins, outs, sems):
            p.forward(pi, po, ps)

    def finish(self, ins, outs, sems):
        for p, pi, po, ps in self._each(ins, outs, sems):
            p.finish(pi, po, ps)

    def split(self, results):
        out, o = [], 0
        for p in self.plans:
            out.append(results[o:o + len(p.out_shape)])
            o += len(p.out_shape)
        return out


def _hosted_call(body, *, name, grid, in_specs, out_specs, out_shape, scratch_shapes, args, comm=None):
    sem = ("arbitrary",) * len(grid)
    if comm is None:
        res = pl.pallas_call(body, name=name, grid=grid, in_specs=in_specs, out_specs=out_specs, out_shape=out_shape,
                             scratch_shapes=scratch_shapes, compiler_params=_params(sem))(*args)
        return list(res), []
    n_in, n_out, n_scr = len(in_specs), len(out_specs), len(scratch_shapes)
    n_cin, n_cout = len(comm.inputs), len(comm.out_shape)
    n_steps = math.prod(grid)

    def hosted(*refs):
        ins, refs = refs[:n_in], refs[n_in:]
        cins, refs = refs[:n_cin], refs[n_cin:]
        outs, refs = refs[:n_out], refs[n_out:]
        couts, refs = refs[:n_cout], refs[n_cout:]
        scr, csems = refs[:n_scr], refs[n_scr:]
        step = 0
        for a, g in enumerate(grid):
            step = step * g + pl.program_id(a)

        @pl.when(step == 0)
        def _():
            comm.start(cins, couts, csems)

        body(*ins, *outs, *scr)

        @pl.when(step == max(n_steps - 1 - comm.lead, 0))
        def _():
            comm.forward(cins, couts, csems)

        @pl.when(step == n_steps - 1)
        def _():
            comm.finish(cins, couts, csems)

    any_spec = pl.BlockSpec(memory_space=pl.ANY)
    res = pl.pallas_call(
        hosted, name=name, grid=grid, in_specs=list(in_specs) + [any_spec] * n_cin,
        out_specs=list(out_specs) + [any_spec] * n_cout, out_shape=list(out_shape) + comm.out_shape,
        scratch_shapes=list(scratch_shapes) + comm.scratch,
        compiler_params=pltpu.CompilerParams(dimension_semantics=sem, vmem_limit_bytes=VMEM_LIMIT, has_side_effects=True),
    )(*args, *comm.inputs)
    return list(res[:n_out]), list(res[n_out:])


def run_comm(comm, *, name):
    n_cin, n_cout = len(comm.inputs), len(comm.out_shape)

    def body(*refs):
        cins, couts, csems = refs[:n_cin], refs[n_cin:n_cin + n_cout], refs[n_cin + n_cout:]
        comm.start(cins, couts, csems)
        comm.forward(cins, couts, csems)
        comm.finish(cins, couts, csems)

    any_spec = pl.BlockSpec(memory_space=pl.ANY)
    return pl.pallas_call(
        body, name=name, in_specs=[any_spec] * n_cin, out_specs=[any_spec] * n_cout, out_shape=comm.out_shape,
        scratch_shapes=comm.scratch, compiler_params=pltpu.CompilerParams(has_side_effects=True),
    )(*comm.inputs)


def mixer_fwd(x, g1, winT, convw, convb, clng, clnb, wpw, slng, slnb, wm, bias, wbd, pscale, wout, *, seq, tm,
              comm=None):
    T = x.shape[0]
    tiles_per_seq = seq // tm
    n_chunks = tm // CHUNK

    def body(x_ref, g1_ref, winT_ref, convw_ref, convb_ref, clng_ref, clnb_ref, wpw_ref, slng_ref, slnb_ref,
             wm_ref, bias_ref, wbd_ref, pscale_ref, wout_ref,
             z_ref, ycv_ref, p_ref, mix_ref, x1_ref, ybuf, zcbuf, *pbufs):
        i = pl.program_id(0)
        tile_in_seq = i % tiles_per_seq

        @pl.when(tile_in_seq == 0)
        def _():
            ybuf[0:CONV_HALO, :] = jnp.zeros((CONV_HALO, D_A), F32)
            zcbuf[0:POOL_HALO, :] = jnp.zeros((POOL_HALO, D_C), F32)

        x = x_ref[...]
        r = lax.rsqrt(jnp.mean(x * x, axis=-1, keepdims=True) + RMS_EPS)
        h = (x * r * g1_ref[...]).astype(BF16)
        z = _dot_nt(h, winT_ref[...])
        z_ref[...] = z

        y = z[:, 0:D_A] * _sigmoid(z[:, D_A:2 * D_A])
        ybuf[CONV_HALO:CONV_HALO + tm, :] = y
        acc = jnp.zeros((tm, D_A), F32) + convb_ref[...]
        for k, rows in _shifted_taps(ybuf, CONV_HALO - (CONV_WIDTH - 1), CONV_WIDTH, tm):
            acc = acc + convw_ref[k:k + 1, :] * rows
        ybuf[0:CONV_HALO, :] = ybuf[tm:tm + CONV_HALO, :]
        ycv_ref[...] = acc
        xhat, _ = _ln_stats(acc)
        ln = xhat * clng_ref[...] + clnb_ref[...]
        s = ln * _sigmoid(ln)
        ya = _dot(s.astype(BF16), wpw_ref[...])

        gb, _ = _gelu_and_grad(z[:, 2 * D_A:2 * D_A + 2 * D_B])
        u = gb[:, 0:D_B]
        vhat, _ = _ln_stats(gb[:, D_B:2 * D_B])
        vn = vhat * slng_ref[...] + slnb_ref[...]
        masks = _head_masks(D_B)
        yb_parts = []
        for c in range(n_chunks):
            vn_c = vn[c * CHUNK:(c + 1) * CHUNK, :]
            sg = bias_ref[...]
            for hh in range(N_HEADS_B):
                sg = sg + _dot(wm_ref[hh], jnp.where(masks[hh], vn_c, 0.0).astype(BF16))
            yb_parts.append(u[c * CHUNK:(c + 1) * CHUNK, :] * sg)
        yb = jnp.concatenate(yb_parts, axis=0) if n_chunks > 1 else yb_parts[0]

        zc = z[:, 2 * D_A + 2 * D_B:D_IN]
        zcbuf[POOL_HALO:POOL_HALO + tm, :] = zc
        sums = [v[POOL_HALO - SUBLANES * (l + 1):POOL_HALO - SUBLANES * (l + 1) + tm, :]
                for l, v in enumerate(_window_sums_back(zcbuf, pbufs, POOL_HALO + tm))]
        zcbuf[0:POOL_HALO, :] = zcbuf[tm:tm + POOL_HALO, :]
        pos = (tile_in_seq * tm + lax.broadcasted_iota(jnp.int32, (tm, 1), 0)).astype(F32)
        p = _pool_select(sums, D_C) / _pool_counts(pos) - zc
        p_ref[...] = p
        yc = _dot(p.astype(BF16), wbd_ref[...]) * pscale_ref[...]

        mix = jnp.concatenate([ya, yb, yc], axis=1).astype(BF16)
        mix_ref[...] = mix
        x1_ref[...] = x + _dot(mix, wout_ref[...])

    row = lambda w: pl.BlockSpec((tm, w), lambda i: (i, 0))
    return _hosted_call(
        body, name="mixer_fwd", grid=(T // tm,),
        in_specs=[row(D_MODEL), _full((1, D_MODEL)), _full((D_IN, D_MODEL)), _full((CONV_WIDTH, D_A)),
                  _full((1, D_A)), _full((1, D_A)), _full((1, D_A)), _full((D_A, D_A)), _full((1, D_B)), _full((1, D_B)),
                  _full((N_HEADS_B, CHUNK, CHUNK)), _full((CHUNK, D_B)), _full((D_C, D_C)), _full((1, D_C)),
                  _full((D_MODEL, D_MODEL))],
        out_specs=[row(D_IN), row(D_A), row(D_C), row(D_MODEL), row(D_MODEL)],
        out_shape=[jax.ShapeDtypeStruct((T, D_IN), F32), jax.ShapeDtypeStruct((T, D_A), F32),
                   jax.ShapeDtypeStruct((T, D_C), F32), jax.ShapeDtypeStruct((T, D_MODEL), BF16),
                   jax.ShapeDtypeStruct((T, D_MODEL), F32)],
        scratch_shapes=[pltpu.VMEM((CONV_HALO + tm, D_A), F32)]
        + [pltpu.VMEM((POOL_HALO + tm, D_C), F32)] * len(POOL_WINDOWS),
        args=(x, g1, winT, convw, convb, clng, clnb, wpw, slng, slnb, wm, bias, wbd, pscale, wout), comm=comm)


def mixer_bwd(dx1, x, z, ycv, p, g1, winT, convw, clng, clnb, wpw, slng, slnb, wm, wmT, bias, wbd, pscale, wout,
              *, seq, tm, comm=None):
    T = x.shape[0]
    tiles_per_seq = seq // tm
    n_tiles = T // tm
    n_chunks = tm // CHUNK

    def body(dx1_ref, x_ref, z_ref, ycv_ref, p_ref, g1_ref, winT_ref, convw_ref, clng_ref, clnb_ref, wpw_ref,
             slng_ref, slnb_ref, wm_ref, wmT_ref, bias_ref, wbd_ref, pscale_ref, wout_ref,
             dx_ref, dz_ref, h_ref, s_ref, dya_ref,
             dg1_ref, dconvw_ref, dconvb_ref, dclng_ref, dclnb_ref, dslng_ref, dslnb_ref, dwm_ref, dbs_ref,
             dwbd_ref, dpscale_ref, dycbuf, dpcbuf, *pbufs):
        i = pl.program_id(0)
        tile_in_seq = (n_tiles - 1 - i) % tiles_per_seq

        @pl.when(i == 0)
        def _():
            for ref in (dg1_ref, dconvw_ref, dconvb_ref, dclng_ref, dclnb_ref, dslng_ref, dslnb_ref, dwm_ref,
                        dbs_ref, dwbd_ref, dpscale_ref):
                ref[...] = jnp.zeros(ref.shape, F32)

        @pl.when(tile_in_seq == tiles_per_seq - 1)
        def _():
            dycbuf[tm:tm + CONV_HALO, :] = jnp.zeros((CONV_HALO, D_A), F32)
            dpcbuf[tm:tm + POOL_HALO, :] = jnp.zeros((POOL_HALO, D_C), F32)

        dx1 = dx1_ref[...]
        z = z_ref[...]
        dmix = _dot_nt(dx1.astype(BF16), wout_ref[...])
        dya = dmix[:, 0:D_A]
        dyb = dmix[:, D_A:D_A + D_B]
        dyc = dmix[:, D_A + D_B:D_MODEL]

        p = p_ref[...]
        pb = p.astype(BF16)
        q = _dot(pb, wbd_ref[...])
        dpscale_ref[...] += jnp.sum(dyc * q, axis=0, keepdims=True)
        dq = (dyc * pscale_ref[...]).astype(BF16)
        dwbd_ref[...] += _dot_tn(pb, dq)
        dp = _dot_nt(dq, wbd_ref[...])
        pos = (tile_in_seq * tm + lax.broadcasted_iota(jnp.int32, (tm, 1), 0)).astype(F32)
        dpc = dp / _pool_counts(pos)
        dpcbuf[0:tm, :] = dpc
        sums = [v[0:tm, :] for v in _window_sums_ahead(dpcbuf, pbufs, tm + POOL_HALO)]
        dpcbuf[tm:tm + POOL_HALO, :] = dpcbuf[0:POOL_HALO, :]
        dzc = _pool_select(sums, D_C) - dp

        dya_b = dya.astype(BF16)
        dya_ref[...] = dya_b
        ds = _dot_nt(dya_b, wpw_ref[...])
        xhat, rstd = _ln_stats(ycv_ref[...])
        ln = xhat * clng_ref[...] + clnb_ref[...]
        sg = _sigmoid(ln)
        s_ref[...] = (ln * sg).astype(BF16)
        dln = ds * (sg * (1.0 + ln * (1.0 - sg)))
        dclng_ref[...] += jnp.sum(dln * xhat, axis=0, keepdims=True)
        dclnb_ref[...] += jnp.sum(dln, axis=0, keepdims=True)
        dycv = _ln_bwd(dln, xhat, rstd, clng_ref[...])
        dconvb_ref[...] += jnp.sum(dycv, axis=0, keepdims=True)
        a = z[:, 0:D_A]
        sgate = _sigmoid(z[:, D_A:2 * D_A])
        y = a * sgate
        dycbuf[0:tm, :] = dycv
        dy = jnp.zeros((tm, D_A), F32)
        for d, sh in _shifted_taps(dycbuf, 0, CONV_WIDTH, tm):
            k = CONV_WIDTH - 1 - d
            dy = dy + convw_ref[k:k + 1, :] * sh
            dconvw_ref[k:k + 1, :] += jnp.sum(y * sh, axis=0, keepdims=True)
        dycbuf[tm:tm + CONV_HALO, :] = dycbuf[0:CONV_HALO, :]
        da = dy * sgate
        dgate = dy * a * sgate * (1.0 - sgate)

        gb, dgb = _gelu_and_grad(z[:, 2 * D_A:2 * D_A + 2 * D_B])
        u = gb[:, 0:D_B]
        vhat, vrstd = _ln_stats(gb[:, D_B:2 * D_B])
        vn = vhat * slng_ref[...] + slnb_ref[...]
        masks = _head_masks(D_B)
        tril = (lax.broadcasted_iota(jnp.int32, (CHUNK, CHUNK), 0)
                >= lax.broadcasted_iota(jnp.int32, (CHUNK, CHUNK), 1))
        lane128 = lax.broadcasted_iota(jnp.int32, (1, CHUNK), 1)
        du_parts, dvn_parts = [], []
        for c in range(n_chunks):
            rows = slice(c * CHUNK, (c + 1) * CHUNK)
            vn_c = vn[rows, :]
            vh = [jnp.where(masks[hh], vn_c, 0.0).astype(BF16) for hh in range(N_HEADS_B)]
            sgc = bias_ref[...]
            for hh in range(N_HEADS_B):
                sgc = sgc + _dot(wm_ref[hh], vh[hh])
            dyb_c = dyb[rows, :]
            du_parts.append(dyb_c * sgc)
            dsg = dyb_c * u[rows, :]
            dvn_c = jnp.zeros((CHUNK, D_B), F32)
            dbs = jnp.zeros((CHUNK, CHUNK), F32)
            for hh in range(N_HEADS_B):
                dsg_h = jnp.where(masks[hh], dsg, 0.0)
                dsg_hb = dsg_h.astype(BF16)
                dwm_ref[hh] += jnp.where(tril, _dot_nt(dsg_hb, vh[hh]), 0.0)
                dvn_c = dvn_c + _dot(wmT_ref[hh], dsg_hb)
                dbs = dbs + jnp.where(lane128 == hh, jnp.sum(dsg_h, axis=1, keepdims=True), 0.0)
            dbs_ref[...] += dbs
            dvn_parts.append(dvn_c)
        du = jnp.concatenate(du_parts, axis=0) if n_chunks > 1 else du_parts[0]
        dvn = jnp.concatenate(dvn_parts, axis=0) if n_chunks > 1 else dvn_parts[0]
        dslng_ref[...] += jnp.sum(dvn * vhat, axis=0, keepdims=True)
        dslnb_ref[...] += jnp.sum(dvn, axis=0, keepdims=True)
        dv = _ln_bwd(dvn, vhat, vrstd, slng_ref[...])
        dzb = jnp.concatenate([du, dv], axis=1) * dgb

        dz = jnp.concatenate([da, dgate, dzb, dzc], axis=1).astype(BF16)
        dz_ref[...] = dz
        dh = _dot(dz, winT_ref[...])
        x = x_ref[...]
        r = lax.rsqrt(jnp.mean(x * x, axis=-1, keepdims=True) + RMS_EPS)
        xn = x * r
        h_ref[...] = (xn * g1_ref[...]).astype(BF16)
        dg1_ref[...] += jnp.sum(dh * xn, axis=0, keepdims=True)
        dx_ref[...] = dx1 + _rms_bwd(dh, xn, r, g1_ref[...])

    row = lambda w: pl.BlockSpec((tm, w), lambda i: (n_tiles - 1 - i, 0))
    acc_shapes = [(1, D_MODEL), (CONV_WIDTH, D_A), (1, D_A), (1, D_A), (1, D_A), (1, D_B), (1, D_B),
                  (N_HEADS_B, CHUNK, CHUNK), (CHUNK, CHUNK), (D_C, D_C), (1, D_C)]
    return _hosted_call(
        body, name="mixer_bwd", grid=(n_tiles,),
        in_specs=[row(D_MODEL), row(D_MODEL), row(D_IN), row(D_A), row(D_C),
                  _full((1, D_MODEL)), _full((D_IN, D_MODEL)), _full((CONV_WIDTH, D_A)), _full((1, D_A)), _full((1, D_A)),
                  _full((D_A, D_A)), _full((1, D_B)), _full((1, D_B)), _full((N_HEADS_B, CHUNK, CHUNK)),
                  _full((N_HEADS_B, CHUNK, CHUNK)), _full((CHUNK, D_B)), _full((D_C, D_C)), _full((1, D_C)),
                  _full((D_MODEL, D_MODEL))],
        out_specs=[row(D_MODEL), row(D_IN), row(D_MODEL), row(D_A), row(D_A)] + [_full(s) for s in acc_shapes],
        out_shape=[jax.ShapeDtypeStruct((T, D_MODEL), F32), jax.ShapeDtypeStruct((T, D_IN), BF16),
                   jax.ShapeDtypeStruct((T, D_MODEL), BF16), jax.ShapeDtypeStruct((T, D_A), BF16),
                   jax.ShapeDtypeStruct((T, D_A), BF16)] + [jax.ShapeDtypeStruct(s, F32) for s in acc_shapes],
        scratch_shapes=[pltpu.VMEM((tm + CONV_HALO, D_A), F32)]
        + [pltpu.VMEM((tm + POOL_HALO, D_C), F32)] * len(POOL_WINDOWS),
        args=(dx1, x, z, ycv, p, g1, winT, convw, clng, clnb, wpw, slng, slnb, wm, wmT, bias, wbd, pscale, wout),
        comm=comm)


def ffn_fwd(x1, g2, wguT, wd, *, tm, th, comm=None):
    T = x1.shape[0]
    n_h = D_FF // th

    def body(x1_ref, g2_ref, wgu_ref, wd_ref, x2_ref, fac_ref, f_ref, h2_buf, acc):
        j = pl.program_id(1)

        @pl.when(j == 0)
        def _():
            x = x1_ref[...]
            r = lax.rsqrt(jnp.mean(x * x, axis=-1, keepdims=True) + RMS_EPS)
            h2_buf[...] = (x * r * g2_ref[...]).astype(BF16)
            acc[...] = x

        h2 = h2_buf[...]
        g = _dot_nt(h2, wgu_ref[0])
        u = _dot_nt(h2, wgu_ref[1])
        sg = _sigmoid(g)
        silu = g * sg
        fac_ref[0] = silu.astype(BF16)
        fac_ref[1] = (u * (sg * (1.0 + g * (1.0 - sg)))).astype(BF16)
        f = (silu * u).astype(BF16)
        f_ref[...] = f
        acc[...] += _dot(f, wd_ref[...])

        @pl.when(j == n_h - 1)
        def _():
            x2_ref[...] = acc[...]

    return _hosted_call(
        body, name="ffn_fwd", grid=(T // tm, n_h),
        in_specs=[pl.BlockSpec((tm, D_MODEL), lambda i, j: (i, 0)), _full((1, D_MODEL)),
                  pl.BlockSpec((2, th, D_MODEL), lambda i, j: (0, j, 0)), pl.BlockSpec((th, D_MODEL), lambda i, j: (j, 0))],
        out_specs=[pl.BlockSpec((tm, D_MODEL), lambda i, j: (i, 0)),
                   pl.BlockSpec((2, tm, th), lambda i, j: (0, i, j)), pl.BlockSpec((tm, th), lambda i, j: (i, j))],
        out_shape=[jax.ShapeDtypeStruct((T, D_MODEL), F32), jax.ShapeDtypeStruct((2, T, D_FF), BF16),
                   jax.ShapeDtypeStruct((T, D_FF), BF16)],
        scratch_shapes=[pltpu.VMEM((tm, D_MODEL), BF16), pltpu.VMEM((tm, D_MODEL), F32)],
        args=(x1, g2, wguT, wd), comm=comm)


def ffn_bwd(dx2, x1, fac, g2, wguT, wd, *, tm, th, comm=None):
    T = x1.shape[0]
    n_h = D_FF // th

    def body(dx2_ref, x1_ref, fac_ref, g2_ref, wgu_ref, wd_ref, dx1_ref, h2_ref, dgu_ref, dg2_ref, acc):
        i = pl.program_id(0)
        j = pl.program_id(1)

        @pl.when((i == 0) & (j == 0))
        def _():
            dg2_ref[...] = jnp.zeros(dg2_ref.shape, F32)

        dx2 = dx2_ref[...]
        rows = pl.ds(pl.multiple_of(j * th, th), th)
        df = _dot_nt(dx2.astype(BF16), wd_ref[rows, :])
        dup = (df * fac_ref[0].astype(F32)).astype(BF16)
        dgate = (df * fac_ref[1].astype(F32)).astype(BF16)
        dgu_ref[0] = dgate
        dgu_ref[1] = dup

        @pl.when(j == 0)
        def _():
            acc[...] = jnp.zeros(acc.shape, F32)

        acc[...] += _dot(dgate, wgu_ref[0, rows, :]) + _dot(dup, wgu_ref[1, rows, :])

        @pl.when(j == n_h - 1)
        def _():
            x = x1_ref[...]
            r = lax.rsqrt(jnp.mean(x * x, axis=-1, keepdims=True) + RMS_EPS)
            xn = x * r
            dh = acc[...]
            h2_ref[...] = (xn * g2_ref[...]).astype(BF16)
            dg2_ref[...] += jnp.sum(dh * xn, axis=0, keepdims=True)
            dx1_ref[...] = dx2 + _rms_bwd(dh, xn, r, g2_ref[...])

    return _hosted_call(
        body, name="ffn_bwd", grid=(T // tm, n_h),
        in_specs=[pl.BlockSpec((tm, D_MODEL), lambda i, j: (i, 0)), pl.BlockSpec((tm, D_MODEL), lambda i, j: (i, 0)),
                  pl.BlockSpec((2, tm, th), lambda i, j: (0, i, j)), _full((1, D_MODEL)),
                  _full((2, D_FF, D_MODEL)), _full((D_FF, D_MODEL))],
        out_specs=[pl.BlockSpec((tm, D_MODEL), lambda i, j: (i, 0)), pl.BlockSpec((tm, D_MODEL), lambda i, j: (i, 0)),
                   pl.BlockSpec((2, tm, th), lambda i, j: (0, i, j)), _full((1, D_MODEL))],
        out_shape=[jax.ShapeDtypeStruct((T, D_MODEL), F32), jax.ShapeDtypeStruct((T, D_MODEL), BF16),
                   jax.ShapeDtypeStruct((2, T, D_FF), BF16), jax.ShapeDtypeStruct((1, D_MODEL), F32)],
        scratch_shapes=[pltpu.VMEM((tm, D_MODEL), F32)],
        args=(dx2, x1, fac, g2, wguT, wd), comm=comm)


def head_fwd_bwd(x, target, fg, *, tm):
    T = x.shape[0]
    n_tiles = T // tm

    def body(x_ref, t_ref, fg_ref, loss_ref, dx_ref, dfg_ref, lacc):
        i = pl.program_id(0)

        @pl.when(i == 0)
        def _():
            lacc[...] = jnp.zeros(lacc.shape, F32)
            dfg_ref[...] = jnp.zeros(dfg_ref.shape, F32)

        x = x_ref[...]
        r = lax.rsqrt(jnp.mean(x * x, axis=-1, keepdims=True) + RMS_EPS)
        xn = x * r
        e = xn * fg_ref[...] - t_ref[...]
        lacc[...] += jnp.sum(e * e, axis=0, keepdims=True)
        dy = e * (1.0 / D_MODEL)
        dfg_ref[...] += jnp.sum(dy * xn, axis=0, keepdims=True)
        dx_ref[...] = _rms_bwd(dy, xn, r, fg_ref[...])

        @pl.when(i == n_tiles - 1)
        def _():
            loss_ref[...] = jnp.sum(lacc[...], axis=1, keepdims=True) * (0.5 / D_MODEL)

    row = pl.BlockSpec((tm, D_MODEL), lambda i: (i, 0))
    return pl.pallas_call(
        body, name="head_fwd_bwd", grid=(n_tiles,),
        in_specs=[row, row, _full((1, D_MODEL))],
        out_specs=[_full((1, 1)), row, _full((1, D_MODEL))],
        out_shape=[jax.ShapeDtypeStruct((1, 1), F32), jax.ShapeDtypeStruct((T, D_MODEL), F32),
                   jax.ShapeDtypeStruct((1, D_MODEL), F32)],
        scratch_shapes=[pltpu.VMEM((1, D_MODEL), F32)],
        compiler_params=_params(("arbitrary",)),
    )(x, target, fg)


def wgrad(a, b, *, tmo, tk, name, comm=None):
    G, T, M = a.shape
    N = b.shape[1]
    n_k = T // tk

    def body(a_ref, b_ref, o_ref, acc):
        k = pl.program_id(2)
        if n_k == 1:
            o_ref[0] = _dot_tn(a_ref[0].astype(BF16), b_ref[...].astype(BF16)).astype(BF16)
            return

        @pl.when(k == 0)
        def _():
            acc[...] = jnp.zeros(acc.shape, F32)

        acc[...] += _dot_tn(a_ref[0].astype(BF16), b_ref[...].astype(BF16))

        @pl.when(k == n_k - 1)
        def _():
            o_ref[0] = acc[...].astype(BF16)

    (out,), got = _hosted_call(
        body, name=name, grid=(G, M // tmo, n_k),
        in_specs=[pl.BlockSpec((1, tk, tmo), lambda g, m, k: (g, k, m)),
                  pl.BlockSpec((tk, N), lambda g, m, k: (k, 0))],
        out_specs=[pl.BlockSpec((1, tmo, N), lambda g, m, k: (g, m, 0))],
        out_shape=[jax.ShapeDtypeStruct((G, M, N), BF16)],
        scratch_shapes=[pltpu.VMEM((tmo, N), F32)],
        args=(a, b), comm=comm)
    return out, got


def sum_partials(gathered):
    n = len(gathered)

    def body(*refs):
        for in_ref, out_ref in zip(refs[:n], refs[n:]):
            total = in_ref[0].astype(F32)
            for d in range(1, N_DEV):
                total = total + in_ref[d].astype(F32)
            out_ref[...] = total

    vmem = pl.BlockSpec(memory_space=pltpu.VMEM)
    return pl.pallas_call(
        body, name="sum_partials", in_specs=[vmem] * n, out_specs=[vmem] * n,
        out_shape=[jax.ShapeDtypeStruct(g.shape[1:], F32) for g in gathered],
        compiler_params=pltpu.CompilerParams(vmem_limit_bytes=VMEM_LIMIT),
    )(*gathered)


_ADAM_C1 = 1.0 - ADAM_B1 ** ADAM_STEP
_ADAM_C2 = 1.0 - ADAM_B2 ** ADAM_STEP


def _adamw_math(w, g, m, v):
    m = ADAM_B1 * m + (1.0 - ADAM_B1) * g
    v = ADAM_B2 * v + (1.0 - ADAM_B2) * (g * g)
    m_hat = m / _ADAM_C1
    v_hat = v / _ADAM_C2
    delta = -ADAM_LR * (m_hat / (jnp.sqrt(v_hat) + ADAM_EPS) + ADAM_WD * w)
    return delta, m, v


def adamw_sharded(parts, w, m, v, *, tr, name, comm=None):
    _, R, C = parts[0].shape

    def body(p0_ref, p1_ref, w_ref, m_ref, v_ref, g_ref, d_ref, nm_ref, nv_ref):
        def update(p_ref):
            g = p_ref[0].astype(F32)
            for d in range(1, N_DEV):
                g = g + p_ref[d].astype(F32)
            delta, nm, nv = _adamw_math(w_ref[0], g, m_ref[0], v_ref[0])
            g_ref[0] = g
            d_ref[0] = delta
            nm_ref[0] = nm
            nv_ref[0] = nv

        @pl.when(pl.program_id(0) == 0)
        def _():
            update(p0_ref)

        @pl.when(pl.program_id(0) == 1)
        def _():
            update(p1_ref)

    n_i = R // tr
    p_specs = [pl.BlockSpec((N_DEV, tr, C), lambda l, i: (0, jnp.where(l == 0, i, n_i - 1), 0)),
               pl.BlockSpec((N_DEV, tr, C), lambda l, i: (0, jnp.where(l == 1, i, 0), 0))]
    o_spec = pl.BlockSpec((1, tr, C), lambda l, i: (l, i, 0))
    return _hosted_call(
        body, name=name, grid=(DEPTH, n_i),
        in_specs=p_specs + [o_spec, o_spec, o_spec], out_specs=[o_spec] * 4,
        out_shape=[jax.ShapeDtypeStruct(w.shape, F32)] * 4, scratch_shapes=[],
        args=(parts[0], parts[1], w, m, v), comm=comm)


def adamw_small(gs, ws, ms, vs):
    n = len(gs)

    def body(*refs):
        g_refs, w_refs, m_refs, v_refs = refs[:n], refs[n:2 * n], refs[2 * n:3 * n], refs[3 * n:4 * n]
        d_refs, nm_refs, nv_refs = refs[4 * n:5 * n], refs[5 * n:6 * n], refs[6 * n:]
        for k in range(n):
            delta, nm, nv = _adamw_math(w_refs[k][...], g_refs[k][...], m_refs[k][...], v_refs[k][...])
            d_refs[k][...] = delta
            nm_refs[k][...] = nm
            nv_refs[k][...] = nv

    vmem = pl.BlockSpec(memory_space=pltpu.VMEM)
    res = pl.pallas_call(
        body, name="adamw_small", in_specs=[vmem] * (4 * n), out_specs=[vmem] * (3 * n),
        out_shape=[jax.ShapeDtypeStruct(w.shape, F32) for w in ws] * 3,
        compiler_params=pltpu.CompilerParams(vmem_limit_bytes=VMEM_LIMIT),
    )(*gs, *ws, *ms, *vs)
    return res[:n], res[n:2 * n], res[2 * n:]


def _pack(arrays, row_multiple):
    flat = jnp.concatenate([a.reshape(-1) for a in arrays])
    rows = -(-flat.shape[0] // (LANES * row_multiple)) * row_multiple
    return jnp.pad(flat, (0, rows * LANES - flat.shape[0])).reshape(rows, LANES)


def _unpack(buf, shapes):
    flat = buf.reshape(-1)
    out, off = [], 0
    for s in shapes:
        n = math.prod(s)
        out.append(flat[off:off + n].reshape(s))
        off += n
    return out


def _block_diag(w_pool):
    G, d, _ = w_pool.shape
    eye = jnp.eye(G, dtype=w_pool.dtype)
    return (eye[:, None, :, None] * w_pool[:, :, None, :]).reshape(G * d, G * d)


def kernel(x, norm1_g, w_in, conv_w, conv_b, conv_ln_g, conv_ln_b, w_pw, sg_ln_g, sg_ln_b, w_s, b_s, w_pool, pool_scale, w_out, norm2_g, w_gate_up, w_down, final_g, loss_target, m_norm1_g, m_w_in, m_conv_w, m_conv_b, m_conv_ln_g, m_conv_ln_b, m_w_pw, m_sg_ln_g, m_sg_ln_b, m_w_s, m_b_s, m_w_pool, m_pool_scale, m_w_out, m_norm2_g, m_w_gate_up, m_w_down, m_final_g, v_norm1_g, v_w_in, v_conv_w, v_conv_b, v_conv_ln_g, v_conv_ln_b, v_w_pw, v_sg_ln_g, v_sg_ln_b, v_w_s, v_b_s, v_w_pool, v_pool_scale, v_w_out, v_norm2_g, v_w_gate_up, v_w_down, v_final_g):
    b_loc, seq, _ = x.shape
    T = b_loc * seq
    tm_mix = min(256, seq)
    tm_ffn_fwd = min(512, T)
    tm_ffn_bwd = min(512, T)
    tm_head = min(512, T)
    tk = min(2048, T)
    tk_f32 = min(1024, T)
    th = D_FF // 2
    cw = conv_w.shape[2]
    my_index = _index_of(_my_coords())

    xf = x.reshape(T, D_MODEL)
    tgt = loss_target.reshape(T, D_MODEL)

    mixer_shards = [[w_in[l].T.astype(BF16), w_out[l].astype(BF16), w_pw[l].astype(BF16)] for l in range(DEPTH)]
    ffn_shards = [[w_gate_up[l].T.astype(BF16), w_down[l].astype(BF16)] for l in range(DEPTH)]

    tril = jnp.tril(jnp.ones((CHUNK, CHUNK), dtype=bool))
    layers = []
    for l in range(DEPTH):
        wm = jnp.where(tril[None], w_s[l], 0.0).astype(BF16)
        layers.append(dict(
            g1=norm1_g[l][None], convb=conv_b[l][None], clng=conv_ln_g[l][None], clnb=conv_ln_b[l][None],
            slng=sg_ln_g[l][None], slnb=sg_ln_b[l][None], wm=wm, wmT=jnp.swapaxes(wm, 1, 2),
            bias=jnp.repeat(b_s[l].T, HEAD_DIM_B, axis=1), wbd=_block_diag(w_pool[l]).astype(BF16),
            pscale=pool_scale[l][None], g2=norm2_g[l][None]))

    def set_mixer_weights(l, g_in, g_out, g_pw):
        layers[l].update(winT=g_in.reshape(D_IN, D_MODEL), wout=g_out.reshape(D_MODEL, D_MODEL), wpw=g_pw.reshape(D_A, D_A))

    def set_ffn_weights(l, g_gu, g_d):
        layers[l].update(wguT=g_gu.reshape(2, D_FF, D_MODEL), wd=g_d.reshape(D_FF, D_MODEL))

    first = run_comm(Gather(mixer_shards[0] + [conv_w.reshape(DEPTH * CONV_WIDTH, cw).T]), name="gather_first")
    set_mixer_weights(0, *first[:3])
    convw_full = first[3].reshape(D_A, DEPTH * CONV_WIDTH).T.reshape(DEPTH, CONV_WIDTH, D_A)
    for l in range(DEPTH):
        layers[l]["convw"] = convw_full[l]

    saved = []
    cur = xf
    for l in range(DEPTH):
        w = layers[l]
        (z, ycv, p, mix, x1), got = mixer_fwd(
            cur, w["g1"], w["winT"], w["convw"], w["convb"], w["clng"], w["clnb"], w["wpw"], w["slng"], w["slnb"], w["wm"],
            w["bias"], w["wbd"], w["pscale"], w["wout"], seq=seq, tm=tm_mix,
            comm=Gather(ffn_shards[l], lead=0) if l == 0 else Gather(ffn_shards[l][:1]))
        if l == 0:
            set_ffn_weights(l, *got)
        else:
            set_ffn_weights(l, got[0], early_wd)
        (x2, fac, f), got = ffn_fwd(x1, w["g2"], w["wguT"], w["wd"], tm=tm_ffn_fwd, th=th,
                                comm=Gather(mixer_shards[l + 1] + ffn_shards[l + 1][1:]) if l + 1 < DEPTH else None)
        if l + 1 < DEPTH:
            set_mixer_weights(l + 1, *got[:3])
            early_wd = got[3]
        saved.append((cur, z, ycv, p, mix, x1, fac, f))
        cur = x2
    loss_part, dx, dfg = head_fwd_bwd(cur, tgt, final_g[None], tm=tm_head)

    blocks = {"gu": (2 * D_FF // N_DEV, D_MODEL), "d": (D_FF // N_DEV, D_MODEL), "in": (D_IN // N_DEV, D_MODEL),
              "out": (D_MODEL // N_DEV, D_MODEL), "pw": (D_A // N_DEV, D_A)}
    by_device = lambda kind, g: g.reshape((N_DEV,) + blocks[kind])
    small = [None] * DEPTH
    parts = {}
    packs = [None] * DEPTH

    pending = None
    for l in reversed(range(DEPTH)):
        w = layers[l]
        x0, z, ycv, p, mix, x1, fac, f = saved[l]
        plan = Together([Exchange(pending[1]), Gather([pending[2]])]) if pending else None
        (dx1, h2, dgu, dg2), got = ffn_bwd(dx, x1, fac, w["g2"], w["wguT"], w["wd"], tm=tm_ffn_bwd, th=th, comm=plan)
        if pending:
            got_parts, (packs[l + 1],) = plan.split(got)
            parts.update(zip(pending[0], got_parts))
        gw_d, _ = wgrad(f[None], dx, tmo=th, tk=tk_f32, name="wgrad_down")
        last = l == 0
        gw_gu, got = wgrad(dgu, h2, tmo=th, tk=tk, name="wgrad_gate_up",
                           comm=Exchange([by_device("d", gw_d)]) if last else None)
        if last:
            parts[("d", l)], = got
        outs, got = mixer_bwd(
            dx1, x0, z, ycv, p, w["g1"], w["winT"], w["convw"], w["clng"], w["clnb"], w["wpw"], w["slng"], w["slnb"],
            w["wm"], w["wmT"], w["bias"], w["wbd"], w["pscale"], w["wout"], seq=seq, tm=tm_mix,
            comm=Exchange([by_device("gu", gw_gu)]))
        parts[("gu", l)], = got
        (dx, dz, h, s, dya, dg1, dconvw, dconvb, dclng, dclnb, dslng, dslnb, dwm, dbs, dwbd, dpscale) = outs
        small[l] = [dg1, dconvw, dconvb, dclng, dclnb, dslng, dslnb, dwm, dbs, dwbd, dpscale, dg2]
        pack = _pack(small[l], 2 * SUBLANES).astype(BF16)
        if last:
            gw_in, _ = wgrad(dz[None], h, tmo=D_IN // 2, tk=tk, name="wgrad_in")
            plan = Together([Exchange([by_device("in", gw_in)]),
                             Gather([pack, _pack([dfg, loss_part], SUBLANES)], lead=1)])
            gw_out, got = wgrad(mix[None], dx1, tmo=D_MODEL, tk=tk_f32, name="wgrad_out", comm=plan)
            (parts[("in", l)],), (packs[l], head_pack) = plan.split(got)
            gw_pw, got = wgrad(s[None], dya, tmo=D_A, tk=tk, name="wgrad_pw", comm=Exchange([by_device("out", gw_out)]))
            parts[("out", l)], = got
            parts[("pw", l)], = run_comm(Exchange([by_device("pw", gw_pw)]), name="exchange_last")
        else:
            gw_out, _ = wgrad(mix[None], dx1, tmo=D_MODEL, tk=tk_f32, name="wgrad_out")
            gw_in, got = wgrad(dz[None], h, tmo=D_IN // 2, tk=tk, name="wgrad_in", comm=Exchange([by_device("out", gw_out)]))
            parts[("out", l)], = got
            gw_pw, _ = wgrad(s[None], dya, tmo=D_A, tk=tk, name="wgrad_pw")
            pending = ([("d", l), ("in", l), ("pw", l)],
                       [by_device("d", gw_d), by_device("in", gw_in), by_device("pw", gw_pw)], pack)
    grad_x = dx.reshape(x.shape)
    p_in, p_gu, p_d, p_out, p_pw = [[parts[(k, l)] for l in range(DEPTH)] for k in ("in", "gu", "d", "out", "pw")]

    summed = sum_partials(packs + [head_pack])
    dfg_sum, loss_sum = _unpack(summed[DEPTH], [dfg.shape, loss_part.shape])
    loss = loss_sum[0, 0]
    per_layer = len(small[0])
    sums = [a for l in range(DEPTH) for a in _unpack(summed[l], [a.shape for a in small[l]])]
    g_small = {k: [] for k in ("norm1_g", "conv_w", "conv_b", "conv_ln_g", "conv_ln_b", "sg_ln_g", "sg_ln_b", "w_s", "b_s",
                               "w_pool", "pool_scale", "norm2_g")}
    for l in range(DEPTH):
        dg1, dconvw, dconvb, dclng, dclnb, dslng, dslnb, dwm, dbs, dwbd, dpscale, dg2 = sums[per_layer * l:per_layer * (l + 1)]
        g_small["norm1_g"].append(dg1[0])
        g_small["conv_w"].append(lax.dynamic_slice_in_dim(dconvw, my_index * cw, cw, axis=1))
        g_small["conv_b"].append(dconvb[0])
        g_small["conv_ln_g"].append(dclng[0])
        g_small["conv_ln_b"].append(dclnb[0])
        g_small["sg_ln_g"].append(dslng[0])
        g_small["sg_ln_b"].append(dslnb[0])
        g_small["w_s"].append(dwm)
        g_small["b_s"].append(dbs[:, :N_HEADS_B].T)
        g_small["w_pool"].append(jnp.stack([dwbd[g * GROUP_DIM_C:(g + 1) * GROUP_DIM_C, g * GROUP_DIM_C:(g + 1) * GROUP_DIM_C]
                                            for g in range(len(POOL_WINDOWS))]))
        g_small["pool_scale"].append(dpscale[0])
        g_small["norm2_g"].append(dg2[0])
    g_small = {k: jnp.stack(v) for k, v in g_small.items()}
    g_small["final_g"] = dfg_sum[0]

    t = lambda a: jnp.swapaxes(a, 1, 2)
    g_w_in, d_w_in, nm_w_in, nv_w_in = map(t, adamw_sharded(p_in, t(w_in), t(m_w_in), t(v_w_in), tr=D_IN // N_DEV // 2,
                                                            name="adamw_w_in")[0])
    g_w_gu, d_w_gu, nm_w_gu, nv_w_gu = map(t, adamw_sharded(p_gu, t(w_gate_up), t(m_w_gate_up), t(v_w_gate_up),
                                                            tr=2 * D_FF // N_DEV // 4, name="adamw_w_gate_up")[0])
    g_w_d, d_w_d, nm_w_d, nv_w_d = adamw_sharded(p_d, w_down, m_w_down, v_w_down, tr=D_FF // N_DEV // 2,
                                                 name="adamw_w_down")[0]
    g_w_out, d_w_out, nm_w_out, nv_w_out = adamw_sharded(p_out, w_out, m_w_out, v_w_out, tr=D_MODEL // N_DEV,
                                                         name="adamw_w_out")[0]
    g_w_pw, d_w_pw, nm_w_pw, nv_w_pw = adamw_sharded(p_pw, w_pw, m_w_pw, v_w_pw, tr=D_A // N_DEV, name="adamw_w_pw")[0]

    small_names = ["norm1_g", "conv_w", "conv_b", "conv_ln_g", "conv_ln_b", "sg_ln_g", "sg_ln_b", "w_s", "b_s", "w_pool",
                   "pool_scale", "norm2_g", "final_g"]
    small_w = dict(norm1_g=norm1_g, conv_w=conv_w, conv_b=conv_b, conv_ln_g=conv_ln_g, conv_ln_b=conv_ln_b, sg_ln_g=sg_ln_g,
                   sg_ln_b=sg_ln_b, w_s=w_s, b_s=b_s, w_pool=w_pool, pool_scale=pool_scale, norm2_g=norm2_g, final_g=final_g)
    small_m = dict(norm1_g=m_norm1_g, conv_w=m_conv_w, conv_b=m_conv_b, conv_ln_g=m_conv_ln_g, conv_ln_b=m_conv_ln_b,
                   sg_ln_g=m_sg_ln_g, sg_ln_b=m_sg_ln_b, w_s=m_w_s, b_s=m_b_s, w_pool=m_w_pool, pool_scale=m_pool_scale,
                   norm2_g=m_norm2_g, final_g=m_final_g)
    small_v = dict(norm1_g=v_norm1_g, conv_w=v_conv_w, conv_b=v_conv_b, conv_ln_g=v_conv_ln_g, conv_ln_b=v_conv_ln_b,
                   sg_ln_g=v_sg_ln_g, sg_ln_b=v_sg_ln_b, w_s=v_w_s, b_s=v_b_s, w_pool=v_w_pool, pool_scale=v_pool_scale,
                   norm2_g=v_norm2_g, final_g=v_final_g)
    two_d = lambda a: a[None] if a.ndim == 1 else a
    d_s, nm_s, nv_s = adamw_small(*[[two_d(d[k]) for k in small_names] for d in (g_small, small_w, small_m, small_v)])
    d_small = {k: a.reshape(small_w[k].shape) for k, a in zip(small_names, d_s)}
    nm_small = {k: a.reshape(small_w[k].shape) for k, a in zip(small_names, nm_s)}
    nv_small = {k: a.reshape(small_w[k].shape) for k, a in zip(small_names, nv_s)}

    order = ["norm1_g", "w_in", "conv_w", "conv_b", "conv_ln_g", "conv_ln_b", "w_pw", "sg_ln_g", "sg_ln_b", "w_s", "b_s",
             "w_pool", "pool_scale", "w_out", "norm2_g", "w_gate_up", "w_down", "final_g"]
    grads = dict(g_small, w_in=g_w_in, w_pw=g_w_pw, w_out=g_w_out, w_gate_up=g_w_gu, w_down=g_w_d)
    deltas = dict(d_small, w_in=d_w_in, w_pw=d_w_pw, w_out=d_w_out, w_gate_up=d_w_gu, w_down=d_w_d)
    new_m = dict(nm_small, w_in=nm_w_in, w_pw=nm_w_pw, w_out=nm_w_out, w_gate_up=nm_w_gu, w_down=nm_w_d)
    new_v = dict(nv_small, w_in=nv_w_in, w_pw=nv_w_pw, w_out=nv_w_out, w_gate_up=nv_w_gu, w_down=nv_w_d)
    return (loss, grad_x, *[grads[k] for k in order], *[deltas[k] for k in order], *[new_m[k] for k in order],
            *[new_v[k] for k in order])
```

```python
import functools
import math

import jax
import jax.numpy as jnp
from jax import lax
from jax.experimental import pallas as pl
from jax.experimental.pallas import tpu as pltpu

F32 = jnp.float32
BF16 = jnp.bfloat16

D_MODEL = 1024
D_A = 384
D_B = 384
D_C = 256
D_IN = 2 * D_A + 2 * D_B + D_C
N_HEADS_B = 4
HEAD_DIM_B = 96
POOL_WINDOWS = (2, 4, 8, 16)
GROUP_DIM_C = 64
CONV_WIDTH = 31
CHUNK = 128
D_FF = 2816
RMS_EPS = 1e-6
LN_EPS = 1e-5
DEPTH = 2
N_DEV = 8

ADAM_LR = 0.001
ADAM_B1 = 0.9
ADAM_B2 = 0.999
ADAM_EPS = 1e-08
ADAM_WD = 0.01
ADAM_STEP = 10

LANES = 128
SUBLANES = 8

CONV_HALO = 32
POOL_HALO = 32
assert POOL_WINDOWS == (2, 4, 8, 16) and POOL_HALO == SUBLANES * len(POOL_WINDOWS)

VMEM_LIMIT = 56 * 1024 * 1024

MESH_ID = pl.DeviceIdType.MESH


def _dot(a, b):
    return jnp.dot(a, b, preferred_element_type=F32)


def _dot_nt(a, b):
    return lax.dot_general(a, b, (((1,), (1,)), ((), ())), preferred_element_type=F32)


def _dot_tn(a, b):
    return lax.dot_general(a, b, (((0,), (0,)), ((), ())), preferred_element_type=F32)


def _sigmoid(x):
    return 0.5 * jnp.tanh(0.5 * x) + 0.5


def _shifted_taps(buf, first_row, n_shifts, tm):
    for phase in range(min(SUBLANES, n_shifts)):
        shifts = list(range(phase, n_shifts, SUBLANES))
        span = buf[first_row + phase:first_row + shifts[-1] + tm, :]
        for s in shifts:
            yield s, span[s - phase:s - phase + tm, :]


def _window_sums_back(x_ref, bufs, n_rows):
    out, src, w = [], x_ref, 1
    for l in range(len(POOL_WINDOWS)):
        lo = SUBLANES * (l + 1)
        cur = src[lo:n_rows, :] + src[lo - w:n_rows - w, :]
        out.append(cur)
        if l < len(bufs):
            bufs[l][lo:n_rows, :] = cur
            src = bufs[l]
        w *= 2
    return out


def _window_sums_ahead(x_ref, bufs, n_rows):
    out, src, w = [], x_ref, 1
    for l in range(len(POOL_WINDOWS)):
        hi = n_rows - SUBLANES * (l + 1)
        cur = src[0:hi, :] + src[w:hi + w, :]
        out.append(cur)
        if l < len(bufs):
            bufs[l][0:hi, :] = cur
            src = bufs[l]
        w *= 2
    return out


_GELU_C = math.sqrt(2.0 / math.pi)


def _gelu_and_grad(x):
    x2 = x * x
    inner = _GELU_C * (x + 0.044715 * x2 * x)
    t = jnp.tanh(inner)
    g = 0.5 * x * (1.0 + t)
    dg = 0.5 * (1.0 + t) + 0.5 * x * (1.0 - t * t) * _GELU_C * (1.0 + 3.0 * 0.044715 * x2)
    return g, dg


def _ln_stats(x):
    mu = jnp.mean(x, axis=-1, keepdims=True)
    xc = x - mu
    var = jnp.mean(xc * xc, axis=-1, keepdims=True)
    rstd = lax.rsqrt(var + LN_EPS)
    return xc * rstd, rstd


def _ln_bwd(dy, xhat, rstd, g):
    dxhat = dy * g
    return rstd * (dxhat - jnp.mean(dxhat, axis=-1, keepdims=True)
                   - xhat * jnp.mean(dxhat * xhat, axis=-1, keepdims=True))


def _rms_bwd(dh, xn, r, g):
    dxn = dh * g
    return r * (dxn - xn * jnp.mean(dxn * xn, axis=-1, keepdims=True))


def _head_masks(width):
    lane = lax.broadcasted_iota(jnp.int32, (1, width), 1)
    return [(lane >= h * HEAD_DIM_B) & (lane < (h + 1) * HEAD_DIM_B) for h in range(N_HEADS_B)]


def _pool_select(vals, width):
    lane = lax.broadcasted_iota(jnp.int32, (1, width), 1)
    out = vals[-1]
    for g in range(len(vals) - 2, -1, -1):
        out = jnp.where(lane < (g + 1) * GROUP_DIM_C, vals[g], out)
    return out


def _pool_counts(pos):
    return _pool_select([jnp.minimum(pos + 1.0, float(w)) for w in POOL_WINDOWS], D_C)


def _full(shape):
    n = len(shape)
    return pl.BlockSpec(shape, lambda *_: (0,) * n)


def _params(sem):
    return pltpu.CompilerParams(dimension_semantics=sem, vmem_limit_bytes=VMEM_LIMIT)


def _my_coords():
    return lax.axis_index("x"), lax.axis_index("y"), lax.axis_index("c")


def _peer(me, rel):
    x, y, c = me
    bx, by, bc = (rel >> 2) & 1, (rel >> 1) & 1, rel & 1
    return (1 - x if bx else x, 1 - y if by else y, 1 - c if bc else c)


def _index_of(dev):
    return 4 * dev[0] + 2 * dev[1] + dev[2]


FORWARD_LEAD = 2
SIBLING = 1
OTHER_CHIPS = (2, 4, 6)


class Gather:
    def __init__(self, shards, lead=FORWARD_LEAD):
        n = len(shards)
        self.lead = lead
        self.inputs = list(shards)
        self.out_shape = [jax.ShapeDtypeStruct((N_DEV,) + s.shape, s.dtype) for s in shards]
        self.scratch = [pltpu.SemaphoreType.DMA((N_DEV - 1, n)), pltpu.SemaphoreType.DMA((N_DEV - 1, n)),
                        pltpu.SemaphoreType.DMA((n,))]

    @staticmethod
    def _copy(src, dst, sems, rel, k, to):
        return pltpu.make_async_remote_copy(src_ref=src, dst_ref=dst, send_sem=sems[0].at[rel - 1, k],
                                            recv_sem=sems[1].at[rel - 1, k], device_id=to, device_id_type=MESH_ID)

    def start(self, ins, outs, sems):
        me = _my_coords()
        mine = _index_of(me)
        for k, src in enumerate(ins):
            pltpu.make_async_copy(src, outs[k].at[mine], sems[2].at[k]).start()
            for rel in (SIBLING,) + OTHER_CHIPS:
                self._copy(src, outs[k].at[mine], sems, rel, k, _peer(me, rel)).start()

    def forward(self, ins, outs, sems):
        me = _my_coords()
        sibling = _peer(me, SIBLING)
        for rel in OTHER_CHIPS:
            slot = _index_of(_peer(me, rel))
            for k in range(len(ins)):
                self._copy(ins[k], outs[k].at[slot], sems, rel, k, sibling).wait_recv()
                self._copy(outs[k].at[slot], outs[k].at[slot], sems, rel + 1, k, sibling).start()

    def finish(self, ins, outs, sems):
        me = _my_coords()
        mine = _index_of(me)
        sibling = _peer(me, SIBLING)
        for rel in (SIBLING,) + tuple(r + 1 for r in OTHER_CHIPS):
            slot = _index_of(_peer(me, rel))
            for k in range(len(ins)):
                self._copy(ins[k], outs[k].at[slot], sems, rel, k, sibling).wait_recv()
        for rel in range(1, N_DEV):
            for k in range(len(ins)):
                self._copy(ins[k], outs[k].at[mine], sems, rel, k, sibling).wait_send()
        for k, src in enumerate(ins):
            pltpu.make_async_copy(src, outs[k].at[mine], sems[2].at[k]).wait()


class Exchange:
    lead = 0

    def __init__(self, fulls):
        n = len(fulls)
        self.inputs = list(fulls)
        self.out_shape = [jax.ShapeDtypeStruct(f.shape, f.dtype) for f in fulls]
        self.scratch = [pltpu.SemaphoreType.DMA((N_DEV - 1, n)), pltpu.SemaphoreType.DMA((N_DEV - 1, n)),
                        pltpu.SemaphoreType.DMA((n,))]

    def start(self, ins, outs, sems):
        me = _my_coords()
        mine = _index_of(me)
        for k, src in enumerate(ins):
            pltpu.make_async_copy(src.at[mine], outs[k].at[mine], sems[2].at[k]).start()
            for rel in range(1, N_DEV):
                to = _peer(me, rel)
                Gather._copy(src.at[_index_of(to)], outs[k].at[mine], sems, rel, k, to).start()

    def forward(self, ins, outs, sems):
        pass

    def finish(self, ins, outs, sems):
        me = _my_coords()
        mine = _index_of(me)
        for rel in range(1, N_DEV):
            frm = _peer(me, rel)
            for k, src in enumerate(ins):
                Gather._copy(src.at[mine], outs[k].at[_index_of(frm)], sems, rel, k, frm).wait_recv()
        for rel in range(1, N_DEV):
            for k, src in enumerate(ins):
                Gather._copy(src.at[mine], outs[k].at[mine], sems, rel, k, _peer(me, rel)).wait_send()
        for k, src in enumerate(ins):
            pltpu.make_async_copy(src.at[mine], outs[k].at[mine], sems[2].at[k]).wait()


class Together:
    def __init__(self, plans):
        self.plans = list(plans)
        self.lead = max(p.lead for p in self.plans)
        self.inputs = [a for p in self.plans for a in p.inputs]
        self.out_shape = [s for p in self.plans for s in p.out_shape]
        self.scratch = [s for p in self.plans for s in p.scratch]

    def _each(self, ins, outs, sems):
        i = o = s = 0
        for p in self.plans:
            ni, no, ns = len(p.inputs), len(p.out_shape), len(p.scratch)
            yield p, ins[i:i + ni], outs[o:o + no], sems[s:s + ns]
            i, o, s = i + ni, o + no, s + ns

    def start(self, ins, outs, sems):
        for p, pi, po, ps in self._each(ins, outs, sems):
            p.start(pi, po, ps)

    def forward(self, ins, outs, sems):
        for p, pi, po, ps in self._each(ins, outs, sems):
            p.forward(pi, po, ps)

    def finish(self, ins, outs, sems):
        for p, pi, po, ps in self._each(ins, outs, sems):
            p.finish(pi, po, ps)

    def split(self, results):
        out, o = [], 0
        for p in self.plans:
            out.append(results[o:o + len(p.out_shape)])
            o += len(p.out_shape)
        return out


def _hosted_call(body, *, name, grid, in_specs, out_specs, out_shape, scratch_shapes, args, comm=None):
    sem = ("arbitrary",) * len(grid)
    if comm is None:
        res = pl.pallas_call(body, name=name, grid=grid, in_specs=in_specs, out_specs=out_specs, out_shape=out_shape,
                             scratch_shapes=scratch_shapes, compiler_params=_params(sem))(*args)
        return list(res), []
    n_in, n_out, n_scr = len(in_specs), len(out_specs), len(scratch_shapes)
    n_cin, n_cout = len(comm.inputs), len(comm.out_shape)
    n_steps = math.prod(grid)

    def hosted(*refs):
        ins, refs = refs[:n_in], refs[n_in:]
        cins, refs = refs[:n_cin], refs[n_cin:]
        outs, refs = refs[:n_out], refs[n_out:]
        couts, refs = refs[:n_cout], refs[n_cout:]
        scr, csems = refs[:n_scr], refs[n_scr:]
        step = 0
        for a, g in enumerate(grid):
            step = step * g + pl.program_id(a)

        @pl.when(step == 0)
        def _():
            comm.start(cins, couts, csems)

        body(*ins, *outs, *scr)

        @pl.when(step == max(n_steps - 1 - comm.lead, 0))
        def _():
            comm.forward(cins, couts, csems)

        @pl.when(step == n_steps - 1)
        def _():
            comm.finish(cins, couts, csems)

    any_spec = pl.BlockSpec(memory_space=pl.ANY)
    res = pl.pallas_call(
        hosted, name=name, grid=grid, in_specs=list(in_specs) + [any_spec] * n_cin,
        out_specs=list(out_specs) + [any_spec] * n_cout, out_shape=list(out_shape) + comm.out_shape,
        scratch_shapes=list(scratch_shapes) + comm.scratch,
        compiler_params=pltpu.CompilerParams(dimension_semantics=sem, vmem_limit_bytes=VMEM_LIMIT, has_side_effects=True),
    )(*args, *comm.inputs)
    return list(res[:n_out]), list(res[n_out:])


def run_comm(comm, *, name):
    n_cin, n_cout = len(comm.inputs), len(comm.out_shape)

    def body(*refs):
        cins, couts, csems = refs[:n_cin], refs[n_cin:n_cin + n_cout], refs[n_cin + n_cout:]
        comm.start(cins, couts, csems)
        comm.forward(cins, couts, csems)
        comm.finish(cins, couts, csems)

    any_spec = pl.BlockSpec(memory_space=pl.ANY)
    return pl.pallas_call(
        body, name=name, in_specs=[any_spec] * n_cin, out_specs=[any_spec] * n_cout, out_shape=comm.out_shape,
        scratch_shapes=comm.scratch, compiler_params=pltpu.CompilerParams(has_side_effects=True),
    )(*comm.inputs)


def mixer_fwd(x, g1, winT, convw, convb, clng, clnb, wpw, slng, slnb, wm, bias, wbd, pscale, wout, *, seq, tm,
              comm=None):
    T = x.shape[0]
    tiles_per_seq = seq // tm
    n_chunks = tm // CHUNK

    def body(x_ref, g1_ref, winT_ref, convw_ref, convb_ref, clng_ref, clnb_ref, wpw_ref, slng_ref, slnb_ref,
             wm_ref, bias_ref, wbd_ref, pscale_ref, wout_ref,
             z_ref, ycv_ref, p_ref, mix_ref, x1_ref, ybuf, zcbuf, *pbufs):
        i = pl.program_id(0)
        tile_in_seq = i % tiles_per_seq

        @pl.when(tile_in_seq == 0)
        def _():
            ybuf[0:CONV_HALO, :] = jnp.zeros((CONV_HALO, D_A), F32)
            zcbuf[0:POOL_HALO, :] = jnp.zeros((POOL_HALO, D_C), F32)

        x = x_ref[...]
        r = lax.rsqrt(jnp.mean(x * x, axis=-1, keepdims=True) + RMS_EPS)
        h = (x * r * g1_ref[...]).astype(BF16)
        z = _dot_nt(h, winT_ref[...])
        z_ref[...] = z

        y = z[:, 0:D_A] * _sigmoid(z[:, D_A:2 * D_A])
        ybuf[CONV_HALO:CONV_HALO + tm, :] = y
        acc = jnp.zeros((tm, D_A), F32) + convb_ref[...]
        for k, rows in _shifted_taps(ybuf, CONV_HALO - (CONV_WIDTH - 1), CONV_WIDTH, tm):
            acc = acc + convw_ref[k:k + 1, :] * rows
        ybuf[0:CONV_HALO, :] = ybuf[tm:tm + CONV_HALO, :]
        ycv_ref[...] = acc
        xhat, _ = _ln_stats(acc)
        ln = xhat * clng_ref[...] + clnb_ref[...]
        s = ln * _sigmoid(ln)
        ya = _dot(s.astype(BF16), wpw_ref[...])

        gb, _ = _gelu_and_grad(z[:, 2 * D_A:2 * D_A + 2 * D_B])
        u = gb[:, 0:D_B]
        vhat, _ = _ln_stats(gb[:, D_B:2 * D_B])
        vn = vhat * slng_ref[...] + slnb_ref[...]
        masks = _head_masks(D_B)
        yb_parts = []
        for c in range(n_chunks):
            vn_c = vn[c * CHUNK:(c + 1) * CHUNK, :]
            sg = bias_ref[...]
            for hh in range(N_HEADS_B):
                sg = sg + _dot(wm_ref[hh], jnp.where(masks[hh], vn_c, 0.0).astype(BF16))
            yb_parts.append(u[c * CHUNK:(c + 1) * CHUNK, :] * sg)
        yb = jnp.concatenate(yb_parts, axis=0) if n_chunks > 1 else yb_parts[0]

        zc = z[:, 2 * D_A + 2 * D_B:D_IN]
        zcbuf[POOL_HALO:POOL_HALO + tm, :] = zc
        sums = [v[POOL_HALO - SUBLANES * (l + 1):POOL_HALO - SUBLANES * (l + 1) + tm, :]
                for l, v in enumerate(_window_sums_back(zcbuf, pbufs, POOL_HALO + tm))]
        zcbuf[0:POOL_HALO, :] = zcbuf[tm:tm + POOL_HALO, :]
        pos = (tile_in_seq * tm + lax.broadcasted_iota(jnp.int32, (tm, 1), 0)).astype(F32)
        p = _pool_select(sums, D_C) / _pool_counts(pos) - zc
        p_ref[...] = p
        yc = _dot(p.astype(BF16), wbd_ref[...]) * pscale_ref[...]

        mix = jnp.concatenate([ya, yb, yc], axis=1).astype(BF16)
        mix_ref[...] = mix
        x1_ref[...] = x + _dot(mix, wout_ref[...])

    row = lambda w: pl.BlockSpec((tm, w), lambda i: (i, 0))
    return _hosted_call(
        body, name="mixer_fwd", grid=(T // tm,),
        in_specs=[row(D_MODEL), _full((1, D_MODEL)), _full((D_IN, D_MODEL)), _full((CONV_WIDTH, D_A)),
                  _full((1, D_A)), _full((1, D_A)), _full((1, D_A)), _full((D_A, D_A)), _full((1, D_B)), _full((1, D_B)),
                  _full((N_HEADS_B, CHUNK, CHUNK)), _full((CHUNK, D_B)), _full((D_C, D_C)), _full((1, D_C)),
                  _full((D_MODEL, D_MODEL))],
        out_specs=[row(D_IN), row(D_A), row(D_C), row(D_MODEL), row(D_MODEL)],
        out_shape=[jax.ShapeDtypeStruct((T, D_IN), F32), jax.ShapeDtypeStruct((T, D_A), F32),
                   jax.ShapeDtypeStruct((T, D_C), F32), jax.ShapeDtypeStruct((T, D_MODEL), BF16),
                   jax.ShapeDtypeStruct((T, D_MODEL), F32)],
        scratch_shapes=[pltpu.VMEM((CONV_HALO + tm, D_A), F32)]
        + [pltpu.VMEM((POOL_HALO + tm, D_C), F32)] * len(POOL_WINDOWS),
        args=(x, g1, winT, convw, convb, clng, clnb, wpw, slng, slnb, wm, bias, wbd, pscale, wout), comm=comm)


def mixer_bwd(dx1, x, z, ycv, p, g1, winT, convw, clng, clnb, wpw, slng, slnb, wm, wmT, bias, wbd, pscale, wout,
              *, seq, tm, comm=None):
    T = x.shape[0]
    tiles_per_seq = seq // tm
    n_tiles = T // tm
    n_chunks = tm // CHUNK

    def body(dx1_ref, x_ref, z_ref, ycv_ref, p_ref, g1_ref, winT_ref, convw_ref, clng_ref, clnb_ref, wpw_ref,
             slng_ref, slnb_ref, wm_ref, wmT_ref, bias_ref, wbd_ref, pscale_ref, wout_ref,
             dx_ref, dz_ref, h_ref, s_ref, dya_ref,
             dg1_ref, dconvw_ref, dconvb_ref, dclng_ref, dclnb_ref, dslng_ref, dslnb_ref, dwm_ref, dbs_ref,
             dwbd_ref, dpscale_ref, dycbuf, dpcbuf, *pbufs):
        i = pl.program_id(0)
        tile_in_seq = (n_tiles - 1 - i) % tiles_per_seq

        @pl.when(i == 0)
        def _():
            for ref in (dg1_ref, dconvw_ref, dconvb_ref, dclng_ref, dclnb_ref, dslng_ref, dslnb_ref, dwm_ref,
                        dbs_ref, dwbd_ref, dpscale_ref):
                ref[...] = jnp.zeros(ref.shape, F32)

        @pl.when(tile_in_seq == tiles_per_seq - 1)
        def _():
            dycbuf[tm:tm + CONV_HALO, :] = jnp.zeros((CONV_HALO, D_A), F32)
            dpcbuf[tm:tm + POOL_HALO, :] = jnp.zeros((POOL_HALO, D_C), F32)

        dx1 = dx1_ref[...]
        z = z_ref[...]
        dmix = _dot_nt(dx1.astype(BF16), wout_ref[...])
        dya = dmix[:, 0:D_A]
        dyb = dmix[:, D_A:D_A + D_B]
        dyc = dmix[:, D_A + D_B:D_MODEL]

        p = p_ref[...]
        pb = p.astype(BF16)
        q = _dot(pb, wbd_ref[...])
        dpscale_ref[...] += jnp.sum(dyc * q, axis=0, keepdims=True)
        dq = (dyc * pscale_ref[...]).astype(BF16)
        dwbd_ref[...] += _dot_tn(pb, dq)
        dp = _dot_nt(dq, wbd_ref[...])
        pos = (tile_in_seq * tm + lax.broadcasted_iota(jnp.int32, (tm, 1), 0)).astype(F32)
        dpc = dp / _pool_counts(pos)
        dpcbuf[0:tm, :] = dpc
        sums = [v[0:tm, :] for v in _window_sums_ahead(dpcbuf, pbufs, tm + POOL_HALO)]
        dpcbuf[tm:tm + POOL_HALO, :] = dpcbuf[0:POOL_HALO, :]
        dzc = _pool_select(sums, D_C) - dp

        dya_b = dya.astype(BF16)
        dya_ref[...] = dya_b
        ds = _dot_nt(dya_b, wpw_ref[...])
        xhat, rstd = _ln_stats(ycv_ref[...])
        ln = xhat * clng_ref[...] + clnb_ref[...]
        sg = _sigmoid(ln)
        s_ref[...] = (ln * sg).astype(BF16)
        dln = ds * (sg * (1.0 + ln * (1.0 - sg)))
        dclng_ref[...] += jnp.sum(dln * xhat, axis=0, keepdims=True)
        dclnb_ref[...] += jnp.sum(dln, axis=0, keepdims=True)
        dycv = _ln_bwd(dln, xhat, rstd, clng_ref[...])
        dconvb_ref[...] += jnp.sum(dycv, axis=0, keepdims=True)
        a = z[:, 0:D_A]
        sgate = _sigmoid(z[:, D_A:2 * D_A])
        y = a * sgate
        dycbuf[0:tm, :] = dycv
        dy = jnp.zeros((tm, D_A), F32)
        for d, sh in _shifted_taps(dycbuf, 0, CONV_WIDTH, tm):
            k = CONV_WIDTH - 1 - d
            dy = dy + convw_ref[k:k + 1, :] * sh
            dconvw_ref[k:k + 1, :] += jnp.sum(y * sh, axis=0, keepdims=True)
        dycbuf[tm:tm + CONV_HALO, :] = dycbuf[0:CONV_HALO, :]
        da = dy * sgate
        dgate = dy * a * sgate * (1.0 - sgate)

        gb, dgb = _gelu_and_grad(z[:, 2 * D_A:2 * D_A + 2 * D_B])
        u = gb[:, 0:D_B]
        vhat, vrstd = _ln_stats(gb[:, D_B:2 * D_B])
        vn = vhat * slng_ref[...] + slnb_ref[...]
        masks = _head_masks(D_B)
        tril = (lax.broadcasted_iota(jnp.int32, (CHUNK, CHUNK), 0)
                >= lax.broadcasted_iota(jnp.int32, (CHUNK, CHUNK), 1))
        lane128 = lax.broadcasted_iota(jnp.int32, (1, CHUNK), 1)
        du_parts, dvn_parts = [], []
        for c in range(n_chunks):
            rows = slice(c * CHUNK, (c + 1) * CHUNK)
            vn_c = vn[rows, :]
            vh = [jnp.where(masks[hh], vn_c, 0.0).astype(BF16) for hh in range(N_HEADS_B)]
            sgc = bias_ref[...]
            for hh in range(N_HEADS_B):
                sgc = sgc + _dot(wm_ref[hh], vh[hh])
            dyb_c = dyb[rows, :]
            du_parts.append(dyb_c * sgc)
            dsg = dyb_c * u[rows, :]
            dvn_c = jnp.zeros((CHUNK, D_B), F32)
            dbs = jnp.zeros((CHUNK, CHUNK), F32)
            for hh in range(N_HEADS_B):
                dsg_h = jnp.where(masks[hh], dsg, 0.0)
                dsg_hb = dsg_h.astype(BF16)
                dwm_ref[hh] += jnp.where(tril, _dot_nt(dsg_hb, vh[hh]), 0.0)
                dvn_c = dvn_c + _dot(wmT_ref[hh], dsg_hb)
                dbs = dbs + jnp.where(lane128 == hh, jnp.sum(dsg_h, axis=1, keepdims=True), 0.0)
            dbs_ref[...] += dbs
            dvn_parts.append(dvn_c)
        du = jnp.concatenate(du_parts, axis=0) if n_chunks > 1 else du_parts[0]
        dvn = jnp.concatenate(dvn_parts, axis=0) if n_chunks > 1 else dvn_parts[0]
        dslng_ref[...] += jnp.sum(dvn * vhat, axis=0, keepdims=True)
        dslnb_ref[...] += jnp.sum(dvn, axis=0, keepdims=True)
        dv = _ln_bwd(dvn, vhat, vrstd, slng_ref[...])
        dzb = jnp.concatenate([du, dv], axis=1) * dgb

        dz = jnp.concatenate([da, dgate, dzb, dzc], axis=1).astype(BF16)
        dz_ref[...] = dz
        dh = _dot(dz, winT_ref[...])
        x = x_ref[...]
        r = lax.rsqrt(jnp.mean(x * x, axis=-1, keepdims=True) + RMS_EPS)
        xn = x * r
        h_ref[...] = (xn * g1_ref[...]).astype(BF16)
        dg1_ref[...] += jnp.sum(dh * xn, axis=0, keepdims=True)
        dx_ref[...] = dx1 + _rms_bwd(dh, xn, r, g1_ref[...])

    row = lambda w: pl.BlockSpec((tm, w), lambda i: (n_tiles - 1 - i, 0))
    acc_shapes = [(1, D_MODEL), (CONV_WIDTH, D_A), (1, D_A), (1, D_A), (1, D_A), (1, D_B), (1, D_B),
                  (N_HEADS_B, CHUNK, CHUNK), (CHUNK, CHUNK), (D_C, D_C), (1, D_C)]
    return _hosted_call(
        body, name="mixer_bwd", grid=(n_tiles,),
        in_specs=[row(D_MODEL), row(D_MODEL), row(D_IN), row(D_A), row(D_C),
                  _full((1, D_MODEL)), _full((D_IN, D_MODEL)), _full((CONV_WIDTH, D_A)), _full((1, D_A)), _full((1, D_A)),
                  _full((D_A, D_A)), _full((1, D_B)), _full((1, D_B)), _full((N_HEADS_B, CHUNK, CHUNK)),
                  _full((N_HEADS_B, CHUNK, CHUNK)), _full((CHUNK, D_B)), _full((D_C, D_C)), _full((1, D_C)),
                  _full((D_MODEL, D_MODEL))],
        out_specs=[row(D_MODEL), row(D_IN), row(D_MODEL), row(D_A), row(D_A)] + [_full(s) for s in acc_shapes],
        out_shape=[jax.ShapeDtypeStruct((T, D_MODEL), F32), jax.ShapeDtypeStruct((T, D_IN), BF16),
                   jax.ShapeDtypeStruct((T, D_MODEL), BF16), jax.ShapeDtypeStruct((T, D_A), BF16),
                   jax.ShapeDtypeStruct((T, D_A), BF16)] + [jax.ShapeDtypeStruct(s, F32) for s in acc_shapes],
        scratch_shapes=[pltpu.VMEM((tm + CONV_HALO, D_A), F32)]
        + [pltpu.VMEM((tm + POOL_HALO, D_C), F32)] * len(POOL_WINDOWS),
        args=(dx1, x, z, ycv, p, g1, winT, convw, clng, clnb, wpw, slng, slnb, wm, wmT, bias, wbd, pscale, wout),
        comm=comm)


def ffn_fwd(x1, g2, wguT, wd, *, tm, th, comm=None):
    T = x1.shape[0]
    n_h = D_FF // th

    def body(x1_ref, g2_ref, wgu_ref, wd_ref, x2_ref, fac_ref, f_ref, h2_buf, acc):
        j = pl.program_id(1)

        @pl.when(j == 0)
        def _():
            x = x1_ref[...]
            r = lax.rsqrt(jnp.mean(x * x, axis=-1, keepdims=True) + RMS_EPS)
            h2_buf[...] = (x * r * g2_ref[...]).astype(BF16)
            acc[...] = x

        h2 = h2_buf[...]
        g = _dot_nt(h2, wgu_ref[0])
        u = _dot_nt(h2, wgu_ref[1])
        sg = _sigmoid(g)
        silu = g * sg
        fac_ref[0] = silu.astype(BF16)
        fac_ref[1] = (u * (sg * (1.0 + g * (1.0 - sg)))).astype(BF16)
        f = (silu * u).astype(BF16)
        f_ref[...] = f
        acc[...] += _dot(f, wd_ref[...])

        @pl.when(j == n_h - 1)
        def _():
            x2_ref[...] = acc[...]

    return _hosted_call(
        body, name="ffn_fwd", grid=(T // tm, n_h),
        in_specs=[pl.BlockSpec((tm, D_MODEL), lambda i, j: (i, 0)), _full((1, D_MODEL)),
                  pl.BlockSpec((2, th, D_MODEL), lambda i, j: (0, j, 0)), pl.BlockSpec((th, D_MODEL), lambda i, j: (j, 0))],
        out_specs=[pl.BlockSpec((tm, D_MODEL), lambda i, j: (i, 0)),
                   pl.BlockSpec((2, tm, th), lambda i, j: (0, i, j)), pl.BlockSpec((tm, th), lambda i, j: (i, j))],
        out_shape=[jax.ShapeDtypeStruct((T, D_MODEL), F32), jax.ShapeDtypeStruct((2, T, D_FF), BF16),
                   jax.ShapeDtypeStruct((T, D_FF), BF16)],
        scratch_shapes=[pltpu.VMEM((tm, D_MODEL), BF16), pltpu.VMEM((tm, D_MODEL), F32)],
        args=(x1, g2, wguT, wd), comm=comm)


def ffn_bwd(dx2, x1, fac, g2, wguT, wd, *, tm, th, comm=None):
    T = x1.shape[0]
    n_h = D_FF // th

    def body(dx2_ref, x1_ref, fac_ref, g2_ref, wgu_ref, wd_ref, dx1_ref, h2_ref, dgu_ref, dg2_ref, acc):
        i = pl.program_id(0)
        j = pl.program_id(1)

        @pl.when((i == 0) & (j == 0))
        def _():
            dg2_ref[...] = jnp.zeros(dg2_ref.shape, F32)

        dx2 = dx2_ref[...]
        rows = pl.ds(pl.multiple_of(j * th, th), th)
        df = _dot_nt(dx2.astype(BF16), wd_ref[rows, :])
        dup = (df * fac_ref[0].astype(F32)).astype(BF16)
        dgate = (df * fac_ref[1].astype(F32)).astype(BF16)
        dgu_ref[0] = dgate
        dgu_ref[1] = dup

        @pl.when(j == 0)
        def _():
            acc[...] = jnp.zeros(acc.shape, F32)

        acc[...] += _dot(dgate, wgu_ref[0, rows, :]) + _dot(dup, wgu_ref[1, rows, :])

        @pl.when(j == n_h - 1)
        def _():
            x = x1_ref[...]
            r = lax.rsqrt(jnp.mean(x * x, axis=-1, keepdims=True) + RMS_EPS)
            xn = x * r
            dh = acc[...]
            h2_ref[...] = (xn * g2_ref[...]).astype(BF16)
            dg2_ref[...] += jnp.sum(dh * xn, axis=0, keepdims=True)
            dx1_ref[...] = dx2 + _rms_bwd(dh, xn, r, g2_ref[...])

    return _hosted_call(
        body, name="ffn_bwd", grid=(T // tm, n_h),
        in_specs=[pl.BlockSpec((tm, D_MODEL), lambda i, j: (i, 0)), pl.BlockSpec((tm, D_MODEL), lambda i, j: (i, 0)),
                  pl.BlockSpec((2, tm, th), lambda i, j: (0, i, j)), _full((1, D_MODEL)),
                  _full((2, D_FF, D_MODEL)), _full((D_FF, D_MODEL))],
        out_specs=[pl.BlockSpec((tm, D_MODEL), lambda i, j: (i, 0)), pl.BlockSpec((tm, D_MODEL), lambda i, j: (i, 0)),
                   pl.BlockSpec((2, tm, th), lambda i, j: (0, i, j)), _full((1, D_MODEL))],
        out_shape=[jax.ShapeDtypeStruct((T, D_MODEL), F32), jax.ShapeDtypeStruct((T, D_MODEL), BF16),
                   jax.ShapeDtypeStruct((2, T, D_FF), BF16), jax.ShapeDtypeStruct((1, D_MODEL), F32)],
        scratch_shapes=[pltpu.VMEM((tm, D_MODEL), F32)],
        args=(dx2, x1, fac, g2, wguT, wd), comm=comm)


def head_fwd_bwd(x, target, fg, *, tm):
    T = x.shape[0]
    n_tiles = T // tm

    def body(x_ref, t_ref, fg_ref, loss_ref, dx_ref, dfg_ref, lacc):
        i = pl.program_id(0)

        @pl.when(i == 0)
        def _():
            lacc[...] = jnp.zeros(lacc.shape, F32)
            dfg_ref[...] = jnp.zeros(dfg_ref.shape, F32)

        x = x_ref[...]
        r = lax.rsqrt(jnp.mean(x * x, axis=-1, keepdims=True) + RMS_EPS)
        xn = x * r
        e = xn * fg_ref[...] - t_ref[...]
        lacc[...] += jnp.sum(e * e, axis=0, keepdims=True)
        dy = e * (1.0 / D_MODEL)
        dfg_ref[...] += jnp.sum(dy * xn, axis=0, keepdims=True)
        dx_ref[...] = _rms_bwd(dy, xn, r, fg_ref[...])

        @pl.when(i == n_tiles - 1)
        def _():
            loss_ref[...] = jnp.sum(lacc[...], axis=1, keepdims=True) * (0.5 / D_MODEL)

    row = pl.BlockSpec((tm, D_MODEL), lambda i: (i, 0))
    return pl.pallas_call(
        body, name="head_fwd_bwd", grid=(n_tiles,),
        in_specs=[row, row, _full((1, D_MODEL))],
        out_specs=[_full((1, 1)), row, _full((1, D_MODEL))],
        out_shape=[jax.ShapeDtypeStruct((1, 1), F32), jax.ShapeDtypeStruct((T, D_MODEL), F32),
                   jax.ShapeDtypeStruct((1, D_MODEL), F32)],
        scratch_shapes=[pltpu.VMEM((1, D_MODEL), F32)],
        compiler_params=_params(("arbitrary",)),
    )(x, target, fg)


def wgrad(a, b, *, tmo, tk, name, comm=None):
    G, T, M = a.shape
    N = b.shape[1]
    n_k = T // tk

    def body(a_ref, b_ref, o_ref, acc):
        k = pl.program_id(2)
        if n_k == 1:
            o_ref[0] = _dot_tn(a_ref[0].astype(BF16), b_ref[...].astype(BF16)).astype(BF16)
            return

        @pl.when(k == 0)
        def _():
            acc[...] = jnp.zeros(acc.shape, F32)

        acc[...] += _dot_tn(a_ref[0].astype(BF16), b_ref[...].astype(BF16))

        @pl.when(k == n_k - 1)
        def _():
            o_ref[0] = acc[...].astype(BF16)

    (out,), got = _hosted_call(
        body, name=name, grid=(G, M // tmo, n_k),
        in_specs=[pl.BlockSpec((1, tk, tmo), lambda g, m, k: (g, k, m)),
                  pl.BlockSpec((tk, N), lambda g, m, k: (k, 0))],
        out_specs=[pl.BlockSpec((1, tmo, N), lambda g, m, k: (g, m, 0))],
        out_shape=[jax.ShapeDtypeStruct((G, M, N), BF16)],
        scratch_shapes=[pltpu.VMEM((tmo, N), F32)],
        args=(a, b), comm=comm)
    return out, got


def sum_partials(gathered):
    n = len(gathered)

    def body(*refs):
        for in_ref, out_ref in zip(refs[:n], refs[n:]):
            total = in_ref[0].astype(F32)
            for d in range(1, N_DEV):
                total = total + in_ref[d].astype(F32)
            out_ref[...] = total

    vmem = pl.BlockSpec(memory_space=pltpu.VMEM)
    return pl.pallas_call(
        body, name="sum_partials", in_specs=[vmem] * n, out_specs=[vmem] * n,
        out_shape=[jax.ShapeDtypeStruct(g.shape[1:], F32) for g in gathered],
        compiler_params=pltpu.CompilerParams(vmem_limit_bytes=VMEM_LIMIT),
    )(*gathered)


_ADAM_C1 = 1.0 - ADAM_B1 ** ADAM_STEP
_ADAM_C2 = 1.0 - ADAM_B2 ** ADAM_STEP


def _adamw_math(w, g, m, v):
    m = ADAM_B1 * m + (1.0 - ADAM_B1) * g
    v = ADAM_B2 * v + (1.0 - ADAM_B2) * (g * g)
    m_hat = m / _ADAM_C1
    v_hat = v / _ADAM_C2
    delta = -ADAM_LR * (m_hat / (jnp.sqrt(v_hat) + ADAM_EPS) + ADAM_WD * w)
    return delta, m, v


def adamw_sharded(parts, w, m, v, *, tr, name, comm=None):
    _, R, C = parts[0].shape

    def body(p0_ref, p1_ref, w_ref, m_ref, v_ref, g_ref, d_ref, nm_ref, nv_ref):
        def update(p_ref):
            g = p_ref[0].astype(F32)
            for d in range(1, N_DEV):
                g = g + p_ref[d].astype(F32)
            delta, nm, nv = _adamw_math(w_ref[0], g, m_ref[0], v_ref[0])
            g_ref[0] = g
            d_ref[0] = delta
            nm_ref[0] = nm
            nv_ref[0] = nv

        @pl.when(pl.program_id(0) == 0)
        def _():
            update(p0_ref)

        @pl.when(pl.program_id(0) == 1)
        def _():
            update(p1_ref)

    n_i = R // tr
    p_specs = [pl.BlockSpec((N_DEV, tr, C), lambda l, i: (0, jnp.where(l == 0, i, n_i - 1), 0)),
               pl.BlockSpec((N_DEV, tr, C), lambda l, i: (0, jnp.where(l == 1, i, 0), 0))]
    o_spec = pl.BlockSpec((1, tr, C), lambda l, i: (l, i, 0))
    return _hosted_call(
        body, name=name, grid=(DEPTH, n_i),
        in_specs=p_specs + [o_spec, o_spec, o_spec], out_specs=[o_spec] * 4,
        out_shape=[jax.ShapeDtypeStruct(w.shape, F32)] * 4, scratch_shapes=[],
        args=(parts[0], parts[1], w, m, v), comm=comm)


def adamw_small(gs, ws, ms, vs):
    n = len(gs)

    def body(*refs):
        g_refs, w_refs, m_refs, v_refs = refs[:n], refs[n:2 * n], refs[2 * n:3 * n], refs[3 * n:4 * n]
        d_refs, nm_refs, nv_refs = refs[4 * n:5 * n], refs[5 * n:6 * n], refs[6 * n:]
        for k in range(n):
            delta, nm, nv = _adamw_math(w_refs[k][...], g_refs[k][...], m_refs[k][...], v_refs[k][...])
            d_refs[k][...] = delta
            nm_refs[k][...] = nm
            nv_refs[k][...] = nv

    vmem = pl.BlockSpec(memory_space=pltpu.VMEM)
    res = pl.pallas_call(
        body, name="adamw_small", in_specs=[vmem] * (4 * n), out_specs=[vmem] * (3 * n),
        out_shape=[jax.ShapeDtypeStruct(w.shape, F32) for w in ws] * 3,
        compiler_params=pltpu.CompilerParams(vmem_limit_bytes=VMEM_LIMIT),
    )(*gs, *ws, *ms, *vs)
    return res[:n], res[n:2 * n], res[2 * n:]


def _pack(arrays, row_multiple):
    flat = jnp.concatenate([a.reshape(-1) for a in arrays])
    rows = -(-flat.shape[0] // (LANES * row_multiple)) * row_multiple
    return jnp.pad(flat, (0, rows * LANES - flat.shape[0])).reshape(rows, LANES)


def _unpack(buf, shapes):
    flat = buf.reshape(-1)
    out, off = [], 0
    for s in shapes:
        n = math.prod(s)
        out.append(flat[off:off + n].reshape(s))
        off += n
    return out


def _block_diag(w_pool):
    G, d, _ = w_pool.shape
    eye = jnp.eye(G, dtype=w_pool.dtype)
    return (eye[:, None, :, None] * w_pool[:, :, None, :]).reshape(G * d, G * d)


def kernel(x, norm1_g, w_in, conv_w, conv_b, conv_ln_g, conv_ln_b, w_pw, sg_ln_g, sg_ln_b, w_s, b_s, w_pool, pool_scale, w_out, norm2_g, w_gate_up, w_down, final_g, loss_target, m_norm1_g, m_w_in, m_conv_w, m_conv_b, m_conv_ln_g, m_conv_ln_b, m_w_pw, m_sg_ln_g, m_sg_ln_b, m_w_s, m_b_s, m_w_pool, m_pool_scale, m_w_out, m_norm2_g, m_w_gate_up, m_w_down, m_final_g, v_norm1_g, v_w_in, v_conv_w, v_conv_b, v_conv_ln_g, v_conv_ln_b, v_w_pw, v_sg_ln_g, v_sg_ln_b, v_w_s, v_b_s, v_w_pool, v_pool_scale, v_w_out, v_norm2_g, v_w_gate_up, v_w_down, v_final_g):
    b_loc, seq, _ = x.shape
    T = b_loc * seq
    tm_mix = min(256, seq)
    tm_ffn_fwd = min(512, T)
    tm_ffn_bwd = min(512, T)
    tm_head = min(512, T)
    tk = min(2048, T)
    tk_f32 = min(1024, T)
    th = D_FF // 2
    cw = conv_w.shape[2]
    my_index = _index_of(_my_coords())

    xf = x.reshape(T, D_MODEL)
    tgt = loss_target.reshape(T, D_MODEL)

    mixer_shards = [[w_in[l].T.astype(BF16), w_out[l].astype(BF16), w_pw[l].astype(BF16)] for l in range(DEPTH)]
    ffn_shards = [[w_gate_up[l].T.astype(BF16), w_down[l].astype(BF16)] for l in range(DEPTH)]

    tril = jnp.tril(jnp.ones((CHUNK, CHUNK), dtype=bool))
    layers = []
    for l in range(DEPTH):
        wm = jnp.where(tril[None], w_s[l], 0.0).astype(BF16)
        layers.append(dict(
            g1=norm1_g[l][None], convb=conv_b[l][None], clng=conv_ln_g[l][None], clnb=conv_ln_b[l][None],
            slng=sg_ln_g[l][None], slnb=sg_ln_b[l][None], wm=wm, wmT=jnp.swapaxes(wm, 1, 2),
            bias=jnp.repeat(b_s[l].T, HEAD_DIM_B, axis=1), wbd=_block_diag(w_pool[l]).astype(BF16),
            pscale=pool_scale[l][None], g2=norm2_g[l][None]))

    def set_mixer_weights(l, g_in, g_out, g_pw):
        layers[l].update(winT=g_in.reshape(D_IN, D_MODEL), wout=g_out.reshape(D_MODEL, D_MODEL), wpw=g_pw.reshape(D_A, D_A))

    def set_ffn_weights(l, g_gu, g_d):
        layers[l].update(wguT=g_gu.reshape(2, D_FF, D_MODEL), wd=g_d.reshape(D_FF, D_MODEL))

    first = run_comm(Gather(mixer_shards[0] + [conv_w.reshape(DEPTH * CONV_WIDTH, cw).T]), name="gather_first")
    set_mixer_weights(0, *first[:3])
    convw_full = first[3].reshape(D_A, DEPTH * CONV_WIDTH).T.reshape(DEPTH, CONV_WIDTH, D_A)
    for l in range(DEPTH):
        layers[l]["convw"] = convw_full[l]

    saved = []
    cur = xf
    for l in range(DEPTH):
        w = layers[l]
        (z, ycv, p, mix, x1), got = mixer_fwd(
            cur, w["g1"], w["winT"], w["convw"], w["convb"], w["clng"], w["clnb"], w["wpw"], w["slng"], w["slnb"], w["wm"],
            w["bias"], w["wbd"], w["pscale"], w["wout"], seq=seq, tm=tm_mix,
            comm=Gather(ffn_shards[l], lead=0) if l == 0 else Gather(ffn_shards[l][:1]))
        if l == 0:
            set_ffn_weights(l, *got)
        else:
            set_ffn_weights(l, got[0], early_wd)
        (x2, fac, f), got = ffn_fwd(x1, w["g2"], w["wguT"], w["wd"], tm=tm_ffn_fwd, th=th,
                                comm=Gather(mixer_shards[l + 1] + ffn_shards[l + 1][1:]) if l + 1 < DEPTH else None)
        if l + 1 < DEPTH:
            set_mixer_weights(l + 1, *got[:3])
            early_wd = got[3]
        saved.append((cur, z, ycv, p, mix, x1, fac, f))
        cur = x2
    loss_part, dx, dfg = head_fwd_bwd(cur, tgt, final_g[None], tm=tm_head)

    blocks = {"gu": (2 * D_FF // N_DEV, D_MODEL), "d": (D_FF // N_DEV, D_MODEL), "in": (D_IN // N_DEV, D_MODEL),
              "out": (D_MODEL // N_DEV, D_MODEL), "pw": (D_A // N_DEV, D_A)}
    by_device = lambda kind, g: g.reshape((N_DEV,) + blocks[kind])
    small = [None] * DEPTH
    parts = {}
    packs = [None] * DEPTH

    pending = None
    for l in reversed(range(DEPTH)):
        w = layers[l]
        x0, z, ycv, p, mix, x1, fac, f = saved[l]
        plan = Together([Exchange(pending[1]), Gather([pending[2]])]) if pending else None
        (dx1, h2, dgu, dg2), got = ffn_bwd(dx, x1, fac, w["g2"], w["wguT"], w["wd"], tm=tm_ffn_bwd, th=th, comm=plan)
        if pending:
            got_parts, (packs[l + 1],) = plan.split(got)
            parts.update(zip(pending[0], got_parts))
        gw_d, _ = wgrad(f[None], dx, tmo=th, tk=tk_f32, name="wgrad_down")
        last = l == 0
        gw_gu, got = wgrad(dgu, h2, tmo=th, tk=tk, name="wgrad_gate_up", comm=Exchange([by_device("d", gw_d)]))
        parts[("d", l)], = got
        outs, got = mixer_bwd(
            dx1, x0, z, ycv, p, w["g1"], w["winT"], w["convw"], w["clng"], w["clnb"], w["wpw"], w["slng"], w["slnb"],
            w["wm"], w["wmT"], w["bias"], w["wbd"], w["pscale"], w["wout"], seq=seq, tm=tm_mix,
            comm=Exchange([by_device("gu", gw_gu)]))
        parts[("gu", l)], = got
        (dx, dz, h, s, dya, dg1, dconvw, dconvb, dclng, dclnb, dslng, dslnb, dwm, dbs, dwbd, dpscale) = outs
        small[l] = [dg1, dconvw, dconvb, dclng, dclnb, dslng, dslnb, dwm, dbs, dwbd, dpscale, dg2]
        pack = _pack(small[l], 2 * SUBLANES).astype(BF16)
        if last:
            gw_in, _ = wgrad(dz[None], h, tmo=D_IN // 2, tk=tk, name="wgrad_in")
            plan = Together([Exchange([by_device("in", gw_in)]),
                             Gather([pack, _pack([dfg, loss_part], SUBLANES)], lead=1)])
            gw_out, got = wgrad(mix[None], dx1, tmo=D_MODEL, tk=tk_f32, name="wgrad_out", comm=plan)
            (parts[("in", l)],), (packs[l], head_pack) = plan.split(got)
            gw_pw, got = wgrad(s[None], dya, tmo=D_A, tk=tk, name="wgrad_pw", comm=Exchange([by_device("out", gw_out)]))
            parts[("out", l)], = got
            parts[("pw", l)], = run_comm(Exchange([by_device("pw", gw_pw)]), name="exchange_last")
        else:
            gw_out, _ = wgrad(mix[None], dx1, tmo=D_MODEL, tk=tk_f32, name="wgrad_out")
            gw_in, _ = wgrad(dz[None], h, tmo=D_IN // 2, tk=tk, name="wgrad_in")
            gw_pw, _ = wgrad(s[None], dya, tmo=D_A, tk=tk, name="wgrad_pw")
            pending = ([("in", l), ("out", l), ("pw", l)],
                       [by_device("in", gw_in), by_device("out", gw_out), by_device("pw", gw_pw)], pack)
    grad_x = dx.reshape(x.shape)
    p_in, p_gu, p_d, p_out, p_pw = [[parts[(k, l)] for l in range(DEPTH)] for k in ("in", "gu", "d", "out", "pw")]

    summed = sum_partials(packs + [head_pack])
    dfg_sum, loss_sum = _unpack(summed[DEPTH], [dfg.shape, loss_part.shape])
    loss = loss_sum[0, 0]
    per_layer = len(small[0])
    sums = [a for l in range(DEPTH) for a in _unpack(summed[l], [a.shape for a in small[l]])]
    g_small = {k: [] for k in ("norm1_g", "conv_w", "conv_b", "conv_ln_g", "conv_ln_b", "sg_ln_g", "sg_ln_b", "w_s", "b_s",
                               "w_pool", "pool_scale", "norm2_g")}
    for l in range(DEPTH):
        dg1, dconvw, dconvb, dclng, dclnb, dslng, dslnb, dwm, dbs, dwbd, dpscale, dg2 = sums[per_layer * l:per_layer * (l + 1)]
        g_small["norm1_g"].append(dg1[0])
        g_small["conv_w"].append(lax.dynamic_slice_in_dim(dconvw, my_index * cw, cw, axis=1))
        g_small["conv_b"].append(dconvb[0])
        g_small["conv_ln_g"].append(dclng[0])
        g_small["conv_ln_b"].append(dclnb[0])
        g_small["sg_ln_g"].append(dslng[0])
        g_small["sg_ln_b"].append(dslnb[0])
        g_small["w_s"].append(dwm)
        g_small["b_s"].append(dbs[:, :N_HEADS_B].T)
        g_small["w_pool"].append(jnp.stack([dwbd[g * GROUP_DIM_C:(g + 1) * GROUP_DIM_C, g * GROUP_DIM_C:(g + 1) * GROUP_DIM_C]
                                            for g in range(len(POOL_WINDOWS))]))
        g_small["pool_scale"].append(dpscale[0])
        g_small["norm2_g"].append(dg2[0])
    g_small = {k: jnp.stack(v) for k, v in g_small.items()}
    g_small["final_g"] = dfg_sum[0]

    t = lambda a: jnp.swapaxes(a, 1, 2)
    g_w_in, d_w_in, nm_w_in, nv_w_in = map(t, adamw_sharded(p_in, t(w_in), t(m_w_in), t(v_w_in), tr=D_IN // N_DEV // 2,
                                                            name="adamw_w_in")[0])
    g_w_gu, d_w_gu, nm_w_gu, nv_w_gu = map(t, adamw_sharded(p_gu, t(w_gate_up), t(m_w_gate_up), t(v_w_gate_up),
                                                            tr=2 * D_FF // N_DEV // 4, name="adamw_w_gate_up")[0])
    g_w_d, d_w_d, nm_w_d, nv_w_d = adamw_sharded(p_d, w_down, m_w_down, v_w_down, tr=D_FF // N_DEV // 2,
                                                 name="adamw_w_down")[0]
    g_w_out, d_w_out, nm_w_out, nv_w_out = adamw_sharded(p_out, w_out, m_w_out, v_w_out, tr=D_MODEL // N_DEV,
                                                         name="adamw_w_out")[0]
    g_w_pw, d_w_pw, nm_w_pw, nv_w_pw = adamw_sharded(p_pw, w_pw, m_w_pw, v_w_pw, tr=D_A // N_DEV, name="adamw_w_pw")[0]

    small_names = ["norm1_g", "conv_w", "conv_b", "conv_ln_g", "conv_ln_b", "sg_ln_g", "sg_ln_b", "w_s", "b_s", "w_pool",
                   "pool_scale", "norm2_g", "final_g"]
    small_w = dict(norm1_g=norm1_g, conv_w=conv_w, conv_b=conv_b, conv_ln_g=conv_ln_g, conv_ln_b=conv_ln_b, sg_ln_g=sg_ln_g,
                   sg_ln_b=sg_ln_b, w_s=w_s, b_s=b_s, w_pool=w_pool, pool_scale=pool_scale, norm2_g=norm2_g, final_g=final_g)
    small_m = dict(norm1_g=m_norm1_g, conv_w=m_conv_w, conv_b=m_conv_b, conv_ln_g=m_conv_ln_g, conv_ln_b=m_conv_ln_b,
                   sg_ln_g=m_sg_ln_g, sg_ln_b=m_sg_ln_b, w_s=m_w_s, b_s=m_b_s, w_pool=m_w_pool, pool_scale=m_pool_scale,
                   norm2_g=m_norm2_g, final_g=m_final_g)
    small_v = dict(norm1_g=v_norm1_g, conv_w=v_conv_w, conv_b=v_conv_b, conv_ln_g=v_conv_ln_g, conv_ln_b=v_conv_ln_b,
                   sg_ln_g=v_sg_ln_g, sg_ln_b=v_sg_ln_b, w_s=v_w_s, b_s=v_b_s, w_pool=v_w_pool, pool_scale=v_pool_scale,
                   norm2_g=v_norm2_g, final_g=v_final_g)
    two_d = lambda a: a[None] if a.ndim == 1 else a
    d_s, nm_s, nv_s = adamw_small(*[[two_d(d[k]) for k in small_names] for d in (g_small, small_w, small_m, small_v)])
    d_small = {k: a.reshape(small_w[k].shape) for k, a in zip(small_names, d_s)}
    nm_small = {k: a.reshape(small_w[k].shape) for k, a in zip(small_names, nm_s)}
    nv_small = {k: a.reshape(small_w[k].shape) for k, a in zip(small_names, nv_s)}

    order = ["norm1_g", "w_in", "conv_w", "conv_b", "conv_ln_g", "conv_ln_b", "w_pw", "sg_ln_g", "sg_ln_b", "w_s", "b_s",
             "w_pool", "pool_scale", "w_out", "norm2_g", "w_gate_up", "w_down", "final_g"]
    grads = dict(g_small, w_in=g_w_in, w_pw=g_w_pw, w_out=g_w_out, w_gate_up=g_w_gu, w_down=g_w_d)
    deltas = dict(d_small, w_in=d_w_in, w_pw=d_w_pw, w_out=d_w_out, w_gate_up=d_w_gu, w_down=d_w_d)
    new_m = dict(nm_small, w_in=nm_w_in, w_pw=nm_w_pw, w_out=nm_w_out, w_gate_up=nm_w_gu, w_down=nm_w_d)
    new_v = dict(nv_small, w_in=nv_w_in, w_pw=nv_w_pw, w_out=nv_w_out, w_gate_up=nv_w_gu, w_down=nv_w_d)
    return (loss, grad_x, *[grads[k] for k in order], *[deltas[k] for k in order], *[new_m[k] for k in order],
            *[new_v[k] for k in order])
```
